```python
import math
import jax
import jax.numpy as jnp
from jax import lax
import numpy as np

D_MODEL = 2048
BATCH = 4
SEQ = 2048
DEPTH = 1
DEC_BATCH = 128
DEC_SEQ = 4
PAST_LEN = 16384
PAGE_SIZE = 128

CONV_WIDTH = D_MODEL // 4
MLSTM_HEADS = 4
MLSTM_WIDTH = D_MODEL // 2
MLSTM_HEAD_DIM = MLSTM_WIDTH // MLSTM_HEADS
MEM_HEADS = 4
MEM_WIDTH = D_MODEL - CONV_WIDTH - MLSTM_WIDTH
MEM_HEAD_DIM = MEM_WIDTH // MEM_HEADS
N_MEM = 256
CONV_K = 31
MLSTM_CHUNK = 128
N_GROUPS = 8
EXPERTS_PER_GROUP = 8
N_EXPERTS = N_GROUPS * EXPERTS_PER_GROUP
TOP_K_IN_GROUP = 2
D_EXPERT = D_MODEL // 4
MOE_BLOCK = 64
LN_EPS = 1e-5
ALPHA = (2 * DEPTH) ** 0.25
BETA = (8 * DEPTH) ** -0.25
SPLIT_POINTS = (CONV_WIDTH, 2 * CONV_WIDTH,
                2 * CONV_WIDTH + MLSTM_WIDTH, 2 * CONV_WIDTH + 2 * MLSTM_WIDTH,
                2 * CONV_WIDTH + 3 * MLSTM_WIDTH, 2 * CONV_WIDTH + 4 * MLSTM_WIDTH,
                2 * CONV_WIDTH + 4 * MLSTM_WIDTH + MLSTM_HEADS,
                2 * CONV_WIDTH + 4 * MLSTM_WIDTH + 2 * MLSTM_HEADS)
F_GATE_START = 2 * CONV_WIDTH + 4 * MLSTM_WIDTH + MLSTM_HEADS
IN_COLS = 2 * CONV_WIDTH + 4 * MLSTM_WIDTH + 2 * MLSTM_HEADS + MEM_WIDTH
F32 = jnp.float32

kernel_name = 'hymba_conv_mlstm_memxattn_hiermoe_step'


def layer_norm(x, g, b):
    xf = x.astype(F32)
    mu = xf.mean(-1, keepdims=True)
    var = jnp.square(xf - mu).mean(-1, keepdims=True)
    return ((xf - mu) * lax.rsqrt(var + LN_EPS) * g.astype(F32) + b.astype(F32)).astype(x.dtype)


def conformer_conv(a, gate, conv_buf, w_dw, b_dw, g_cn, b_cn):
    u = a * jax.nn.sigmoid(gate)
    full = jnp.concatenate([conv_buf.astype(u.dtype), u], axis=1)
    y = lax.conv_general_dilated(full, w_dw[:, None, :].astype(u.dtype), window_strides=(1,),
                                 padding='VALID', dimension_numbers=('NWC', 'WIO', 'NWC'),
                                 feature_group_count=CONV_WIDTH) + b_dw
    y = jax.nn.silu(layer_norm(y, g_cn, b_cn))
    return y, full[:, -(CONV_K - 1):]


def mlstm_chunkwise(q, k, v, i_pre, f_pre, C0, n0, m0):
    B, H, L, d = q.shape
    cl = math.gcd(L, MLSTM_CHUNK)
    nc = L // cl
    chunk4 = lambda t: t.astype(F32).reshape(B, H, nc, cl, d).transpose(2, 0, 1, 3, 4)
    chunk3 = lambda t: t.astype(F32).reshape(B, H, nc, cl).transpose(2, 0, 1, 3)
    xs = (chunk4(q), chunk4(k) * d ** -0.5, chunk4(v), chunk3(i_pre), chunk3(jax.nn.log_sigmoid(f_pre.astype(F32))))
    causal = jnp.tril(jnp.ones((cl, cl), bool))

    def step(carry, blk):
        C, n, m = carry
        qb, kb, vb, ib, lfb = blk
        b = jnp.cumsum(lfb, axis=-1)
        log_inter = b + m[..., None]
        log_intra = jnp.where(causal, b[..., :, None] - b[..., None, :] + ib[..., None, :], -jnp.inf)
        m_t = jnp.maximum(log_inter, log_intra.max(-1))
        w_inter = jnp.exp(log_inter - m_t)
        s = jnp.einsum('bhtd,bhsd->bhts', qb, kb) * jnp.exp(log_intra - m_t[..., None])
        num = w_inter[..., None] * jnp.einsum('bhtd,bhde->bhte', qb, C) + jnp.einsum('bhts,bhse->bhte', s, vb)
        den = w_inter * jnp.einsum('bhtd,bhd->bht', qb, n) + s.sum(-1)
        h = num / jnp.maximum(jnp.abs(den), jnp.exp(-m_t))[..., None]
        m_new = m_t[..., -1]
        decay = jnp.exp(b[..., -1] + m - m_new)
        w_s = jnp.exp(b[..., -1:] - b + ib - m_new[..., None])
        C_new = decay[..., None, None] * C + jnp.einsum('bhs,bhsd,bhse->bhde', w_s, kb, vb)
        n_new = decay[..., None] * n + jnp.einsum('bhs,bhsd->bhd', w_s, kb)
        return (C_new, n_new, m_new), h

    (C, n, m), hs = lax.scan(step, (C0, n0, m0), xs)
    h = hs.transpose(1, 2, 0, 3, 4).reshape(B, H, L, d)
    return h, C, n, m


def head_norm(h, g):
    B, H, L, d = h.shape
    mu = h.mean(-1, keepdims=True)
    var = jnp.square(h - mu).mean(-1, keepdims=True)
    hn = ((h - mu) * lax.rsqrt(var + LN_EPS)).transpose(0, 2, 1, 3).reshape(B, L, H * d)
    return hn * g.astype(F32)


def memory_attention(qm, mk, mv):
    s = jnp.einsum('blhd,bnhd->bhln', qm, mk.astype(qm.dtype)).astype(F32) * MEM_HEAD_DIM ** -0.5
    p = jax.nn.softmax(s, axis=-1).astype(qm.dtype)
    return jnp.einsum('bhln,bnhd->blhd', p, mv.astype(qm.dtype))


def hier_moe(x2, w_rg, b_rg, w_re, b_re, w_gate, w_up, w_down):
    T = x2.shape[0]
    xf = x2.astype(F32)
    g_logits = xf @ w_rg.astype(F32) + b_rg.astype(F32)
    g_sel = jnp.argmax(g_logits, axis=-1)
    g_w = jnp.take_along_axis(jax.nn.softmax(g_logits, -1), g_sel[:, None], -1)[:, 0]
    e_logits = (xf @ w_re.astype(F32) + b_re.astype(F32)).reshape(T, N_GROUPS, EXPERTS_PER_GROUP)
    e_sel_logits = jnp.take_along_axis(e_logits, g_sel[:, None, None], 1)[:, 0]
    top_v, top_i = lax.top_k(e_sel_logits, TOP_K_IN_GROUP)
    e_w = jax.nn.softmax(top_v, -1) * g_w[:, None]
    e_id = g_sel[:, None] * EXPERTS_PER_GROUP + top_i
    A = T * TOP_K_IN_GROUP
    flat_e = e_id.reshape(A).astype(jnp.int32)
    flat_t = jnp.repeat(jnp.arange(T, dtype=jnp.int32), TOP_K_IN_GROUP)
    flat_w = e_w.reshape(A)
    order = jnp.argsort(flat_e)
    se, st, sw = flat_e[order], flat_t[order], flat_w[order]
    counts = jnp.bincount(flat_e, length=N_EXPERTS)
    padded = (counts + MOE_BLOCK - 1) // MOE_BLOCK * MOE_BLOCK
    start = jnp.cumsum(counts) - counts
    pend = jnp.cumsum(padded)
    pstart = pend - padded
    dest = pstart[se] + jnp.arange(A, dtype=jnp.int32) - start[se]
    n_blk = (A + MOE_BLOCK - 1) // MOE_BLOCK + N_EXPERTS
    R = n_blk * MOE_BLOCK
    tok_buf = jnp.zeros((R,), jnp.int32).at[dest].set(st)
    w_buf = jnp.zeros((R,), F32).at[dest].set(sw)
    blk_e = jnp.minimum(jnp.searchsorted(pend, jnp.arange(n_blk) * MOE_BLOCK, side='right'), N_EXPERTS - 1)

    def run_block(args):
        idx, e = args
        xb = x2[idx]
        hb = jax.nn.silu(xb @ w_gate[e]) * (xb @ w_up[e])
        return hb @ w_down[e]

    out = lax.map(run_block, (tok_buf.reshape(n_blk, MOE_BLOCK), blk_e))
    y = jnp.zeros((T, x2.shape[1]), F32).at[tok_buf].add(out.reshape(R, -1).astype(F32) * w_buf[:, None])
    return y.astype(x2.dtype)


def layer_forward(x, mem_k, mem_v, conv_buf, C0, n0, m0, w_in, b_in, w_dw, b_dw, g_cn, b_cn, g_mh,
                  w_out, g_ln1, b_ln1, w_rg, b_rg, w_re, b_re, w_gate, w_up, w_down, g_ln2, b_ln2):
    B, L, _ = x.shape
    z = x @ w_in + b_in
    a, gt, q, k, v, o, ip, fp, qm = jnp.split(z, SPLIT_POINTS, axis=-1)
    conv_out, new_buf = conformer_conv(a, gt, conv_buf, w_dw, b_dw, g_cn, b_cn)
    to_heads = lambda t: t.reshape(B, L, MLSTM_HEADS, MLSTM_HEAD_DIM).transpose(0, 2, 1, 3)
    h, C, n, m = mlstm_chunkwise(to_heads(q), to_heads(k), to_heads(v),
                                 ip.transpose(0, 2, 1), fp.transpose(0, 2, 1), C0, n0, m0)
    h = (head_norm(h, g_mh) * jax.nn.sigmoid(o.astype(F32))).astype(x.dtype)
    mem_out = memory_attention(qm.reshape(B, L, MEM_HEADS, MEM_HEAD_DIM), mem_k, mem_v).reshape(B, L, MEM_WIDTH)
    mix = jnp.concatenate([conv_out.astype(x.dtype), h, mem_out.astype(x.dtype)], axis=-1) @ w_out
    x = layer_norm(ALPHA * x + mix, g_ln1, b_ln1)
    ffn = hier_moe(x.reshape(B * L, D_MODEL), w_rg, b_rg, w_re, b_re, w_gate, w_up, w_down).reshape(B, L, D_MODEL)
    x = layer_norm(ALPHA * x + ffn, g_ln2, b_ln2)
    return x, new_buf, C, n, m


def setup_inputs(seed: int = 0) -> dict:
    key = jax.random.key(seed)
    ks = jax.random.split(key, 32)
    nrm = lambda k, shape, s: jax.random.normal(k, shape, jnp.float32) * s
    L = DEPTH
    b_in = nrm(ks[9], (L, IN_COLS), 0.02)
    b_in = b_in.at[:, F_GATE_START:F_GATE_START + MLSTM_HEADS].add(jnp.linspace(3.0, 6.0, MLSTM_HEADS))
    return {
        'x_prompt': nrm(ks[0], (BATCH, SEQ, D_MODEL), 1.0),
        'x_sample': nrm(ks[1], (DEC_BATCH, DEC_SEQ, D_MODEL), 1.0),
        'mem_prompt': nrm(ks[2], (BATCH, N_MEM, D_MODEL), 1.0),
        'state_conv': nrm(ks[3], (L, DEC_BATCH, CONV_K - 1, CONV_WIDTH), 0.5),
        'state_mlstm_C': nrm(ks[4], (L, DEC_BATCH, MLSTM_HEADS, MLSTM_HEAD_DIM, MLSTM_HEAD_DIM), 0.05),
        'state_mlstm_n': nrm(ks[5], (L, DEC_BATCH, MLSTM_HEADS, MLSTM_HEAD_DIM), 0.1),
        'state_mlstm_m': nrm(ks[6], (L, DEC_BATCH, MLSTM_HEADS), 1.0),
        'cache_mem_k': nrm(ks[7], (L, DEC_BATCH, N_MEM, MEM_HEADS, MEM_HEAD_DIM), 1.0),
        'cache_mem_v': nrm(ks[8], (L, DEC_BATCH, N_MEM, MEM_HEADS, MEM_HEAD_DIM), 1.0),
        'w_in': nrm(ks[10], (L, D_MODEL, IN_COLS), D_MODEL ** -0.5),
        'b_in': b_in,
        'w_dw': nrm(ks[11], (L, CONV_K, CONV_WIDTH), CONV_K ** -0.5),
        'b_dw': nrm(ks[12], (L, CONV_WIDTH), 0.02),
        'g_cn': 1.0 + nrm(ks[13], (L, CONV_WIDTH), 0.02),
        'b_cn': nrm(ks[14], (L, CONV_WIDTH), 0.02),
        'g_mh': 1.0 + nrm(ks[15], (L, MLSTM_WIDTH), 0.02),
        'w_mk': nrm(ks[16], (L, D_MODEL, MEM_WIDTH), D_MODEL ** -0.5),
        'w_mv': nrm(ks[17], (L, D_MODEL, MEM_WIDTH), D_MODEL ** -0.5),
        'w_out': nrm(ks[18], (L, D_MODEL, D_MODEL), D_MODEL ** -0.5 * BETA),
        'g_ln1': 1.0 + nrm(ks[19], (L, D_MODEL), 0.02),
        'b_ln1': nrm(ks[20], (L, D_MODEL), 0.02),
        'w_rg': nrm(ks[21], (L, D_MODEL, N_GROUPS), D_MODEL ** -0.5),
        'b_rg': nrm(ks[22], (L, N_GROUPS), 0.01),
        'w_re': nrm(ks[23], (L, D_MODEL, N_EXPERTS), D_MODEL ** -0.5),
        'b_re': nrm(ks[24], (L, N_EXPERTS), 0.01),
        'w_gate': nrm(ks[25], (L, N_EXPERTS, D_MODEL, D_EXPERT), D_MODEL ** -0.5),
        'w_up': nrm(ks[26], (L, N_EXPERTS, D_MODEL, D_EXPERT), D_MODEL ** -0.5),
        'w_down': nrm(ks[27], (L, N_EXPERTS, D_EXPERT, D_MODEL), D_EXPERT ** -0.5 * BETA),
        'g_ln2': 1.0 + nrm(ks[28], (L, D_MODEL), 0.02),
        'b_ln2': nrm(ks[29], (L, D_MODEL), 0.02),
    }


def reference(x_prompt, x_sample, mem_prompt, state_conv, state_mlstm_C, state_mlstm_n, state_mlstm_m,
              cache_mem_k, cache_mem_v, w_in, b_in, w_dw, b_dw, g_cn, b_cn, g_mh, w_mk, w_mv, w_out,
              g_ln1, b_ln1, w_rg, b_rg, w_re, b_re, w_gate, w_up, w_down, g_ln2, b_ln2):
    Bp = x_prompt.shape[0]
    yp, ys = x_prompt, x_sample
    conv_p, conv_s, C_p, C_s, n_p, n_s, m_p, m_s, mk_p, mv_p = [], [], [], [], [], [], [], [], [], []
    for l in range(DEPTH):
        wts = (w_in[l], b_in[l], w_dw[l], b_dw[l], g_cn[l], b_cn[l], g_mh[l], w_out[l], g_ln1[l], b_ln1[l],
               w_rg[l], b_rg[l], w_re[l], b_re[l], w_gate[l], w_up[l], w_down[l], g_ln2[l], b_ln2[l])
        mk = (mem_prompt @ w_mk[l]).reshape(Bp, N_MEM, MEM_HEADS, MEM_HEAD_DIM)
        mv = (mem_prompt @ w_mv[l]).reshape(Bp, N_MEM, MEM_HEADS, MEM_HEAD_DIM)
        buf0 = jnp.zeros((Bp, CONV_K - 1, CONV_WIDTH), x_prompt.dtype)
        C0 = jnp.zeros((Bp, MLSTM_HEADS, MLSTM_HEAD_DIM, MLSTM_HEAD_DIM), F32)
        n0 = jnp.zeros((Bp, MLSTM_HEADS, MLSTM_HEAD_DIM), F32)
        m0 = jnp.zeros((Bp, MLSTM_HEADS), F32)
        yp, bp, cp, nnp, mp = layer_forward(yp, mk, mv, buf0, C0, n0, m0, *wts)
        ys, bs, cs, nns, ms = layer_forward(ys, cache_mem_k[l], cache_mem_v[l], state_conv[l],
                                            state_mlstm_C[l].astype(F32), state_mlstm_n[l].astype(F32),
                                            state_mlstm_m[l].astype(F32), *wts)
        conv_p.append(bp); conv_s.append(bs)
        C_p.append(cp); C_s.append(cs)
        n_p.append(nnp); n_s.append(nns)
        m_p.append(mp); m_s.append(ms)
        mk_p.append(mk); mv_p.append(mv)
    return (yp, ys, jnp.stack(conv_p), jnp.stack(conv_s), jnp.stack(C_p), jnp.stack(C_s),
            jnp.stack(n_p), jnp.stack(n_s), jnp.stack(m_p), jnp.stack(m_s), jnp.stack(mk_p), jnp.stack(mv_p))
```

```python
import functools

import jax
import jax.numpy as jnp
from jax import lax
from jax.experimental import pallas as pl
from jax.experimental.pallas import tpu as pltpu

F32 = jnp.float32
BF16 = jnp.bfloat16

D_MODEL = 2048
CONV_WIDTH = 512
CONV_K = 31
CONV_HIST = CONV_K - 1
MLSTM_HEADS = 4
MLSTM_HEAD_DIM = 256
MLSTM_WIDTH = MLSTM_HEADS * MLSTM_HEAD_DIM
MLSTM_CHUNK = 128
MEM_HEADS = 4
MEM_HEAD_DIM = 128
MEM_WIDTH = MEM_HEADS * MEM_HEAD_DIM
N_MEM = 256
N_GROUPS = 8
EXPERTS_PER_GROUP = 8
N_EXPERTS = N_GROUPS * EXPERTS_PER_GROUP
D_EXPERT = 512
LN_EPS = 1e-5
DEPTH = 1
ALPHA = (2 * DEPTH) ** 0.25

LANES = 128
SUBLANES = 8
VMEM_LIMIT_BYTES = 56 * 1024 * 1024

Z_CONV_A = 0
Z_CONV_G = CONV_WIDTH
Z_Q = 2 * CONV_WIDTH
Z_K = Z_Q + MLSTM_WIDTH
Z_V = Z_K + MLSTM_WIDTH
Z_O = Z_V + MLSTM_WIDTH
Z_QM = Z_O + MLSTM_WIDTH
Z_GATE = Z_QM + MEM_WIDTH
INPROJ_TN = 256
Z_COLS = Z_GATE + INPROJ_TN

SAMPLE_PAD = SUBLANES
ROUTER_LANES = LANES
MOE_ROWS = 256
OUT_TM = 256


def _params(*sem):
    return pltpu.CompilerParams(dimension_semantics=sem, vmem_limit_bytes=VMEM_LIMIT_BYTES)


def _inproj_kernel(x_ref, w_ref, b_ref, o_ref, xb_ref):
    @pl.when(pl.program_id(1) == 0)
    def _():
        xb_ref[...] = x_ref[...].astype(BF16)

    o_ref[...] = jnp.dot(xb_ref[...], w_ref[...], preferred_element_type=F32) + b_ref[...]


def _matmul_bias(x, w_bf16, b, tm, tn, name):
    t, k = x.shape
    n = w_bf16.shape[1]
    return pl.pallas_call(
        _inproj_kernel,
        out_shape=jax.ShapeDtypeStruct((t, n), F32),
        grid=(t // tm, n // tn),
        in_specs=[
            pl.BlockSpec((tm, k), lambda i, j: (i, 0)),
            pl.BlockSpec((k, tn), lambda i, j: (0, j)),
            pl.BlockSpec((1, tn), lambda i, j: (0, j)),
        ],
        out_specs=pl.BlockSpec((tm, tn), lambda i, j: (i, j)),
        scratch_shapes=[pltpu.VMEM((tm, k), BF16)],
        compiler_params=_params("arbitrary", "arbitrary"),
        name=name,
    )(x, w_bf16, b)


def _layer_norm_rows(y, g, b):
    mu = jnp.mean(y, axis=-1, keepdims=True)
    yc = y - mu
    var = jnp.mean(yc * yc, axis=-1, keepdims=True)
    return yc * lax.rsqrt(var + LN_EPS) * g + b


def _conv_kernel(a_ref, g_ref, hist_ref, wdw_ref, bdw_ref, gcn_ref, bcn_ref, out_ref, nb_ref, ubuf,
                 *, tl, valid):
    head = CONV_HIST + 2
    li = pl.program_id(1)

    @pl.when(li == 0)
    def _():
        ubuf[0:2, :] = jnp.zeros((2, CONV_WIDTH), F32)
        ubuf[2:head, :] = hist_ref[0]

    ubuf[head:head + tl, :] = a_ref[...] * jax.nn.sigmoid(g_ref[...])
    acc = jnp.zeros((tl, CONV_WIDTH), F32) + bdw_ref[...]
    for j in range(CONV_K):
        acc = acc + wdw_ref[j:j + 1, :] * ubuf[2 + j:2 + j + tl, :]
    y = _layer_norm_rows(acc, gcn_ref[...], bcn_ref[...])
    out_ref[...] = y * jax.nn.sigmoid(y)

    @pl.when(li == pl.num_programs(1) - 1)
    def _():
        nb_ref[0] = ubuf[2 + valid:head + valid, :]

    ubuf[2:head, :] = ubuf[2 + tl:head + tl, :]


def _conv_group(z, hist, w_dw, b_dw, g_cn, b_cn, batch, seq, tl, valid, name):
    nl = seq // tl
    row = lambda b, l: (b * nl + l, 0)
    vec = pl.BlockSpec((1, CONV_WIDTH), lambda b, l: (0, 0))
    return pl.pallas_call(
        functools.partial(_conv_kernel, tl=tl, valid=valid),
        out_shape=(jax.ShapeDtypeStruct((batch * seq, CONV_WIDTH), F32),
                   jax.ShapeDtypeStruct((batch, CONV_HIST, CONV_WIDTH), F32)),
        grid=(batch, nl),
        in_specs=[
            pl.BlockSpec((tl, CONV_WIDTH), lambda b, l: (b * nl + l, Z_CONV_A // CONV_WIDTH)),
            pl.BlockSpec((tl, CONV_WIDTH), lambda b, l: (b * nl + l, Z_CONV_G // CONV_WIDTH)),
            pl.BlockSpec((1, CONV_HIST, CONV_WIDTH), lambda b, l: (b, 0, 0)),
            pl.BlockSpec((CONV_K, CONV_WIDTH), lambda b, l: (0, 0)),
            vec, vec, vec,
        ],
        out_specs=(pl.BlockSpec((tl, CONV_WIDTH), row),
                   pl.BlockSpec((1, CONV_HIST, CONV_WIDTH), lambda b, l: (b, 0, 0))),
        scratch_shapes=[pltpu.VMEM((CONV_HIST + 2 + tl, CONV_WIDTH), F32)],
        compiler_params=_params("arbitrary", "arbitrary"),
        name=name,
    )(z, z, hist, w_dw, b_dw, g_cn, b_cn)


def _col_to_row(col, eye):
    n = col.shape[0]
    return jnp.sum(jnp.where(eye, jnp.broadcast_to(col, (n, n)), 0.0), axis=0, keepdims=True)


def _mlstm_kernel(q_ref, k_ref, v_ref, o_ref, gate_ref, c0_ref, n0_ref, m0_ref, gmh_ref,
                  h_ref, c_ref, n_ref, m_ref, *, cl, valid):
    ci = pl.program_id(1)

    @pl.when(ci == 0)
    def _():
        c_ref[...] = c0_ref[...]
        n_ref[...] = n0_ref[...]
        m_ref[...] = m0_ref[...]

    rows = lax.broadcasted_iota(jnp.int32, (cl, cl), 0)
    cols = lax.broadcasted_iota(jnp.int32, (cl, cl), 1)
    eye = rows == cols
    tril = rows >= cols
    row_id = lax.broadcasted_iota(jnp.int32, (cl, 1), 0)
    gates = gate_ref[...]
    dh = MLSTM_HEAD_DIM
    for h in range(MLSTM_HEADS):
        sl = slice(h * dh, (h + 1) * dh)
        q = q_ref[:, sl]
        k = k_ref[:, sl] * (dh ** -0.5)
        v = v_ref[:, sl]
        ip = gates[:, h:h + 1]
        fp = gates[:, MLSTM_HEADS + h:MLSTM_HEADS + h + 1]
        lf = jnp.minimum(fp, 0.0) - jnp.log1p(jnp.exp(-jnp.abs(fp)))
        if valid < cl:
            ip = jnp.where(row_id < valid, ip, -jnp.inf)
            lf = jnp.where(row_id < valid, lf, 0.0)
        lf_row = _col_to_row(lf, eye)
        ip_row = _col_to_row(ip, eye)
        b_col = jnp.sum(jnp.where(tril, jnp.broadcast_to(lf_row, (cl, cl)), 0.0), axis=1, keepdims=True)
        b_row = _col_to_row(b_col, eye)
        m_prev = m_ref[0, h:h + 1, 0:1]
        log_inter = b_col + m_prev
        log_intra = jnp.where(tril, b_col - b_row + ip_row, -jnp.inf)
        m_t = jnp.maximum(log_inter, jnp.max(log_intra, axis=1, keepdims=True))
        w_inter = jnp.exp(log_inter - m_t)
        qb = q.astype(BF16)
        kb = k.astype(BF16)
        vb = v.astype(BF16)
        s = lax.dot_general(qb, kb, (((1,), (1,)), ((), ())), preferred_element_type=F32)
        s = s * jnp.exp(log_intra - m_t)
        c_old = c_ref[0, h]
        n_old = n_ref[0, h:h + 1, :]
        num = (w_inter * jnp.dot(qb, c_old.astype(BF16), preferred_element_type=F32)
               + jnp.dot(s.astype(BF16), vb, preferred_element_type=F32))
        den = w_inter * jnp.sum(q * n_old, axis=1, keepdims=True) + jnp.sum(s, axis=1, keepdims=True)
        hh = num / jnp.maximum(jnp.abs(den), jnp.exp(-m_t))
        m_new = m_t[cl - 1:cl, :]
        b_last = b_col[cl - 1:cl, :]
        decay = jnp.exp(b_last + m_prev - m_new)
        w_s = jnp.exp(b_last - b_col + ip - m_new)
        kw = k * w_s
        c_ref[0, h] = decay * c_old + lax.dot_general(
            kw.astype(BF16), vb, (((0,), (0,)), ((), ())), preferred_element_type=F32)
        n_ref[0, h:h + 1, :] = decay * n_old + jnp.sum(kw, axis=0, keepdims=True)
        m_ref[0, h:h + 1, :] = jnp.broadcast_to(m_new, (1, LANES))
        mu = jnp.mean(hh, axis=-1, keepdims=True)
        hc = hh - mu
        var = jnp.mean(hc * hc, axis=-1, keepdims=True)
        hn = hc * lax.rsqrt(var + LN_EPS) * gmh_ref[:, sl]
        h_ref[:, sl] = hn * jax.nn.sigmoid(o_ref[:, sl])


def _mlstm_group(z, c0, n0, m0, g_mh, batch, seq, cl, valid, name):
    nc = seq // cl
    zcol = lambda off: pl.BlockSpec((cl, MLSTM_WIDTH), lambda b, c: (b * nc + c, off // MLSTM_WIDTH))
    state = lambda shape: pl.BlockSpec((1,) + shape, lambda b, c: (b,) + (0,) * len(shape))
    c_shape = (MLSTM_HEADS, MLSTM_HEAD_DIM, MLSTM_HEAD_DIM)
    n_shape = (MLSTM_HEADS, MLSTM_HEAD_DIM)
    m_shape = (SUBLANES, LANES)
    return pl.pallas_call(
        functools.partial(_mlstm_kernel, cl=cl, valid=valid),
        out_shape=(jax.ShapeDtypeStruct((batch * seq, MLSTM_WIDTH), F32),
                   jax.ShapeDtypeStruct((batch,) + c_shape, F32),
                   jax.ShapeDtypeStruct((batch,) + n_shape, F32),
                   jax.ShapeDtypeStruct((batch,) + m_shape, F32)),
        grid=(batch, nc),
        in_specs=[
            zcol(Z_Q), zcol(Z_K), zcol(Z_V), zcol(Z_O),
            pl.BlockSpec((cl, LANES), lambda b, c: (b * nc + c, Z_GATE // LANES)),
            state(c_shape), state(n_shape), state(m_shape),
            pl.BlockSpec((1, MLSTM_WIDTH), lambda b, c: (0, 0)),
        ],
        out_specs=(pl.BlockSpec((cl, MLSTM_WIDTH), lambda b, c: (b * nc + c, 0)),
                   state(c_shape), state(n_shape), state(m_shape)),
        compiler_params=_params("arbitrary", "arbitrary"),
        name=name,
    )(z, z, z, z, z, c0, n0, m0, g_mh)


def _memattn_kernel(q_ref, k_ref, v_ref, o_ref):
    dh = MEM_HEAD_DIM
    for h in range(MEM_HEADS):
        sl = slice(h * dh, (h + 1) * dh)
        q = q_ref[:, sl].astype(BF16)
        k = k_ref[0, :, h, :].astype(BF16)
        v = v_ref[0, :, h, :].astype(BF16)
        s = lax.dot_general(q, k, (((1,), (1,)), ((), ())), preferred_element_type=F32) * (dh ** -0.5)
        e = jnp.exp(s - jnp.max(s, axis=-1, keepdims=True))
        p = e / jnp.sum(e, axis=-1, keepdims=True)
        o_ref[:, sl] = jnp.dot(p.astype(BF16), v, preferred_element_type=F32)


def _memattn_group(z, mk, mv, batch, seq, tq, name):
    nq = seq // tq
    kv = pl.BlockSpec((1, N_MEM, MEM_HEADS, MEM_HEAD_DIM), lambda b, i: (b, 0, 0, 0))
    return pl.pallas_call(
        _memattn_kernel,
        out_shape=jax.ShapeDtypeStruct((batch * seq, MEM_WIDTH), F32),
        grid=(batch, nq),
        in_specs=[pl.BlockSpec((tq, MEM_WIDTH), lambda b, i: (b * nq + i, Z_QM // MEM_WIDTH)), kv, kv],
        out_specs=pl.BlockSpec((tq, MEM_WIDTH), lambda b, i: (b * nq + i, 0)),
        compiler_params=_params("arbitrary", "arbitrary"),
        name=name,
    )(z, mk, mv)


def _route(logits):
    lane = lax.broadcasted_iota(jnp.int32, logits.shape, 1).astype(F32)
    neg = -jnp.inf
    first = lambda mask: jnp.min(jnp.where(mask, lane, float(LANES)), axis=1, keepdims=True)
    is_g = lane < N_GROUPS
    gl = jnp.where(is_g, logits, neg)
    g_max = jnp.max(gl, axis=1, keepdims=True)
    g_sel = first(gl == g_max)
    g_w = 1.0 / jnp.sum(jnp.exp(gl - g_max), axis=1, keepdims=True)
    lo = N_GROUPS + g_sel * EXPERTS_PER_GROUP
    in_grp = (lane >= lo) & (lane < lo + EXPERTS_PER_GROUP)
    el = jnp.where(in_grp, logits, neg)
    v1 = jnp.max(el, axis=1, keepdims=True)
    i1 = first(in_grp & (el == v1))
    rest = in_grp & (lane != i1)
    el2 = jnp.where(rest, logits, neg)
    v2 = jnp.max(el2, axis=1, keepdims=True)
    i2 = first(rest & (el2 == v2))
    t = jnp.exp(v2 - v1)
    w1 = g_w / (1.0 + t)
    w2 = g_w * t / (1.0 + t)
    lane_i = lax.broadcasted_iota(jnp.int32, logits.shape, 1)
    e1 = (i1 - N_GROUPS).astype(jnp.int32)
    e2 = (i2 - N_GROUPS).astype(jnp.int32)
    eid = jnp.where(lane_i == 0, e1, jnp.where(lane_i == 1, e2, 0))
    ew = jnp.where(lane_i == 0, w1, jnp.where(lane_i == 1, w2, 0.0))
    return eid, ew


def _outproj_kernel(cp, hp, mp, xp, cs, hs, ms, xs, wout, g1, b1, wrh, wrl, br,
                    x1_ref, eid_ref, ew_ref, *, n_prompt):
    def body(c, h, m, x):
        c0, c1, c2 = CONV_WIDTH, CONV_WIDTH + MLSTM_WIDTH, D_MODEL
        mix = (jnp.dot(c[...].astype(BF16), wout[0:c0, :], preferred_element_type=F32)
               + jnp.dot(h[...].astype(BF16), wout[c0:c1, :], preferred_element_type=F32)
               + jnp.dot(m[...].astype(BF16), wout[c1:c2, :], preferred_element_type=F32))
        x1 = _layer_norm_rows(ALPHA * x[...] + mix, g1[...], b1[...])
        x1_ref[...] = x1
        xh = x1.astype(BF16)
        xl = (x1 - xh.astype(F32)).astype(BF16)
        logits = (jnp.dot(xh, wrh[...], preferred_element_type=F32)
                  + jnp.dot(xh, wrl[...], preferred_element_type=F32)
                  + jnp.dot(xl, wrh[...], preferred_element_type=F32)) + br[...]
        eid, ew = _route(logits)
        eid_ref[...] = eid
        ew_ref[...] = ew

    i = pl.program_id(0)
    pl.when(i < n_prompt)(lambda: body(cp, hp, mp, xp))
    pl.when(i >= n_prompt)(lambda: body(cs, hs, ms, xs))


def _outproj_router(prompt, sample, w_out_b, g1, b1, wrh, wrl, br):
    tm = OUT_TM
    tp = prompt[0].shape[0]
    ts = sample[0].shape[0]
    n_p, n_s = tp // tm, ts // tm
    total = tp + ts
    widths = (CONV_WIDTH, MLSTM_WIDTH, MEM_WIDTH, D_MODEL)
    p_specs = [pl.BlockSpec((tm, w), lambda i: (jnp.minimum(i, n_p - 1), 0)) for w in widths]
    s_specs = [pl.BlockSpec((tm, w), lambda i: (jnp.maximum(i - n_p, 0), 0)) for w in widths]
    full = lambda shape: pl.BlockSpec(shape, lambda i: (0, 0))
    row = lambda w: pl.BlockSpec((tm, w), lambda i: (i, 0))
    return pl.pallas_call(
        functools.partial(_outproj_kernel, n_prompt=n_p),
        out_shape=(jax.ShapeDtypeStruct((total, D_MODEL), F32),
                   jax.ShapeDtypeStruct((total, ROUTER_LANES), jnp.int32),
                   jax.ShapeDtypeStruct((total, ROUTER_LANES), F32)),
        grid=(n_p + n_s,),
        in_specs=p_specs + s_specs + [
            full((D_MODEL, D_MODEL)), full((1, D_MODEL)), full((1, D_MODEL)),
            full((D_MODEL, ROUTER_LANES)), full((D_MODEL, ROUTER_LANES)), full((1, ROUTER_LANES)),
        ],
        out_specs=(row(D_MODEL), row(ROUTER_LANES), row(ROUTER_LANES)),
        compiler_params=_params("arbitrary"),
        name="outproj_router",
    )(*prompt, *sample, w_out_b, g1, b1, wrh, wrl, br)


def _moe_kernel(stok_ref, sdst_ref, start_ref, cnt_ref, x_hbm, wg_ref, wu_ref, wd_ref, y_hbm,
                wgb, wub, wdb, xbuf, ybuf, gsem, ssem):
    e = pl.program_id(0)
    wgb[...] = wg_ref[0].astype(BF16)
    wub[...] = wu_ref[0].astype(BF16)
    wdb[...] = wd_ref[0].astype(BF16)
    start = start_ref[e]
    cnt = cnt_ref[e]
    rc = MOE_ROWS

    def gather_copy(tok, r):
        return pltpu.make_async_copy(x_hbm.at[pl.ds(tok, 1)], xbuf.at[pl.ds(r, 1)], gsem)

    def scatter_copy(r, dst):
        return pltpu.make_async_copy(ybuf.at[pl.ds(r, 1)], y_hbm.at[pl.ds(dst, 1)], ssem)

    def chunk(c, carry):
        base = start + c * rc
        n_valid = cnt - c * rc

        def g_start(r, _):
            gather_copy(stok_ref[base + r], r).start()
            return 0

        def g_wait(r, _):
            gather_copy(0, r).wait()
            return 0

        lax.fori_loop(0, rc, g_start, 0, unroll=8)
        lax.fori_loop(0, rc, g_wait, 0, unroll=8)
        x = xbuf[...].astype(BF16)
        hg = jnp.dot(x, wgb[...], preferred_element_type=F32)
        hu = jnp.dot(x, wub[...], preferred_element_type=F32)
        hid = (hg * jax.nn.sigmoid(hg) * hu).astype(BF16)
        ybuf[...] = jnp.dot(hid, wdb[...], preferred_element_type=F32)

        def s_start(r, _):
            @pl.when(r < n_valid)
            def _():
                scatter_copy(r, sdst_ref[base + r]).start()
            return 0

        def s_wait(r, _):
            @pl.when(r < n_valid)
            def _():
                scatter_copy(r, 0).wait()
            return 0

        lax.fori_loop(0, rc, s_start, 0, unroll=8)
        lax.fori_loop(0, rc, s_wait, 0, unroll=8)
        return carry

    lax.fori_loop(0, (cnt + rc - 1) // rc, chunk, 0)


def _moe(x1, stok, sdst, start, cnt, w_gate, w_up, w_down):
    total = x1.shape[0]
    wspec = lambda shape: pl.BlockSpec((1,) + shape, lambda e, *_: (e, 0, 0))
    grid_spec = pltpu.PrefetchScalarGridSpec(
        num_scalar_prefetch=4,
        grid=(N_EXPERTS,),
        in_specs=[pl.BlockSpec(memory_space=pl.ANY),
                  wspec((D_MODEL, D_EXPERT)), wspec((D_MODEL, D_EXPERT)), wspec((D_EXPERT, D_MODEL))],
        out_specs=pl.BlockSpec(memory_space=pl.ANY),
        scratch_shapes=[
            pltpu.VMEM((D_MODEL, D_EXPERT), BF16), pltpu.VMEM((D_MODEL, D_EXPERT), BF16),
            pltpu.VMEM((D_EXPERT, D_MODEL), BF16),
            pltpu.VMEM((MOE_ROWS, D_MODEL), F32), pltpu.VMEM((MOE_ROWS, D_MODEL), F32),
            pltpu.SemaphoreType.DMA, pltpu.SemaphoreType.DMA,
        ],
    )
    return pl.pallas_call(
        _moe_kernel,
        out_shape=jax.ShapeDtypeStruct((2 * total, D_MODEL), F32),
        grid_spec=grid_spec,
        compiler_params=_params("arbitrary"),
        name="moe_experts",
    )(stok, sdst, start, cnt, x1, w_gate, w_up, w_down)


def _final_kernel(x1_ref, y0_ref, y1_ref, ew_ref, g2, b2, o_ref):
    ew = ew_ref[...]
    ffn = ew[:, 0:1] * y0_ref[...] + ew[:, 1:2] * y1_ref[...]
    o_ref[...] = _layer_norm_rows(ALPHA * x1_ref[...] + ffn, g2[...], b2[...])


def _final(x1, y, ew, g2, b2, row0, rows, name):
    tm = OUT_TM
    total = x1.shape[0]
    off = row0 // tm
    k1 = total // tm
    vec = pl.BlockSpec((1, D_MODEL), lambda i: (0, 0))
    return pl.pallas_call(
        _final_kernel,
        out_shape=jax.ShapeDtypeStruct((rows, D_MODEL), F32),
        grid=(rows // tm,),
        in_specs=[pl.BlockSpec((tm, D_MODEL), lambda i: (i + off, 0)),
                  pl.BlockSpec((tm, D_MODEL), lambda i: (i + off, 0)),
                  pl.BlockSpec((tm, D_MODEL), lambda i: (i + off + k1, 0)),
                  pl.BlockSpec((tm, ROUTER_LANES), lambda i: (i + off, 0)),
                  vec, vec],
        out_specs=pl.BlockSpec((tm, D_MODEL), lambda i: (i, 0)),
        compiler_params=_params("arbitrary"),
        name=name,
    )(x1, y, y, ew, g2, b2)


def _dispatch_tables(eid, total):
    flat_e = eid[:, :2].reshape(-1)
    order = jnp.argsort(flat_e, stable=True).astype(jnp.int32)
    sorted_e = flat_e[order]
    stok = order >> 1
    sdst = (order & 1) * total + stok
    experts = jnp.arange(N_EXPERTS, dtype=jnp.int32)
    start = jnp.searchsorted(sorted_e, experts, side="left").astype(jnp.int32)
    end = jnp.searchsorted(sorted_e, experts, side="right").astype(jnp.int32)
    pad = jnp.zeros((MOE_ROWS,), jnp.int32)
    return jnp.concatenate([stok, pad]), jnp.concatenate([sdst, pad]), start, end - start


def kernel(x_prompt, x_sample, mem_prompt, state_conv, state_mlstm_C, state_mlstm_n, state_mlstm_m,
           cache_mem_k, cache_mem_v, w_in, b_in, w_dw, b_dw, g_cn, b_cn, g_mh, w_mk, w_mv, w_out,
           g_ln1, b_ln1, w_rg, b_rg, w_re, b_re, w_gate, w_up, w_down, g_ln2, b_ln2):
    bp, sp, _ = x_prompt.shape
    bs, ss, _ = x_sample.shape
    tp, ts = bp * sp, bs * ss

    gate_lo = Z_QM
    gate_hi = gate_lo + 2 * MLSTM_HEADS
    reorder = lambda w: jnp.concatenate(
        [w[..., :gate_lo], w[..., gate_hi:], w[..., gate_lo:gate_hi],
         jnp.zeros(w.shape[:-1] + (Z_COLS - w.shape[-1],), w.dtype)], axis=-1)
    w_in_b = reorder(w_in[0]).astype(BF16)
    b_in_r = reorder(b_in[0])[None, :]
    w_kv_b = jnp.concatenate([w_mk[0], w_mv[0]], axis=-1).astype(BF16)
    w_out_b = w_out[0].astype(BF16)
    w_r = jnp.concatenate([w_rg[0], w_re[0],
                           jnp.zeros((D_MODEL, ROUTER_LANES - N_GROUPS - N_EXPERTS), F32)], axis=-1)
    w_r_hi = w_r.astype(BF16)
    w_r_lo = (w_r - w_r_hi.astype(F32)).astype(BF16)
    b_r = jnp.concatenate([b_rg[0], b_re[0],
                           jnp.zeros((ROUTER_LANES - N_GROUPS - N_EXPERTS,), F32)])[None, :]
    row = lambda a: a[0][None, :]

    xs_pad = jnp.pad(x_sample, ((0, 0), (0, SAMPLE_PAD - ss), (0, 0))).reshape(bs * SAMPLE_PAD, D_MODEL)
    z_p = _matmul_bias(x_prompt.reshape(tp, D_MODEL), w_in_b, b_in_r, 1024, INPROJ_TN, "inproj_prompt")
    z_s = _matmul_bias(xs_pad, w_in_b, b_in_r, 512, INPROJ_TN, "inproj_sample")

    kv = _matmul_bias(mem_prompt.reshape(bp * N_MEM, D_MODEL), w_kv_b,
                      jnp.zeros((1, 2 * MEM_WIDTH), F32), bp * N_MEM, INPROJ_TN, "mem_kv")
    mk_p = kv[:, :MEM_WIDTH].reshape(bp, N_MEM, MEM_HEADS, MEM_HEAD_DIM)
    mv_p = kv[:, MEM_WIDTH:].reshape(bp, N_MEM, MEM_HEADS, MEM_HEAD_DIM)

    conv_args = (w_dw[0], row(b_dw), row(g_cn), row(b_cn))
    conv_p, buf_p = _conv_group(z_p, jnp.zeros((bp, CONV_HIST, CONV_WIDTH), F32), *conv_args,
                                bp, sp, 256, 256, "conv_prompt")
    conv_s, buf_s = _conv_group(z_s, state_conv[0], *conv_args,
                                bs, SAMPLE_PAD, SAMPLE_PAD, ss, "conv_sample")

    m_tile = lambda m: jnp.broadcast_to(
        jnp.pad(m, ((0, 0), (0, SUBLANES - MLSTM_HEADS)))[:, :, None], (m.shape[0], SUBLANES, LANES))
    g_mh_r = row(g_mh)
    h_p, c_p, n_p, m_p = _mlstm_group(
        z_p, jnp.zeros((bp, MLSTM_HEADS, MLSTM_HEAD_DIM, MLSTM_HEAD_DIM), F32),
        jnp.zeros((bp, MLSTM_HEADS, MLSTM_HEAD_DIM), F32), jnp.zeros((bp, SUBLANES, LANES), F32),
        g_mh_r, bp, sp, MLSTM_CHUNK, MLSTM_CHUNK, "mlstm_prompt")
    h_s, c_s, n_s, m_s = _mlstm_group(
        z_s, state_mlstm_C[0], state_mlstm_n[0], m_tile(state_mlstm_m[0]),
        g_mh_r, bs, SAMPLE_PAD, SAMPLE_PAD, ss, "mlstm_sample")

    mem_p = _memattn_group(z_p, mk_p, mv_p, bp, sp, 512, "memattn_prompt")
    mem_s = _memattn_group(z_s, cache_mem_k[0], cache_mem_v[0], bs, SAMPLE_PAD, SAMPLE_PAD, "memattn_sample")

    compact = lambda a: a.reshape(bs, SAMPLE_PAD, a.shape[-1])[:, :ss].reshape(ts, a.shape[-1])
    x1, eid, ew = _outproj_router(
        (conv_p, h_p, mem_p, x_prompt.reshape(tp, D_MODEL)),
        (compact(conv_s), compact(h_s), compact(mem_s), x_sample.reshape(ts, D_MODEL)),
        w_out_b, row(g_ln1), row(b_ln1), w_r_hi, w_r_lo, b_r)

    total = tp + ts
    stok, sdst, start, cnt = _dispatch_tables(eid, total)
    y = _moe(x1, stok, sdst, start, cnt, w_gate[0], w_up[0], w_down[0])
    g2, b2 = row(g_ln2), row(b_ln2)
    y_p = _final(x1, y, ew, g2, b2, 0, tp, "final_prompt").reshape(bp, sp, D_MODEL)
    y_s = _final(x1, y, ew, g2, b2, tp, ts, "final_sample").reshape(bs, ss, D_MODEL)

    return (y_p, y_s, buf_p[None], buf_s[None], c_p[None], c_s[None], n_p[None], n_s[None],
            m_p[:, :MLSTM_HEADS, 0][None], m_s[:, :MLSTM_HEADS, 0][None], mk_p[None], mv_p[None])
```

```python
import functools

import jax
import jax.numpy as jnp
from jax import lax
from jax.experimental import pallas as pl
from jax.experimental.pallas import tpu as pltpu

F32 = jnp.float32
BF16 = jnp.bfloat16

D_MODEL = 2048
CONV_WIDTH = 512
CONV_K = 31
CONV_HIST = CONV_K - 1
MLSTM_HEADS = 4
MLSTM_HEAD_DIM = 256
MLSTM_WIDTH = MLSTM_HEADS * MLSTM_HEAD_DIM
MLSTM_CHUNK = 128
MEM_HEADS = 4
MEM_HEAD_DIM = 128
MEM_WIDTH = MEM_HEADS * MEM_HEAD_DIM
N_MEM = 256
N_GROUPS = 8
EXPERTS_PER_GROUP = 8
N_EXPERTS = N_GROUPS * EXPERTS_PER_GROUP
D_EXPERT = 512
LN_EPS = 1e-5
DEPTH = 1
ALPHA = (2 * DEPTH) ** 0.25

LANES = 128
SUBLANES = 8
VMEM_LIMIT_BYTES = 56 * 1024 * 1024

Z_CONV_A = 0
Z_CONV_G = CONV_WIDTH
Z_Q = 2 * CONV_WIDTH
Z_K = Z_Q + MLSTM_WIDTH
Z_V = Z_K + MLSTM_WIDTH
Z_O = Z_V + MLSTM_WIDTH
Z_MAIN_COLS = Z_O + MLSTM_WIDTH
N_GATE_COLS = 2 * MLSTM_HEADS
ZT_QM = 0
ZT_GATE = MEM_WIDTH
Z_TAIL_COLS = MEM_WIDTH + LANES
INPROJ_TM = 1024
INPROJ_TN = 512
KV_TN = 256

SAMPLE_PAD = SUBLANES
ROUTER_LANES = LANES
MOE_ROWS = 256
OUT_TM = 256


def _params(*sem):
    return pltpu.CompilerParams(dimension_semantics=sem, vmem_limit_bytes=VMEM_LIMIT_BYTES)


def _inproj_kernel(x_ref, wm_ref, bm_ref, wt_ref, bt_ref, zm_ref, zt_ref, xb_ref, *, n_main):
    j = pl.program_id(1)

    @pl.when(j == 0)
    def _():
        xb_ref[...] = x_ref[...].astype(BF16)

    @pl.when(j < n_main)
    def _():
        zm_ref[...] = jnp.dot(xb_ref[...], wm_ref[...].astype(BF16),
                              preferred_element_type=F32) + bm_ref[...]

    @pl.when(j == n_main)
    def _():
        zt_ref[...] = jnp.dot(xb_ref[...], wt_ref[...], preferred_element_type=F32) + bt_ref[...]


def _inproj(x, w_in, b_in, w_tail_b, b_tail, name):
    t, k = x.shape
    tm, tn = INPROJ_TM, INPROJ_TN
    n_main = Z_MAIN_COLS // tn
    main_col = lambda j: jnp.minimum(j, n_main - 1)
    return pl.pallas_call(
        functools.partial(_inproj_kernel, n_main=n_main),
        out_shape=(jax.ShapeDtypeStruct((t, Z_MAIN_COLS), F32),
                   jax.ShapeDtypeStruct((t, Z_TAIL_COLS), F32)),
        grid=(t // tm, n_main + 1),
        in_specs=[
            pl.BlockSpec((tm, k), lambda i, j: (i, 0)),
            pl.BlockSpec((None, k, tn), lambda i, j: (0, 0, main_col(j))),
            pl.BlockSpec((1, tn), lambda i, j: (0, main_col(j))),
            pl.BlockSpec((k, Z_TAIL_COLS), lambda i, j: (0, 0)),
            pl.BlockSpec((1, Z_TAIL_COLS), lambda i, j: (0, 0)),
        ],
        out_specs=(pl.BlockSpec((tm, tn), lambda i, j: (i, main_col(j))),
                   pl.BlockSpec((tm, Z_TAIL_COLS), lambda i, j: (i, 0))),
        scratch_shapes=[pltpu.VMEM((tm, k), BF16)],
        compiler_params=_params("arbitrary", "arbitrary"),
        name=name,
    )(x, w_in, b_in, w_tail_b, b_tail)


def _matmul_kernel(x_ref, w_ref, o_ref):
    o_ref[...] = jnp.dot(x_ref[...].astype(BF16), w_ref[...], preferred_element_type=F32)


def _matmul(x, w_bf16, tn, name):
    t, k = x.shape
    n = w_bf16.shape[1]
    return pl.pallas_call(
        _matmul_kernel,
        out_shape=jax.ShapeDtypeStruct((t, n), F32),
        grid=(n // tn,),
        in_specs=[pl.BlockSpec((t, k), lambda j: (0, 0)), pl.BlockSpec((k, tn), lambda j: (0, j))],
        out_specs=pl.BlockSpec((t, tn), lambda j: (0, j)),
        compiler_params=_params("arbitrary"),
        name=name,
    )(x, w_bf16)


def _layer_norm_rows(y, g, b):
    mu = jnp.mean(y, axis=-1, keepdims=True)
    yc = y - mu
    var = jnp.mean(yc * yc, axis=-1, keepdims=True)
    return yc * lax.rsqrt(var + LN_EPS) * g + b


def _conv_kernel(a_ref, g_ref, hist_ref, wdw_ref, bdw_ref, gcn_ref, bcn_ref, out_ref, nb_ref, ubuf,
                 *, tl):
    head = CONV_HIST + 2
    li = pl.program_id(1)

    @pl.when(li == 0)
    def _():
        ubuf[0:2, :] = jnp.zeros((2, CONV_WIDTH), F32)
        ubuf[2:head, :] = hist_ref[0]

    ubuf[head:head + tl, :] = a_ref[...] * jax.nn.sigmoid(g_ref[...])
    acc = jnp.zeros((tl, CONV_WIDTH), F32) + bdw_ref[...]
    for j in range(CONV_K):
        acc = acc + wdw_ref[j:j + 1, :] * ubuf[2 + j:2 + j + tl, :]
    y = _layer_norm_rows(acc, gcn_ref[...], bcn_ref[...])
    out_ref[...] = y * jax.nn.sigmoid(y)

    @pl.when(li == pl.num_programs(1) - 1)
    def _():
        nb_ref[0] = ubuf[2 + tl:head + tl, :]

    ubuf[2:head, :] = ubuf[2 + tl:head + tl, :]


def _conv_group(z, hist, w_dw, b_dw, g_cn, b_cn, batch, seq, tl, name):
    nl = seq // tl
    row = lambda b, l: (b * nl + l, 0)
    vec = pl.BlockSpec((1, CONV_WIDTH), lambda b, l: (0, 0))
    return pl.pallas_call(
        functools.partial(_conv_kernel, tl=tl),
        out_shape=(jax.ShapeDtypeStruct((batch * seq, CONV_WIDTH), F32),
                   jax.ShapeDtypeStruct((batch, CONV_HIST, CONV_WIDTH), F32)),
        grid=(batch, nl),
        in_specs=[
            pl.BlockSpec((tl, CONV_WIDTH), lambda b, l: (b * nl + l, Z_CONV_A // CONV_WIDTH)),
            pl.BlockSpec((tl, CONV_WIDTH), lambda b, l: (b * nl + l, Z_CONV_G // CONV_WIDTH)),
            pl.BlockSpec((1, CONV_HIST, CONV_WIDTH), lambda b, l: (b, 0, 0)),
            pl.BlockSpec((CONV_K, CONV_WIDTH), lambda b, l: (0, 0)),
            vec, vec, vec,
        ],
        out_specs=(pl.BlockSpec((tl, CONV_WIDTH), row),
                   pl.BlockSpec((1, CONV_HIST, CONV_WIDTH), lambda b, l: (b, 0, 0))),
        scratch_shapes=[pltpu.VMEM((CONV_HIST + 2 + tl, CONV_WIDTH), F32)],
        compiler_params=_params("arbitrary", "arbitrary"),
        name=name,
    )(z, z, hist, w_dw, b_dw, g_cn, b_cn)


def _conv_step_kernel(a_ref, g_ref, hist_ref, wdw_ref, bdw_ref, gcn_ref, bcn_ref, out_ref, nb_ref,
                      *, steps):
    u = [a_ref[:, t, :] * jax.nn.sigmoid(g_ref[:, t, :]) for t in range(steps)]
    full = lambda r: hist_ref[r] if r < CONV_HIST else u[r - CONV_HIST]
    for t in range(steps):
        acc = bdw_ref[...] + wdw_ref[0:1, :] * full(t)
        for j in range(1, CONV_K):
            acc = acc + wdw_ref[j:j + 1, :] * full(t + j)
        y = _layer_norm_rows(acc, gcn_ref[...], bcn_ref[...])
        out_ref[:, t, :] = y * jax.nn.sigmoid(y)
    for r in range(CONV_HIST):
        nb_ref[r] = full(r + steps)


def _conv_step_group(z3, hist_t, w_dw, b_dw, g_cn, b_cn, steps, bb, name):
    batch = z3.shape[0]
    vec = pl.BlockSpec((1, CONV_WIDTH), lambda b: (0, 0))
    hist_spec = pl.BlockSpec((CONV_HIST, bb, CONV_WIDTH), lambda b: (0, b, 0))
    return pl.pallas_call(
        functools.partial(_conv_step_kernel, steps=steps),
        out_shape=(jax.ShapeDtypeStruct((batch, steps, CONV_WIDTH), F32),
                   jax.ShapeDtypeStruct((CONV_HIST, batch, CONV_WIDTH), F32)),
        grid=(batch // bb,),
        in_specs=[
            pl.BlockSpec((bb, SAMPLE_PAD, CONV_WIDTH), lambda b: (b, 0, Z_CONV_A // CONV_WIDTH)),
            pl.BlockSpec((bb, SAMPLE_PAD, CONV_WIDTH), lambda b: (b, 0, Z_CONV_G // CONV_WIDTH)),
            hist_spec,
            pl.BlockSpec((CONV_K, CONV_WIDTH), lambda b: (0, 0)),
            vec, vec, vec,
        ],
        out_specs=(pl.BlockSpec((bb, steps, CONV_WIDTH), lambda b: (b, 0, 0)), hist_spec),
        compiler_params=_params("arbitrary"),
        name=name,
    )(z3, z3, hist_t, w_dw, b_dw, g_cn, b_cn)


def _col_to_row(col, eye):
    n = col.shape[0]
    return jnp.sum(jnp.where(eye, jnp.broadcast_to(col, (n, n)), 0.0), axis=0, keepdims=True)


def _mlstm_kernel(q_ref, k_ref, v_ref, o_ref, gate_ref, c0_ref, n0_ref, m0_ref, gmh_ref,
                  h_ref, c_ref, n_ref, m_ref, *, cl, valid, bb, single_chunk):
    if single_chunk:
        c_in, n_in, m_in = c0_ref, n0_ref, m0_ref
    else:
        c_in, n_in, m_in = c_ref, n_ref, m_ref

        @pl.when(pl.program_id(1) == 0)
        def _():
            c_ref[...] = c0_ref[...]
            n_ref[...] = n0_ref[...]
            m_ref[...] = m0_ref[...]

    rows = lax.broadcasted_iota(jnp.int32, (cl, cl), 0)
    cols = lax.broadcasted_iota(jnp.int32, (cl, cl), 1)
    eye = rows == cols
    tril = rows >= cols
    row_id = lax.broadcasted_iota(jnp.int32, (cl, 1), 0)
    dh = MLSTM_HEAD_DIM
    for bi, h in [(bi, h) for bi in range(bb) for h in range(MLSTM_HEADS)]:
        rs = slice(bi * cl, (bi + 1) * cl)
        sl = slice(h * dh, (h + 1) * dh)
        gates = gate_ref[rs, :]
        q = q_ref[rs, sl]
        k = k_ref[rs, sl] * (dh ** -0.5)
        v = v_ref[rs, sl]
        ip = gates[:, h:h + 1]
        fp = gates[:, MLSTM_HEADS + h:MLSTM_HEADS + h + 1]
        lf = jnp.minimum(fp, 0.0) - jnp.log1p(jnp.exp(-jnp.abs(fp)))
        if valid < cl:
            ip = jnp.where(row_id < valid, ip, -jnp.inf)
            lf = jnp.where(row_id < valid, lf, 0.0)
        lf_row = _col_to_row(lf, eye)
        ip_row = _col_to_row(ip, eye)
        b_col = jnp.sum(jnp.where(tril, jnp.broadcast_to(lf_row, (cl, cl)), 0.0), axis=1, keepdims=True)
        b_row = _col_to_row(b_col, eye)
        m_prev = m_in[bi, h:h + 1, 0:1]
        log_inter = b_col + m_prev
        log_intra = jnp.where(tril, b_col - b_row + ip_row, -jnp.inf)
        m_t = jnp.maximum(log_inter, jnp.max(log_intra, axis=1, keepdims=True))
        w_inter = jnp.exp(log_inter - m_t)
        qb = q.astype(BF16)
        kb = k.astype(BF16)
        vb = v.astype(BF16)
        s = lax.dot_general(qb, kb, (((1,), (1,)), ((), ())), preferred_element_type=F32)
        s = s * jnp.exp(log_intra - m_t)
        c_old = c_in[bi, h]
        n_old = n_in[bi, h:h + 1, :]
        num = (w_inter * jnp.dot(qb, c_old.astype(BF16), preferred_element_type=F32)
               + jnp.dot(s.astype(BF16), vb, preferred_element_type=F32))
        den = w_inter * jnp.sum(q * n_old, axis=1, keepdims=True) + jnp.sum(s, axis=1, keepdims=True)
        hh = num / jnp.maximum(jnp.abs(den), jnp.exp(-m_t))
        m_new = m_t[cl - 1:cl, :]
        b_last = b_col[cl - 1:cl, :]
        decay = jnp.exp(b_last + m_prev - m_new)
        w_s = jnp.exp(b_last - b_col + ip - m_new)
        kw = k * w_s
        c_ref[bi, h] = decay * c_old + lax.dot_general(
            kw.astype(BF16), vb, (((0,), (0,)), ((), ())), preferred_element_type=F32)
        n_ref[bi, h:h + 1, :] = decay * n_old + jnp.sum(kw, axis=0, keepdims=True)
        m_ref[bi, h:h + 1, :] = jnp.broadcast_to(m_new, (1, LANES))
        mu = jnp.mean(hh, axis=-1, keepdims=True)
        hc = hh - mu
        var = jnp.mean(hc * hc, axis=-1, keepdims=True)
        hn = hc * lax.rsqrt(var + LN_EPS) * gmh_ref[:, sl]
        h_ref[rs, sl] = hn * jax.nn.sigmoid(o_ref[rs, sl])
    if single_chunk:
        m_ref[:, MLSTM_HEADS:, :] = jnp.zeros((bb, SUBLANES - MLSTM_HEADS, LANES), F32)


def _mlstm_group(z_main, z_tail, c0, n0, m0, g_mh, batch, seq, cl, valid, bb, name):
    nc = seq // cl
    assert bb == 1 or nc == 1, "several sequences per step only for single-chunk sequences"
    rows = bb * cl
    zcol = lambda off: pl.BlockSpec((rows, MLSTM_WIDTH), lambda b, c: (b * nc + c, off // MLSTM_WIDTH))
    state = lambda shape: pl.BlockSpec((bb,) + shape, lambda b, c: (b,) + (0,) * len(shape))
    c_shape = (MLSTM_HEADS, MLSTM_HEAD_DIM, MLSTM_HEAD_DIM)
    n_shape = (MLSTM_HEADS, MLSTM_HEAD_DIM)
    m_shape = (SUBLANES, LANES)
    return pl.pallas_call(
        functools.partial(_mlstm_kernel, cl=cl, valid=valid, bb=bb, single_chunk=nc == 1),
        out_shape=(jax.ShapeDtypeStruct((batch * seq, MLSTM_WIDTH), F32),
                   jax.ShapeDtypeStruct((batch,) + c_shape, F32),
                   jax.ShapeDtypeStruct((batch,) + n_shape, F32),
                   jax.ShapeDtypeStruct((batch,) + m_shape, F32)),
        grid=(batch // bb, nc),
        in_specs=[
            zcol(Z_Q), zcol(Z_K), zcol(Z_V), zcol(Z_O),
            pl.BlockSpec((rows, LANES), lambda b, c: (b * nc + c, ZT_GATE // LANES)),
            state(c_shape), state(n_shape), state(m_shape),
            pl.BlockSpec((1, MLSTM_WIDTH), lambda b, c: (0, 0)),
        ],
        out_specs=(pl.BlockSpec((rows, MLSTM_WIDTH), lambda b, c: (b * nc + c, 0)),
                   state(c_shape), state(n_shape), state(m_shape)),
        compiler_params=_params("arbitrary", "arbitrary"),
        name=name,
    )(z_main, z_main, z_main, z_main, z_tail, c0, n0, m0, g_mh)


def _memattn_kernel(q_ref, k_ref, v_ref, o_ref, *, tq, bb):
    dh = MEM_HEAD_DIM
    for bi, h in [(bi, h) for bi in range(bb) for h in range(MEM_HEADS)]:
        rs = slice(bi * tq, (bi + 1) * tq)
        sl = slice(h * dh, (h + 1) * dh)
        q = q_ref[rs, sl].astype(BF16)
        k = k_ref[bi, :, h, :].astype(BF16)
        v = v_ref[bi, :, h, :].astype(BF16)
        s = lax.dot_general(q, k, (((1,), (1,)), ((), ())), preferred_element_type=F32) * (dh ** -0.5)
        e = jnp.exp(s - jnp.max(s, axis=-1, keepdims=True))
        p = e / jnp.sum(e, axis=-1, keepdims=True)
        o_ref[rs, sl] = jnp.dot(p.astype(BF16), v, preferred_element_type=F32)


def _memattn_group(z_tail, mk, mv, batch, seq, tq, bb, name):
    nq = seq // tq
    assert bb == 1 or nq == 1, "several batch elements per step only when one tile covers the sequence"
    rows = bb * tq
    kv = pl.BlockSpec((bb, N_MEM, MEM_HEADS, MEM_HEAD_DIM), lambda b, i: (b, 0, 0, 0))
    return pl.pallas_call(
        functools.partial(_memattn_kernel, tq=tq, bb=bb),
        out_shape=jax.ShapeDtypeStruct((batch * seq, MEM_WIDTH), F32),
        grid=(batch // bb, nq),
        in_specs=[pl.BlockSpec((rows, MEM_WIDTH), lambda b, i: (b * nq + i, ZT_QM // MEM_WIDTH)), kv, kv],
        out_specs=pl.BlockSpec((rows, MEM_WIDTH), lambda b, i: (b * nq + i, 0)),
        compiler_params=_params("arbitrary", "arbitrary"),
        name=name,
    )(z_tail, mk, mv)


def _route(logits):
    lane = lax.broadcasted_iota(jnp.int32, logits.shape, 1).astype(F32)
    neg = -jnp.inf
    first = lambda mask: jnp.min(jnp.where(mask, lane, float(LANES)), axis=1, keepdims=True)
    is_g = lane < N_GROUPS
    gl = jnp.where(is_g, logits, neg)
    g_max = jnp.max(gl, axis=1, keepdims=True)
    g_sel = first(gl == g_max)
    g_w = 1.0 / jnp.sum(jnp.exp(gl - g_max), axis=1, keepdims=True)
    lo = N_GROUPS + g_sel * EXPERTS_PER_GROUP
    in_grp = (lane >= lo) & (lane < lo + EXPERTS_PER_GROUP)
    el = jnp.where(in_grp, logits, neg)
    v1 = jnp.max(el, axis=1, keepdims=True)
    i1 = first(in_grp & (el == v1))
    rest = in_grp & (lane != i1)
    el2 = jnp.where(rest, logits, neg)
    v2 = jnp.max(el2, axis=1, keepdims=True)
    i2 = first(rest & (el2 == v2))
    t = jnp.exp(v2 - v1)
    w1 = g_w / (1.0 + t)
    w2 = g_w * t / (1.0 + t)
    lane_i = lax.broadcasted_iota(jnp.int32, logits.shape, 1)
    e1 = (i1 - N_GROUPS).astype(jnp.int32)
    e2 = (i2 - N_GROUPS).astype(jnp.int32)
    eid = jnp.where(lane_i == 0, e1, jnp.where(lane_i == 1, e2, 0))
    ew = jnp.where(lane_i == 0, w1, jnp.where(lane_i == 1, w2, 0.0))
    return eid, ew


def _outproj_kernel(cp, hp, mp, xp, cs, hs, ms, xs, wout, g1, b1, wrh, wrl, br,
                    x1_ref, eid_ref, ew_ref, *, n_prompt):
    def body(c, h, m, x):
        c0, c1, c2 = CONV_WIDTH, CONV_WIDTH + MLSTM_WIDTH, D_MODEL
        mix = (jnp.dot(c[...].astype(BF16), wout[0:c0, :], preferred_element_type=F32)
               + jnp.dot(h[...].astype(BF16), wout[c0:c1, :], preferred_element_type=F32)
               + jnp.dot(m[...].astype(BF16), wout[c1:c2, :], preferred_element_type=F32))
        x1 = _layer_norm_rows(ALPHA * x[...] + mix, g1[...], b1[...])
        x1_ref[...] = x1
        xh = x1.astype(BF16)
        xl = (x1 - xh.astype(F32)).astype(BF16)
        logits = (jnp.dot(xh, wrh[...], preferred_element_type=F32)
                  + jnp.dot(xh, wrl[...], preferred_element_type=F32)
                  + jnp.dot(xl, wrh[...], preferred_element_type=F32)) + br[...]
        eid, ew = _route(logits)
        eid_ref[...] = eid
        ew_ref[...] = ew

    i = pl.program_id(0)
    pl.when(i < n_prompt)(lambda: body(cp, hp, mp, xp))
    pl.when(i >= n_prompt)(lambda: body(cs, hs, ms, xs))


def _outproj_router(prompt, sample, w_out_b, g1, b1, wrh, wrl, br):
    tm = OUT_TM
    tp = prompt[0].shape[0]
    ts = sample[0].shape[0]
    n_p, n_s = tp // tm, ts // tm
    total = tp + ts
    widths = (CONV_WIDTH, MLSTM_WIDTH, MEM_WIDTH, D_MODEL)
    p_specs = [pl.BlockSpec((tm, w), lambda i: (jnp.minimum(i, n_p - 1), 0)) for w in widths]
    s_specs = [pl.BlockSpec((tm, w), lambda i: (jnp.maximum(i - n_p, 0), 0)) for w in widths]
    full = lambda shape: pl.BlockSpec(shape, lambda i: (0, 0))
    row = lambda w: pl.BlockSpec((tm, w), lambda i: (i, 0))
    return pl.pallas_call(
        functools.partial(_outproj_kernel, n_prompt=n_p),
        out_shape=(jax.ShapeDtypeStruct((total, D_MODEL), F32),
                   jax.ShapeDtypeStruct((total, ROUTER_LANES), jnp.int32),
                   jax.ShapeDtypeStruct((total, ROUTER_LANES), F32)),
        grid=(n_p + n_s,),
        in_specs=p_specs + s_specs + [
            full((D_MODEL, D_MODEL)), full((1, D_MODEL)), full((1, D_MODEL)),
            full((D_MODEL, ROUTER_LANES)), full((D_MODEL, ROUTER_LANES)), full((1, ROUTER_LANES)),
        ],
        out_specs=(row(D_MODEL), row(ROUTER_LANES), row(ROUTER_LANES)),
        compiler_params=_params("arbitrary"),
        name="outproj_router",
    )(*prompt, *sample, w_out_b, g1, b1, wrh, wrl, br)


def _moe_kernel(stok_ref, sdst_ref, start_ref, cnt_ref, x_hbm, wg_ref, wu_ref, wd_ref, y_hbm,
                wgb, wub, wdb, xbuf, ybuf, gsem, ssem):
    e = pl.program_id(0)
    wgb[...] = wg_ref[0].astype(BF16)
    wub[...] = wu_ref[0].astype(BF16)
    wdb[...] = wd_ref[0].astype(BF16)
    start = start_ref[e]
    cnt = cnt_ref[e]
    rc = MOE_ROWS

    def gather_copy(tok, r):
        return pltpu.make_async_copy(x_hbm.at[pl.ds(tok, 1)], xbuf.at[pl.ds(r, 1)], gsem)

    def scatter_copy(r, dst):
        return pltpu.make_async_copy(ybuf.at[pl.ds(r, 1)], y_hbm.at[pl.ds(dst, 1)], ssem)

    def chunk(c, carry):
        base = start + c * rc
        n_valid = cnt - c * rc

        def g_start(r, _):
            gather_copy(stok_ref[base + r], r).start()
            return 0

        def g_wait(r, _):
            gather_copy(0, r).wait()
            return 0

        lax.fori_loop(0, rc, g_start, 0, unroll=8)
        lax.fori_loop(0, rc, g_wait, 0, unroll=8)
        x = xbuf[...].astype(BF16)
        hg = jnp.dot(x, wgb[...], preferred_element_type=F32)
        hu = jnp.dot(x, wub[...], preferred_element_type=F32)
        hid = (hg * jax.nn.sigmoid(hg) * hu).astype(BF16)
        ybuf[...] = jnp.dot(hid, wdb[...], preferred_element_type=F32)

        def s_start(r, _):
            @pl.when(r < n_valid)
            def _():
                scatter_copy(r, sdst_ref[base + r]).start()
            return 0

        def s_wait(r, _):
            @pl.when(r < n_valid)
            def _():
                scatter_copy(r, 0).wait()
            return 0

        lax.fori_loop(0, rc, s_start, 0, unroll=8)
        lax.fori_loop(0, rc, s_wait, 0, unroll=8)
        return carry

    lax.fori_loop(0, (cnt + rc - 1) // rc, chunk, 0)


def _moe(x1, stok, sdst, start, cnt, w_gate, w_up, w_down):
    total = x1.shape[0]
    wspec = lambda shape: pl.BlockSpec((1,) + shape, lambda e, *_: (e, 0, 0))
    grid_spec = pltpu.PrefetchScalarGridSpec(
        num_scalar_prefetch=4,
        grid=(N_EXPERTS,),
        in_specs=[pl.BlockSpec(memory_space=pl.ANY),
                  wspec((D_MODEL, D_EXPERT)), wspec((D_MODEL, D_EXPERT)), wspec((D_EXPERT, D_MODEL))],
        out_specs=pl.BlockSpec(memory_space=pl.ANY),
        scratch_shapes=[
            pltpu.VMEM((D_MODEL, D_EXPERT), BF16), pltpu.VMEM((D_MODEL, D_EXPERT), BF16),
            pltpu.VMEM((D_EXPERT, D_MODEL), BF16),
            pltpu.VMEM((MOE_ROWS, D_MODEL), F32), pltpu.VMEM((MOE_ROWS, D_MODEL), F32),
            pltpu.SemaphoreType.DMA, pltpu.SemaphoreType.DMA,
        ],
    )
    return pl.pallas_call(
        _moe_kernel,
        out_shape=jax.ShapeDtypeStruct((2 * total, D_MODEL), F32),
        grid_spec=grid_spec,
        compiler_params=_params("arbitrary"),
        name="moe_experts",
    )(stok, sdst, start, cnt, x1, w_gate, w_up, w_down)


def _final_kernel(x1_ref, y0_ref, y1_ref, ew_ref, g2, b2, o_ref):
    ew = ew_ref[...]
    ffn = ew[:, 0:1] * y0_ref[...] + ew[:, 1:2] * y1_ref[...]
    o_ref[...] = _layer_norm_rows(ALPHA * x1_ref[...] + ffn, g2[...], b2[...])


def _final(x1, y, ew, g2, b2, row0, rows, name):
    tm = OUT_TM
    total = x1.shape[0]
    off = row0 // tm
    k1 = total // tm
    vec = pl.BlockSpec((1, D_MODEL), lambda i: (0, 0))
    return pl.pallas_call(
        _final_kernel,
        out_shape=jax.ShapeDtypeStruct((rows, D_MODEL), F32),
        grid=(rows // tm,),
        in_specs=[pl.BlockSpec((tm, D_MODEL), lambda i: (i + off, 0)),
                  pl.BlockSpec((tm, D_MODEL), lambda i: (i + off, 0)),
                  pl.BlockSpec((tm, D_MODEL), lambda i: (i + off + k1, 0)),
                  pl.BlockSpec((tm, ROUTER_LANES), lambda i: (i + off, 0)),
                  vec, vec],
        out_specs=pl.BlockSpec((tm, D_MODEL), lambda i: (i, 0)),
        compiler_params=_params("arbitrary"),
        name=name,
    )(x1, y, y, ew, g2, b2)


def _dispatch_tables(eid, total):
    flat_e = eid[:, :2].reshape(-1)
    order = jnp.argsort(flat_e, stable=True).astype(jnp.int32)
    stok = order >> 1
    sdst = (order & 1) * total + stok
    experts = jnp.arange(N_EXPERTS, dtype=jnp.int32)
    cnt = jnp.sum((flat_e[:, None] == experts[None, :]).astype(jnp.int32), axis=0)
    start = jnp.cumsum(cnt) - cnt
    pad = jnp.zeros((MOE_ROWS,), jnp.int32)
    return jnp.concatenate([stok, pad]), jnp.concatenate([sdst, pad]), start, cnt


def kernel(x_prompt, x_sample, mem_prompt, state_conv, state_mlstm_C, state_mlstm_n, state_mlstm_m,
           cache_mem_k, cache_mem_v, w_in, b_in, w_dw, b_dw, g_cn, b_cn, g_mh, w_mk, w_mv, w_out,
           g_ln1, b_ln1, w_rg, b_rg, w_re, b_re, w_gate, w_up, w_down, g_ln2, b_ln2):
    bp, sp, _ = x_prompt.shape
    bs, ss, _ = x_sample.shape
    tp, ts = bp * sp, bs * ss

    gate_hi = Z_MAIN_COLS + N_GATE_COLS
    tail = lambda w: jnp.concatenate(
        [w[..., gate_hi:], w[..., Z_MAIN_COLS:gate_hi],
         jnp.zeros(w.shape[:-1] + (LANES - N_GATE_COLS,), w.dtype)], axis=-1)
    w_tail_b = tail(w_in[0]).astype(BF16)
    b_tail = tail(b_in)
    w_kv_b = jnp.concatenate([w_mk[0], w_mv[0]], axis=-1).astype(BF16)
    w_out_b = w_out[0].astype(BF16)
    w_r = jnp.concatenate([w_rg[0], w_re[0],
                           jnp.zeros((D_MODEL, ROUTER_LANES - N_GROUPS - N_EXPERTS), F32)], axis=-1)
    w_r_hi = w_r.astype(BF16)
    w_r_lo = (w_r - w_r_hi.astype(F32)).astype(BF16)
    b_r = jnp.concatenate([b_rg[0], b_re[0],
                           jnp.zeros((ROUTER_LANES - N_GROUPS - N_EXPERTS,), F32)])[None, :]
    row = lambda a: a[0][None, :]

    xs_pad = jnp.pad(x_sample, ((0, 0), (0, SAMPLE_PAD - ss), (0, 0))).reshape(bs * SAMPLE_PAD, D_MODEL)
    zm_p, zt_p = _inproj(x_prompt.reshape(tp, D_MODEL), w_in, b_in, w_tail_b, b_tail, "inproj_prompt")
    zm_s, zt_s = _inproj(xs_pad, w_in, b_in, w_tail_b, b_tail, "inproj_sample")

    kv = _matmul(mem_prompt.reshape(bp * N_MEM, D_MODEL), w_kv_b, KV_TN, "mem_kv")
    mk_p = kv[:, :MEM_WIDTH].reshape(bp, N_MEM, MEM_HEADS, MEM_HEAD_DIM)
    mv_p = kv[:, MEM_WIDTH:].reshape(bp, N_MEM, MEM_HEADS, MEM_HEAD_DIM)

    conv_args = (w_dw[0], row(b_dw), row(g_cn), row(b_cn))
    conv_p, buf_p = _conv_group(zm_p, jnp.zeros((bp, CONV_HIST, CONV_WIDTH), F32), *conv_args,
                                bp, sp, 256, "conv_prompt")
    conv_s, buf_s_t = _conv_step_group(
        zm_s.reshape(bs, SAMPLE_PAD, Z_MAIN_COLS), jnp.transpose(state_conv[0], (1, 0, 2)),
        *conv_args, ss, 32, "conv_sample")
    buf_s = jnp.transpose(buf_s_t, (1, 0, 2))

    m_tile = lambda m: jnp.broadcast_to(
        jnp.pad(m, ((0, 0), (0, SUBLANES - MLSTM_HEADS)))[:, :, None], (m.shape[0], SUBLANES, LANES))
    g_mh_r = row(g_mh)
    h_p, c_p, n_p, m_p = _mlstm_group(
        zm_p, zt_p, jnp.zeros((bp, MLSTM_HEADS, MLSTM_HEAD_DIM, MLSTM_HEAD_DIM), F32),
        jnp.zeros((bp, MLSTM_HEADS, MLSTM_HEAD_DIM), F32), jnp.zeros((bp, SUBLANES, LANES), F32),
        g_mh_r, bp, sp, MLSTM_CHUNK, MLSTM_CHUNK, 1, "mlstm_prompt")
    h_s, c_s, n_s, m_s = _mlstm_group(
        zm_s, zt_s, state_mlstm_C[0], state_mlstm_n[0], m_tile(state_mlstm_m[0]),
        g_mh_r, bs, SAMPLE_PAD, SAMPLE_PAD, ss, 4, "mlstm_sample")

    mem_p = _memattn_group(zt_p, mk_p, mv_p, bp, sp, 512, 1, "memattn_prompt")
    mem_s = _memattn_group(zt_s, cache_mem_k[0], cache_mem_v[0], bs, SAMPLE_PAD, SAMPLE_PAD, 8,
                           "memattn_sample")

    compact = lambda a: a.reshape(bs, SAMPLE_PAD, a.shape[-1])[:, :ss].reshape(ts, a.shape[-1])
    x1, eid, ew = _outproj_router(
        (conv_p, h_p, mem_p, x_prompt.reshape(tp, D_MODEL)),
        (conv_s.reshape(ts, CONV_WIDTH), compact(h_s), compact(mem_s), x_sample.reshape(ts, D_MODEL)),
        w_out_b, row(g_ln1), row(b_ln1), w_r_hi, w_r_lo, b_r)

    total = tp + ts
    stok, sdst, start, cnt = _dispatch_tables(eid, total)
    y = _moe(x1, stok, sdst, start, cnt, w_gate[0], w_up[0], w_down[0])
    g2, b2 = row(g_ln2), row(b_ln2)
    y_p = _final(x1, y, ew, g2, b2, 0, tp, "final_prompt").reshape(bp, sp, D_MODEL)
    y_s = _final(x1, y, ew, g2, b2, tp, ts, "final_sample").reshape(bs, ss, D_MODEL)

    return (y_p, y_s, buf_p[None], buf_s[None], c_p[None], c_s[None], n_p[None], n_s[None],
            m_p[:, :MLSTM_HEADS, 0][None], m_s[:, :MLSTM_HEADS, 0][None], mk_p[None], mv_p[None])
```

```python
import functools

import jax
import jax.numpy as jnp
from jax import lax
from jax.experimental import pallas as pl
from jax.experimental.pallas import tpu as pltpu

F32 = jnp.float32
BF16 = jnp.bfloat16

D_MODEL = 2048
CONV_WIDTH = 512
CONV_K = 31
CONV_HIST = CONV_K - 1
MLSTM_HEADS = 4
MLSTM_HEAD_DIM = 256
MLSTM_WIDTH = MLSTM_HEADS * MLSTM_HEAD_DIM
MLSTM_CHUNK = 128
MEM_HEADS = 4
MEM_HEAD_DIM = 128
MEM_WIDTH = MEM_HEADS * MEM_HEAD_DIM
N_MEM = 256
N_GROUPS = 8
EXPERTS_PER_GROUP = 8
N_EXPERTS = N_GROUPS * EXPERTS_PER_GROUP
D_EXPERT = 512
LN_EPS = 1e-5
DEPTH = 1
ALPHA = (2 * DEPTH) ** 0.25

LANES = 128
SUBLANES = 8
VMEM_LIMIT_BYTES = 56 * 1024 * 1024

Z_CONV_A = 0
Z_CONV_G = CONV_WIDTH
Z_Q = 2 * CONV_WIDTH
Z_K = Z_Q + MLSTM_WIDTH
Z_V = Z_K + MLSTM_WIDTH
Z_O = Z_V + MLSTM_WIDTH
Z_MAIN_COLS = Z_O + MLSTM_WIDTH
N_GATE_COLS = 2 * MLSTM_HEADS
ZT_QM = 0
ZT_GATE = MEM_WIDTH
Z_TAIL_COLS = MEM_WIDTH + LANES
INPROJ_TM = 1024
INPROJ_TN = 512
KV_TN = 256

SAMPLE_PAD = SUBLANES
ROUTER_LANES = LANES
MOE_ROWS = 128
OUT_TM = 256


def _params(*sem):
    return pltpu.CompilerParams(dimension_semantics=sem, vmem_limit_bytes=VMEM_LIMIT_BYTES)


def _inproj_kernel(x_ref, wm_ref, bm_ref, wt_ref, bt_ref, zm_ref, zt_ref, xb_ref, *, n_main):
    j = pl.program_id(1)

    @pl.when(j == 0)
    def _():
        xb_ref[...] = x_ref[...].astype(BF16)

    nt = (((1,), (1,)), ((), ()))

    @pl.when(j < n_main)
    def _():
        zm_ref[...] = lax.dot_general(xb_ref[...], wm_ref[...].astype(BF16), nt,
                                      preferred_element_type=F32) + bm_ref[...]

    @pl.when(j == n_main)
    def _():
        zt_ref[...] = lax.dot_general(xb_ref[...], wt_ref[...], nt,
                                      preferred_element_type=F32) + bt_ref[...]


def _inproj(x, w_in_t, b_in, w_tail_t, b_tail, name):
    t, k = x.shape
    tm, tn = INPROJ_TM, INPROJ_TN
    n_main = Z_MAIN_COLS // tn
    main_col = lambda j: jnp.minimum(j, n_main - 1)
    return pl.pallas_call(
        functools.partial(_inproj_kernel, n_main=n_main),
        out_shape=(jax.ShapeDtypeStruct((t, Z_MAIN_COLS), F32),
                   jax.ShapeDtypeStruct((t, Z_TAIL_COLS), F32)),
        grid=(t // tm, n_main + 1),
        in_specs=[
            pl.BlockSpec((tm, k), lambda i, j: (i, 0)),
            pl.BlockSpec((tn, k), lambda i, j: (main_col(j), 0)),
            pl.BlockSpec((1, tn), lambda i, j: (0, main_col(j))),
            pl.BlockSpec((Z_TAIL_COLS, k), lambda i, j: (0, 0)),
            pl.BlockSpec((1, Z_TAIL_COLS), lambda i, j: (0, 0)),
        ],
        out_specs=(pl.BlockSpec((tm, tn), lambda i, j: (i, main_col(j))),
                   pl.BlockSpec((tm, Z_TAIL_COLS), lambda i, j: (i, 0))),
        scratch_shapes=[pltpu.VMEM((tm, k), BF16)],
        compiler_params=_params("arbitrary", "arbitrary"),
        name=name,
    )(x, w_in_t, b_in, w_tail_t, b_tail)


def _matmul_kernel(x_ref, w_ref, o_ref):
    o_ref[...] = jnp.dot(x_ref[...].astype(BF16), w_ref[...], preferred_element_type=F32)


def _matmul(x, w_bf16, tn, name):
    t, k = x.shape
    n = w_bf16.shape[1]
    return pl.pallas_call(
        _matmul_kernel,
        out_shape=jax.ShapeDtypeStruct((t, n), F32),
        grid=(n // tn,),
        in_specs=[pl.BlockSpec((t, k), lambda j: (0, 0)), pl.BlockSpec((k, tn), lambda j: (0, j))],
        out_specs=pl.BlockSpec((t, tn), lambda j: (0, j)),
        compiler_params=_params("arbitrary"),
        name=name,
    )(x, w_bf16)


def _layer_norm_rows(y, g, b):
    mu = jnp.mean(y, axis=-1, keepdims=True)
    yc = y - mu
    var = jnp.mean(yc * yc, axis=-1, keepdims=True)
    return yc * lax.rsqrt(var + LN_EPS) * g + b


def _conv_kernel(a_ref, g_ref, hist_ref, wdw_ref, bdw_ref, gcn_ref, bcn_ref, out_ref, nb_ref, ubuf,
                 *, tl):
    head = CONV_HIST + 2
    li = pl.program_id(1)

    @pl.when(li == 0)
    def _():
        ubuf[0:2, :] = jnp.zeros((2, CONV_WIDTH), F32)
        ubuf[2:head, :] = hist_ref[0]

    ubuf[head:head + tl, :] = a_ref[...] * jax.nn.sigmoid(g_ref[...])
    acc = jnp.zeros((tl, CONV_WIDTH), F32) + bdw_ref[...]
    for j in range(CONV_K):
        acc = acc + wdw_ref[j:j + 1, :] * ubuf[2 + j:2 + j + tl, :]
    y = _layer_norm_rows(acc, gcn_ref[...], bcn_ref[...])
    out_ref[...] = y * jax.nn.sigmoid(y)

    @pl.when(li == pl.num_programs(1) - 1)
    def _():
        nb_ref[0] = ubuf[2 + tl:head + tl, :]

    ubuf[2:head, :] = ubuf[2 + tl:head + tl, :]


def _conv_group(z, hist, w_dw, b_dw, g_cn, b_cn, batch, seq, tl, name):
    nl = seq // tl
    row = lambda b, l: (b * nl + l, 0)
    vec = pl.BlockSpec((1, CONV_WIDTH), lambda b, l: (0, 0))
    return pl.pallas_call(
        functools.partial(_conv_kernel, tl=tl),
        out_shape=(jax.ShapeDtypeStruct((batch * seq, CONV_WIDTH), F32),
                   jax.ShapeDtypeStruct((batch, CONV_HIST, CONV_WIDTH), F32)),
        grid=(batch, nl),
        in_specs=[
            pl.BlockSpec((tl, CONV_WIDTH), lambda b, l: (b * nl + l, Z_CONV_A // CONV_WIDTH)),
            pl.BlockSpec((tl, CONV_WIDTH), lambda b, l: (b * nl + l, Z_CONV_G // CONV_WIDTH)),
            pl.BlockSpec((1, CONV_HIST, CONV_WIDTH), lambda b, l: (b, 0, 0)),
            pl.BlockSpec((CONV_K, CONV_WIDTH), lambda b, l: (0, 0)),
            vec, vec, vec,
        ],
        out_specs=(pl.BlockSpec((tl, CONV_WIDTH), row),
                   pl.BlockSpec((1, CONV_HIST, CONV_WIDTH), lambda b, l: (b, 0, 0))),
        scratch_shapes=[pltpu.VMEM((CONV_HIST + 2 + tl, CONV_WIDTH), F32)],
        compiler_params=_params("arbitrary", "arbitrary"),
        name=name,
    )(z, z, hist, w_dw, b_dw, g_cn, b_cn)


def _conv_step_kernel(a_ref, g_ref, hist_ref, wdw_ref, bdw_ref, gcn_ref, bcn_ref, out_ref, nb_ref,
                      *, steps):
    u = [a_ref[:, t, :] * jax.nn.sigmoid(g_ref[:, t, :]) for t in range(steps)]
    full = lambda r: hist_ref[r] if r < CONV_HIST else u[r - CONV_HIST]
    for t in range(steps):
        acc = bdw_ref[...] + wdw_ref[0:1, :] * full(t)
        for j in range(1, CONV_K):
            acc = acc + wdw_ref[j:j + 1, :] * full(t + j)
        y = _layer_norm_rows(acc, gcn_ref[...], bcn_ref[...])
        out_ref[:, t, :] = y * jax.nn.sigmoid(y)
    for r in range(CONV_HIST):
        nb_ref[r] = full(r + steps)


def _conv_step_group(z3, hist_t, w_dw, b_dw, g_cn, b_cn, steps, bb, name):
    batch = z3.shape[0]
    vec = pl.BlockSpec((1, CONV_WIDTH), lambda b: (0, 0))
    hist_spec = pl.BlockSpec((CONV_HIST, bb, CONV_WIDTH), lambda b: (0, b, 0))
    return pl.pallas_call(
        functools.partial(_conv_step_kernel, steps=steps),
        out_shape=(jax.ShapeDtypeStruct((batch, steps, CONV_WIDTH), F32),
                   jax.ShapeDtypeStruct((CONV_HIST, batch, CONV_WIDTH), F32)),
        grid=(batch // bb,),
        in_specs=[
            pl.BlockSpec((bb, SAMPLE_PAD, CONV_WIDTH), lambda b: (b, 0, Z_CONV_A // CONV_WIDTH)),
            pl.BlockSpec((bb, SAMPLE_PAD, CONV_WIDTH), lambda b: (b, 0, Z_CONV_G // CONV_WIDTH)),
            hist_spec,
            pl.BlockSpec((CONV_K, CONV_WIDTH), lambda b: (0, 0)),
            vec, vec, vec,
        ],
        out_specs=(pl.BlockSpec((bb, steps, CONV_WIDTH), lambda b: (b, 0, 0)), hist_spec),
        compiler_params=_params("arbitrary"),
        name=name,
    )(z3, z3, hist_t, w_dw, b_dw, g_cn, b_cn)


def _col_to_row(col, eye):
    n = col.shape[0]
    return jnp.sum(jnp.where(eye, jnp.broadcast_to(col, (n, n)), 0.0), axis=0, keepdims=True)


def _mlstm_kernel(q_ref, k_ref, v_ref, o_ref, gate_ref, c0_ref, n0_ref, m0_ref, gmh_ref,
                  h_ref, c_ref, n_ref, m_ref, *, cl, valid, bb, single_chunk):
    if single_chunk:
        c_in, n_in, m_in = c0_ref, n0_ref, m0_ref
    else:
        c_in, n_in, m_in = c_ref, n_ref, m_ref

        @pl.when(pl.program_id(1) == 0)
        def _():
            c_ref[...] = c0_ref[...]
            n_ref[...] = n0_ref[...]
            m_ref[...] = m0_ref[...]

    rows = lax.broadcasted_iota(jnp.int32, (cl, cl), 0)
    cols = lax.broadcasted_iota(jnp.int32, (cl, cl), 1)
    eye = rows == cols
    tril = rows >= cols
    row_id = lax.broadcasted_iota(jnp.int32, (cl, 1), 0)
    dh = MLSTM_HEAD_DIM
    for bi, h in [(bi, h) for bi in range(bb) for h in range(MLSTM_HEADS)]:
        rs = slice(bi * cl, (bi + 1) * cl)
        sl = slice(h * dh, (h + 1) * dh)
        gates = gate_ref[rs, :]
        q = q_ref[rs, sl]
        k = k_ref[rs, sl] * (dh ** -0.5)
        v = v_ref[rs, sl]
        ip = gates[:, h:h + 1]
        fp = gates[:, MLSTM_HEADS + h:MLSTM_HEADS + h + 1]
        lf = jnp.minimum(fp, 0.0) - jnp.log1p(jnp.exp(-jnp.abs(fp)))
        if valid < cl:
            ip = jnp.where(row_id < valid, ip, -jnp.inf)
            lf = jnp.where(row_id < valid, lf, 0.0)
        lf_row = _col_to_row(lf, eye)
        ip_row = _col_to_row(ip, eye)
        b_col = jnp.sum(jnp.where(tril, jnp.broadcast_to(lf_row, (cl, cl)), 0.0), axis=1, keepdims=True)
        b_row = _col_to_row(b_col, eye)
        m_prev = m_in[bi, h:h + 1, 0:1]
        log_inter = b_col + m_prev
        log_intra = jnp.where(tril, b_col - b_row + ip_row, -jnp.inf)
        m_t = jnp.maximum(log_inter, jnp.max(log_intra, axis=1, keepdims=True))
        w_inter = jnp.exp(log_inter - m_t)
        qb = q.astype(BF16)
        kb = k.astype(BF16)
        vb = v.astype(BF16)
        s = lax.dot_general(qb, kb, (((1,), (1,)), ((), ())), preferred_element_type=F32)
        s = s * jnp.exp(log_intra - m_t)
        c_old = c_in[bi, h]
        n_old = n_in[bi, h:h + 1, :]
        num = (w_inter * jnp.dot(qb, c_old.astype(BF16), preferred_element_type=F32)
               + jnp.dot(s.astype(BF16), vb, preferred_element_type=F32))
        den = w_inter * jnp.sum(q * n_old, axis=1, keepdims=True) + jnp.sum(s, axis=1, keepdims=True)
        hh = num / jnp.maximum(jnp.abs(den), jnp.exp(-m_t))
        m_new = m_t[cl - 1:cl, :]
        b_last = b_col[cl - 1:cl, :]
        decay = jnp.exp(b_last + m_prev - m_new)
        w_s = jnp.exp(b_last - b_col + ip - m_new)
        kw = k * w_s
        c_ref[bi, h] = decay * c_old + lax.dot_general(
            kw.astype(BF16), vb, (((0,), (0,)), ((), ())), preferred_element_type=F32)
        n_ref[bi, h:h + 1, :] = decay * n_old + jnp.sum(kw, axis=0, keepdims=True)
        m_ref[bi, h:h + 1, :] = jnp.broadcast_to(m_new, (1, LANES))
        mu = jnp.mean(hh, axis=-1, keepdims=True)
        hc = hh - mu
        var = jnp.mean(hc * hc, axis=-1, keepdims=True)
        hn = hc * lax.rsqrt(var + LN_EPS) * gmh_ref[:, sl]
        h_ref[rs, sl] = hn * jax.nn.sigmoid(o_ref[rs, sl])
    if single_chunk:
        m_ref[:, MLSTM_HEADS:, :] = jnp.zeros((bb, SUBLANES - MLSTM_HEADS, LANES), F32)


def _mlstm_group(z_main, z_tail, c0, n0, m0, g_mh, batch, seq, cl, valid, bb, name):
    nc = seq // cl
    assert bb == 1 or nc == 1, "several sequences per step only for single-chunk sequences"
    rows = bb * cl
    zcol = lambda off: pl.BlockSpec((rows, MLSTM_WIDTH), lambda b, c: (b * nc + c, off // MLSTM_WIDTH))
    state = lambda shape: pl.BlockSpec((bb,) + shape, lambda b, c: (b,) + (0,) * len(shape))
    c_shape = (MLSTM_HEADS, MLSTM_HEAD_DIM, MLSTM_HEAD_DIM)
    n_shape = (MLSTM_HEADS, MLSTM_HEAD_DIM)
    m_shape = (SUBLANES, LANES)
    return pl.pallas_call(
        functools.partial(_mlstm_kernel, cl=cl, valid=valid, bb=bb, single_chunk=nc == 1),
        out_shape=(jax.ShapeDtypeStruct((batch * seq, MLSTM_WIDTH), F32),
                   jax.ShapeDtypeStruct((batch,) + c_shape, F32),
                   jax.ShapeDtypeStruct((batch,) + n_shape, F32),
                   jax.ShapeDtypeStruct((batch,) + m_shape, F32)),
        grid=(batch // bb, nc),
        in_specs=[
            zcol(Z_Q), zcol(Z_K), zcol(Z_V), zcol(Z_O),
            pl.BlockSpec((rows, LANES), lambda b, c: (b * nc + c, ZT_GATE // LANES)),
            state(c_shape), state(n_shape), state(m_shape),
            pl.BlockSpec((1, MLSTM_WIDTH), lambda b, c: (0, 0)),
        ],
        out_specs=(pl.BlockSpec((rows, MLSTM_WIDTH), lambda b, c: (b * nc + c, 0)),
                   state(c_shape), state(n_shape), state(m_shape)),
        compiler_params=_params("arbitrary", "arbitrary"),
        name=name,
    )(z_main, z_main, z_main, z_main, z_tail, c0, n0, m0, g_mh)


def _softmax_rows(s):
    e = jnp.exp(s - jnp.max(s, axis=-1, keepdims=True))
    return e / jnp.sum(e, axis=-1, keepdims=True)


_NT_DIMS = (((1,), (1,)), ((), ()))


def _memattn_head_kernel(q_ref, k_ref, v_ref, o_ref):
    s = lax.dot_general(q_ref[...].astype(BF16), k_ref[...].astype(BF16), _NT_DIMS,
                        preferred_element_type=F32) * (MEM_HEAD_DIM ** -0.5)
    p = _softmax_rows(s)
    o_ref[...] = jnp.dot(p.astype(BF16), v_ref[...].astype(BF16), preferred_element_type=F32)


def _memattn_heads(z_tail, kv, batch, seq, tq, name):
    nq = seq // tq
    dh = MEM_HEAD_DIM
    return pl.pallas_call(
        _memattn_head_kernel,
        out_shape=jax.ShapeDtypeStruct((batch * seq, MEM_WIDTH), F32),
        grid=(batch, MEM_HEADS, nq),
        in_specs=[pl.BlockSpec((tq, dh), lambda b, h, i: (b * nq + i, ZT_QM // dh + h)),
                  pl.BlockSpec((N_MEM, dh), lambda b, h, i: (b, h)),
                  pl.BlockSpec((N_MEM, dh), lambda b, h, i: (b, MEM_HEADS + h))],
        out_specs=pl.BlockSpec((tq, dh), lambda b, h, i: (b * nq + i, h)),
        compiler_params=_params("arbitrary", "arbitrary", "arbitrary"),
        name=name,
    )(z_tail, kv, kv)


def _memattn_packed_kernel(q_ref, k_ref, v_ref, o_ref, *, tq, bb):
    nh, dh = MEM_HEADS, MEM_HEAD_DIM
    shape = (nh * tq, N_MEM * nh)
    row_head = lax.broadcasted_iota(jnp.int32, shape, 0) // tq
    col_head = lax.broadcasted_iota(jnp.int32, shape, 1) % nh
    same_head = row_head == col_head
    for bi in range(bb):
        rs = slice(bi * tq, (bi + 1) * tq)
        q = jnp.concatenate([q_ref[rs, h * dh:(h + 1) * dh] for h in range(nh)], axis=0)
        s = lax.dot_general(q.astype(BF16), k_ref[bi].astype(BF16), _NT_DIMS,
                            preferred_element_type=F32) * (dh ** -0.5)
        p = _softmax_rows(jnp.where(same_head, s, -jnp.inf))
        o = jnp.dot(p.astype(BF16), v_ref[bi].astype(BF16), preferred_element_type=F32)
        for h in range(nh):
            o_ref[rs, h * dh:(h + 1) * dh] = o[h * tq:(h + 1) * tq, :]


def _memattn_packed(z_tail, mk, mv, batch, tq, bb, name):
    rows = bb * tq
    kv = pl.BlockSpec((bb, N_MEM * MEM_HEADS, MEM_HEAD_DIM), lambda b: (b, 0, 0))
    return pl.pallas_call(
        functools.partial(_memattn_packed_kernel, tq=tq, bb=bb),
        out_shape=jax.ShapeDtypeStruct((batch * tq, MEM_WIDTH), F32),
        grid=(batch // bb,),
        in_specs=[pl.BlockSpec((rows, MEM_WIDTH), lambda b: (b, ZT_QM // MEM_WIDTH)), kv, kv],
        out_specs=pl.BlockSpec((rows, MEM_WIDTH), lambda b: (b, 0)),
        compiler_params=_params("arbitrary"),
        name=name,
    )(z_tail, mk, mv)


def _route(logits):
    lane = lax.broadcasted_iota(jnp.int32, logits.shape, 1).astype(F32)
    neg = -jnp.inf
    first = lambda mask: jnp.min(jnp.where(mask, lane, float(LANES)), axis=1, keepdims=True)
    is_g = lane < N_GROUPS
    gl = jnp.where(is_g, logits, neg)
    g_max = jnp.max(gl, axis=1, keepdims=True)
    g_sel = first(gl == g_max)
    g_w = 1.0 / jnp.sum(jnp.exp(gl - g_max), axis=1, keepdims=True)
    lo = N_GROUPS + g_sel * EXPERTS_PER_GROUP
    in_grp = (lane >= lo) & (lane < lo + EXPERTS_PER_GROUP)
    el = jnp.where(in_grp, logits, neg)
    v1 = jnp.max(el, axis=1, keepdims=True)
    i1 = first(in_grp & (el == v1))
    rest = in_grp & (lane != i1)
    el2 = jnp.where(rest, logits, neg)
    v2 = jnp.max(el2, axis=1, keepdims=True)
    i2 = first(rest & (el2 == v2))
    t = jnp.exp(v2 - v1)
    w1 = g_w / (1.0 + t)
    w2 = g_w * t / (1.0 + t)
    lane_i = lax.broadcasted_iota(jnp.int32, logits.shape, 1)
    e1 = (i1 - N_GROUPS).astype(jnp.int32)
    e2 = (i2 - N_GROUPS).astype(jnp.int32)
    eid = jnp.where(lane_i == 0, e1, jnp.where(lane_i == 1, e2, 0))
    ew = jnp.where(lane_i == 0, w1, jnp.where(lane_i == 1, w2, 0.0))
    return eid, ew


def _outproj_kernel(cp, hp, mp, xp, cs, hs, ms, xs, wout, g1, b1, wrh, wrl, br,
                    x1_ref, eid_ref, ew_ref, *, n_prompt):
    def body(c, h, m, x):
        c0, c1, c2 = CONV_WIDTH, CONV_WIDTH + MLSTM_WIDTH, D_MODEL
        mix = (jnp.dot(c[...].astype(BF16), wout[0:c0, :], preferred_element_type=F32)
               + jnp.dot(h[...].astype(BF16), wout[c0:c1, :], preferred_element_type=F32)
               + jnp.dot(m[...].astype(BF16), wout[c1:c2, :], preferred_element_type=F32))
        x1 = _layer_norm_rows(ALPHA * x[...] + mix, g1[...], b1[...])
        x1_ref[...] = x1
        xh = x1.astype(BF16)
        xl = (x1 - xh.astype(F32)).astype(BF16)
        logits = (jnp.dot(xh, wrh[...], preferred_element_type=F32)
                  + jnp.dot(xh, wrl[...], preferred_element_type=F32)
                  + jnp.dot(xl, wrh[...], preferred_element_type=F32)) + br[...]
        eid, ew = _route(logits)
        eid_ref[...] = eid
        ew_ref[...] = ew

    i = pl.program_id(0)
    pl.when(i < n_prompt)(lambda: body(cp, hp, mp, xp))
    pl.when(i >= n_prompt)(lambda: body(cs, hs, ms, xs))


def _outproj_router(prompt, sample, w_out_b, g1, b1, wrh, wrl, br):
    tm = OUT_TM
    tp = prompt[0].shape[0]
    ts = sample[0].shape[0]
    n_p, n_s = tp // tm, ts // tm
    total = tp + ts
    widths = (CONV_WIDTH, MLSTM_WIDTH, MEM_WIDTH, D_MODEL)
    p_specs = [pl.BlockSpec((tm, w), lambda i: (jnp.minimum(i, n_p - 1), 0)) for w in widths]
    s_specs = [pl.BlockSpec((tm, w), lambda i: (jnp.maximum(i - n_p, 0), 0)) for w in widths]
    full = lambda shape: pl.BlockSpec(shape, lambda i: (0, 0))
    row = lambda w: pl.BlockSpec((tm, w), lambda i: (i, 0))
    return pl.pallas_call(
        functools.partial(_outproj_kernel, n_prompt=n_p),
        out_shape=(jax.ShapeDtypeStruct((total, D_MODEL), F32),
                   jax.ShapeDtypeStruct((total, ROUTER_LANES), jnp.int32),
                   jax.ShapeDtypeStruct((total, ROUTER_LANES), F32)),
        grid=(n_p + n_s,),
        in_specs=p_specs + s_specs + [
            full((D_MODEL, D_MODEL)), full((1, D_MODEL)), full((1, D_MODEL)),
            full((D_MODEL, ROUTER_LANES)), full((D_MODEL, ROUTER_LANES)), full((1, ROUTER_LANES)),
        ],
        out_specs=(row(D_MODEL), row(ROUTER_LANES), row(ROUTER_LANES)),
        compiler_params=_params("arbitrary"),
        name="outproj_router",
    )(*prompt, *sample, w_out_b, g1, b1, wrh, wrl, br)


def _moe_kernel(stok_ref, sdst_ref, start_ref, cnt_ref, x_hbm, wg_ref, wu_ref, wd_ref, y_hbm,
                wgb, wub, wdb, xbuf, ybuf, gsem, ssem, state, *, dump_row):
    e = pl.program_id(0)
    last = pl.num_programs(0) - 1
    rc = MOE_ROWS

    def gather_start(base, slot):
        for r in range(rc):
            pltpu.make_async_copy(x_hbm.at[pl.ds(stok_ref[base + r], 1)],
                                  xbuf.at[slot, pl.ds(r, 1)], gsem.at[slot]).start()

    def gather_wait(slot):
        pltpu.make_async_copy(x_hbm.at[pl.ds(0, rc)], xbuf.at[slot], gsem.at[slot]).wait()

    def scatter_start(base, n_valid, slot):
        for r in range(rc):
            dst = jnp.where(r < n_valid, sdst_ref[base + r], dump_row + slot * rc + r)
            pltpu.make_async_copy(ybuf.at[slot, pl.ds(r, 1)], y_hbm.at[pl.ds(dst, 1)],
                                  ssem.at[slot]).start()

    def scatter_wait(slot):
        pltpu.make_async_copy(ybuf.at[slot], y_hbm.at[pl.ds(0, rc)], ssem.at[slot]).wait()

    @pl.when(e == 0)
    def _():
        state[0] = 0
        state[1] = 0
        state[2] = 0
        state[3] = 0
        state[4] = 0
        ybuf[...] = jnp.zeros(ybuf.shape, F32)
        pltpu.make_async_copy(ybuf.at[0], y_hbm.at[pl.ds(dump_row, rc)], ssem.at[0]).start()
        scatter_wait(0)
        gather_start(start_ref[0], 0)

    wgb[...] = wg_ref[0].astype(BF16)
    wub[...] = wu_ref[0].astype(BF16)
    wdb[...] = wd_ref[0].astype(BF16)
    start = start_ref[e]
    cnt = cnt_ref[e]
    n_chunks = jnp.maximum((cnt + rc - 1) // rc, 1)
    next_start = start_ref[jnp.minimum(e + 1, last)]

    def chunk_on(slot, c):
        other = 1 - slot
        base = start + c * rc
        gather_wait(slot)
        gather_start(jnp.where(c + 1 < n_chunks, base + rc, next_start), other)
        scatter_start(state[3], state[4], other)
        state[1 + other] = 1
        x = xbuf[slot].astype(BF16)
        hg = jnp.dot(x, wgb[...], preferred_element_type=F32)
        hu = jnp.dot(x, wub[...], preferred_element_type=F32)
        hid = (hg * jax.nn.sigmoid(hg) * hu).astype(BF16)

        @pl.when(state[1 + slot] == 1)
        def _():
            scatter_wait(slot)

        ybuf[slot] = jnp.dot(hid, wdb[...], preferred_element_type=F32)
        state[3] = base
        state[4] = cnt - c * rc
        state[0] = state[0] + 1

    def chunk(c, carry):
        parity = state[0] & 1
        for slot in range(2):
            pl.when(parity == slot)(functools.partial(chunk_on, slot, c))
        return carry

    lax.fori_loop(0, n_chunks, chunk, 0)

    @pl.when(e == last)
    def _():
        parity = state[0] & 1
        for slot in range(2):
            @pl.when(parity == slot)
            def _():
                gather_wait(slot)
                scatter_start(state[3], state[4], 1 - slot)
                scatter_wait(1 - slot)

                @pl.when(state[1 + slot] == 1)
                def _():
                    scatter_wait(slot)


def _moe(x1, stok, sdst, start, cnt, w_gate, w_up, w_down):
    total = x1.shape[0]
    dump_row = 2 * total
    wspec = lambda shape: pl.BlockSpec((1,) + shape, lambda e, *_: (e, 0, 0))
    grid_spec = pltpu.PrefetchScalarGridSpec(
        num_scalar_prefetch=4,
        grid=(N_EXPERTS,),
        in_specs=[pl.BlockSpec(memory_space=pl.ANY),
                  wspec((D_MODEL, D_EXPERT)), wspec((D_MODEL, D_EXPERT)), wspec((D_EXPERT, D_MODEL))],
        out_specs=pl.BlockSpec(memory_space=pl.ANY),
        scratch_shapes=[
            pltpu.VMEM((D_MODEL, D_EXPERT), BF16), pltpu.VMEM((D_MODEL, D_EXPERT), BF16),
            pltpu.VMEM((D_EXPERT, D_MODEL), BF16),
            pltpu.VMEM((2, MOE_ROWS, D_MODEL), F32), pltpu.VMEM((2, MOE_ROWS, D_MODEL), F32),
            pltpu.SemaphoreType.DMA((2,)), pltpu.SemaphoreType.DMA((2,)),
            pltpu.SMEM((5,), jnp.int32),
        ],
    )
    return pl.pallas_call(
        functools.partial(_moe_kernel, dump_row=dump_row),
        out_shape=jax.ShapeDtypeStruct((2 * total + 2 * MOE_ROWS, D_MODEL), F32),
        grid_spec=grid_spec,
        compiler_params=_params("arbitrary"),
        name="moe_experts",
    )(stok, sdst, start, cnt, x1, w_gate, w_up, w_down)


def _final_kernel(x1_ref, y0_ref, y1_ref, ew_ref, g2, b2, o_ref):
    ew = ew_ref[...]
    ffn = ew[:, 0:1] * y0_ref[...] + ew[:, 1:2] * y1_ref[...]
    o_ref[...] = _layer_norm_rows(ALPHA * x1_ref[...] + ffn, g2[...], b2[...])


def _final(x1, y, ew, g2, b2, row0, rows, name):
    tm = OUT_TM
    total = x1.shape[0]
    off = row0 // tm
    k1 = total // tm
    vec = pl.BlockSpec((1, D_MODEL), lambda i: (0, 0))
    return pl.pallas_call(
        _final_kernel,
        out_shape=jax.ShapeDtypeStruct((rows, D_MODEL), F32),
        grid=(rows // tm,),
        in_specs=[pl.BlockSpec((tm, D_MODEL), lambda i: (i + off, 0)),
                  pl.BlockSpec((tm, D_MODEL), lambda i: (i + off, 0)),
                  pl.BlockSpec((tm, D_MODEL), lambda i: (i + off + k1, 0)),
                  pl.BlockSpec((tm, ROUTER_LANES), lambda i: (i + off, 0)),
                  vec, vec],
        out_specs=pl.BlockSpec((tm, D_MODEL), lambda i: (i, 0)),
        compiler_params=_params("arbitrary"),
        name=name,
    )(x1, y, y, ew, g2, b2)


def _dispatch_tables(eid, total):
    flat_e = eid[:, :2].reshape(-1)
    order = jnp.argsort(flat_e, stable=True).astype(jnp.int32)
    stok = order >> 1
    sdst = (order & 1) * total + stok
    experts = jnp.arange(N_EXPERTS, dtype=jnp.int32)
    cnt = jnp.sum((flat_e[:, None] == experts[None, :]).astype(jnp.int32), axis=0)
    start = jnp.cumsum(cnt) - cnt
    pad = jnp.zeros((MOE_ROWS,), jnp.int32)
    return jnp.concatenate([stok, pad]), jnp.concatenate([sdst, pad]), start, cnt


def kernel(x_prompt, x_sample, mem_prompt, state_conv, state_mlstm_C, state_mlstm_n, state_mlstm_m,
           cache_mem_k, cache_mem_v, w_in, b_in, w_dw, b_dw, g_cn, b_cn, g_mh, w_mk, w_mv, w_out,
           g_ln1, b_ln1, w_rg, b_rg, w_re, b_re, w_gate, w_up, w_down, g_ln2, b_ln2):
    bp, sp, _ = x_prompt.shape
    bs, ss, _ = x_sample.shape
    tp, ts = bp * sp, bs * ss

    gate_hi = Z_MAIN_COLS + N_GATE_COLS
    tail = lambda wt: jnp.concatenate(
        [wt[gate_hi:], wt[Z_MAIN_COLS:gate_hi],
         jnp.zeros((LANES - N_GATE_COLS,) + wt.shape[1:], wt.dtype)], axis=0)
    w_in_t = jnp.transpose(w_in[0])
    w_tail_t = tail(w_in_t).astype(BF16)
    b_tail = tail(b_in[0])[None, :]
    w_kv_b = jnp.concatenate([w_mk[0], w_mv[0]], axis=-1).astype(BF16)
    w_out_b = w_out[0].astype(BF16)
    w_r = jnp.concatenate([w_rg[0], w_re[0],
                           jnp.zeros((D_MODEL, ROUTER_LANES - N_GROUPS - N_EXPERTS), F32)], axis=-1)
    w_r_hi = w_r.astype(BF16)
    w_r_lo = (w_r - w_r_hi.astype(F32)).astype(BF16)
    b_r = jnp.concatenate([b_rg[0], b_re[0],
                           jnp.zeros((ROUTER_LANES - N_GROUPS - N_EXPERTS,), F32)])[None, :]
    row = lambda a: a[0][None, :]

    xs_pad = jnp.pad(x_sample, ((0, 0), (0, SAMPLE_PAD - ss), (0, 0))).reshape(bs * SAMPLE_PAD, D_MODEL)
    zm_p, zt_p = _inproj(x_prompt.reshape(tp, D_MODEL), w_in_t, b_in, w_tail_t, b_tail, "inproj_prompt")
    zm_s, zt_s = _inproj(xs_pad, w_in_t, b_in, w_tail_t, b_tail, "inproj_sample")

    kv = _matmul(mem_prompt.reshape(bp * N_MEM, D_MODEL), w_kv_b, KV_TN, "mem_kv")
    mk_p = kv[:, :MEM_WIDTH].reshape(bp, N_MEM, MEM_HEADS, MEM_HEAD_DIM)
    mv_p = kv[:, MEM_WIDTH:].reshape(bp, N_MEM, MEM_HEADS, MEM_HEAD_DIM)

    conv_args = (w_dw[0], row(b_dw), row(g_cn), row(b_cn))
    conv_p, buf_p = _conv_group(zm_p, jnp.zeros((bp, CONV_HIST, CONV_WIDTH), F32), *conv_args,
                                bp, sp, 256, "conv_prompt")
    conv_s, buf_s_t = _conv_step_group(
        zm_s.reshape(bs, SAMPLE_PAD, Z_MAIN_COLS), jnp.transpose(state_conv[0], (1, 0, 2)),
        *conv_args, ss, 32, "conv_sample")
    buf_s = jnp.transpose(buf_s_t, (1, 0, 2))

    m_tile = lambda m: jnp.broadcast_to(
        jnp.pad(m, ((0, 0), (0, SUBLANES - MLSTM_HEADS)))[:, :, None], (m.shape[0], SUBLANES, LANES))
    g_mh_r = row(g_mh)
    h_p, c_p, n_p, m_p = _mlstm_group(
        zm_p, zt_p, jnp.zeros((bp, MLSTM_HEADS, MLSTM_HEAD_DIM, MLSTM_HEAD_DIM), F32),
        jnp.zeros((bp, MLSTM_HEADS, MLSTM_HEAD_DIM), F32), jnp.zeros((bp, SUBLANES, LANES), F32),
        g_mh_r, bp, sp, MLSTM_CHUNK, MLSTM_CHUNK, 1, "mlstm_prompt")
    h_s, c_s, n_s, m_s = _mlstm_group(
        zm_s, zt_s, state_mlstm_C[0], state_mlstm_n[0], m_tile(state_mlstm_m[0]),
        g_mh_r, bs, SAMPLE_PAD, SAMPLE_PAD, ss, 4, "mlstm_sample")

    mem_p = _memattn_heads(zt_p, kv, bp, sp, 1024, "memattn_prompt")
    packed = lambda c: c[0].reshape(bs, N_MEM * MEM_HEADS, MEM_HEAD_DIM)
    mem_s = _memattn_packed(zt_s, packed(cache_mem_k), packed(cache_mem_v), bs, SAMPLE_PAD, 8,
                            "memattn_sample")

    compact = lambda a: a.reshape(bs, SAMPLE_PAD, a.shape[-1])[:, :ss].reshape(ts, a.shape[-1])
    x1, eid, ew = _outproj_router(
        (conv_p, h_p, mem_p, x_prompt.reshape(tp, D_MODEL)),
        (conv_s.reshape(ts, CONV_WIDTH), compact(h_s), compact(mem_s), x_sample.reshape(ts, D_MODEL)),
        w_out_b, row(g_ln1), row(b_ln1), w_r_hi, w_r_lo, b_r)

    total = tp + ts
    stok, sdst, start, cnt = _dispatch_tables(eid, total)
    y = _moe(x1, stok, sdst, start, cnt, w_gate[0], w_up[0], w_down[0])
    g2, b2 = row(g_ln2), row(b_ln2)
    y_p = _final(x1, y, ew, g2, b2, 0, tp, "final_prompt").reshape(bp, sp, D_MODEL)
    y_s = _final(x1, y, ew, g2, b2, tp, ts, "final_sample").reshape(bs, ss, D_MODEL)

    return (y_p, y_s, buf_p[None], buf_s[None], c_p[None], c_s[None], n_p[None], n_s[None],
            m_p[:, :MLSTM_HEADS, 0][None], m_s[:, :MLSTM_HEADS, 0][None], mk_p[None], mv_p[None])
```

```python
import functools

import jax
import jax.numpy as jnp
from jax import lax
from jax.experimental import pallas as pl
from jax.experimental.pallas import tpu as pltpu

F32 = jnp.float32
BF16 = jnp.bfloat16

D_MODEL = 2048
CONV_WIDTH = 512
CONV_K = 31
CONV_HIST = CONV_K - 1
MLSTM_HEADS = 4
MLSTM_HEAD_DIM = 256
MLSTM_WIDTH = MLSTM_HEADS * MLSTM_HEAD_DIM
MLSTM_CHUNK = 128
MEM_HEADS = 4
MEM_HEAD_DIM = 128
MEM_WIDTH = MEM_HEADS * MEM_HEAD_DIM
N_MEM = 256
N_GROUPS = 8
EXPERTS_PER_GROUP = 8
N_EXPERTS = N_GROUPS * EXPERTS_PER_GROUP
D_EXPERT = 512
LN_EPS = 1e-5
DEPTH = 1
ALPHA = (2 * DEPTH) ** 0.25

LANES = 128
SUBLANES = 8
VMEM_LIMIT_BYTES = 56 * 1024 * 1024

Z_CONV_A = 0
Z_CONV_G = CONV_WIDTH
Z_Q = 2 * CONV_WIDTH
Z_K = Z_Q + MLSTM_WIDTH
Z_V = Z_K + MLSTM_WIDTH
Z_O = Z_V + MLSTM_WIDTH
Z_MAIN_COLS = Z_O + MLSTM_WIDTH
N_GATE_COLS = 2 * MLSTM_HEADS
ZT_QM = 0
ZT_GATE = MEM_WIDTH
Z_TAIL_COLS = MEM_WIDTH + LANES
INPROJ_TM = 1024
INPROJ_TN = 512
KV_TN = 256

SAMPLE_PAD = SUBLANES
ROUTER_LANES = LANES
MOE_ROWS = 128
OUT_TM = 256


def _params(*sem):
    return pltpu.CompilerParams(dimension_semantics=sem, vmem_limit_bytes=VMEM_LIMIT_BYTES)


def _inproj_kernel(x_ref, wm_ref, bm_ref, wt_ref, bt_ref, zm_ref, zt_ref, xb_ref, *, n_main):
    j = pl.program_id(1)

    @pl.when(j == 0)
    def _():
        xb_ref[...] = x_ref[...].astype(BF16)

    nt = (((1,), (1,)), ((), ()))

    @pl.when(j < n_main)
    def _():
        zm_ref[...] = lax.dot_general(xb_ref[...], wm_ref[...], nt,
                                      preferred_element_type=F32) + bm_ref[...]

    @pl.when(j == n_main)
    def _():
        zt_ref[...] = lax.dot_general(xb_ref[...], wt_ref[...], nt,
                                      preferred_element_type=F32) + bt_ref[...]


def _inproj(x, w_in_t, b_in, w_tail_t, b_tail, name):
    t, k = x.shape
    tm, tn = INPROJ_TM, INPROJ_TN
    n_main = Z_MAIN_COLS // tn
    main_col = lambda j: jnp.minimum(j, n_main - 1)
    return pl.pallas_call(
        functools.partial(_inproj_kernel, n_main=n_main),
        out_shape=(jax.ShapeDtypeStruct((t, Z_MAIN_COLS), F32),
                   jax.ShapeDtypeStruct((t, Z_TAIL_COLS), F32)),
        grid=(t // tm, n_main + 1),
        in_specs=[
            pl.BlockSpec((tm, k), lambda i, j: (i, 0)),
            pl.BlockSpec((tn, k), lambda i, j: (main_col(j), 0)),
            pl.BlockSpec((1, tn), lambda i, j: (0, main_col(j))),
            pl.BlockSpec((Z_TAIL_COLS, k), lambda i, j: (0, 0)),
            pl.BlockSpec((1, Z_TAIL_COLS), lambda i, j: (0, 0)),
        ],
        out_specs=(pl.BlockSpec((tm, tn), lambda i, j: (i, main_col(j))),
                   pl.BlockSpec((tm, Z_TAIL_COLS), lambda i, j: (i, 0))),
        scratch_shapes=[pltpu.VMEM((tm, k), BF16)],
        compiler_params=_params("arbitrary", "arbitrary"),
        name=name,
    )(x, w_in_t, b_in, w_tail_t, b_tail)


def _matmul_kernel(x_ref, w_ref, o_ref):
    o_ref[...] = jnp.dot(x_ref[...].astype(BF16), w_ref[...], preferred_element_type=F32)


def _matmul(x, w_bf16, tn, name):
    t, k = x.shape
    n = w_bf16.shape[1]
    return pl.pallas_call(
        _matmul_kernel,
        out_shape=jax.ShapeDtypeStruct((t, n), F32),
        grid=(n // tn,),
        in_specs=[pl.BlockSpec((t, k), lambda j: (0, 0)), pl.BlockSpec((k, tn), lambda j: (0, j))],
        out_specs=pl.BlockSpec((t, tn), lambda j: (0, j)),
        compiler_params=_params("arbitrary"),
        name=name,
    )(x, w_bf16)


def _layer_norm_rows(y, g, b):
    mu = jnp.mean(y, axis=-1, keepdims=True)
    yc = y - mu
    var = jnp.mean(yc * yc, axis=-1, keepdims=True)
    return yc * lax.rsqrt(var + LN_EPS) * g + b


CONV_ROWS = 64


def _conv_kernel(a_ref, g_ref, hist_ref, wdw_ref, bdw_ref, gcn_ref, bcn_ref, out_ref, nb_ref,
                 ubuf, shifted, *, tl):
    head = CONV_HIST + 2
    li = pl.program_id(1)

    @pl.when(li == 0)
    def _():
        ubuf[0:2, :] = jnp.zeros((2, CONV_WIDTH), F32)
        ubuf[2:head, :] = hist_ref[0]

    ubuf[head:head + tl, :] = a_ref[...] * jax.nn.sigmoid(g_ref[...])
    span = shifted.shape[1]
    for k in range(1, SUBLANES):
        shifted[k - 1] = ubuf[k:k + span, :]
    for r0 in range(0, tl, CONV_ROWS):
        acc = jnp.zeros((CONV_ROWS, CONV_WIDTH), F32) + bdw_ref[...]
        for j in range(CONV_K):
            lo, k = divmod(2 + j, SUBLANES)
            lo = lo * SUBLANES + r0
            src = ubuf if k == 0 else shifted.at[k - 1]
            acc = acc + wdw_ref[j:j + 1, :] * src[lo:lo + CONV_ROWS, :]
        y = _layer_norm_rows(acc, gcn_ref[...], bcn_ref[...])
        out_ref[r0:r0 + CONV_ROWS, :] = y * jax.nn.sigmoid(y)

    @pl.when(li == pl.num_programs(1) - 1)
    def _():
        nb_ref[0] = ubuf[2 + tl:head + tl, :]

    ubuf[2:head, :] = ubuf[2 + tl:head + tl, :]


def _conv_group(z, hist, w_dw, b_dw, g_cn, b_cn, batch, seq, tl, name):
    nl = seq // tl
    row = lambda b, l: (b * nl + l, 0)
    vec = pl.BlockSpec((1, CONV_WIDTH), lambda b, l: (0, 0))
    return pl.pallas_call(
        functools.partial(_conv_kernel, tl=tl),
        out_shape=(jax.ShapeDtypeStruct((batch * seq, CONV_WIDTH), F32),
                   jax.ShapeDtypeStruct((batch, CONV_HIST, CONV_WIDTH), F32)),
        grid=(batch, nl),
        in_specs=[
            pl.BlockSpec((tl, CONV_WIDTH), lambda b, l: (b * nl + l, Z_CONV_A // CONV_WIDTH)),
            pl.BlockSpec((tl, CONV_WIDTH), lambda b, l: (b * nl + l, Z_CONV_G // CONV_WIDTH)),
            pl.BlockSpec((1, CONV_HIST, CONV_WIDTH), lambda b, l: (b, 0, 0)),
            pl.BlockSpec((CONV_K, CONV_WIDTH), lambda b, l: (0, 0)),
            vec, vec, vec,
        ],
        out_specs=(pl.BlockSpec((tl, CONV_WIDTH), row),
                   pl.BlockSpec((1, CONV_HIST, CONV_WIDTH), lambda b, l: (b, 0, 0))),
        scratch_shapes=[pltpu.VMEM((CONV_HIST + 2 + tl, CONV_WIDTH), F32),
                        pltpu.VMEM((SUBLANES - 1, CONV_HIST + 2 + tl - SUBLANES, CONV_WIDTH), F32)],
        compiler_params=_params("arbitrary", "arbitrary"),
        name=name,
    )(z, z, hist, w_dw, b_dw, g_cn, b_cn)


def _conv_step_kernel(a_ref, g_ref, hist_ref, wdw_ref, bdw_ref, gcn_ref, bcn_ref, out_ref, nb_ref,
                      *, steps):
    u = [a_ref[:, t, :] * jax.nn.sigmoid(g_ref[:, t, :]) for t in range(steps)]
    full = lambda r: hist_ref[r] if r < CONV_HIST else u[r - CONV_HIST]
    for t in range(steps):
        acc = bdw_ref[...] + wdw_ref[0:1, :] * full(t)
        for j in range(1, CONV_K):
            acc = acc + wdw_ref[j:j + 1, :] * full(t + j)
        y = _layer_norm_rows(acc, gcn_ref[...], bcn_ref[...])
        out_ref[:, t, :] = y * jax.nn.sigmoid(y)
    for r in range(CONV_HIST):
        nb_ref[r] = full(r + steps)


def _conv_step_group(z3, hist_t, w_dw, b_dw, g_cn, b_cn, steps, bb, name):
    batch = z3.shape[0]
    vec = pl.BlockSpec((1, CONV_WIDTH), lambda b: (0, 0))
    hist_spec = pl.BlockSpec((CONV_HIST, bb, CONV_WIDTH), lambda b: (0, b, 0))
    return pl.pallas_call(
        functools.partial(_conv_step_kernel, steps=steps),
        out_shape=(jax.ShapeDtypeStruct((batch, steps, CONV_WIDTH), F32),
                   jax.ShapeDtypeStruct((CONV_HIST, batch, CONV_WIDTH), F32)),
        grid=(batch // bb,),
        in_specs=[
            pl.BlockSpec((bb, SAMPLE_PAD, CONV_WIDTH), lambda b: (b, 0, Z_CONV_A // CONV_WIDTH)),
            pl.BlockSpec((bb, SAMPLE_PAD, CONV_WIDTH), lambda b: (b, 0, Z_CONV_G // CONV_WIDTH)),
            hist_spec,
            pl.BlockSpec((CONV_K, CONV_WIDTH), lambda b: (0, 0)),
            vec, vec, vec,
        ],
        out_specs=(pl.BlockSpec((bb, steps, CONV_WIDTH), lambda b: (b, 0, 0)), hist_spec),
        compiler_params=_params("arbitrary"),
        name=name,
    )(z3, z3, hist_t, w_dw, b_dw, g_cn, b_cn)


def _col_to_row(col, eye):
    n = col.shape[0]
    return jnp.sum(jnp.where(eye, jnp.broadcast_to(col, (n, n)), 0.0), axis=0, keepdims=True)


def _mlstm_kernel(q_ref, k_ref, v_ref, o_ref, gate_ref, c0_ref, n0_ref, m0_ref, gmh_ref,
                  h_ref, c_ref, n_ref, m_ref, *, cl, valid, bb, single_chunk):
    if single_chunk:
        c_in, n_in, m_in = c0_ref, n0_ref, m0_ref
    else:
        c_in, n_in, m_in = c_ref, n_ref, m_ref

        @pl.when(pl.program_id(1) == 0)
        def _():
            c_ref[...] = c0_ref[...]
            n_ref[...] = n0_ref[...]
            m_ref[...] = m0_ref[...]

    rows = lax.broadcasted_iota(jnp.int32, (cl, cl), 0)
    cols = lax.broadcasted_iota(jnp.int32, (cl, cl), 1)
    eye = rows == cols
    tril = rows >= cols
    row_id = lax.broadcasted_iota(jnp.int32, (cl, 1), 0)
    dh = MLSTM_HEAD_DIM
    for bi, h in [(bi, h) for bi in range(bb) for h in range(MLSTM_HEADS)]:
        rs = slice(bi * cl, (bi + 1) * cl)
        sl = slice(h * dh, (h + 1) * dh)
        gates = gate_ref[rs, :]
        q = q_ref[rs, sl]
        k = k_ref[rs, sl] * (dh ** -0.5)
        v = v_ref[rs, sl]
        ip = gates[:, h:h + 1]
        fp = gates[:, MLSTM_HEADS + h:MLSTM_HEADS + h + 1]
        lf = jnp.minimum(fp, 0.0) - jnp.log1p(jnp.exp(-jnp.abs(fp)))
        if valid < cl:
            ip = jnp.where(row_id < valid, ip, -jnp.inf)
            lf = jnp.where(row_id < valid, lf, 0.0)
        lf_row = _col_to_row(lf, eye)
        ip_row = _col_to_row(ip, eye)
        b_col = jnp.sum(jnp.where(tril, jnp.broadcast_to(lf_row, (cl, cl)), 0.0), axis=1, keepdims=True)
        b_row = _col_to_row(b_col, eye)
        m_prev = m_in[bi, h:h + 1, 0:1]
        log_inter = b_col + m_prev
        log_intra = jnp.where(tril, b_col - b_row + ip_row, -jnp.inf)
        m_t = jnp.maximum(log_inter, jnp.max(log_intra, axis=1, keepdims=True))
        w_inter = jnp.exp(log_inter - m_t)
        qb = q.astype(BF16)
        kb = k.astype(BF16)
        vb = v.astype(BF16)
        s = lax.dot_general(qb, kb, (((1,), (1,)), ((), ())), preferred_element_type=F32)
        s = s * jnp.exp(log_intra - m_t)
        c_old = c_in[bi, h]
        n_old = n_in[bi, h:h + 1, :]
        num = (w_inter * jnp.dot(qb, c_old.astype(BF16), preferred_element_type=F32)
               + jnp.dot(s.astype(BF16), vb, preferred_element_type=F32))
        den = w_inter * jnp.sum(q * n_old, axis=1, keepdims=True) + jnp.sum(s, axis=1, keepdims=True)
        hh = num / jnp.maximum(jnp.abs(den), jnp.exp(-m_t))
        m_new = m_t[cl - 1:cl, :]
        b_last = b_col[cl - 1:cl, :]
        decay = jnp.exp(b_last + m_prev - m_new)
        w_s = jnp.exp(b_last - b_col + ip - m_new)
        kw = k * w_s
        c_ref[bi, h] = decay * c_old + lax.dot_general(
            kw.astype(BF16), vb, (((0,), (0,)), ((), ())), preferred_element_type=F32)
        n_ref[bi, h:h + 1, :] = decay * n_old + jnp.sum(kw, axis=0, keepdims=True)
        m_ref[bi, h:h + 1, :] = jnp.broadcast_to(m_new, (1, LANES))
        mu = jnp.mean(hh, axis=-1, keepdims=True)
        hc = hh - mu
        var = jnp.mean(hc * hc, axis=-1, keepdims=True)
        hn = hc * lax.rsqrt(var + LN_EPS) * gmh_ref[:, sl]
        h_ref[rs, sl] = hn * jax.nn.sigmoid(o_ref[rs, sl])
    if single_chunk:
        m_ref[:, MLSTM_HEADS:, :] = jnp.zeros((bb, SUBLANES - MLSTM_HEADS, LANES), F32)


def _mlstm_group(z_main, z_tail, c0, n0, m0, g_mh, batch, seq, cl, valid, bb, name):
    nc = seq // cl
    assert bb == 1 or nc == 1, "several sequences per step only for single-chunk sequences"
    rows = bb * cl
    zcol = lambda off: pl.BlockSpec((rows, MLSTM_WIDTH), lambda b, c: (b * nc + c, off // MLSTM_WIDTH))
    state = lambda shape: pl.BlockSpec((bb,) + shape, lambda b, c: (b,) + (0,) * len(shape))
    c_shape = (MLSTM_HEADS, MLSTM_HEAD_DIM, MLSTM_HEAD_DIM)
    n_shape = (MLSTM_HEADS, MLSTM_HEAD_DIM)
    m_shape = (SUBLANES, LANES)
    return pl.pallas_call(
        functools.partial(_mlstm_kernel, cl=cl, valid=valid, bb=bb, single_chunk=nc == 1),
        out_shape=(jax.ShapeDtypeStruct((batch * seq, MLSTM_WIDTH), F32),
                   jax.ShapeDtypeStruct((batch,) + c_shape, F32),
                   jax.ShapeDtypeStruct((batch,) + n_shape, F32),
                   jax.ShapeDtypeStruct((batch,) + m_shape, F32)),
        grid=(batch // bb, nc),
        in_specs=[
            zcol(Z_Q), zcol(Z_K), zcol(Z_V), zcol(Z_O),
            pl.BlockSpec((rows, LANES), lambda b, c: (b * nc + c, ZT_GATE // LANES)),
            state(c_shape), state(n_shape), state(m_shape),
            pl.BlockSpec((1, MLSTM_WIDTH), lambda b, c: (0, 0)),
        ],
        out_specs=(pl.BlockSpec((rows, MLSTM_WIDTH), lambda b, c: (b * nc + c, 0)),
                   state(c_shape), state(n_shape), state(m_shape)),
        compiler_params=_params("arbitrary", "arbitrary"),
        name=name,
    )(z_main, z_main, z_main, z_main, z_tail, c0, n0, m0, g_mh)


def _softmax_rows(s):
    e = jnp.exp(s - jnp.max(s, axis=-1, keepdims=True))
    return e / jnp.sum(e, axis=-1, keepdims=True)


_NT_DIMS = (((1,), (1,)), ((), ()))


def _memattn_head_kernel(q_ref, k_ref, v_ref, o_ref):
    s = lax.dot_general(q_ref[...].astype(BF16), k_ref[...].astype(BF16), _NT_DIMS,
                        preferred_element_type=F32) * (MEM_HEAD_DIM ** -0.5)
    p = _softmax_rows(s)
    o_ref[...] = jnp.dot(p.astype(BF16), v_ref[...].astype(BF16), preferred_element_type=F32)


def _memattn_heads(z_tail, kv, batch, seq, tq, name):
    nq = seq // tq
    dh = MEM_HEAD_DIM
    return pl.pallas_call(
        _memattn_head_kernel,
        out_shape=jax.ShapeDtypeStruct((batch * seq, MEM_WIDTH), F32),
        grid=(batch, MEM_HEADS, nq),
        in_specs=[pl.BlockSpec((tq, dh), lambda b, h, i: (b * nq + i, ZT_QM // dh + h)),
                  pl.BlockSpec((N_MEM, dh), lambda b, h, i: (b, h)),
                  pl.BlockSpec((N_MEM, dh), lambda b, h, i: (b, MEM_HEADS + h))],
        out_specs=pl.BlockSpec((tq, dh), lambda b, h, i: (b * nq + i, h)),
        compiler_params=_params("arbitrary", "arbitrary", "arbitrary"),
        name=name,
    )(z_tail, kv, kv)


def _memattn_packed_kernel(q_ref, k_ref, v_ref, o_ref, *, tq, bb):
    nh, dh = MEM_HEADS, MEM_HEAD_DIM
    shape = (nh * tq, N_MEM * nh)
    row_head = lax.broadcasted_iota(jnp.int32, shape, 0) // tq
    col_head = lax.broadcasted_iota(jnp.int32, shape, 1) % nh
    same_head = row_head == col_head
    for bi in range(bb):
        rs = slice(bi * tq, (bi + 1) * tq)
        q = jnp.concatenate([q_ref[rs, h * dh:(h + 1) * dh] for h in range(nh)], axis=0)
        s = lax.dot_general(q.astype(BF16), k_ref[bi].astype(BF16), _NT_DIMS,
                            preferred_element_type=F32) * (dh ** -0.5)
        p = _softmax_rows(jnp.where(same_head, s, -jnp.inf))
        o = jnp.dot(p.astype(BF16), v_ref[bi].astype(BF16), preferred_element_type=F32)
        for h in range(nh):
            o_ref[rs, h * dh:(h + 1) * dh] = o[h * tq:(h + 1) * tq, :]


def _memattn_packed(z_tail, mk, mv, batch, tq, bb, name):
    rows = bb * tq
    kv = pl.BlockSpec((bb, N_MEM * MEM_HEADS, MEM_HEAD_DIM), lambda b: (b, 0, 0))
    return pl.pallas_call(
        functools.partial(_memattn_packed_kernel, tq=tq, bb=bb),
        out_shape=jax.ShapeDtypeStruct((batch * tq, MEM_WIDTH), F32),
        grid=(batch // bb,),
        in_specs=[pl.BlockSpec((rows, MEM_WIDTH), lambda b: (b, ZT_QM // MEM_WIDTH)), kv, kv],
        out_specs=pl.BlockSpec((rows, MEM_WIDTH), lambda b: (b, 0)),
        compiler_params=_params("arbitrary"),
        name=name,
    )(z_tail, mk, mv)


def _route(logits):
    lane = lax.broadcasted_iota(jnp.int32, logits.shape, 1).astype(F32)
    neg = -jnp.inf
    first = lambda mask: jnp.min(jnp.where(mask, lane, float(LANES)), axis=1, keepdims=True)
    is_g = lane < N_GROUPS
    gl = jnp.where(is_g, logits, neg)
    g_max = jnp.max(gl, axis=1, keepdims=True)
    g_sel = first(gl == g_max)
    g_w = 1.0 / jnp.sum(jnp.exp(gl - g_max), axis=1, keepdims=True)
    lo = N_GROUPS + g_sel * EXPERTS_PER_GROUP
    in_grp = (lane >= lo) & (lane < lo + EXPERTS_PER_GROUP)
    el = jnp.where(in_grp, logits, neg)
    v1 = jnp.max(el, axis=1, keepdims=True)
    i1 = first(in_grp & (el == v1))
    rest = in_grp & (lane != i1)
    el2 = jnp.where(rest, logits, neg)
    v2 = jnp.max(el2, axis=1, keepdims=True)
    i2 = first(rest & (el2 == v2))
    t = jnp.exp(v2 - v1)
    w1 = g_w / (1.0 + t)
    w2 = g_w * t / (1.0 + t)
    lane_i = lax.broadcasted_iota(jnp.int32, logits.shape, 1)
    e1 = (i1 - N_GROUPS).astype(jnp.int32)
    e2 = (i2 - N_GROUPS).astype(jnp.int32)
    eid = jnp.where(lane_i == 0, e1, jnp.where(lane_i == 1, e2, 0))
    ew = jnp.where(lane_i == 0, w1, jnp.where(lane_i == 1, w2, 0.0))
    return eid, ew


def _outproj_kernel(cp, hp, mp, xp, cs, hs, ms, xs, wout, g1, b1, wr, br,
                    x1_ref, eid_ref, ew_ref, *, n_prompt):
    def body(c, h, m, x):
        groups = jnp.concatenate(
            [c[...].astype(BF16), h[...].astype(BF16), m[...].astype(BF16)], axis=1)
        mix = jnp.dot(groups, wout[...], preferred_element_type=F32)
        x1 = _layer_norm_rows(ALPHA * x[...] + mix, g1[...], b1[...])
        x1_ref[...] = x1
        xh = x1.astype(BF16)
        xl = (x1 - xh.astype(F32)).astype(BF16)
        tm = x1.shape[0]
        cross = jnp.dot(jnp.concatenate([xh, xl], axis=0), wr[...], preferred_element_type=F32)
        logits = ((cross[:tm, :ROUTER_LANES] + cross[:tm, ROUTER_LANES:])
                  + (cross[tm:, :ROUTER_LANES] + cross[tm:, ROUTER_LANES:])) + br[...]
        eid, ew = _route(logits)
        eid_ref[...] = eid
        ew_ref[...] = ew

    i = pl.program_id(0)
    pl.when(i < n_prompt)(lambda: body(cp, hp, mp, xp))
    pl.when(i >= n_prompt)(lambda: body(cs, hs, ms, xs))


def _outproj_router(prompt, sample, w_out_b, g1, b1, wr, br):
    tm = OUT_TM
    tp = prompt[0].shape[0]
    ts = sample[0].shape[0]
    n_p, n_s = tp // tm, ts // tm
    total = tp + ts
    widths = (CONV_WIDTH, MLSTM_WIDTH, MEM_WIDTH, D_MODEL)
    p_specs = [pl.BlockSpec((tm, w), lambda i: (jnp.minimum(i, n_p - 1), 0)) for w in widths]
    s_specs = [pl.BlockSpec((tm, w), lambda i: (jnp.maximum(i - n_p, 0), 0)) for w in widths]
    full = lambda shape: pl.BlockSpec(shape, lambda i: (0, 0))
    row = lambda w: pl.BlockSpec((tm, w), lambda i: (i, 0))
    return pl.pallas_call(
        functools.partial(_outproj_kernel, n_prompt=n_p),
        out_shape=(jax.ShapeDtypeStruct((total, D_MODEL), F32),
                   jax.ShapeDtypeStruct((total, ROUTER_LANES), jnp.int32),
                   jax.ShapeDtypeStruct((total, ROUTER_LANES), F32)),
        grid=(n_p + n_s,),
        in_specs=p_specs + s_specs + [
            full((D_MODEL, D_MODEL)), full((1, D_MODEL)), full((1, D_MODEL)),
            full((D_MODEL, 2 * ROUTER_LANES)), full((1, ROUTER_LANES)),
        ],
        out_specs=(row(D_MODEL), row(ROUTER_LANES), row(ROUTER_LANES)),
        compiler_params=_params("arbitrary"),
        name="outproj_router",
    )(*prompt, *sample, w_out_b, g1, b1, wr, br)


def _moe_kernel(stok_ref, sdst_ref, start_ref, cnt_ref, x_hbm, wg_ref, wu_ref, wd_ref, y_hbm,
                wgb, wub, wdb, xbuf, ybuf, gsem, ssem, state, *, dump_row):
    e = pl.program_id(0)
    last = pl.num_programs(0) - 1
    rc = MOE_ROWS

    def gather_start(base, slot):
        for r in range(rc):
            pltpu.make_async_copy(x_hbm.at[pl.ds(stok_ref[base + r], 1)],
                                  xbuf.at[slot, pl.ds(r, 1)], gsem.at[slot]).start()

    def gather_wait(slot):
        pltpu.make_async_copy(x_hbm.at[pl.ds(0, rc)], xbuf.at[slot], gsem.at[slot]).wait()

    def scatter_start(base, n_valid, slot):
        for r in range(rc):
            dst = jnp.where(r < n_valid, sdst_ref[base + r], dump_row + slot * rc + r)
            pltpu.make_async_copy(ybuf.at[slot, pl.ds(r, 1)], y_hbm.at[pl.ds(dst, 1)],
                                  ssem.at[slot]).start()

    def scatter_wait(slot):
        pltpu.make_async_copy(ybuf.at[slot], y_hbm.at[pl.ds(0, rc)], ssem.at[slot]).wait()

    @pl.when(e == 0)
    def _():
        state[0] = 0
        state[1] = 0
        state[2] = 0
        state[3] = 0
        state[4] = 0
        ybuf[...] = jnp.zeros(ybuf.shape, F32)
        pltpu.make_async_copy(ybuf.at[0], y_hbm.at[pl.ds(dump_row, rc)], ssem.at[0]).start()
        scatter_wait(0)
        gather_start(start_ref[0], 0)

    wgb[...] = wg_ref[0].astype(BF16)
    wub[...] = wu_ref[0].astype(BF16)
    wdb[...] = wd_ref[0].astype(BF16)
    start = start_ref[e]
    cnt = cnt_ref[e]
    n_chunks = jnp.maximum((cnt + rc - 1) // rc, 1)
    next_start = start_ref[jnp.minimum(e + 1, last)]

    def chunk_on(slot, c):
        other = 1 - slot
        base = start + c * rc
        gather_wait(slot)
        gather_start(jnp.where(c + 1 < n_chunks, base + rc, next_start), other)
        scatter_start(state[3], state[4], other)
        state[1 + other] = 1
        x = xbuf[slot].astype(BF16)
        hg = jnp.dot(x, wgb[...], preferred_element_type=F32)
        hu = jnp.dot(x, wub[...], preferred_element_type=F32)
        hid = (hg * jax.nn.sigmoid(hg) * hu).astype(BF16)

        @pl.when(state[1 + slot] == 1)
        def _():
            scatter_wait(slot)

        ybuf[slot] = jnp.dot(hid, wdb[...], preferred_element_type=F32)
        state[3] = base
        state[4] = cnt - c * rc
        state[0] = state[0] + 1

    def chunk(c, carry):
        parity = state[0] & 1
        for slot in range(2):
            pl.when(parity == slot)(functools.partial(chunk_on, slot, c))
        return carry

    lax.fori_loop(0, n_chunks, chunk, 0)

    @pl.when(e == last)
    def _():
        parity = state[0] & 1
        for slot in range(2):
            @pl.when(parity == slot)
            def _():
                gather_wait(slot)
                scatter_start(state[3], state[4], 1 - slot)
                scatter_wait(1 - slot)

                @pl.when(state[1 + slot] == 1)
                def _():
                    scatter_wait(slot)


def _moe(x1, stok, sdst, start, cnt, w_gate, w_up, w_down):
    total = x1.shape[0]
    dump_row = 2 * total
    wspec = lambda shape: pl.BlockSpec((1,) + shape, lambda e, *_: (e, 0, 0))
    grid_spec = pltpu.PrefetchScalarGridSpec(
        num_scalar_prefetch=4,
        grid=(N_EXPERTS,),
        in_specs=[pl.BlockSpec(memory_space=pl.ANY),
                  wspec((D_MODEL, D_EXPERT)), wspec((D_MODEL, D_EXPERT)), wspec((D_EXPERT, D_MODEL))],
        out_specs=pl.BlockSpec(memory_space=pl.ANY),
        scratch_shapes=[
            pltpu.VMEM((D_MODEL, D_EXPERT), BF16), pltpu.VMEM((D_MODEL, D_EXPERT), BF16),
            pltpu.VMEM((D_EXPERT, D_MODEL), BF16),
            pltpu.VMEM((2, MOE_ROWS, D_MODEL), F32), pltpu.VMEM((2, MOE_ROWS, D_MODEL), F32),
            pltpu.SemaphoreType.DMA((2,)), pltpu.SemaphoreType.DMA((2,)),
            pltpu.SMEM((5,), jnp.int32),
        ],
    )
    return pl.pallas_call(
        functools.partial(_moe_kernel, dump_row=dump_row),
        out_shape=jax.ShapeDtypeStruct((2 * total + 2 * MOE_ROWS, D_MODEL), F32),
        grid_spec=grid_spec,
        compiler_params=_params("arbitrary"),
        name="moe_experts",
    )(stok, sdst, start, cnt, x1, w_gate, w_up, w_down)


def _final_kernel(x1_ref, y0_ref, y1_ref, ew_ref, g2, b2, o_ref):
    ew = ew_ref[...]
    ffn = ew[:, 0:1] * y0_ref[...] + ew[:, 1:2] * y1_ref[...]
    o_ref[...] = _layer_norm_rows(ALPHA * x1_ref[...] + ffn, g2[...], b2[...])


def _final(x1, y, ew, g2, b2, row0, rows, name):
    tm = OUT_TM
    total = x1.shape[0]
    off = row0 // tm
    k1 = total // tm
    vec = pl.BlockSpec((1, D_MODEL), lambda i: (0, 0))
    return pl.pallas_call(
        _final_kernel,
        out_shape=jax.ShapeDtypeStruct((rows, D_MODEL), F32),
        grid=(rows // tm,),
        in_specs=[pl.BlockSpec((tm, D_MODEL), lambda i: (i + off, 0)),
                  pl.BlockSpec((tm, D_MODEL), lambda i: (i + off, 0)),
                  pl.BlockSpec((tm, D_MODEL), lambda i: (i + off + k1, 0)),
                  pl.BlockSpec((tm, ROUTER_LANES), lambda i: (i + off, 0)),
                  vec, vec],
        out_specs=pl.BlockSpec((tm, D_MODEL), lambda i: (i, 0)),
        compiler_params=_params("arbitrary"),
        name=name,
    )(x1, y, y, ew, g2, b2)


def _dispatch_tables(eid, total):
    flat_e = eid[:, :2].reshape(-1)
    order = jnp.argsort(flat_e, stable=True).astype(jnp.int32)
    stok = order >> 1
    sdst = (order & 1) * total + stok
    experts = jnp.arange(N_EXPERTS, dtype=jnp.int32)
    cnt = jnp.sum((flat_e[:, None] == experts[None, :]).astype(jnp.int32), axis=0)
    start = jnp.cumsum(cnt) - cnt
    pad = jnp.zeros((MOE_ROWS,), jnp.int32)
    return jnp.concatenate([stok, pad]), jnp.concatenate([sdst, pad]), start, cnt


def kernel(x_prompt, x_sample, mem_prompt, state_conv, state_mlstm_C, state_mlstm_n, state_mlstm_m,
           cache_mem_k, cache_mem_v, w_in, b_in, w_dw, b_dw, g_cn, b_cn, g_mh, w_mk, w_mv, w_out,
           g_ln1, b_ln1, w_rg, b_rg, w_re, b_re, w_gate, w_up, w_down, g_ln2, b_ln2):
    bp, sp, _ = x_prompt.shape
    bs, ss, _ = x_sample.shape
    tp, ts = bp * sp, bs * ss

    gate_hi = Z_MAIN_COLS + N_GATE_COLS
    tail = lambda wt: jnp.concatenate(
        [wt[gate_hi:], wt[Z_MAIN_COLS:gate_hi],
         jnp.zeros((LANES - N_GATE_COLS,) + wt.shape[1:], wt.dtype)], axis=0)
    w_in_t = jnp.transpose(w_in[0]).astype(BF16)
    w_tail_t = tail(w_in_t)
    b_tail = tail(b_in[0])[None, :]
    w_kv_b = jnp.concatenate([w_mk[0], w_mv[0]], axis=-1).astype(BF16)
    w_out_b = w_out[0].astype(BF16)
    w_r = jnp.concatenate([w_rg[0], w_re[0],
                           jnp.zeros((D_MODEL, ROUTER_LANES - N_GROUPS - N_EXPERTS), F32)], axis=-1)
    w_r_hi = w_r.astype(BF16)
    w_r_lo = (w_r - w_r_hi.astype(F32)).astype(BF16)
    w_r_split = jnp.concatenate([w_r_hi, w_r_lo], axis=1)
    b_r = jnp.concatenate([b_rg[0], b_re[0],
                           jnp.zeros((ROUTER_LANES - N_GROUPS - N_EXPERTS,), F32)])[None, :]
    row = lambda a: a[0][None, :]

    xs_pad = jnp.pad(x_sample, ((0, 0), (0, SAMPLE_PAD - ss), (0, 0))).reshape(bs * SAMPLE_PAD, D_MODEL)
    zm_p, zt_p = _inproj(x_prompt.reshape(tp, D_MODEL), w_in_t, b_in, w_tail_t, b_tail, "inproj_prompt")
    zm_s, zt_s = _inproj(xs_pad, w_in_t, b_in, w_tail_t, b_tail, "inproj_sample")

    kv = _matmul(mem_prompt.reshape(bp * N_MEM, D_MODEL), w_kv_b, KV_TN, "mem_kv")
    mk_p = kv[:, :MEM_WIDTH].reshape(bp, N_MEM, MEM_HEADS, MEM_HEAD_DIM)
    mv_p = kv[:, MEM_WIDTH:].reshape(bp, N_MEM, MEM_HEADS, MEM_HEAD_DIM)

    conv_args = (w_dw[0], row(b_dw), row(g_cn), row(b_cn))
    conv_p, buf_p = _conv_group(zm_p, jnp.zeros((bp, CONV_HIST, CONV_WIDTH), F32), *conv_args,
                                bp, sp, 256, "conv_prompt")
    conv_s, buf_s_t = _conv_step_group(
        zm_s.reshape(bs, SAMPLE_PAD, Z_MAIN_COLS), jnp.transpose(state_conv[0], (1, 0, 2)),
        *conv_args, ss, 32, "conv_sample")
    buf_s = jnp.transpose(buf_s_t, (1, 0, 2))

    m_tile = lambda m: jnp.broadcast_to(
        jnp.pad(m, ((0, 0), (0, SUBLANES - MLSTM_HEADS)))[:, :, None], (m.shape[0], SUBLANES, LANES))
    g_mh_r = row(g_mh)
    h_p, c_p, n_p, m_p = _mlstm_group(
        zm_p, zt_p, jnp.zeros((bp, MLSTM_HEADS, MLSTM_HEAD_DIM, MLSTM_HEAD_DIM), F32),
        jnp.zeros((bp, MLSTM_HEADS, MLSTM_HEAD_DIM), F32), jnp.zeros((bp, SUBLANES, LANES), F32),
        g_mh_r, bp, sp, MLSTM_CHUNK, MLSTM_CHUNK, 1, "mlstm_prompt")
    h_s, c_s, n_s, m_s = _mlstm_group(
        zm_s, zt_s, state_mlstm_C[0], state_mlstm_n[0], m_tile(state_mlstm_m[0]),
        g_mh_r, bs, SAMPLE_PAD, SAMPLE_PAD, ss, 8, "mlstm_sample")

    mem_p = _memattn_heads(zt_p, kv, bp, sp, 1024, "memattn_prompt")
    packed = lambda c: c[0].reshape(bs, N_MEM * MEM_HEADS, MEM_HEAD_DIM)
    mem_s = _memattn_packed(zt_s, packed(cache_mem_k), packed(cache_mem_v), bs, SAMPLE_PAD, 8,
                            "memattn_sample")

    compact = lambda a: a.reshape(bs, SAMPLE_PAD, a.shape[-1])[:, :ss].reshape(ts, a.shape[-1])
    x1, eid, ew = _outproj_router(
        (conv_p, h_p, mem_p, x_prompt.reshape(tp, D_MODEL)),
        (conv_s.reshape(ts, CONV_WIDTH), compact(h_s), compact(mem_s), x_sample.reshape(ts, D_MODEL)),
        w_out_b, row(g_ln1), row(b_ln1), w_r_split, b_r)

    total = tp + ts
    stok, sdst, start, cnt = _dispatch_tables(eid, total)
    y = _moe(x1, stok, sdst, start, cnt, w_gate[0], w_up[0], w_down[0])
    g2, b2 = row(g_ln2), row(b_ln2)
    y_p = _final(x1, y, ew, g2, b2, 0, tp, "final_prompt").reshape(bp, sp, D_MODEL)
    y_s = _final(x1, y, ew, g2, b2, tp, ts, "final_sample").reshape(bs, ss, D_MODEL)

    return (y_p, y_s, buf_p[None], buf_s[None], c_p[None], c_s[None], n_p[None], n_s[None],
            m_p[:, :MLSTM_HEADS, 0][None], m_s[:, :MLSTM_HEADS, 0][None], mk_p[None], mv_p[None])
```

```python
import functools

import jax
import jax.numpy as jnp
from jax import lax
from jax.experimental import pallas as pl
from jax.experimental.pallas import tpu as pltpu

F32 = jnp.float32
BF16 = jnp.bfloat16

D_MODEL = 2048
CONV_WIDTH = 512
CONV_K = 31
CONV_HIST = CONV_K - 1
MLSTM_HEADS = 4
MLSTM_HEAD_DIM = 256
MLSTM_WIDTH = MLSTM_HEADS * MLSTM_HEAD_DIM
MLSTM_CHUNK = 128
MEM_HEADS = 4
MEM_HEAD_DIM = 128
MEM_WIDTH = MEM_HEADS * MEM_HEAD_DIM
N_MEM = 256
N_GROUPS = 8
EXPERTS_PER_GROUP = 8
N_EXPERTS = N_GROUPS * EXPERTS_PER_GROUP
D_EXPERT = 512
LN_EPS = 1e-5
DEPTH = 1
ALPHA = (2 * DEPTH) ** 0.25

LANES = 128
SUBLANES = 8
VMEM_LIMIT_BYTES = 56 * 1024 * 1024

Z_CONV_A = 0
Z_CONV_G = CONV_WIDTH
Z_Q = 2 * CONV_WIDTH
Z_K = Z_Q + MLSTM_WIDTH
Z_V = Z_K + MLSTM_WIDTH
Z_O = Z_V + MLSTM_WIDTH
Z_MAIN_COLS = Z_O + MLSTM_WIDTH
N_GATE_COLS = 2 * MLSTM_HEADS
ZT_QM = 0
ZT_GATE = MEM_WIDTH
Z_TAIL_COLS = MEM_WIDTH + LANES
INPROJ_TM = 1024
INPROJ_TN = 1024
KV_TN = 256

SAMPLE_PAD = SUBLANES
ROUTER_LANES = LANES
MOE_ROWS = 128
ROW_DMA_PRIORITY = 1
OUT_TM = 256


def _params(*sem):
    return pltpu.CompilerParams(dimension_semantics=sem, vmem_limit_bytes=VMEM_LIMIT_BYTES)


def _inproj_kernel(x_ref, wm_ref, bm_ref, wt_ref, bt_ref, zm_ref, zt_ref, xb_ref, *, n_main):
    j = pl.program_id(1)

    @pl.when(j == 0)
    def _():
        xb_ref[...] = x_ref[...].astype(BF16)

    nt = (((1,), (1,)), ((), ()))

    @pl.when(j < n_main)
    def _():
        zm_ref[...] = lax.dot_general(xb_ref[...], wm_ref[...], nt,
                                      preferred_element_type=F32) + bm_ref[...]

    @pl.when(j == n_main)
    def _():
        zt_ref[...] = lax.dot_general(xb_ref[...], wt_ref[...], nt,
                                      preferred_element_type=F32) + bt_ref[...]


def _inproj(x, w_in_t, b_in, w_tail_t, b_tail, name):
    t, k = x.shape
    tm, tn = INPROJ_TM, INPROJ_TN
    n_main = Z_MAIN_COLS // tn
    main_col = lambda j: jnp.minimum(j, n_main - 1)
    return pl.pallas_call(
        functools.partial(_inproj_kernel, n_main=n_main),
        out_shape=(jax.ShapeDtypeStruct((t, Z_MAIN_COLS), F32),
                   jax.ShapeDtypeStruct((t, Z_TAIL_COLS), F32)),
        grid=(t // tm, n_main + 1),
        in_specs=[
            pl.BlockSpec((tm, k), lambda i, j: (i, 0)),
            pl.BlockSpec((tn, k), lambda i, j: (main_col(j), 0)),
            pl.BlockSpec((1, tn), lambda i, j: (0, main_col(j))),
            pl.BlockSpec((Z_TAIL_COLS, k), lambda i, j: (0, 0)),
            pl.BlockSpec((1, Z_TAIL_COLS), lambda i, j: (0, 0)),
        ],
        out_specs=(pl.BlockSpec((tm, tn), lambda i, j: (i, main_col(j))),
                   pl.BlockSpec((tm, Z_TAIL_COLS), lambda i, j: (i, 0))),
        scratch_shapes=[pltpu.VMEM((tm, k), BF16)],
        compiler_params=_params("arbitrary", "arbitrary"),
        name=name,
    )(x, w_in_t, b_in, w_tail_t, b_tail)


def _matmul_kernel(x_ref, w_ref, o_ref):
    o_ref[...] = jnp.dot(x_ref[...].astype(BF16), w_ref[...], preferred_element_type=F32)


def _matmul(x, w_bf16, tn, name):
    t, k = x.shape
    n = w_bf16.shape[1]
    return pl.pallas_call(
        _matmul_kernel,
        out_shape=jax.ShapeDtypeStruct((t, n), F32),
        grid=(n // tn,),
        in_specs=[pl.BlockSpec((t, k), lambda j: (0, 0)), pl.BlockSpec((k, tn), lambda j: (0, j))],
        out_specs=pl.BlockSpec((t, tn), lambda j: (0, j)),
        compiler_params=_params("arbitrary"),
        name=name,
    )(x, w_bf16)


def _layer_norm_rows(y, g, b):
    mu = jnp.mean(y, axis=-1, keepdims=True)
    yc = y - mu
    var = jnp.mean(yc * yc, axis=-1, keepdims=True)
    return yc * lax.rsqrt(var + LN_EPS) * g + b


CONV_ROWS = 64


def _conv_kernel(a_ref, g_ref, hist_ref, wdw_ref, bdw_ref, gcn_ref, bcn_ref, out_ref, nb_ref,
                 ubuf, shifted, *, tl):
    head = CONV_HIST + 2
    li = pl.program_id(1)

    @pl.when(li == 0)
    def _():
        ubuf[0:2, :] = jnp.zeros((2, CONV_WIDTH), F32)
        ubuf[2:head, :] = hist_ref[0]

    ubuf[head:head + tl, :] = a_ref[...] * jax.nn.sigmoid(g_ref[...])
    span = shifted.shape[1]
    for k in range(1, SUBLANES):
        shifted[k - 1] = ubuf[k:k + span, :]
    for r0 in range(0, tl, CONV_ROWS):
        acc = jnp.zeros((CONV_ROWS, CONV_WIDTH), F32) + bdw_ref[...]
        for j in range(CONV_K):
            lo, k = divmod(2 + j, SUBLANES)
            lo = lo * SUBLANES + r0
            src = ubuf if k == 0 else shifted.at[k - 1]
            acc = acc + wdw_ref[j:j + 1, :] * src[lo:lo + CONV_ROWS, :]
        y = _layer_norm_rows(acc, gcn_ref[...], bcn_ref[...])
        out_ref[r0:r0 + CONV_ROWS, :] = y * jax.nn.sigmoid(y)

    @pl.when(li == pl.num_programs(1) - 1)
    def _():
        nb_ref[0] = ubuf[2 + tl:head + tl, :]

    ubuf[2:head, :] = ubuf[2 + tl:head + tl, :]


def _conv_group(z, hist, w_dw, b_dw, g_cn, b_cn, batch, seq, tl, name):
    nl = seq // tl
    row = lambda b, l: (b * nl + l, 0)
    vec = pl.BlockSpec((1, CONV_WIDTH), lambda b, l: (0, 0))
    return pl.pallas_call(
        functools.partial(_conv_kernel, tl=tl),
        out_shape=(jax.ShapeDtypeStruct((batch * seq, CONV_WIDTH), F32),
                   jax.ShapeDtypeStruct((batch, CONV_HIST, CONV_WIDTH), F32)),
        grid=(batch, nl),
        in_specs=[
            pl.BlockSpec((tl, CONV_WIDTH), lambda b, l: (b * nl + l, Z_CONV_A // CONV_WIDTH)),
            pl.BlockSpec((tl, CONV_WIDTH), lambda b, l: (b * nl + l, Z_CONV_G // CONV_WIDTH)),
            pl.BlockSpec((1, CONV_HIST, CONV_WIDTH), lambda b, l: (b, 0, 0)),
            pl.BlockSpec((CONV_K, CONV_WIDTH), lambda b, l: (0, 0)),
            vec, vec, vec,
        ],
        out_specs=(pl.BlockSpec((tl, CONV_WIDTH), row),
                   pl.BlockSpec((1, CONV_HIST, CONV_WIDTH), lambda b, l: (b, 0, 0))),
        scratch_shapes=[pltpu.VMEM((CONV_HIST + 2 + tl, CONV_WIDTH), F32),
                        pltpu.VMEM((SUBLANES - 1, CONV_HIST + 2 + tl - SUBLANES, CONV_WIDTH), F32)],
        compiler_params=_params("arbitrary", "arbitrary"),
        name=name,
    )(z, z, hist, w_dw, b_dw, g_cn, b_cn)


def _conv_step_kernel(a_ref, g_ref, hist_ref, wdw_ref, bdw_ref, gcn_ref, bcn_ref, out_ref, nb_ref,
                      *, steps):
    u = [a_ref[:, t, :] * jax.nn.sigmoid(g_ref[:, t, :]) for t in range(steps)]
    full = lambda r: hist_ref[r] if r < CONV_HIST else u[r - CONV_HIST]
    for t in range(steps):
        acc = bdw_ref[...] + wdw_ref[0:1, :] * full(t)
        for j in range(1, CONV_K):
            acc = acc + wdw_ref[j:j + 1, :] * full(t + j)
        y = _layer_norm_rows(acc, gcn_ref[...], bcn_ref[...])
        out_ref[:, t, :] = y * jax.nn.sigmoid(y)
    for r in range(CONV_HIST):
        nb_ref[r] = full(r + steps)


def _conv_step_group(z3, hist_t, w_dw, b_dw, g_cn, b_cn, steps, bb, name):
    batch = z3.shape[0]
    vec = pl.BlockSpec((1, CONV_WIDTH), lambda b: (0, 0))
    hist_spec = pl.BlockSpec((CONV_HIST, bb, CONV_WIDTH), lambda b: (0, b, 0))
    return pl.pallas_call(
        functools.partial(_conv_step_kernel, steps=steps),
        out_shape=(jax.ShapeDtypeStruct((batch, steps, CONV_WIDTH), F32),
                   jax.ShapeDtypeStruct((CONV_HIST, batch, CONV_WIDTH), F32)),
        grid=(batch // bb,),
        in_specs=[
            pl.BlockSpec((bb, SAMPLE_PAD, CONV_WIDTH), lambda b: (b, 0, Z_CONV_A // CONV_WIDTH)),
            pl.BlockSpec((bb, SAMPLE_PAD, CONV_WIDTH), lambda b: (b, 0, Z_CONV_G // CONV_WIDTH)),
            hist_spec,
            pl.BlockSpec((CONV_K, CONV_WIDTH), lambda b: (0, 0)),
            vec, vec, vec,
        ],
        out_specs=(pl.BlockSpec((bb, steps, CONV_WIDTH), lambda b: (b, 0, 0)), hist_spec),
        compiler_params=_params("arbitrary"),
        name=name,
    )(z3, z3, hist_t, w_dw, b_dw, g_cn, b_cn)


def _col_to_row(col, eye):
    n = col.shape[0]
    return jnp.sum(jnp.where(eye, jnp.broadcast_to(col, (n, n)), 0.0), axis=0, keepdims=True)


def _mlstm_kernel(q_ref, k_ref, v_ref, o_ref, gate_ref, c0_ref, n0_ref, m0_ref, gmh_ref,
                  h_ref, c_ref, n_ref, m_ref, *, cl, valid, bb, single_chunk):
    if single_chunk:
        c_in, n_in, m_in = c0_ref, n0_ref, m0_ref
    else:
        c_in, n_in, m_in = c_ref, n_ref, m_ref

        @pl.when(pl.program_id(1) == 0)
        def _():
            c_ref[...] = c0_ref[...]
            n_ref[...] = n0_ref[...]
            m_ref[...] = m0_ref[...]

    rows = lax.broadcasted_iota(jnp.int32, (cl, cl), 0)
    cols = lax.broadcasted_iota(jnp.int32, (cl, cl), 1)
    eye = rows == cols
    tril = rows >= cols
    row_id = lax.broadcasted_iota(jnp.int32, (cl, 1), 0)
    dh = MLSTM_HEAD_DIM
    for bi, h in [(bi, h) for bi in range(bb) for h in range(MLSTM_HEADS)]:
        rs = slice(bi * cl, (bi + 1) * cl)
        sl = slice(h * dh, (h + 1) * dh)
        gates = gate_ref[rs, :]
        q = q_ref[rs, sl]
        k = k_ref[rs, sl] * (dh ** -0.5)
        v = v_ref[rs, sl]
        ip = gates[:, h:h + 1]
        fp = gates[:, MLSTM_HEADS + h:MLSTM_HEADS + h + 1]
        lf = jnp.minimum(fp, 0.0) - jnp.log1p(jnp.exp(-jnp.abs(fp)))
        if valid < cl:
            ip = jnp.where(row_id < valid, ip, -jnp.inf)
            lf = jnp.where(row_id < valid, lf, 0.0)
        lf_row = _col_to_row(lf, eye)
        ip_row = _col_to_row(ip, eye)
        b_col = jnp.sum(jnp.where(tril, jnp.broadcast_to(lf_row, (cl, cl)), 0.0), axis=1, keepdims=True)
        b_row = _col_to_row(b_col, eye)
        m_prev = m_in[bi, h:h + 1, 0:1]
        log_inter = b_col + m_prev
        log_intra = jnp.where(tril, b_col - b_row + ip_row, -jnp.inf)
        m_t = jnp.maximum(log_inter, jnp.max(log_intra, axis=1, keepdims=True))
        w_inter = jnp.exp(log_inter - m_t)
        qb = q.astype(BF16)
        kb = k.astype(BF16)
        vb = v.astype(BF16)
        s = lax.dot_general(qb, kb, (((1,), (1,)), ((), ())), preferred_element_type=F32)
        s = s * jnp.exp(log_intra - m_t)
        c_old = c_in[bi, h]
        n_old = n_in[bi, h:h + 1, :]
        num = (w_inter * jnp.dot(qb, c_old.astype(BF16), preferred_element_type=F32)
               + jnp.dot(s.astype(BF16), vb, preferred_element_type=F32))
        den = w_inter * jnp.sum(q * n_old, axis=1, keepdims=True) + jnp.sum(s, axis=1, keepdims=True)
        hh = num / jnp.maximum(jnp.abs(den), jnp.exp(-m_t))
        m_new = m_t[cl - 1:cl, :]
        b_last = b_col[cl - 1:cl, :]
        decay = jnp.exp(b_last + m_prev - m_new)
        w_s = jnp.exp(b_last - b_col + ip - m_new)
        kw = k * w_s
        c_ref[bi, h] = decay * c_old + lax.dot_general(
            kw.astype(BF16), vb, (((0,), (0,)), ((), ())), preferred_element_type=F32)
        n_ref[bi, h:h + 1, :] = decay * n_old + jnp.sum(kw, axis=0, keepdims=True)
        m_ref[bi, h:h + 1, :] = jnp.broadcast_to(m_new, (1, LANES))
        mu = jnp.mean(hh, axis=-1, keepdims=True)
        hc = hh - mu
        var = jnp.mean(hc * hc, axis=-1, keepdims=True)
        hn = hc * lax.rsqrt(var + LN_EPS) * gmh_ref[:, sl]
        h_ref[rs, sl] = hn * jax.nn.sigmoid(o_ref[rs, sl])
    if single_chunk:
        m_ref[:, MLSTM_HEADS:, :] = jnp.zeros((bb, SUBLANES - MLSTM_HEADS, LANES), F32)


def _mlstm_group(z_main, z_tail, c0, n0, m0, g_mh, batch, seq, cl, valid, bb, name):
    nc = seq // cl
    assert bb == 1 or nc == 1, "several sequences per step only for single-chunk sequences"
    rows = bb * cl
    zcol = lambda off: pl.BlockSpec((rows, MLSTM_WIDTH), lambda b, c: (b * nc + c, off // MLSTM_WIDTH))
    state = lambda shape: pl.BlockSpec((bb,) + shape, lambda b, c: (b,) + (0,) * len(shape))
    c_shape = (MLSTM_HEADS, MLSTM_HEAD_DIM, MLSTM_HEAD_DIM)
    n_shape = (MLSTM_HEADS, MLSTM_HEAD_DIM)
    m_shape = (SUBLANES, LANES)
    return pl.pallas_call(
        functools.partial(_mlstm_kernel, cl=cl, valid=valid, bb=bb, single_chunk=nc == 1),
        out_shape=(jax.ShapeDtypeStruct((batch * seq, MLSTM_WIDTH), F32),
                   jax.ShapeDtypeStruct((batch,) + c_shape, F32),
                   jax.ShapeDtypeStruct((batch,) + n_shape, F32),
                   jax.ShapeDtypeStruct((batch,) + m_shape, F32)),
        grid=(batch // bb, nc),
        in_specs=[
            zcol(Z_Q), zcol(Z_K), zcol(Z_V), zcol(Z_O),
            pl.BlockSpec((rows, LANES), lambda b, c: (b * nc + c, ZT_GATE // LANES)),
            state(c_shape), state(n_shape), state(m_shape),
            pl.BlockSpec((1, MLSTM_WIDTH), lambda b, c: (0, 0)),
        ],
        out_specs=(pl.BlockSpec((rows, MLSTM_WIDTH), lambda b, c: (b * nc + c, 0)),
                   state(c_shape), state(n_shape), state(m_shape)),
        compiler_params=_params("arbitrary", "arbitrary"),
        name=name,
    )(z_main, z_main, z_main, z_main, z_tail, c0, n0, m0, g_mh)


def _softmax_rows(s):
    e = jnp.exp(s - jnp.max(s, axis=-1, keepdims=True))
    return e / jnp.sum(e, axis=-1, keepdims=True)


_NT_DIMS = (((1,), (1,)), ((), ()))


def _memattn_head_kernel(q_ref, k_ref, v_ref, o_ref):
    s = lax.dot_general(q_ref[...].astype(BF16), k_ref[...].astype(BF16), _NT_DIMS,
                        preferred_element_type=F32) * (MEM_HEAD_DIM ** -0.5)
    p = _softmax_rows(s)
    o_ref[...] = jnp.dot(p.astype(BF16), v_ref[...].astype(BF16), preferred_element_type=F32)


def _memattn_heads(z_tail, kv, batch, seq, tq, name):
    nq = seq // tq
    dh = MEM_HEAD_DIM
    return pl.pallas_call(
        _memattn_head_kernel,
        out_shape=jax.ShapeDtypeStruct((batch * seq, MEM_WIDTH), F32),
        grid=(batch, MEM_HEADS, nq),
        in_specs=[pl.BlockSpec((tq, dh), lambda b, h, i: (b * nq + i, ZT_QM // dh + h)),
                  pl.BlockSpec((N_MEM, dh), lambda b, h, i: (b, h)),
                  pl.BlockSpec((N_MEM, dh), lambda b, h, i: (b, MEM_HEADS + h))],
        out_specs=pl.BlockSpec((tq, dh), lambda b, h, i: (b * nq + i, h)),
        compiler_params=_params("arbitrary", "arbitrary", "arbitrary"),
        name=name,
    )(z_tail, kv, kv)


def _memattn_packed_kernel(q_ref, k_ref, v_ref, o_ref, *, tq, bb):
    nh, dh = MEM_HEADS, MEM_HEAD_DIM
    shape = (nh * tq, N_MEM * nh)
    row_head = lax.broadcasted_iota(jnp.int32, shape, 0) // tq
    col_head = lax.broadcasted_iota(jnp.int32, shape, 1) % nh
    same_head = row_head == col_head
    for bi in range(bb):
        rs = slice(bi * tq, (bi + 1) * tq)
        q = jnp.concatenate([q_ref[rs, h * dh:(h + 1) * dh] for h in range(nh)], axis=0)
        s = lax.dot_general(q.astype(BF16), k_ref[bi].astype(BF16), _NT_DIMS,
                            preferred_element_type=F32) * (dh ** -0.5)
        p = _softmax_rows(jnp.where(same_head, s, -jnp.inf))
        o = jnp.dot(p.astype(BF16), v_ref[bi].astype(BF16), preferred_element_type=F32)
        for h in range(nh):
            o_ref[rs, h * dh:(h + 1) * dh] = o[h * tq:(h + 1) * tq, :]


def _memattn_packed(z_tail, mk, mv, batch, tq, bb, name):
    rows = bb * tq
    kv = pl.BlockSpec((bb, N_MEM * MEM_HEADS, MEM_HEAD_DIM), lambda b: (b, 0, 0))
    return pl.pallas_call(
        functools.partial(_memattn_packed_kernel, tq=tq, bb=bb),
        out_shape=jax.ShapeDtypeStruct((batch * tq, MEM_WIDTH), F32),
        grid=(batch // bb,),
        in_specs=[pl.BlockSpec((rows, MEM_WIDTH), lambda b: (b, ZT_QM // MEM_WIDTH)), kv, kv],
        out_specs=pl.BlockSpec((rows, MEM_WIDTH), lambda b: (b, 0)),
        compiler_params=_params("arbitrary"),
        name=name,
    )(z_tail, mk, mv)


def _route(logits):
    lane = lax.broadcasted_iota(jnp.int32, logits.shape, 1).astype(F32)
    neg = -jnp.inf
    first = lambda mask: jnp.min(jnp.where(mask, lane, float(LANES)), axis=1, keepdims=True)
    is_g = lane < N_GROUPS
    gl = jnp.where(is_g, logits, neg)
    g_max = jnp.max(gl, axis=1, keepdims=True)
    g_sel = first(gl == g_max)
    g_w = 1.0 / jnp.sum(jnp.exp(gl - g_max), axis=1, keepdims=True)
    lo = N_GROUPS + g_sel * EXPERTS_PER_GROUP
    in_grp = (lane >= lo) & (lane < lo + EXPERTS_PER_GROUP)
    el = jnp.where(in_grp, logits, neg)
    v1 = jnp.max(el, axis=1, keepdims=True)
    i1 = first(in_grp & (el == v1))
    rest = in_grp & (lane != i1)
    el2 = jnp.where(rest, logits, neg)
    v2 = jnp.max(el2, axis=1, keepdims=True)
    i2 = first(rest & (el2 == v2))
    t = jnp.exp(v2 - v1)
    w1 = g_w / (1.0 + t)
    w2 = g_w * t / (1.0 + t)
    lane_i = lax.broadcasted_iota(jnp.int32, logits.shape, 1)
    e1 = (i1 - N_GROUPS).astype(jnp.int32)
    e2 = (i2 - N_GROUPS).astype(jnp.int32)
    eid = jnp.where(lane_i == 0, e1, jnp.where(lane_i == 1, e2, 0))
    ew = jnp.where(lane_i == 0, w1, jnp.where(lane_i == 1, w2, 0.0))
    return eid, ew


def _outproj_kernel(cp, hp, mp, xp, cs, hs, ms, xs, wout, g1, b1, wr, br,
                    x1_ref, eid_ref, ew_ref, *, n_prompt):
    def rows(c, h, m, x, rs):
        groups = jnp.concatenate(
            [c[rs, :].astype(BF16), h[rs, :].astype(BF16), m[rs, :].astype(BF16)], axis=1)
        mix = jnp.dot(groups, wout[...], preferred_element_type=F32)
        x1 = _layer_norm_rows(ALPHA * x[rs, :] + mix, g1[...], b1[...])
        x1_ref[rs, :] = x1
        xh = x1.astype(BF16)
        xl = (x1 - xh.astype(F32)).astype(BF16)
        n = x1.shape[0]
        cross = jnp.dot(jnp.concatenate([xh, xl], axis=0), wr[...], preferred_element_type=F32)
        logits = ((cross[:n, :ROUTER_LANES] + cross[:n, ROUTER_LANES:])
                  + (cross[n:, :ROUTER_LANES] + cross[n:, ROUTER_LANES:])) + br[...]
        eid, ew = _route(logits)
        eid_ref[rs, :] = eid
        ew_ref[rs, :] = ew

    def body(c, h, m, x):
        rows(c, h, m, x, slice(None))

    i = pl.program_id(0)
    pl.when(i < n_prompt)(lambda: body(cp, hp, mp, xp))
    pl.when(i >= n_prompt)(lambda: body(cs, hs, ms, xs))


def _outproj_router(prompt, sample, w_out_b, g1, b1, wr, br):
    tm = OUT_TM
    tp = prompt[0].shape[0]
    ts = sample[0].shape[0]
    n_p, n_s = tp // tm, ts // tm
    total = tp + ts
    widths = (CONV_WIDTH, MLSTM_WIDTH, MEM_WIDTH, D_MODEL)
    p_specs = [pl.BlockSpec((tm, w), lambda i: (jnp.minimum(i, n_p - 1), 0)) for w in widths]
    s_specs = [pl.BlockSpec((tm, w), lambda i: (jnp.maximum(i - n_p, 0), 0)) for w in widths]
    full = lambda shape: pl.BlockSpec(shape, lambda i: (0, 0))
    row = lambda w: pl.BlockSpec((tm, w), lambda i: (i, 0))
    return pl.pallas_call(
        functools.partial(_outproj_kernel, n_prompt=n_p),
        out_shape=(jax.ShapeDtypeStruct((total, D_MODEL), F32),
                   jax.ShapeDtypeStruct((total, ROUTER_LANES), jnp.int32),
                   jax.ShapeDtypeStruct((total, ROUTER_LANES), F32)),
        grid=(n_p + n_s,),
        in_specs=p_specs + s_specs + [
            full((D_MODEL, D_MODEL)), full((1, D_MODEL)), full((1, D_MODEL)),
            full((D_MODEL, 2 * ROUTER_LANES)), full((1, ROUTER_LANES)),
        ],
        out_specs=(row(D_MODEL), row(ROUTER_LANES), row(ROUTER_LANES)),
        compiler_params=_params("arbitrary"),
        name="outproj_router",
    )(*prompt, *sample, w_out_b, g1, b1, wr, br)


def _moe_kernel(stok_ref, sdst_ref, start_ref, cnt_ref, x_hbm, wg_ref, wu_ref, wd_ref, y_hbm,
                wgb, wub, wdb, xbuf, ybuf, gsem, ssem, state, *, dump_row):
    e = pl.program_id(0)
    last = pl.num_programs(0) - 1
    rc = MOE_ROWS

    def gather_start(base, slot):
        for r in range(rc):
            pltpu.make_async_copy(x_hbm.at[pl.ds(stok_ref[base + r], 1)],
                                  xbuf.at[slot, pl.ds(r, 1)], gsem.at[slot]).start(priority=ROW_DMA_PRIORITY)

    def gather_wait(slot):
        pltpu.make_async_copy(x_hbm.at[pl.ds(0, rc)], xbuf.at[slot], gsem.at[slot]).wait()

    def scatter_start(base, n_valid, slot):
        for r in range(rc):
            dst = jnp.where(r < n_valid, sdst_ref[base + r], dump_row + slot * rc + r)
            pltpu.make_async_copy(ybuf.at[slot, pl.ds(r, 1)], y_hbm.at[pl.ds(dst, 1)],
                                  ssem.at[slot]).start(priority=ROW_DMA_PRIORITY)

    def scatter_wait(slot):
        pltpu.make_async_copy(ybuf.at[slot], y_hbm.at[pl.ds(0, rc)], ssem.at[slot]).wait()

    @pl.when(e == 0)
    def _():
        state[0] = 0
        state[1] = 0
        state[2] = 0
        state[3] = 0
        state[4] = 0
        ybuf[...] = jnp.zeros(ybuf.shape, F32)
        pltpu.make_async_copy(ybuf.at[0], y_hbm.at[pl.ds(dump_row, rc)], ssem.at[0]).start()
        scatter_wait(0)
        gather_start(start_ref[0], 0)

    wgb[...] = wg_ref[0].astype(BF16)
    wub[...] = wu_ref[0].astype(BF16)
    wdb[...] = wd_ref[0].astype(BF16)
    start = start_ref[e]
    cnt = cnt_ref[e]
    n_chunks = jnp.maximum((cnt + rc - 1) // rc, 1)
    next_start = start_ref[jnp.minimum(e + 1, last)]

    def chunk_on(slot, c):
        other = 1 - slot
        base = start + c * rc
        gather_wait(slot)
        gather_start(jnp.where(c + 1 < n_chunks, base + rc, next_start), other)
        scatter_start(state[3], state[4], other)
        state[1 + other] = 1
        x = xbuf[slot].astype(BF16)
        hg = jnp.dot(x, wgb[...], preferred_element_type=F32)
        hu = jnp.dot(x, wub[...], preferred_element_type=F32)
        hid = (hg * jax.nn.sigmoid(hg) * hu).astype(BF16)

        @pl.when(state[1 + slot] == 1)
        def _():
            scatter_wait(slot)

        ybuf[slot] = jnp.dot(hid, wdb[...], preferred_element_type=F32)
        state[3] = base
        state[4] = cnt - c * rc
        state[0] = state[0] + 1

    def chunk(c, carry):
        parity = state[0] & 1
        for slot in range(2):
            pl.when(parity == slot)(functools.partial(chunk_on, slot, c))
        return carry

    lax.fori_loop(0, n_chunks, chunk, 0)

    @pl.when(e == last)
    def _():
        parity = state[0] & 1
        for slot in range(2):
            @pl.when(parity == slot)
            def _():
                gather_wait(slot)
                scatter_start(state[3], state[4], 1 - slot)
                scatter_wait(1 - slot)

                @pl.when(state[1 + slot] == 1)
                def _():
                    scatter_wait(slot)


def _moe(x1, stok, sdst, start, cnt, w_gate, w_up, w_down):
    total = x1.shape[0]
    dump_row = 2 * total
    wspec = lambda shape: pl.BlockSpec((1,) + shape, lambda e, *_: (e, 0, 0))
    grid_spec = pltpu.PrefetchScalarGridSpec(
        num_scalar_prefetch=4,
        grid=(N_EXPERTS,),
        in_specs=[pl.BlockSpec(memory_space=pl.ANY),
                  wspec((D_MODEL, D_EXPERT)), wspec((D_MODEL, D_EXPERT)), wspec((D_EXPERT, D_MODEL))],
        out_specs=pl.BlockSpec(memory_space=pl.ANY),
        scratch_shapes=[
            pltpu.VMEM((D_MODEL, D_EXPERT), BF16), pltpu.VMEM((D_MODEL, D_EXPERT), BF16),
            pltpu.VMEM((D_EXPERT, D_MODEL), BF16),
            pltpu.VMEM((2, MOE_ROWS, D_MODEL), F32), pltpu.VMEM((2, MOE_ROWS, D_MODEL), F32),
            pltpu.SemaphoreType.DMA((2,)), pltpu.SemaphoreType.DMA((2,)),
            pltpu.SMEM((5,), jnp.int32),
        ],
    )
    return pl.pallas_call(
        functools.partial(_moe_kernel, dump_row=dump_row),
        out_shape=jax.ShapeDtypeStruct((2 * total + 2 * MOE_ROWS, D_MODEL), F32),
        grid_spec=grid_spec,
        compiler_params=_params("arbitrary"),
        name="moe_experts",
    )(stok, sdst, start, cnt, x1, w_gate, w_up, w_down)


def _final_kernel(x1_ref, y0_ref, y1_ref, ew_ref, g2, b2, o_ref):
    ew = ew_ref[...]
    ffn = ew[:, 0:1] * y0_ref[...] + ew[:, 1:2] * y1_ref[...]
    o_ref[...] = _layer_norm_rows(ALPHA * x1_ref[...] + ffn, g2[...], b2[...])


def _final(x1, y, ew, g2, b2, row0, rows, name):
    tm = OUT_TM
    total = x1.shape[0]
    off = row0 // tm
    k1 = total // tm
    vec = pl.BlockSpec((1, D_MODEL), lambda i: (0, 0))
    return pl.pallas_call(
        _final_kernel,
        out_shape=jax.ShapeDtypeStruct((rows, D_MODEL), F32),
        grid=(rows // tm,),
        in_specs=[pl.BlockSpec((tm, D_MODEL), lambda i: (i + off, 0)),
                  pl.BlockSpec((tm, D_MODEL), lambda i: (i + off, 0)),
                  pl.BlockSpec((tm, D_MODEL), lambda i: (i + off + k1, 0)),
                  pl.BlockSpec((tm, ROUTER_LANES), lambda i: (i + off, 0)),
                  vec, vec],
        out_specs=pl.BlockSpec((tm, D_MODEL), lambda i: (i, 0)),
        compiler_params=_params("arbitrary"),
        name=name,
    )(x1, y, y, ew, g2, b2)


def _dispatch_tables(eid, total):
    flat_e = eid[:, :2].reshape(-1)
    order = jnp.argsort(flat_e, stable=True).astype(jnp.int32)
    stok = order >> 1
    sdst = (order & 1) * total + stok
    experts = jnp.arange(N_EXPERTS, dtype=jnp.int32)
    cnt = jnp.sum((flat_e[:, None] == experts[None, :]).astype(jnp.int32), axis=0)
    start = jnp.cumsum(cnt) - cnt
    pad = jnp.zeros((MOE_ROWS,), jnp.int32)
    return jnp.concatenate([stok, pad]), jnp.concatenate([sdst, pad]), start, cnt


def kernel(x_prompt, x_sample, mem_prompt, state_conv, state_mlstm_C, state_mlstm_n, state_mlstm_m,
           cache_mem_k, cache_mem_v, w_in, b_in, w_dw, b_dw, g_cn, b_cn, g_mh, w_mk, w_mv, w_out,
           g_ln1, b_ln1, w_rg, b_rg, w_re, b_re, w_gate, w_up, w_down, g_ln2, b_ln2):
    bp, sp, _ = x_prompt.shape
    bs, ss, _ = x_sample.shape
    tp, ts = bp * sp, bs * ss

    gate_hi = Z_MAIN_COLS + N_GATE_COLS
    tail = lambda wt: jnp.concatenate(
        [wt[gate_hi:], wt[Z_MAIN_COLS:gate_hi],
         jnp.zeros((LANES - N_GATE_COLS,) + wt.shape[1:], wt.dtype)], axis=0)
    w_in_t = jnp.transpose(w_in[0]).astype(BF16)
    w_tail_t = tail(w_in_t)
    b_tail = tail(b_in[0])[None, :]
    w_kv_b = jnp.concatenate([w_mk[0], w_mv[0]], axis=-1).astype(BF16)
    w_out_b = w_out[0].astype(BF16)
    w_r = jnp.concatenate([w_rg[0], w_re[0],
                           jnp.zeros((D_MODEL, ROUTER_LANES - N_GROUPS - N_EXPERTS), F32)], axis=-1)
    w_r_hi = w_r.astype(BF16)
    w_r_lo = (w_r - w_r_hi.astype(F32)).astype(BF16)
    w_r_split = jnp.concatenate([w_r_hi, w_r_lo], axis=1)
    b_r = jnp.concatenate([b_rg[0], b_re[0],
                           jnp.zeros((ROUTER_LANES - N_GROUPS - N_EXPERTS,), F32)])[None, :]
    row = lambda a: a[0][None, :]

    xs_pad = jnp.pad(x_sample, ((0, 0), (0, SAMPLE_PAD - ss), (0, 0))).reshape(bs * SAMPLE_PAD, D_MODEL)
    zm_p, zt_p = _inproj(x_prompt.reshape(tp, D_MODEL), w_in_t, b_in, w_tail_t, b_tail, "inproj_prompt")
    zm_s, zt_s = _inproj(xs_pad, w_in_t, b_in, w_tail_t, b_tail, "inproj_sample")

    kv = _matmul(mem_prompt.reshape(bp * N_MEM, D_MODEL), w_kv_b, KV_TN, "mem_kv")
    mk_p = kv[:, :MEM_WIDTH].reshape(bp, N_MEM, MEM_HEADS, MEM_HEAD_DIM)
    mv_p = kv[:, MEM_WIDTH:].reshape(bp, N_MEM, MEM_HEADS, MEM_HEAD_DIM)

    conv_args = (w_dw[0], row(b_dw), row(g_cn), row(b_cn))
    conv_p, buf_p = _conv_group(zm_p, jnp.zeros((bp, CONV_HIST, CONV_WIDTH), F32), *conv_args,
                                bp, sp, 256, "conv_prompt")
    conv_s, buf_s_t = _conv_step_group(
        zm_s.reshape(bs, SAMPLE_PAD, Z_MAIN_COLS), jnp.transpose(state_conv[0], (1, 0, 2)),
        *conv_args, ss, 32, "conv_sample")
    buf_s = jnp.transpose(buf_s_t, (1, 0, 2))

    m_tile = lambda m: jnp.broadcast_to(
        jnp.pad(m, ((0, 0), (0, SUBLANES - MLSTM_HEADS)))[:, :, None], (m.shape[0], SUBLANES, LANES))
    g_mh_r = row(g_mh)
    h_p, c_p, n_p, m_p = _mlstm_group(
        zm_p, zt_p, jnp.zeros((bp, MLSTM_HEADS, MLSTM_HEAD_DIM, MLSTM_HEAD_DIM), F32),
        jnp.zeros((bp, MLSTM_HEADS, MLSTM_HEAD_DIM), F32), jnp.zeros((bp, SUBLANES, LANES), F32),
        g_mh_r, bp, sp, MLSTM_CHUNK, MLSTM_CHUNK, 1, "mlstm_prompt")
    h_s, c_s, n_s, m_s = _mlstm_group(
        zm_s, zt_s, state_mlstm_C[0], state_mlstm_n[0], m_tile(state_mlstm_m[0]),
        g_mh_r, bs, SAMPLE_PAD, SAMPLE_PAD, ss, 8, "mlstm_sample")

    mem_p = _memattn_heads(zt_p, kv, bp, sp, sp, "memattn_prompt")
    packed = lambda c: c[0].reshape(bs, N_MEM * MEM_HEADS, MEM_HEAD_DIM)
    mem_s = _memattn_packed(zt_s, packed(cache_mem_k), packed(cache_mem_v), bs, SAMPLE_PAD, 8,
                            "memattn_sample")

    compact = lambda a: a.reshape(bs, SAMPLE_PAD, a.shape[-1])[:, :ss].reshape(ts, a.shape[-1])
    x1, eid, ew = _outproj_router(
        (conv_p, h_p, mem_p, x_prompt.reshape(tp, D_MODEL)),
        (conv_s.reshape(ts, CONV_WIDTH), compact(h_s), compact(mem_s), x_sample.reshape(ts, D_MODEL)),
        w_out_b, row(g_ln1), row(b_ln1), w_r_split, b_r)

    total = tp + ts
    stok, sdst, start, cnt = _dispatch_tables(eid, total)
    y = _moe(x1, stok, sdst, start, cnt, w_gate[0], w_up[0], w_down[0])
    g2, b2 = row(g_ln2), row(b_ln2)
    y_p = _final(x1, y, ew, g2, b2, 0, tp, "final_prompt").reshape(bp, sp, D_MODEL)
    y_s = _final(x1, y, ew, g2, b2, tp, ts, "final_sample").reshape(bs, ss, D_MODEL)

    return (y_p, y_s, buf_p[None], buf_s[None], c_p[None], c_s[None], n_p[None], n_s[None],
            m_p[:, :MLSTM_HEADS, 0][None], m_s[:, :MLSTM_HEADS, 0][None], mk_p[None], mv_p[None])
```

```python
import functools

import jax
import jax.numpy as jnp
from jax import lax
from jax.experimental import pallas as pl
from jax.experimental.pallas import tpu as pltpu

F32 = jnp.float32
BF16 = jnp.bfloat16

D_MODEL = 2048
CONV_WIDTH = 512
CONV_K = 31
CONV_HIST = CONV_K - 1
MLSTM_HEADS = 4
MLSTM_HEAD_DIM = 256
MLSTM_WIDTH = MLSTM_HEADS * MLSTM_HEAD_DIM
MLSTM_CHUNK = 128
MEM_HEADS = 4
MEM_HEAD_DIM = 128
MEM_WIDTH = MEM_HEADS * MEM_HEAD_DIM
N_MEM = 256
N_GROUPS = 8
EXPERTS_PER_GROUP = 8
N_EXPERTS = N_GROUPS * EXPERTS_PER_GROUP
D_EXPERT = 512
LN_EPS = 1e-5
DEPTH = 1
ALPHA = (2 * DEPTH) ** 0.25

LANES = 128
SUBLANES = 8
VMEM_LIMIT_BYTES = 56 * 1024 * 1024

Z_CONV_A = 0
Z_CONV_G = CONV_WIDTH
Z_Q = 2 * CONV_WIDTH
Z_K = Z_Q + MLSTM_WIDTH
Z_V = Z_K + MLSTM_WIDTH
Z_O = Z_V + MLSTM_WIDTH
Z_MAIN_COLS = Z_O + MLSTM_WIDTH
N_GATE_COLS = 2 * MLSTM_HEADS
ZT_QM = 0
ZT_GATE = MEM_WIDTH
Z_TAIL_COLS = MEM_WIDTH + LANES
INPROJ_TM = 1024
INPROJ_TN = 1024
KV_TN = 256

SAMPLE_PAD = SUBLANES
ROUTER_LANES = LANES
MOE_ROWS = 128
ROW_DMA_PRIORITY = 1
OUT_TM = 256


def _params(*sem):
    return pltpu.CompilerParams(dimension_semantics=sem, vmem_limit_bytes=VMEM_LIMIT_BYTES)


def _inproj_kernel(x_ref, wm_ref, bm_ref, wt_ref, bt_ref, zm_ref, zt_ref, xb_ref, *, n_main):
    j = pl.program_id(1)

    @pl.when(j == 0)
    def _():
        xb_ref[...] = x_ref[...].astype(BF16)

    nt = (((1,), (1,)), ((), ()))

    @pl.when(j < n_main)
    def _():
        zm_ref[...] = lax.dot_general(xb_ref[...], wm_ref[...], nt,
                                      preferred_element_type=F32) + bm_ref[...]

    @pl.when(j == n_main)
    def _():
        zt_ref[...] = lax.dot_general(xb_ref[...], wt_ref[...], nt,
                                      preferred_element_type=F32) + bt_ref[...]


def _inproj(x, w_in_t, b_in, w_tail_t, b_tail, name):
    t, k = x.shape
    tm, tn = INPROJ_TM, INPROJ_TN
    n_main = Z_MAIN_COLS // tn
    main_col = lambda j: jnp.minimum(j, n_main - 1)
    return pl.pallas_call(
        functools.partial(_inproj_kernel, n_main=n_main),
        out_shape=(jax.ShapeDtypeStruct((t, Z_MAIN_COLS), F32),
                   jax.ShapeDtypeStruct((t, Z_TAIL_COLS), F32)),
        grid=(t // tm, n_main + 1),
        in_specs=[
            pl.BlockSpec((tm, k), lambda i, j: (i, 0)),
            pl.BlockSpec((tn, k), lambda i, j: (main_col(j), 0)),
            pl.BlockSpec((1, tn), lambda i, j: (0, main_col(j))),
            pl.BlockSpec((Z_TAIL_COLS, k), lambda i, j: (0, 0)),
            pl.BlockSpec((1, Z_TAIL_COLS), lambda i, j: (0, 0)),
        ],
        out_specs=(pl.BlockSpec((tm, tn), lambda i, j: (i, main_col(j))),
                   pl.BlockSpec((tm, Z_TAIL_COLS), lambda i, j: (i, 0))),
        scratch_shapes=[pltpu.VMEM((tm, k), BF16)],
        compiler_params=_params("arbitrary", "arbitrary"),
        name=name,
    )(x, w_in_t, b_in, w_tail_t, b_tail)


def _matmul_kernel(x_ref, w_ref, o_ref):
    o_ref[...] = jnp.dot(x_ref[...].astype(BF16), w_ref[...], preferred_element_type=F32)


def _matmul(x, w_bf16, tn, name):
    t, k = x.shape
    n = w_bf16.shape[1]
    return pl.pallas_call(
        _matmul_kernel,
        out_shape=jax.ShapeDtypeStruct((t, n), F32),
        grid=(n // tn,),
        in_specs=[pl.BlockSpec((t, k), lambda j: (0, 0)), pl.BlockSpec((k, tn), lambda j: (0, j))],
        out_specs=pl.BlockSpec((t, tn), lambda j: (0, j)),
        compiler_params=_params("arbitrary"),
        name=name,
    )(x, w_bf16)


def _layer_norm_rows(y, g, b):
    mu = jnp.mean(y, axis=-1, keepdims=True)
    yc = y - mu
    var = jnp.mean(yc * yc, axis=-1, keepdims=True)
    return yc * lax.rsqrt(var + LN_EPS) * g + b


CONV_ROWS = 64


def _conv_kernel(a_ref, g_ref, hist_ref, wdw_ref, bdw_ref, gcn_ref, bcn_ref, out_ref, nb_ref,
                 ubuf, shifted, *, tl):
    head = CONV_HIST + 2
    li = pl.program_id(1)

    @pl.when(li == 0)
    def _():
        ubuf[0:2, :] = jnp.zeros((2, CONV_WIDTH), F32)
        ubuf[2:head, :] = hist_ref[0]

    ubuf[head:head + tl, :] = a_ref[...] * jax.nn.sigmoid(g_ref[...])
    span = shifted.shape[1]
    for k in range(1, SUBLANES):
        shifted[k - 1] = ubuf[k:k + span, :]
    for r0 in range(0, tl, CONV_ROWS):
        acc = jnp.zeros((CONV_ROWS, CONV_WIDTH), F32) + bdw_ref[...]
        for j in range(CONV_K):
            lo, k = divmod(2 + j, SUBLANES)
            lo = lo * SUBLANES + r0
            src = ubuf if k == 0 else shifted.at[k - 1]
            acc = acc + wdw_ref[j:j + 1, :] * src[lo:lo + CONV_ROWS, :]
        y = _layer_norm_rows(acc, gcn_ref[...], bcn_ref[...])
        out_ref[r0:r0 + CONV_ROWS, :] = y * jax.nn.sigmoid(y)

    @pl.when(li == pl.num_programs(1) - 1)
    def _():
        nb_ref[0] = ubuf[2 + tl:head + tl, :]

    ubuf[2:head, :] = ubuf[2 + tl:head + tl, :]


def _conv_group(z, hist, w_dw, b_dw, g_cn, b_cn, batch, seq, tl, name):
    nl = seq // tl
    row = lambda b, l: (b * nl + l, 0)
    vec = pl.BlockSpec((1, CONV_WIDTH), lambda b, l: (0, 0))
    return pl.pallas_call(
        functools.partial(_conv_kernel, tl=tl),
        out_shape=(jax.ShapeDtypeStruct((batch * seq, CONV_WIDTH), F32),
                   jax.ShapeDtypeStruct((batch, CONV_HIST, CONV_WIDTH), F32)),
        grid=(batch, nl),
        in_specs=[
            pl.BlockSpec((tl, CONV_WIDTH), lambda b, l: (b * nl + l, Z_CONV_A // CONV_WIDTH)),
            pl.BlockSpec((tl, CONV_WIDTH), lambda b, l: (b * nl + l, Z_CONV_G // CONV_WIDTH)),
            pl.BlockSpec((1, CONV_HIST, CONV_WIDTH), lambda b, l: (b, 0, 0)),
            pl.BlockSpec((CONV_K, CONV_WIDTH), lambda b, l: (0, 0)),
            vec, vec, vec,
        ],
        out_specs=(pl.BlockSpec((tl, CONV_WIDTH), row),
                   pl.BlockSpec((1, CONV_HIST, CONV_WIDTH), lambda b, l: (b, 0, 0))),
        scratch_shapes=[pltpu.VMEM((CONV_HIST + 2 + tl, CONV_WIDTH), F32),
                        pltpu.VMEM((SUBLANES - 1, CONV_HIST + 2 + tl - SUBLANES, CONV_WIDTH), F32)],
        compiler_params=_params("arbitrary", "arbitrary"),
        name=name,
    )(z, z, hist, w_dw, b_dw, g_cn, b_cn)


def _conv_step_kernel(a_ref, g_ref, hist_ref, wdw_ref, bdw_ref, gcn_ref, bcn_ref, out_ref, nb_ref,
                      *, steps):
    u = [a_ref[:, t, :] * jax.nn.sigmoid(g_ref[:, t, :]) for t in range(steps)]
    full = lambda r: hist_ref[r] if r < CONV_HIST else u[r - CONV_HIST]
    for t in range(steps):
        acc = bdw_ref[...] + wdw_ref[0:1, :] * full(t)
        for j in range(1, CONV_K):
            acc = acc + wdw_ref[j:j + 1, :] * full(t + j)
        y = _layer_norm_rows(acc, gcn_ref[...], bcn_ref[...])
        out_ref[:, t, :] = y * jax.nn.sigmoid(y)
    for r in range(CONV_HIST):
        nb_ref[r] = full(r + steps)


def _conv_step_group(z3, hist_t, w_dw, b_dw, g_cn, b_cn, steps, bb, name):
    batch = z3.shape[0]
    vec = pl.BlockSpec((1, CONV_WIDTH), lambda b: (0, 0))
    hist_spec = pl.BlockSpec((CONV_HIST, bb, CONV_WIDTH), lambda b: (0, b, 0))
    return pl.pallas_call(
        functools.partial(_conv_step_kernel, steps=steps),
        out_shape=(jax.ShapeDtypeStruct((batch, steps, CONV_WIDTH), F32),
                   jax.ShapeDtypeStruct((CONV_HIST, batch, CONV_WIDTH), F32)),
        grid=(batch // bb,),
        in_specs=[
            pl.BlockSpec((bb, SAMPLE_PAD, CONV_WIDTH), lambda b: (b, 0, Z_CONV_A // CONV_WIDTH)),
            pl.BlockSpec((bb, SAMPLE_PAD, CONV_WIDTH), lambda b: (b, 0, Z_CONV_G // CONV_WIDTH)),
            hist_spec,
            pl.BlockSpec((CONV_K, CONV_WIDTH), lambda b: (0, 0)),
            vec, vec, vec,
        ],
        out_specs=(pl.BlockSpec((bb, steps, CONV_WIDTH), lambda b: (b, 0, 0)), hist_spec),
        compiler_params=_params("arbitrary"),
        name=name,
    )(z3, z3, hist_t, w_dw, b_dw, g_cn, b_cn)


def _col_to_row(col, eye):
    n = col.shape[0]
    return jnp.sum(jnp.where(eye, jnp.broadcast_to(col, (n, n)), 0.0), axis=0, keepdims=True)


def _mlstm_kernel(q_ref, k_ref, v_ref, o_ref, gate_ref, c0_ref, n0_ref, m0_ref, gmh_ref,
                  h_ref, c_ref, n_ref, m_ref, *, cl, valid, bb, single_chunk):
    if single_chunk:
        c_in, n_in, m_in = c0_ref, n0_ref, m0_ref
    else:
        c_in, n_in, m_in = c_ref, n_ref, m_ref

        @pl.when(pl.program_id(1) == 0)
        def _():
            c_ref[...] = c0_ref[...]
            n_ref[...] = n0_ref[...]
            m_ref[...] = m0_ref[...]

    rows = lax.broadcasted_iota(jnp.int32, (cl, cl), 0)
    cols = lax.broadcasted_iota(jnp.int32, (cl, cl), 1)
    eye = rows == cols
    tril = rows >= cols
    row_id = lax.broadcasted_iota(jnp.int32, (cl, 1), 0)
    dh = MLSTM_HEAD_DIM
    for bi, h in [(bi, h) for bi in range(bb) for h in range(MLSTM_HEADS)]:
        rs = slice(bi * cl, (bi + 1) * cl)
        sl = slice(h * dh, (h + 1) * dh)
        gates = gate_ref[rs, :]
        q = q_ref[rs, sl]
        k = k_ref[rs, sl] * (dh ** -0.5)
        v = v_ref[rs, sl]
        ip = gates[:, h:h + 1]
        fp = gates[:, MLSTM_HEADS + h:MLSTM_HEADS + h + 1]
        lf = jnp.minimum(fp, 0.0) - jnp.log1p(jnp.exp(-jnp.abs(fp)))
        if valid < cl:
            ip = jnp.where(row_id < valid, ip, -jnp.inf)
            lf = jnp.where(row_id < valid, lf, 0.0)
        lf_row = _col_to_row(lf, eye)
        ip_row = _col_to_row(ip, eye)
        b_col = jnp.sum(jnp.where(tril, jnp.broadcast_to(lf_row, (cl, cl)), 0.0), axis=1, keepdims=True)
        b_row = _col_to_row(b_col, eye)
        m_prev = m_in[bi, h:h + 1, 0:1]
        log_inter = b_col + m_prev
        log_intra = jnp.where(tril, b_col - b_row + ip_row, -jnp.inf)
        m_t = jnp.maximum(log_inter, jnp.max(log_intra, axis=1, keepdims=True))
        w_inter = jnp.exp(log_inter - m_t)
        qb = q.astype(BF16)
        kb = k.astype(BF16)
        vb = v.astype(BF16)
        s = lax.dot_general(qb, kb, (((1,), (1,)), ((), ())), preferred_element_type=F32)
        s = s * jnp.exp(log_intra - m_t)
        c_old = c_in[bi, h]
        n_old = n_in[bi, h:h + 1, :]
        num = (w_inter * jnp.dot(qb, c_old.astype(BF16), preferred_element_type=F32)
               + jnp.dot(s.astype(BF16), vb, preferred_element_type=F32))
        den = w_inter * jnp.sum(q * n_old, axis=1, keepdims=True) + jnp.sum(s, axis=1, keepdims=True)
        hh = num / jnp.maximum(jnp.abs(den), jnp.exp(-m_t))
        m_new = m_t[cl - 1:cl, :]
        b_last = b_col[cl - 1:cl, :]
        decay = jnp.exp(b_last + m_prev - m_new)
        w_s = jnp.exp(b_last - b_col + ip - m_new)
        kw = k * w_s
        c_ref[bi, h] = decay * c_old + lax.dot_general(
            kw.astype(BF16), vb, (((0,), (0,)), ((), ())), preferred_element_type=F32)
        n_ref[bi, h:h + 1, :] = decay * n_old + jnp.sum(kw, axis=0, keepdims=True)
        m_ref[bi, h:h + 1, :] = jnp.broadcast_to(m_new, (1, LANES))
        mu = jnp.mean(hh, axis=-1, keepdims=True)
        hc = hh - mu
        var = jnp.mean(hc * hc, axis=-1, keepdims=True)
        hn = hc * lax.rsqrt(var + LN_EPS) * gmh_ref[:, sl]
        h_ref[rs, sl] = hn * jax.nn.sigmoid(o_ref[rs, sl])
    if single_chunk:
        m_ref[:, MLSTM_HEADS:, :] = jnp.zeros((bb, SUBLANES - MLSTM_HEADS, LANES), F32)


def _mlstm_group(z_main, z_tail, c0, n0, m0, g_mh, batch, seq, cl, valid, bb, name):
    nc = seq // cl
    assert bb == 1 or nc == 1, "several sequences per step only for single-chunk sequences"
    rows = bb * cl
    zcol = lambda off: pl.BlockSpec((rows, MLSTM_WIDTH), lambda b, c: (b * nc + c, off // MLSTM_WIDTH))
    state = lambda shape: pl.BlockSpec((bb,) + shape, lambda b, c: (b,) + (0,) * len(shape))
    c_shape = (MLSTM_HEADS, MLSTM_HEAD_DIM, MLSTM_HEAD_DIM)
    n_shape = (MLSTM_HEADS, MLSTM_HEAD_DIM)
    m_shape = (SUBLANES, LANES)
    return pl.pallas_call(
        functools.partial(_mlstm_kernel, cl=cl, valid=valid, bb=bb, single_chunk=nc == 1),
        out_shape=(jax.ShapeDtypeStruct((batch * seq, MLSTM_WIDTH), F32),
                   jax.ShapeDtypeStruct((batch,) + c_shape, F32),
                   jax.ShapeDtypeStruct((batch,) + n_shape, F32),
                   jax.ShapeDtypeStruct((batch,) + m_shape, F32)),
        grid=(batch // bb, nc),
        in_specs=[
            zcol(Z_Q), zcol(Z_K), zcol(Z_V), zcol(Z_O),
            pl.BlockSpec((rows, LANES), lambda b, c: (b * nc + c, ZT_GATE // LANES)),
            state(c_shape), state(n_shape), state(m_shape),
            pl.BlockSpec((1, MLSTM_WIDTH), lambda b, c: (0, 0)),
        ],
        out_specs=(pl.BlockSpec((rows, MLSTM_WIDTH), lambda b, c: (b * nc + c, 0)),
                   state(c_shape), state(n_shape), state(m_shape)),
        compiler_params=_params("arbitrary", "arbitrary"),
        name=name,
    )(z_main, z_main, z_main, z_main, z_tail, c0, n0, m0, g_mh)


def _softmax_rows(s):
    e = jnp.exp(s - jnp.max(s, axis=-1, keepdims=True))
    return e / jnp.sum(e, axis=-1, keepdims=True)


_NT_DIMS = (((1,), (1,)), ((), ()))


def _memattn_head_kernel(q_ref, k_ref, v_ref, o_ref):
    s = lax.dot_general(q_ref[...].astype(BF16), k_ref[...].astype(BF16), _NT_DIMS,
                        preferred_element_type=F32) * (MEM_HEAD_DIM ** -0.5)
    p = _softmax_rows(s)
    o_ref[...] = jnp.dot(p.astype(BF16), v_ref[...].astype(BF16), preferred_element_type=F32)


def _memattn_heads(z_tail, kv, batch, seq, tq, name):
    nq = seq // tq
    dh = MEM_HEAD_DIM
    return pl.pallas_call(
        _memattn_head_kernel,
        out_shape=jax.ShapeDtypeStruct((batch * seq, MEM_WIDTH), F32),
        grid=(batch, MEM_HEADS, nq),
        in_specs=[pl.BlockSpec((tq, dh), lambda b, h, i: (b * nq + i, ZT_QM // dh + h)),
                  pl.BlockSpec((N_MEM, dh), lambda b, h, i: (b, h)),
                  pl.BlockSpec((N_MEM, dh), lambda b, h, i: (b, MEM_HEADS + h))],
        out_specs=pl.BlockSpec((tq, dh), lambda b, h, i: (b * nq + i, h)),
        compiler_params=_params("arbitrary", "arbitrary", "arbitrary"),
        name=name,
    )(z_tail, kv, kv)


def _memattn_packed_kernel(q_ref, k_ref, v_ref, o_ref, *, tq, bb):
    nh, dh = MEM_HEADS, MEM_HEAD_DIM
    shape = (nh * tq, N_MEM * nh)
    row_head = lax.broadcasted_iota(jnp.int32, shape, 0) // tq
    col_head = lax.broadcasted_iota(jnp.int32, shape, 1) % nh
    same_head = row_head == col_head
    for bi in range(bb):
        rs = slice(bi * tq, (bi + 1) * tq)
        q = jnp.concatenate([q_ref[rs, h * dh:(h + 1) * dh] for h in range(nh)], axis=0)
        s = lax.dot_general(q.astype(BF16), k_ref[bi].astype(BF16), _NT_DIMS,
                            preferred_element_type=F32) * (dh ** -0.5)
        p = _softmax_rows(jnp.where(same_head, s, -jnp.inf))
        o = jnp.dot(p.astype(BF16), v_ref[bi].astype(BF16), preferred_element_type=F32)
        for h in range(nh):
            o_ref[rs, h * dh:(h + 1) * dh] = o[h * tq:(h + 1) * tq, :]


def _memattn_packed(z_tail, mk, mv, batch, tq, bb, name):
    rows = bb * tq
    kv = pl.BlockSpec((bb, N_MEM * MEM_HEADS, MEM_HEAD_DIM), lambda b: (b, 0, 0))
    return pl.pallas_call(
        functools.partial(_memattn_packed_kernel, tq=tq, bb=bb),
        out_shape=jax.ShapeDtypeStruct((batch * tq, MEM_WIDTH), F32),
        grid=(batch // bb,),
        in_specs=[pl.BlockSpec((rows, MEM_WIDTH), lambda b: (b, ZT_QM // MEM_WIDTH)), kv, kv],
        out_specs=pl.BlockSpec((rows, MEM_WIDTH), lambda b: (b, 0)),
        compiler_params=_params("arbitrary"),
        name=name,
    )(z_tail, mk, mv)


def _route(logits):
    lane = lax.broadcasted_iota(jnp.int32, logits.shape, 1).astype(F32)
    neg = -jnp.inf
    first = lambda mask: jnp.min(jnp.where(mask, lane, float(LANES)), axis=1, keepdims=True)
    is_g = lane < N_GROUPS
    gl = jnp.where(is_g, logits, neg)
    g_max = jnp.max(gl, axis=1, keepdims=True)
    g_sel = first(gl == g_max)
    g_w = 1.0 / jnp.sum(jnp.exp(gl - g_max), axis=1, keepdims=True)
    lo = N_GROUPS + g_sel * EXPERTS_PER_GROUP
    in_grp = (lane >= lo) & (lane < lo + EXPERTS_PER_GROUP)
    el = jnp.where(in_grp, logits, neg)
    v1 = jnp.max(el, axis=1, keepdims=True)
    i1 = first(in_grp & (el == v1))
    rest = in_grp & (lane != i1)
    el2 = jnp.where(rest, logits, neg)
    v2 = jnp.max(el2, axis=1, keepdims=True)
    i2 = first(rest & (el2 == v2))
    t = jnp.exp(v2 - v1)
    w1 = g_w / (1.0 + t)
    w2 = g_w * t / (1.0 + t)
    lane_i = lax.broadcasted_iota(jnp.int32, logits.shape, 1)
    e1 = (i1 - N_GROUPS).astype(jnp.int32)
    e2 = (i2 - N_GROUPS).astype(jnp.int32)
    eid = jnp.where(lane_i == 0, e1, jnp.where(lane_i == 1, e2, 0))
    ew = jnp.where(lane_i == 0, w1, jnp.where(lane_i == 1, w2, 0.0))
    return eid, ew


def _outproj_kernel(cp, hp, mp, xp, cs, hs, ms, xs, wout, g1, b1, wr, br,
                    x1_ref, x1p_ref, eid_ref, ew_ref, *, n_prompt):
    def rows(c, h, m, x, rs):
        groups = jnp.concatenate(
            [c[rs, :].astype(BF16), h[rs, :].astype(BF16), m[rs, :].astype(BF16)], axis=1)
        mix = jnp.dot(groups, wout[...], preferred_element_type=F32)
        x1 = _layer_norm_rows(ALPHA * x[rs, :] + mix, g1[...], b1[...])
        x1_ref[rs, :] = x1
        xh = x1.astype(BF16)
        xh_f32 = xh.astype(F32)
        bits = pltpu.bitcast(xh_f32, jnp.int32)
        x1p_ref[rs, :] = bits[:, :D_MODEL // 2] | lax.shift_right_logical(bits[:, D_MODEL // 2:], 16)
        xl = (x1 - xh_f32).astype(BF16)
        n = x1.shape[0]
        cross = jnp.dot(jnp.concatenate([xh, xl], axis=0), wr[...], preferred_element_type=F32)
        logits = ((cross[:n, :ROUTER_LANES] + cross[:n, ROUTER_LANES:])
                  + (cross[n:, :ROUTER_LANES] + cross[n:, ROUTER_LANES:])) + br[...]
        eid, ew = _route(logits)
        eid_ref[rs, :] = eid
        ew_ref[rs, :] = ew

    def body(c, h, m, x):
        rows(c, h, m, x, slice(None))

    i = pl.program_id(0)
    pl.when(i < n_prompt)(lambda: body(cp, hp, mp, xp))
    pl.when(i >= n_prompt)(lambda: body(cs, hs, ms, xs))


def _outproj_router(prompt, sample, w_out_b, g1, b1, wr, br):
    tm = OUT_TM
    tp = prompt[0].shape[0]
    ts = sample[0].shape[0]
    n_p, n_s = tp // tm, ts // tm
    total = tp + ts
    widths = (CONV_WIDTH, MLSTM_WIDTH, MEM_WIDTH, D_MODEL)
    p_specs = [pl.BlockSpec((tm, w), lambda i: (jnp.minimum(i, n_p - 1), 0)) for w in widths]
    s_specs = [pl.BlockSpec((tm, w), lambda i: (jnp.maximum(i - n_p, 0), 0)) for w in widths]
    full = lambda shape: pl.BlockSpec(shape, lambda i: (0, 0))
    row = lambda w: pl.BlockSpec((tm, w), lambda i: (i, 0))
    return pl.pallas_call(
        functools.partial(_outproj_kernel, n_prompt=n_p),
        out_shape=(jax.ShapeDtypeStruct((total, D_MODEL), F32),
                   jax.ShapeDtypeStruct((total, D_MODEL // 2), jnp.int32),
                   jax.ShapeDtypeStruct((total, ROUTER_LANES), jnp.int32),
                   jax.ShapeDtypeStruct((total, ROUTER_LANES), F32)),
        grid=(n_p + n_s,),
        in_specs=p_specs + s_specs + [
            full((D_MODEL, D_MODEL)), full((1, D_MODEL)), full((1, D_MODEL)),
            full((D_MODEL, 2 * ROUTER_LANES)), full((1, ROUTER_LANES)),
        ],
        out_specs=(row(D_MODEL), row(D_MODEL // 2), row(ROUTER_LANES), row(ROUTER_LANES)),
        compiler_params=_params("arbitrary"),
        name="outproj_router",
    )(*prompt, *sample, w_out_b, g1, b1, wr, br)


def _moe_kernel(stok_ref, sdst_ref, start_ref, cnt_ref, x_hbm, wg_ref, wu_ref, wd_ref, y_hbm,
                wgb, wub, wdb, xbuf, ybuf, gsem, ssem, state, *, dump_row):
    e = pl.program_id(0)
    last = pl.num_programs(0) - 1
    rc = MOE_ROWS

    def gather_start(base, slot):
        for r in range(rc):
            pltpu.make_async_copy(x_hbm.at[pl.ds(stok_ref[base + r], 1)],
                                  xbuf.at[slot, pl.ds(r, 1)], gsem.at[slot]).start(priority=ROW_DMA_PRIORITY)

    def gather_wait(slot):
        pltpu.make_async_copy(x_hbm.at[pl.ds(0, rc)], xbuf.at[slot], gsem.at[slot]).wait()

    def scatter_start(base, n_valid, slot):
        for r in range(rc):
            dst = jnp.where(r < n_valid, sdst_ref[base + r], dump_row + slot * rc + r)
            pltpu.make_async_copy(ybuf.at[slot, pl.ds(r, 1)], y_hbm.at[pl.ds(dst, 1)],
                                  ssem.at[slot]).start(priority=ROW_DMA_PRIORITY)

    def scatter_wait(slot):
        pltpu.make_async_copy(ybuf.at[slot], y_hbm.at[pl.ds(0, rc)], ssem.at[slot]).wait()

    @pl.when(e == 0)
    def _():
        state[0] = 0
        state[1] = 0
        state[2] = 0
        state[3] = 0
        state[4] = 0
        ybuf[...] = jnp.zeros(ybuf.shape, F32)
        pltpu.make_async_copy(ybuf.at[0], y_hbm.at[pl.ds(dump_row, rc)], ssem.at[0]).start()
        scatter_wait(0)
        gather_start(start_ref[0], 0)

    wgb[...] = wg_ref[0].astype(BF16)
    wub[...] = wu_ref[0].astype(BF16)
    wdb[...] = wd_ref[0].astype(BF16)
    start = start_ref[e]
    cnt = cnt_ref[e]
    n_chunks = jnp.maximum((cnt + rc - 1) // rc, 1)
    next_start = start_ref[jnp.minimum(e + 1, last)]

    def chunk_on(slot, c):
        other = 1 - slot
        base = start + c * rc
        gather_wait(slot)
        gather_start(jnp.where(c + 1 < n_chunks, base + rc, next_start), other)
        scatter_start(state[3], state[4], other)
        state[1 + other] = 1
        packed = xbuf[slot]
        x = jnp.concatenate(
            [pltpu.bitcast(packed & jnp.int32(-65536), F32).astype(BF16),
             pltpu.bitcast(packed << 16, F32).astype(BF16)], axis=1)
        hg = jnp.dot(x, wgb[...], preferred_element_type=F32)
        hu = jnp.dot(x, wub[...], preferred_element_type=F32)
        hid = (hg * jax.nn.sigmoid(hg) * hu).astype(BF16)

        @pl.when(state[1 + slot] == 1)
        def _():
            scatter_wait(slot)

        ybuf[slot] = jnp.dot(hid, wdb[...], preferred_element_type=F32)
        state[3] = base
        state[4] = cnt - c * rc
        state[0] = state[0] + 1

    def chunk(c, carry):
        parity = state[0] & 1
        for slot in range(2):
            pl.when(parity == slot)(functools.partial(chunk_on, slot, c))
        return carry

    lax.fori_loop(0, n_chunks, chunk, 0)

    @pl.when(e == last)
    def _():
        parity = state[0] & 1
        for slot in range(2):
            @pl.when(parity == slot)
            def _():
                gather_wait(slot)
                scatter_start(state[3], state[4], 1 - slot)
                scatter_wait(1 - slot)

                @pl.when(state[1 + slot] == 1)
                def _():
                    scatter_wait(slot)


def _moe(x1, stok, sdst, start, cnt, w_gate, w_up, w_down):
    total = x1.shape[0]
    dump_row = 2 * total
    wspec = lambda shape: pl.BlockSpec((1,) + shape, lambda e, *_: (e, 0, 0))
    grid_spec = pltpu.PrefetchScalarGridSpec(
        num_scalar_prefetch=4,
        grid=(N_EXPERTS,),
        in_specs=[pl.BlockSpec(memory_space=pl.ANY),
                  wspec((D_MODEL, D_EXPERT)), wspec((D_MODEL, D_EXPERT)), wspec((D_EXPERT, D_MODEL))],
        out_specs=pl.BlockSpec(memory_space=pl.ANY),
        scratch_shapes=[
            pltpu.VMEM((D_MODEL, D_EXPERT), BF16), pltpu.VMEM((D_MODEL, D_EXPERT), BF16),
            pltpu.VMEM((D_EXPERT, D_MODEL), BF16),
            pltpu.VMEM((2, MOE_ROWS, D_MODEL // 2), jnp.int32),
            pltpu.VMEM((2, MOE_ROWS, D_MODEL), F32),
            pltpu.SemaphoreType.DMA((2,)), pltpu.SemaphoreType.DMA((2,)),
            pltpu.SMEM((5,), jnp.int32),
        ],
    )
    return pl.pallas_call(
        functools.partial(_moe_kernel, dump_row=dump_row),
        out_shape=jax.ShapeDtypeStruct((2 * total + 2 * MOE_ROWS, D_MODEL), F32),
        grid_spec=grid_spec,
        compiler_params=_params("arbitrary"),
        name="moe_experts",
    )(stok, sdst, start, cnt, x1, w_gate, w_up, w_down)


def _final_kernel(x1_ref, y0_ref, y1_ref, ew_ref, g2, b2, o_ref):
    ew = ew_ref[...]
    ffn = ew[:, 0:1] * y0_ref[...] + ew[:, 1:2] * y1_ref[...]
    o_ref[...] = _layer_norm_rows(ALPHA * x1_ref[...] + ffn, g2[...], b2[...])


def _final(x1, y, ew, g2, b2, row0, rows, name):
    tm = OUT_TM
    total = x1.shape[0]
    off = row0 // tm
    k1 = total // tm
    vec = pl.BlockSpec((1, D_MODEL), lambda i: (0, 0))
    return pl.pallas_call(
        _final_kernel,
        out_shape=jax.ShapeDtypeStruct((rows, D_MODEL), F32),
        grid=(rows // tm,),
        in_specs=[pl.BlockSpec((tm, D_MODEL), lambda i: (i + off, 0)),
                  pl.BlockSpec((tm, D_MODEL), lambda i: (i + off, 0)),
                  pl.BlockSpec((tm, D_MODEL), lambda i: (i + off + k1, 0)),
                  pl.BlockSpec((tm, ROUTER_LANES), lambda i: (i + off, 0)),
                  vec, vec],
        out_specs=pl.BlockSpec((tm, D_MODEL), lambda i: (i, 0)),
        compiler_params=_params("arbitrary"),
        name=name,
    )(x1, y, y, ew, g2, b2)


def _dispatch_tables(eid, total):
    flat_e = eid[:, :2].reshape(-1)
    order = jnp.argsort(flat_e, stable=True).astype(jnp.int32)
    stok = order >> 1
    sdst = (order & 1) * total + stok
    experts = jnp.arange(N_EXPERTS, dtype=jnp.int32)
    cnt = jnp.sum((flat_e[:, None] == experts[None, :]).astype(jnp.int32), axis=0)
    start = jnp.cumsum(cnt) - cnt
    pad = jnp.zeros((MOE_ROWS,), jnp.int32)
    return jnp.concatenate([stok, pad]), jnp.concatenate([sdst, pad]), start, cnt


def kernel(x_prompt, x_sample, mem_prompt, state_conv, state_mlstm_C, state_mlstm_n, state_mlstm_m,
           cache_mem_k, cache_mem_v, w_in, b_in, w_dw, b_dw, g_cn, b_cn, g_mh, w_mk, w_mv, w_out,
           g_ln1, b_ln1, w_rg, b_rg, w_re, b_re, w_gate, w_up, w_down, g_ln2, b_ln2):
    bp, sp, _ = x_prompt.shape
    bs, ss, _ = x_sample.shape
    tp, ts = bp * sp, bs * ss

    gate_hi = Z_MAIN_COLS + N_GATE_COLS
    tail = lambda wt: jnp.concatenate(
        [wt[gate_hi:], wt[Z_MAIN_COLS:gate_hi],
         jnp.zeros((LANES - N_GATE_COLS,) + wt.shape[1:], wt.dtype)], axis=0)
    w_in_t = jnp.transpose(w_in[0]).astype(BF16)
    w_tail_t = tail(w_in_t)
    b_tail = tail(b_in[0])[None, :]
    w_kv_b = jnp.concatenate([w_mk[0], w_mv[0]], axis=-1).astype(BF16)
    w_out_b = w_out[0].astype(BF16)
    w_r = jnp.concatenate([w_rg[0], w_re[0],
                           jnp.zeros((D_MODEL, ROUTER_LANES - N_GROUPS - N_EXPERTS), F32)], axis=-1)
    w_r_hi = w_r.astype(BF16)
    w_r_lo = (w_r - w_r_hi.astype(F32)).astype(BF16)
    w_r_split = jnp.concatenate([w_r_hi, w_r_lo], axis=1)
    b_r = jnp.concatenate([b_rg[0], b_re[0],
                           jnp.zeros((ROUTER_LANES - N_GROUPS - N_EXPERTS,), F32)])[None, :]
    row = lambda a: a[0][None, :]

    xs_pad = jnp.pad(x_sample, ((0, 0), (0, SAMPLE_PAD - ss), (0, 0))).reshape(bs * SAMPLE_PAD, D_MODEL)
    zm_p, zt_p = _inproj(x_prompt.reshape(tp, D_MODEL), w_in_t, b_in, w_tail_t, b_tail, "inproj_prompt")
    zm_s, zt_s = _inproj(xs_pad, w_in_t, b_in, w_tail_t, b_tail, "inproj_sample")

    kv = _matmul(mem_prompt.reshape(bp * N_MEM, D_MODEL), w_kv_b, KV_TN, "mem_kv")
    mk_p = kv[:, :MEM_WIDTH].reshape(bp, N_MEM, MEM_HEADS, MEM_HEAD_DIM)
    mv_p = kv[:, MEM_WIDTH:].reshape(bp, N_MEM, MEM_HEADS, MEM_HEAD_DIM)

    conv_args = (w_dw[0], row(b_dw), row(g_cn), row(b_cn))
    conv_p, buf_p = _conv_group(zm_p, jnp.zeros((bp, CONV_HIST, CONV_WIDTH), F32), *conv_args,
                                bp, sp, 256, "conv_prompt")
    conv_s, buf_s_t = _conv_step_group(
        zm_s.reshape(bs, SAMPLE_PAD, Z_MAIN_COLS), jnp.transpose(state_conv[0], (1, 0, 2)),
        *conv_args, ss, 32, "conv_sample")
    buf_s = jnp.transpose(buf_s_t, (1, 0, 2))

    m_tile = lambda m: jnp.broadcast_to(
        jnp.pad(m, ((0, 0), (0, SUBLANES - MLSTM_HEADS)))[:, :, None], (m.shape[0], SUBLANES, LANES))
    g_mh_r = row(g_mh)
    h_p, c_p, n_p, m_p = _mlstm_group(
        zm_p, zt_p, jnp.zeros((bp, MLSTM_HEADS, MLSTM_HEAD_DIM, MLSTM_HEAD_DIM), F32),
        jnp.zeros((bp, MLSTM_HEADS, MLSTM_HEAD_DIM), F32), jnp.zeros((bp, SUBLANES, LANES), F32),
        g_mh_r, bp, sp, MLSTM_CHUNK, MLSTM_CHUNK, 1, "mlstm_prompt")
    h_s, c_s, n_s, m_s = _mlstm_group(
        zm_s, zt_s, state_mlstm_C[0], state_mlstm_n[0], m_tile(state_mlstm_m[0]),
        g_mh_r, bs, SAMPLE_PAD, SAMPLE_PAD, ss, 8, "mlstm_sample")

    mem_p = _memattn_heads(zt_p, kv, bp, sp, sp, "memattn_prompt")
    packed = lambda c: c[0].reshape(bs, N_MEM * MEM_HEADS, MEM_HEAD_DIM)
    mem_s = _memattn_packed(zt_s, packed(cache_mem_k), packed(cache_mem_v), bs, SAMPLE_PAD, 8,
                            "memattn_sample")

    compact = lambda a: a.reshape(bs, SAMPLE_PAD, a.shape[-1])[:, :ss].reshape(ts, a.shape[-1])
    x1, x1_packed, eid, ew = _outproj_router(
        (conv_p, h_p, mem_p, x_prompt.reshape(tp, D_MODEL)),
        (conv_s.reshape(ts, CONV_WIDTH), compact(h_s), compact(mem_s), x_sample.reshape(ts, D_MODEL)),
        w_out_b, row(g_ln1), row(b_ln1), w_r_split, b_r)

    total = tp + ts
    stok, sdst, start, cnt = _dispatch_tables(eid, total)
    y = _moe(x1_packed, stok, sdst, start, cnt, w_gate[0], w_up[0], w_down[0])
    g2, b2 = row(g_ln2), row(b_ln2)
    y_p = _final(x1, y, ew, g2, b2, 0, tp, "final_prompt").reshape(bp, sp, D_MODEL)
    y_s = _final(x1, y, ew, g2, b2, tp, ts, "final_sample").reshape(bs, ss, D_MODEL)

    return (y_p, y_s, buf_p[None], buf_s[None], c_p[None], c_s[None], n_p[None], n_s[None],
            m_p[:, :MLSTM_HEADS, 0][None], m_s[:, :MLSTM_HEADS, 0][None], mk_p[None], mv_p[None])
```

```python
import functools

import jax
import jax.numpy as jnp
from jax import lax
from jax.experimental import pallas as pl
from jax.experimental.pallas import tpu as pltpu

F32 = jnp.float32
BF16 = jnp.bfloat16

D_MODEL = 2048
CONV_WIDTH = 512
CONV_K = 31
CONV_HIST = CONV_K - 1
MLSTM_HEADS = 4
MLSTM_HEAD_DIM = 256
MLSTM_WIDTH = MLSTM_HEADS * MLSTM_HEAD_DIM
MLSTM_CHUNK = 128
MEM_HEADS = 4
MEM_HEAD_DIM = 128
MEM_WIDTH = MEM_HEADS * MEM_HEAD_DIM
N_MEM = 256
N_GROUPS = 8
EXPERTS_PER_GROUP = 8
N_EXPERTS = N_GROUPS * EXPERTS_PER_GROUP
D_EXPERT = 512
LN_EPS = 1e-5
DEPTH = 1
ALPHA = (2 * DEPTH) ** 0.25

LANES = 128
SUBLANES = 8
VMEM_LIMIT_BYTES = 56 * 1024 * 1024

Z_CONV_A = 0
Z_CONV_G = CONV_WIDTH
Z_Q = 2 * CONV_WIDTH
Z_K = Z_Q + MLSTM_WIDTH
Z_V = Z_K + MLSTM_WIDTH
Z_O = Z_V + MLSTM_WIDTH
Z_MAIN_COLS = Z_O + MLSTM_WIDTH
N_GATE_COLS = 2 * MLSTM_HEADS
ZT_QM = 0
ZT_GATE = MEM_WIDTH
Z_TAIL_COLS = MEM_WIDTH + LANES
INPROJ_TM = 1024
INPROJ_TN = 1024
KV_TN = 256

SAMPLE_PAD = SUBLANES
ROUTER_LANES = LANES
MOE_ROWS = 128
MOE_SLOTS = 3
ROW_DMA_PRIORITY = 1
OUT_TM = 256


def _params(*sem):
    return pltpu.CompilerParams(dimension_semantics=sem, vmem_limit_bytes=VMEM_LIMIT_BYTES)


def _inproj_kernel(x_ref, wm_ref, bm_ref, wt_ref, bt_ref, zm_ref, zt_ref, xb_ref, *, n_main):
    j = pl.program_id(1)

    @pl.when(j == 0)
    def _():
        xb_ref[...] = x_ref[...].astype(BF16)

    nt = (((1,), (1,)), ((), ()))

    @pl.when(j < n_main)
    def _():
        zm_ref[...] = lax.dot_general(xb_ref[...], wm_ref[...], nt,
                                      preferred_element_type=F32) + bm_ref[...]

    @pl.when(j == n_main)
    def _():
        zt_ref[...] = lax.dot_general(xb_ref[...], wt_ref[...], nt,
                                      preferred_element_type=F32) + bt_ref[...]


def _inproj(x, w_in_t, b_in, w_tail_t, b_tail, name):
    t, k = x.shape
    tm, tn = INPROJ_TM, INPROJ_TN
    n_main = Z_MAIN_COLS // tn
    main_col = lambda j: jnp.minimum(j, n_main - 1)
    return pl.pallas_call(
        functools.partial(_inproj_kernel, n_main=n_main),
        out_shape=(jax.ShapeDtypeStruct((t, Z_MAIN_COLS), F32),
                   jax.ShapeDtypeStruct((t, Z_TAIL_COLS), F32)),
        grid=(t // tm, n_main + 1),
        in_specs=[
            pl.BlockSpec((tm, k), lambda i, j: (i, 0)),
            pl.BlockSpec((tn, k), lambda i, j: (main_col(j), 0)),
            pl.BlockSpec((1, tn), lambda i, j: (0, main_col(j))),
            pl.BlockSpec((Z_TAIL_COLS, k), lambda i, j: (0, 0)),
            pl.BlockSpec((1, Z_TAIL_COLS), lambda i, j: (0, 0)),
        ],
        out_specs=(pl.BlockSpec((tm, tn), lambda i, j: (i, main_col(j))),
                   pl.BlockSpec((tm, Z_TAIL_COLS), lambda i, j: (i, 0))),
        scratch_shapes=[pltpu.VMEM((tm, k), BF16)],
        compiler_params=_params("arbitrary", "arbitrary"),
        name=name,
    )(x, w_in_t, b_in, w_tail_t, b_tail)


def _matmul_kernel(x_ref, w_ref, o_ref):
    o_ref[...] = jnp.dot(x_ref[...].astype(BF16), w_ref[...], preferred_element_type=F32)


def _matmul(x, w_bf16, tn, name):
    t, k = x.shape
    n = w_bf16.shape[1]
    return pl.pallas_call(
        _matmul_kernel,
        out_shape=jax.ShapeDtypeStruct((t, n), F32),
        grid=(n // tn,),
        in_specs=[pl.BlockSpec((t, k), lambda j: (0, 0)), pl.BlockSpec((k, tn), lambda j: (0, j))],
        out_specs=pl.BlockSpec((t, tn), lambda j: (0, j)),
        compiler_params=_params("arbitrary"),
        name=name,
    )(x, w_bf16)


def _layer_norm_rows(y, g, b):
    mu = jnp.mean(y, axis=-1, keepdims=True)
    yc = y - mu
    var = jnp.mean(yc * yc, axis=-1, keepdims=True)
    return yc * lax.rsqrt(var + LN_EPS) * g + b


CONV_ROWS = 64


def _conv_kernel(a_ref, g_ref, hist_ref, wdw_ref, bdw_ref, gcn_ref, bcn_ref, out_ref, nb_ref,
                 ubuf, shifted, *, tl):
    head = CONV_HIST + 2
    li = pl.program_id(1)

    @pl.when(li == 0)
    def _():
        ubuf[0:2, :] = jnp.zeros((2, CONV_WIDTH), F32)
        ubuf[2:head, :] = hist_ref[0]

    ubuf[head:head + tl, :] = a_ref[...] * jax.nn.sigmoid(g_ref[...])
    span = shifted.shape[1]
    for k in range(1, SUBLANES):
        shifted[k - 1] = ubuf[k:k + span, :]
    for r0 in range(0, tl, CONV_ROWS):
        acc = jnp.zeros((CONV_ROWS, CONV_WIDTH), F32) + bdw_ref[...]
        for j in range(CONV_K):
            lo, k = divmod(2 + j, SUBLANES)
            lo = lo * SUBLANES + r0
            src = ubuf if k == 0 else shifted.at[k - 1]
            acc = acc + wdw_ref[j:j + 1, :] * src[lo:lo + CONV_ROWS, :]
        y = _layer_norm_rows(acc, gcn_ref[...], bcn_ref[...])
        out_ref[r0:r0 + CONV_ROWS, :] = y * jax.nn.sigmoid(y)

    @pl.when(li == pl.num_programs(1) - 1)
    def _():
        nb_ref[0] = ubuf[2 + tl:head + tl, :]

    ubuf[2:head, :] = ubuf[2 + tl:head + tl, :]


def _conv_group(z, hist, w_dw, b_dw, g_cn, b_cn, batch, seq, tl, name):
    nl = seq // tl
    row = lambda b, l: (b * nl + l, 0)
    vec = pl.BlockSpec((1, CONV_WIDTH), lambda b, l: (0, 0))
    return pl.pallas_call(
        functools.partial(_conv_kernel, tl=tl),
        out_shape=(jax.ShapeDtypeStruct((batch * seq, CONV_WIDTH), F32),
                   jax.ShapeDtypeStruct((batch, CONV_HIST, CONV_WIDTH), F32)),
        grid=(batch, nl),
        in_specs=[
            pl.BlockSpec((tl, CONV_WIDTH), lambda b, l: (b * nl + l, Z_CONV_A // CONV_WIDTH)),
            pl.BlockSpec((tl, CONV_WIDTH), lambda b, l: (b * nl + l, Z_CONV_G // CONV_WIDTH)),
            pl.BlockSpec((1, CONV_HIST, CONV_WIDTH), lambda b, l: (b, 0, 0)),
            pl.BlockSpec((CONV_K, CONV_WIDTH), lambda b, l: (0, 0)),
            vec, vec, vec,
        ],
        out_specs=(pl.BlockSpec((tl, CONV_WIDTH), row),
                   pl.BlockSpec((1, CONV_HIST, CONV_WIDTH), lambda b, l: (b, 0, 0))),
        scratch_shapes=[pltpu.VMEM((CONV_HIST + 2 + tl, CONV_WIDTH), F32),
                        pltpu.VMEM((SUBLANES - 1, CONV_HIST + 2 + tl - SUBLANES, CONV_WIDTH), F32)],
        compiler_params=_params("arbitrary", "arbitrary"),
        name=name,
    )(z, z, hist, w_dw, b_dw, g_cn, b_cn)


def _conv_step_kernel(a_ref, g_ref, hist_ref, wdw_ref, bdw_ref, gcn_ref, bcn_ref, out_ref, nb_ref,
                      *, steps):
    u = [a_ref[:, t, :] * jax.nn.sigmoid(g_ref[:, t, :]) for t in range(steps)]
    full = lambda r: hist_ref[r] if r < CONV_HIST else u[r - CONV_HIST]
    for t in range(steps):
        acc = bdw_ref[...] + wdw_ref[0:1, :] * full(t)
        for j in range(1, CONV_K):
            acc = acc + wdw_ref[j:j + 1, :] * full(t + j)
        y = _layer_norm_rows(acc, gcn_ref[...], bcn_ref[...])
        out_ref[:, t, :] = y * jax.nn.sigmoid(y)
    for r in range(CONV_HIST):
        nb_ref[r] = full(r + steps)


def _conv_step_group(z3, hist_t, w_dw, b_dw, g_cn, b_cn, steps, bb, name):
    batch = z3.shape[0]
    vec = pl.BlockSpec((1, CONV_WIDTH), lambda b: (0, 0))
    hist_spec = pl.BlockSpec((CONV_HIST, bb, CONV_WIDTH), lambda b: (0, b, 0))
    return pl.pallas_call(
        functools.partial(_conv_step_kernel, steps=steps),
        out_shape=(jax.ShapeDtypeStruct((batch, steps, CONV_WIDTH), F32),
                   jax.ShapeDtypeStruct((CONV_HIST, batch, CONV_WIDTH), F32)),
        grid=(batch // bb,),
        in_specs=[
            pl.BlockSpec((bb, SAMPLE_PAD, CONV_WIDTH), lambda b: (b, 0, Z_CONV_A // CONV_WIDTH)),
            pl.BlockSpec((bb, SAMPLE_PAD, CONV_WIDTH), lambda b: (b, 0, Z_CONV_G // CONV_WIDTH)),
            hist_spec,
            pl.BlockSpec((CONV_K, CONV_WIDTH), lambda b: (0, 0)),
            vec, vec, vec,
        ],
        out_specs=(pl.BlockSpec((bb, steps, CONV_WIDTH), lambda b: (b, 0, 0)), hist_spec),
        compiler_params=_params("arbitrary"),
        name=name,
    )(z3, z3, hist_t, w_dw, b_dw, g_cn, b_cn)


def _col_to_row(col, eye):
    n = col.shape[0]
    return jnp.sum(jnp.where(eye, jnp.broadcast_to(col, (n, n)), 0.0), axis=0, keepdims=True)


def _mlstm_kernel(q_ref, k_ref, v_ref, o_ref, gate_ref, c0_ref, n0_ref, m0_ref, gmh_ref,
                  h_ref, c_ref, n_ref, m_ref, *, cl, valid, bb, single_chunk):
    if single_chunk:
        c_in, n_in, m_in = c0_ref, n0_ref, m0_ref
    else:
        c_in, n_in, m_in = c_ref, n_ref, m_ref

        @pl.when(pl.program_id(1) == 0)
        def _():
            c_ref[...] = c0_ref[...]
            n_ref[...] = n0_ref[...]
            m_ref[...] = m0_ref[...]

    rows = lax.broadcasted_iota(jnp.int32, (cl, cl), 0)
    cols = lax.broadcasted_iota(jnp.int32, (cl, cl), 1)
    eye = rows == cols
    tril = rows >= cols
    row_id = lax.broadcasted_iota(jnp.int32, (cl, 1), 0)
    dh = MLSTM_HEAD_DIM
    for bi, h in [(bi, h) for bi in range(bb) for h in range(MLSTM_HEADS)]:
        rs = slice(bi * cl, (bi + 1) * cl)
        sl = slice(h * dh, (h + 1) * dh)
        gates = gate_ref[rs, :]
        q = q_ref[rs, sl]
        k = k_ref[rs, sl] * (dh ** -0.5)
        v = v_ref[rs, sl]
        ip = gates[:, h:h + 1]
        fp = gates[:, MLSTM_HEADS + h:MLSTM_HEADS + h + 1]
        lf = jnp.minimum(fp, 0.0) - jnp.log1p(jnp.exp(-jnp.abs(fp)))
        if valid < cl:
            ip = jnp.where(row_id < valid, ip, -jnp.inf)
            lf = jnp.where(row_id < valid, lf, 0.0)
        lf_row = _col_to_row(lf, eye)
        ip_row = _col_to_row(ip, eye)
        b_col = jnp.sum(jnp.where(tril, jnp.broadcast_to(lf_row, (cl, cl)), 0.0), axis=1, keepdims=True)
        b_row = _col_to_row(b_col, eye)
        m_prev = m_in[bi, h:h + 1, 0:1]
        log_inter = b_col + m_prev
        log_intra = jnp.where(tril, b_col - b_row + ip_row, -jnp.inf)
        m_t = jnp.maximum(log_inter, jnp.max(log_intra, axis=1, keepdims=True))
        w_inter = jnp.exp(log_inter - m_t)
        qb = q.astype(BF16)
        kb = k.astype(BF16)
        vb = v.astype(BF16)
        s = lax.dot_general(qb, kb, (((1,), (1,)), ((), ())), preferred_element_type=F32)
        s = s * jnp.exp(log_intra - m_t)
        c_old = c_in[bi, h]
        n_old = n_in[bi, h:h + 1, :]
        num = (w_inter * jnp.dot(qb, c_old.astype(BF16), preferred_element_type=F32)
               + jnp.dot(s.astype(BF16), vb, preferred_element_type=F32))
        den = w_inter * jnp.sum(q * n_old, axis=1, keepdims=True) + jnp.sum(s, axis=1, keepdims=True)
        hh = num / jnp.maximum(jnp.abs(den), jnp.exp(-m_t))
        m_new = m_t[cl - 1:cl, :]
        b_last = b_col[cl - 1:cl, :]
        decay = jnp.exp(b_last + m_prev - m_new)
        w_s = jnp.exp(b_last - b_col + ip - m_new)
        kw = k * w_s
        c_ref[bi, h] = decay * c_old + lax.dot_general(
            kw.astype(BF16), vb, (((0,), (0,)), ((), ())), preferred_element_type=F32)
        n_ref[bi, h:h + 1, :] = decay * n_old + jnp.sum(kw, axis=0, keepdims=True)
        m_ref[bi, h:h + 1, :] = jnp.broadcast_to(m_new, (1, LANES))
        mu = jnp.mean(hh, axis=-1, keepdims=True)
        hc = hh - mu
        var = jnp.mean(hc * hc, axis=-1, keepdims=True)
        hn = hc * lax.rsqrt(var + LN_EPS) * gmh_ref[:, sl]
        h_ref[rs, sl] = hn * jax.nn.sigmoid(o_ref[rs, sl])
    if single_chunk:
        m_ref[:, MLSTM_HEADS:, :] = jnp.zeros((bb, SUBLANES - MLSTM_HEADS, LANES), F32)


def _mlstm_group(z_main, z_tail, c0, n0, m0, g_mh, batch, seq, cl, valid, bb, name):
    nc = seq // cl
    assert bb == 1 or nc == 1, "several sequences per step only for single-chunk sequences"
    rows = bb * cl
    zcol = lambda off: pl.BlockSpec((rows, MLSTM_WIDTH), lambda b, c: (b * nc + c, off // MLSTM_WIDTH))
    state = lambda shape: pl.BlockSpec((bb,) + shape, lambda b, c: (b,) + (0,) * len(shape))
    c_shape = (MLSTM_HEADS, MLSTM_HEAD_DIM, MLSTM_HEAD_DIM)
    n_shape = (MLSTM_HEADS, MLSTM_HEAD_DIM)
    m_shape = (SUBLANES, LANES)
    return pl.pallas_call(
        functools.partial(_mlstm_kernel, cl=cl, valid=valid, bb=bb, single_chunk=nc == 1),
        out_shape=(jax.ShapeDtypeStruct((batch * seq, MLSTM_WIDTH), F32),
                   jax.ShapeDtypeStruct((batch,) + c_shape, F32),
                   jax.ShapeDtypeStruct((batch,) + n_shape, F32),
                   jax.ShapeDtypeStruct((batch,) + m_shape, F32)),
        grid=(batch // bb, nc),
        in_specs=[
            zcol(Z_Q), zcol(Z_K), zcol(Z_V), zcol(Z_O),
            pl.BlockSpec((rows, LANES), lambda b, c: (b * nc + c, ZT_GATE // LANES)),
            state(c_shape), state(n_shape), state(m_shape),
            pl.BlockSpec((1, MLSTM_WIDTH), lambda b, c: (0, 0)),
        ],
        out_specs=(pl.BlockSpec((rows, MLSTM_WIDTH), lambda b, c: (b * nc + c, 0)),
                   state(c_shape), state(n_shape), state(m_shape)),
        compiler_params=_params("arbitrary", "arbitrary"),
        name=name,
    )(z_main, z_main, z_main, z_main, z_tail, c0, n0, m0, g_mh)


def _softmax_rows(s):
    e = jnp.exp(s - jnp.max(s, axis=-1, keepdims=True))
    return e / jnp.sum(e, axis=-1, keepdims=True)


_NT_DIMS = (((1,), (1,)), ((), ()))


def _memattn_head_kernel(q_ref, k_ref, v_ref, o_ref):
    s = lax.dot_general(q_ref[...].astype(BF16), k_ref[...].astype(BF16), _NT_DIMS,
                        preferred_element_type=F32) * (MEM_HEAD_DIM ** -0.5)
    p = _softmax_rows(s)
    o_ref[...] = jnp.dot(p.astype(BF16), v_ref[...].astype(BF16), preferred_element_type=F32)


def _memattn_heads(z_tail, kv, batch, seq, tq, name):
    nq = seq // tq
    dh = MEM_HEAD_DIM
    return pl.pallas_call(
        _memattn_head_kernel,
        out_shape=jax.ShapeDtypeStruct((batch * seq, MEM_WIDTH), F32),
        grid=(batch, MEM_HEADS, nq),
        in_specs=[pl.BlockSpec((tq, dh), lambda b, h, i: (b * nq + i, ZT_QM // dh + h)),
                  pl.BlockSpec((N_MEM, dh), lambda b, h, i: (b, h)),
                  pl.BlockSpec((N_MEM, dh), lambda b, h, i: (b, MEM_HEADS + h))],
        out_specs=pl.BlockSpec((tq, dh), lambda b, h, i: (b * nq + i, h)),
        compiler_params=_params("arbitrary", "arbitrary", "arbitrary"),
        name=name,
    )(z_tail, kv, kv)


def _memattn_packed_kernel(q_ref, k_ref, v_ref, o_ref, *, tq, bb):
    nh, dh = MEM_HEADS, MEM_HEAD_DIM
    shape = (nh * tq, N_MEM * nh)
    row_head = lax.broadcasted_iota(jnp.int32, shape, 0) // tq
    col_head = lax.broadcasted_iota(jnp.int32, shape, 1) % nh
    same_head = row_head == col_head
    for bi in range(bb):
        rs = slice(bi * tq, (bi + 1) * tq)
        q = jnp.concatenate([q_ref[rs, h * dh:(h + 1) * dh] for h in range(nh)], axis=0)
        s = lax.dot_general(q.astype(BF16), k_ref[bi].astype(BF16), _NT_DIMS,
                            preferred_element_type=F32) * (dh ** -0.5)
        p = _softmax_rows(jnp.where(same_head, s, -jnp.inf))
        o = jnp.dot(p.astype(BF16), v_ref[bi].astype(BF16), preferred_element_type=F32)
        for h in range(nh):
            o_ref[rs, h * dh:(h + 1) * dh] = o[h * tq:(h + 1) * tq, :]


def _memattn_packed(z_tail, mk, mv, batch, tq, bb, name):
    rows = bb * tq
    kv = pl.BlockSpec((bb, N_MEM * MEM_HEADS, MEM_HEAD_DIM), lambda b: (b, 0, 0))
    return pl.pallas_call(
        functools.partial(_memattn_packed_kernel, tq=tq, bb=bb),
        out_shape=jax.ShapeDtypeStruct((batch * tq, MEM_WIDTH), F32),
        grid=(batch // bb,),
        in_specs=[pl.BlockSpec((rows, MEM_WIDTH), lambda b: (b, ZT_QM // MEM_WIDTH)), kv, kv],
        out_specs=pl.BlockSpec((rows, MEM_WIDTH), lambda b: (b, 0)),
        compiler_params=_params("arbitrary"),
        name=name,
    )(z_tail, mk, mv)


def _route(logits):
    lane = lax.broadcasted_iota(jnp.int32, logits.shape, 1).astype(F32)
    neg = -jnp.inf
    first = lambda mask: jnp.min(jnp.where(mask, lane, float(LANES)), axis=1, keepdims=True)
    is_g = lane < N_GROUPS
    gl = jnp.where(is_g, logits, neg)
    g_max = jnp.max(gl, axis=1, keepdims=True)
    g_sel = first(gl == g_max)
    g_w = 1.0 / jnp.sum(jnp.exp(gl - g_max), axis=1, keepdims=True)
    lo = N_GROUPS + g_sel * EXPERTS_PER_GROUP
    in_grp = (lane >= lo) & (lane < lo + EXPERTS_PER_GROUP)
    el = jnp.where(in_grp, logits, neg)
    v1 = jnp.max(el, axis=1, keepdims=True)
    i1 = first(in_grp & (el == v1))
    rest = in_grp & (lane != i1)
    el2 = jnp.where(rest, logits, neg)
    v2 = jnp.max(el2, axis=1, keepdims=True)
    i2 = first(rest & (el2 == v2))
    t = jnp.exp(v2 - v1)
    w1 = g_w / (1.0 + t)
    w2 = g_w * t / (1.0 + t)
    lane_i = lax.broadcasted_iota(jnp.int32, logits.shape, 1)
    e1 = (i1 - N_GROUPS).astype(jnp.int32)
    e2 = (i2 - N_GROUPS).astype(jnp.int32)
    eid = jnp.where(lane_i == 0, e1, jnp.where(lane_i == 1, e2, 0))
    ew = jnp.where(lane_i == 0, w1, jnp.where(lane_i == 1, w2, 0.0))
    return eid, ew


def _outproj_kernel(cp, hp, mp, xp, cs, hs, ms, xs, wout, g1, b1, wr, br,
                    x1_ref, x1p_ref, eid_ref, ew_ref, *, n_prompt):
    def rows(c, h, m, x, rs):
        groups = jnp.concatenate(
            [c[rs, :].astype(BF16), h[rs, :].astype(BF16), m[rs, :].astype(BF16)], axis=1)
        mix = jnp.dot(groups, wout[...], preferred_element_type=F32)
        x1 = _layer_norm_rows(ALPHA * x[rs, :] + mix, g1[...], b1[...])
        x1_ref[rs, :] = x1
        xh = x1.astype(BF16)
        xh_f32 = xh.astype(F32)
        bits = pltpu.bitcast(xh_f32, jnp.int32)
        x1p_ref[rs, :] = bits[:, :D_MODEL // 2] | lax.shift_right_logical(bits[:, D_MODEL // 2:], 16)
        xl = (x1 - xh_f32).astype(BF16)
        n = x1.shape[0]
        cross = jnp.dot(jnp.concatenate([xh, xl], axis=0), wr[...], preferred_element_type=F32)
        logits = ((cross[:n, :ROUTER_LANES] + cross[:n, ROUTER_LANES:])
                  + (cross[n:, :ROUTER_LANES] + cross[n:, ROUTER_LANES:])) + br[...]
        eid, ew = _route(logits)
        eid_ref[rs, :] = eid
        ew_ref[rs, :] = ew

    def body(c, h, m, x):
        rows(c, h, m, x, slice(None))

    i = pl.program_id(0)
    pl.when(i < n_prompt)(lambda: body(cp, hp, mp, xp))
    pl.when(i >= n_prompt)(lambda: body(cs, hs, ms, xs))


def _outproj_router(prompt, sample, w_out_b, g1, b1, wr, br):
    tm = OUT_TM
    tp = prompt[0].shape[0]
    ts = sample[0].shape[0]
    n_p, n_s = tp // tm, ts // tm
    total = tp + ts
    widths = (CONV_WIDTH, MLSTM_WIDTH, MEM_WIDTH, D_MODEL)
    p_specs = [pl.BlockSpec((tm, w), lambda i: (jnp.minimum(i, n_p - 1), 0)) for w in widths]
    s_specs = [pl.BlockSpec((tm, w), lambda i: (jnp.maximum(i - n_p, 0), 0)) for w in widths]
    full = lambda shape: pl.BlockSpec(shape, lambda i: (0, 0))
    row = lambda w: pl.BlockSpec((tm, w), lambda i: (i, 0))
    return pl.pallas_call(
        functools.partial(_outproj_kernel, n_prompt=n_p),
        out_shape=(jax.ShapeDtypeStruct((total, D_MODEL), F32),
                   jax.ShapeDtypeStruct((total, D_MODEL // 2), jnp.int32),
                   jax.ShapeDtypeStruct((total, ROUTER_LANES), jnp.int32),
                   jax.ShapeDtypeStruct((total, ROUTER_LANES), F32)),
        grid=(n_p + n_s,),
        in_specs=p_specs + s_specs + [
            full((D_MODEL, D_MODEL)), full((1, D_MODEL)), full((1, D_MODEL)),
            full((D_MODEL, 2 * ROUTER_LANES)), full((1, ROUTER_LANES)),
        ],
        out_specs=(row(D_MODEL), row(D_MODEL // 2), row(ROUTER_LANES), row(ROUTER_LANES)),
        compiler_params=_params("arbitrary"),
        name="outproj_router",
    )(*prompt, *sample, w_out_b, g1, b1, wr, br)


def _moe_kernel(stok_ref, sdst_ref, start_ref, cnt_ref, x_hbm, wg_ref, wu_ref, wd_ref, y_hbm,
                wgb, wub, wdb, xbuf, ybuf, gsem, ssem, state, *, dump_row):
    e = pl.program_id(0)
    last = pl.num_programs(0) - 1
    rc = MOE_ROWS
    ns = MOE_SLOTS
    own = 1 + ns

    def gather_start(base, slot):
        for r in range(rc):
            pltpu.make_async_copy(x_hbm.at[pl.ds(stok_ref[base + r], 1)],
                                  xbuf.at[slot, pl.ds(r, 1)], gsem.at[slot]).start(priority=ROW_DMA_PRIORITY)

    def gather_wait(slot):
        pltpu.make_async_copy(x_hbm.at[pl.ds(0, rc)], xbuf.at[slot], gsem.at[slot]).wait()

    def scatter_start(base, n_valid, slot):
        for r in range(rc):
            dst = jnp.where(r < n_valid, sdst_ref[base + r], dump_row + slot * rc + r)
            pltpu.make_async_copy(ybuf.at[slot, pl.ds(r, 1)], y_hbm.at[pl.ds(dst, 1)],
                                  ssem.at[slot]).start(priority=ROW_DMA_PRIORITY)

    def scatter_wait(slot):
        pltpu.make_async_copy(ybuf.at[slot], y_hbm.at[pl.ds(0, rc)], ssem.at[slot]).wait()

    @pl.when(e == 0)
    def _():
        for i in range(own + 2):
            state[i] = 0
        ybuf[...] = jnp.zeros(ybuf.shape, F32)
        for slot in range(ns - 1):
            pltpu.make_async_copy(ybuf.at[slot], y_hbm.at[pl.ds(dump_row + slot * rc, rc)],
                                  ssem.at[slot]).start()
            scatter_wait(slot)
        gather_start(start_ref[0], 0)

    wgb[...] = wg_ref[0].astype(BF16)
    wub[...] = wu_ref[0].astype(BF16)
    wdb[...] = wd_ref[0].astype(BF16)
    start = start_ref[e]
    cnt = cnt_ref[e]
    n_chunks = jnp.maximum((cnt + rc - 1) // rc, 1)
    next_start = start_ref[jnp.minimum(e + 1, last)]

    def chunk_on(slot, c):
        nxt, prv = (slot + 1) % ns, (slot - 1) % ns
        base = start + c * rc
        gather_wait(slot)

        @pl.when(state[1 + slot] == 1)
        def _():
            scatter_wait(slot)

        gather_start(jnp.where(c + 1 < n_chunks, base + rc, next_start), nxt)
        scatter_start(state[own], state[own + 1], prv)
        state[1 + prv] = 1
        packed = xbuf[slot]
        x = jnp.concatenate(
            [pltpu.bitcast(packed & jnp.int32(-65536), F32).astype(BF16),
             pltpu.bitcast(packed << 16, F32).astype(BF16)], axis=1)
        hg = jnp.dot(x, wgb[...], preferred_element_type=F32)
        hu = jnp.dot(x, wub[...], preferred_element_type=F32)
        hid = (hg * jax.nn.sigmoid(hg) * hu).astype(BF16)
        ybuf[slot] = jnp.dot(hid, wdb[...], preferred_element_type=F32)
        state[own] = base
        state[own + 1] = cnt - c * rc
        state[0] = nxt

    def chunk(c, carry):
        cur = state[0]
        for slot in range(ns):
            pl.when(cur == slot)(functools.partial(chunk_on, slot, c))
        return carry

    lax.fori_loop(0, n_chunks, chunk, 0)

    @pl.when(e == last)
    def _():
        cur = state[0]
        for slot in range(ns):
            @pl.when(cur == slot)
            def _():
                gather_wait(slot)
                scatter_start(state[own], state[own + 1], (slot - 1) % ns)
                state[1 + (slot - 1) % ns] = 1

        for slot in range(ns):
            @pl.when(state[1 + slot] == 1)
            def _():
                scatter_wait(slot)


def _moe(x1, stok, sdst, start, cnt, w_gate, w_up, w_down):
    total = x1.shape[0]
    dump_row = 2 * total
    wspec = lambda shape: pl.BlockSpec((1,) + shape, lambda e, *_: (e, 0, 0))
    grid_spec = pltpu.PrefetchScalarGridSpec(
        num_scalar_prefetch=4,
        grid=(N_EXPERTS,),
        in_specs=[pl.BlockSpec(memory_space=pl.ANY),
                  wspec((D_MODEL, D_EXPERT)), wspec((D_MODEL, D_EXPERT)), wspec((D_EXPERT, D_MODEL))],
        out_specs=pl.BlockSpec(memory_space=pl.ANY),
        scratch_shapes=[
            pltpu.VMEM((D_MODEL, D_EXPERT), BF16), pltpu.VMEM((D_MODEL, D_EXPERT), BF16),
            pltpu.VMEM((D_EXPERT, D_MODEL), BF16),
            pltpu.VMEM((MOE_SLOTS, MOE_ROWS, D_MODEL // 2), jnp.int32),
            pltpu.VMEM((MOE_SLOTS, MOE_ROWS, D_MODEL), F32),
            pltpu.SemaphoreType.DMA((MOE_SLOTS,)), pltpu.SemaphoreType.DMA((MOE_SLOTS,)),
            pltpu.SMEM((MOE_SLOTS + 3,), jnp.int32),
        ],
    )
    return pl.pallas_call(
        functools.partial(_moe_kernel, dump_row=dump_row),
        out_shape=jax.ShapeDtypeStruct((2 * total + MOE_SLOTS * MOE_ROWS, D_MODEL), F32),
        grid_spec=grid_spec,
        compiler_params=_params("arbitrary"),
        name="moe_experts",
    )(stok, sdst, start, cnt, x1, w_gate, w_up, w_down)


def _final_kernel(x1_ref, y0_ref, y1_ref, ew_ref, g2, b2, o_ref):
    ew = ew_ref[...]
    ffn = ew[:, 0:1] * y0_ref[...] + ew[:, 1:2] * y1_ref[...]
    o_ref[...] = _layer_norm_rows(ALPHA * x1_ref[...] + ffn, g2[...], b2[...])


def _final(x1, y, ew, g2, b2, row0, rows, name):
    tm = OUT_TM
    total = x1.shape[0]
    off = row0 // tm
    k1 = total // tm
    vec = pl.BlockSpec((1, D_MODEL), lambda i: (0, 0))
    return pl.pallas_call(
        _final_kernel,
        out_shape=jax.ShapeDtypeStruct((rows, D_MODEL), F32),
        grid=(rows // tm,),
        in_specs=[pl.BlockSpec((tm, D_MODEL), lambda i: (i + off, 0)),
                  pl.BlockSpec((tm, D_MODEL), lambda i: (i + off, 0)),
                  pl.BlockSpec((tm, D_MODEL), lambda i: (i + off + k1, 0)),
                  pl.BlockSpec((tm, ROUTER_LANES), lambda i: (i + off, 0)),
                  vec, vec],
        out_specs=pl.BlockSpec((tm, D_MODEL), lambda i: (i, 0)),
        compiler_params=_params("arbitrary"),
        name=name,
    )(x1, y, y, ew, g2, b2)


def _dispatch_tables(eid, total):
    flat_e = eid[:, :2].reshape(-1)
    order = jnp.argsort(flat_e, stable=True).astype(jnp.int32)
    stok = order >> 1
    sdst = (order & 1) * total + stok
    experts = jnp.arange(N_EXPERTS, dtype=jnp.int32)
    cnt = jnp.sum((flat_e[:, None] == experts[None, :]).astype(jnp.int32), axis=0)
    start = jnp.cumsum(cnt) - cnt
    pad = jnp.zeros((MOE_ROWS,), jnp.int32)
    return jnp.concatenate([stok, pad]), jnp.concatenate([sdst, pad]), start, cnt


def kernel(x_prompt, x_sample, mem_prompt, state_conv, state_mlstm_C, state_mlstm_n, state_mlstm_m,
           cache_mem_k, cache_mem_v, w_in, b_in, w_dw, b_dw, g_cn, b_cn, g_mh, w_mk, w_mv, w_out,
           g_ln1, b_ln1, w_rg, b_rg, w_re, b_re, w_gate, w_up, w_down, g_ln2, b_ln2):
    bp, sp, _ = x_prompt.shape
    bs, ss, _ = x_sample.shape
    tp, ts = bp * sp, bs * ss

    gate_hi = Z_MAIN_COLS + N_GATE_COLS
    tail = lambda wt: jnp.concatenate(
        [wt[gate_hi:], wt[Z_MAIN_COLS:gate_hi],
         jnp.zeros((LANES - N_GATE_COLS,) + wt.shape[1:], wt.dtype)], axis=0)
    w_in_t = jnp.transpose(w_in[0]).astype(BF16)
    w_tail_t = tail(w_in_t)
    b_tail = tail(b_in[0])[None, :]
    w_kv_b = jnp.concatenate([w_mk[0], w_mv[0]], axis=-1).astype(BF16)
    w_out_b = w_out[0].astype(BF16)
    w_r = jnp.concatenate([w_rg[0], w_re[0],
                           jnp.zeros((D_MODEL, ROUTER_LANES - N_GROUPS - N_EXPERTS), F32)], axis=-1)
    w_r_hi = w_r.astype(BF16)
    w_r_lo = (w_r - w_r_hi.astype(F32)).astype(BF16)
    w_r_split = jnp.concatenate([w_r_hi, w_r_lo], axis=1)
    b_r = jnp.concatenate([b_rg[0], b_re[0],
                           jnp.zeros((ROUTER_LANES - N_GROUPS - N_EXPERTS,), F32)])[None, :]
    row = lambda a: a[0][None, :]

    xs_pad = jnp.pad(x_sample, ((0, 0), (0, SAMPLE_PAD - ss), (0, 0))).reshape(bs * SAMPLE_PAD, D_MODEL)
    zm_p, zt_p = _inproj(x_prompt.reshape(tp, D_MODEL), w_in_t, b_in, w_tail_t, b_tail, "inproj_prompt")
    zm_s, zt_s = _inproj(xs_pad, w_in_t, b_in, w_tail_t, b_tail, "inproj_sample")

    kv = _matmul(mem_prompt.reshape(bp * N_MEM, D_MODEL), w_kv_b, KV_TN, "mem_kv")
    mk_p = kv[:, :MEM_WIDTH].reshape(bp, N_MEM, MEM_HEADS, MEM_HEAD_DIM)
    mv_p = kv[:, MEM_WIDTH:].reshape(bp, N_MEM, MEM_HEADS, MEM_HEAD_DIM)

    conv_args = (w_dw[0], row(b_dw), row(g_cn), row(b_cn))
    conv_p, buf_p = _conv_group(zm_p, jnp.zeros((bp, CONV_HIST, CONV_WIDTH), F32), *conv_args,
                                bp, sp, 256, "conv_prompt")
    conv_s, buf_s_t = _conv_step_group(
        zm_s.reshape(bs, SAMPLE_PAD, Z_MAIN_COLS), jnp.transpose(state_conv[0], (1, 0, 2)),
        *conv_args, ss, 32, "conv_sample")
    buf_s = jnp.transpose(buf_s_t, (1, 0, 2))

    m_tile = lambda m: jnp.broadcast_to(
        jnp.pad(m, ((0, 0), (0, SUBLANES - MLSTM_HEADS)))[:, :, None], (m.shape[0], SUBLANES, LANES))
    g_mh_r = row(g_mh)
    h_p, c_p, n_p, m_p = _mlstm_group(
        zm_p, zt_p, jnp.zeros((bp, MLSTM_HEADS, MLSTM_HEAD_DIM, MLSTM_HEAD_DIM), F32),
        jnp.zeros((bp, MLSTM_HEADS, MLSTM_HEAD_DIM), F32), jnp.zeros((bp, SUBLANES, LANES), F32),
        g_mh_r, bp, sp, MLSTM_CHUNK, MLSTM_CHUNK, 1, "mlstm_prompt")
    h_s, c_s, n_s, m_s = _mlstm_group(
        zm_s, zt_s, state_mlstm_C[0], state_mlstm_n[0], m_tile(state_mlstm_m[0]),
        g_mh_r, bs, SAMPLE_PAD, SAMPLE_PAD, ss, 8, "mlstm_sample")

    mem_p = _memattn_heads(zt_p, kv, bp, sp, sp, "memattn_prompt")
    packed = lambda c: c[0].reshape(bs, N_MEM * MEM_HEADS, MEM_HEAD_DIM)
    mem_s = _memattn_packed(zt_s, packed(cache_mem_k), packed(cache_mem_v), bs, SAMPLE_PAD, 8,
                            "memattn_sample")

    compact = lambda a: a.reshape(bs, SAMPLE_PAD, a.shape[-1])[:, :ss].reshape(ts, a.shape[-1])
    x1, x1_packed, eid, ew = _outproj_router(
        (conv_p, h_p, mem_p, x_prompt.reshape(tp, D_MODEL)),
        (conv_s.reshape(ts, CONV_WIDTH), compact(h_s), compact(mem_s), x_sample.reshape(ts, D_MODEL)),
        w_out_b, row(g_ln1), row(b_ln1), w_r_split, b_r)

    total = tp + ts
    stok, sdst, start, cnt = _dispatch_tables(eid, total)
    y = _moe(x1_packed, stok, sdst, start, cnt, w_gate[0], w_up[0], w_down[0])
    g2, b2 = row(g_ln2), row(b_ln2)
    y_p = _final(x1, y, ew, g2, b2, 0, tp, "final_prompt").reshape(bp, sp, D_MODEL)
    y_s = _final(x1, y, ew, g2, b2, tp, ts, "final_sample").reshape(bs, ss, D_MODEL)

    return (y_p, y_s, buf_p[None], buf_s[None], c_p[None], c_s[None], n_p[None], n_s[None],
            m_p[:, :MLSTM_HEADS, 0][None], m_s[:, :MLSTM_HEADS, 0][None], mk_p[None], mv_p[None])
```

```python
import functools

import jax
import jax.numpy as jnp
from jax import lax
from jax.experimental import pallas as pl
from jax.experimental.pallas import tpu as pltpu

F32 = jnp.float32
BF16 = jnp.bfloat16

D_MODEL = 2048
CONV_WIDTH = 512
CONV_K = 31
CONV_HIST = CONV_K - 1
MLSTM_HEADS = 4
MLSTM_HEAD_DIM = 256
MLSTM_WIDTH = MLSTM_HEADS * MLSTM_HEAD_DIM
MLSTM_CHUNK = 128
MEM_HEADS = 4
MEM_HEAD_DIM = 128
MEM_WIDTH = MEM_HEADS * MEM_HEAD_DIM
N_MEM = 256
N_GROUPS = 8
EXPERTS_PER_GROUP = 8
N_EXPERTS = N_GROUPS * EXPERTS_PER_GROUP
D_EXPERT = 512
LN_EPS = 1e-5
DEPTH = 1
ALPHA = (2 * DEPTH) ** 0.25

LANES = 128
SUBLANES = 8
VMEM_LIMIT_BYTES = 56 * 1024 * 1024

Z_CONV_A = 0
Z_CONV_G = CONV_WIDTH
Z_Q = 2 * CONV_WIDTH
Z_K = Z_Q + MLSTM_WIDTH
Z_V = Z_K + MLSTM_WIDTH
Z_O = Z_V + MLSTM_WIDTH
Z_MAIN_COLS = Z_O + MLSTM_WIDTH
N_GATE_COLS = 2 * MLSTM_HEADS
ZT_QM = 0
ZT_GATE = MEM_WIDTH
Z_TAIL_COLS = MEM_WIDTH + LANES
INPROJ_TM = 1024
INPROJ_TN = 1024
KV_TN = 256

SAMPLE_PAD = SUBLANES
ROUTER_LANES = LANES
MOE_ROWS = 128
MOE_SLOTS = 3
ROW_DMA_PRIORITY = 1
OUT_TM = 256


def _params(*sem):
    return pltpu.CompilerParams(dimension_semantics=sem, vmem_limit_bytes=VMEM_LIMIT_BYTES)


def _inproj_kernel(x_ref, wm_ref, bm_ref, wt_ref, bt_ref, zm_ref, zt_ref, xb_ref, *, n_main):
    j = pl.program_id(1)

    @pl.when(j == 0)
    def _():
        xb_ref[...] = x_ref[...].astype(BF16)

    nt = (((1,), (1,)), ((), ()))

    @pl.when(j < n_main)
    def _():
        zm_ref[...] = lax.dot_general(xb_ref[...], wm_ref[...], nt,
                                      preferred_element_type=F32) + bm_ref[...]

    @pl.when(j == n_main)
    def _():
        zt_ref[...] = lax.dot_general(xb_ref[...], wt_ref[...], nt,
                                      preferred_element_type=F32) + bt_ref[...]


def _inproj(x, w_in_t, b_in, w_tail_t, b_tail, name):
    t, k = x.shape
    tm, tn = INPROJ_TM, INPROJ_TN
    n_main = Z_MAIN_COLS // tn
    main_col = lambda j: jnp.minimum(j, n_main - 1)
    return pl.pallas_call(
        functools.partial(_inproj_kernel, n_main=n_main),
        out_shape=(jax.ShapeDtypeStruct((t, Z_MAIN_COLS), F32),
                   jax.ShapeDtypeStruct((t, Z_TAIL_COLS), F32)),
        grid=(t // tm, n_main + 1),
        in_specs=[
            pl.BlockSpec((tm, k), lambda i, j: (i, 0)),
            pl.BlockSpec((tn, k), lambda i, j: (main_col(j), 0)),
            pl.BlockSpec((1, tn), lambda i, j: (0, main_col(j))),
            pl.BlockSpec((Z_TAIL_COLS, k), lambda i, j: (0, 0)),
            pl.BlockSpec((1, Z_TAIL_COLS), lambda i, j: (0, 0)),
        ],
        out_specs=(pl.BlockSpec((tm, tn), lambda i, j: (i, main_col(j))),
                   pl.BlockSpec((tm, Z_TAIL_COLS), lambda i, j: (i, 0))),
        scratch_shapes=[pltpu.VMEM((tm, k), BF16)],
        compiler_params=_params("arbitrary", "arbitrary"),
        name=name,
    )(x, w_in_t, b_in, w_tail_t, b_tail)


def _matmul_kernel(x_ref, w_ref, o_ref):
    o_ref[...] = jnp.dot(x_ref[...].astype(BF16), w_ref[...], preferred_element_type=F32)


def _matmul(x, w_bf16, tn, name):
    t, k = x.shape
    n = w_bf16.shape[1]
    return pl.pallas_call(
        _matmul_kernel,
        out_shape=jax.ShapeDtypeStruct((t, n), F32),
        grid=(n // tn,),
        in_specs=[pl.BlockSpec((t, k), lambda j: (0, 0)), pl.BlockSpec((k, tn), lambda j: (0, j))],
        out_specs=pl.BlockSpec((t, tn), lambda j: (0, j)),
        compiler_params=_params("arbitrary"),
        name=name,
    )(x, w_bf16)


def _layer_norm_rows(y, g, b):
    mu = jnp.mean(y, axis=-1, keepdims=True)
    yc = y - mu
    var = jnp.mean(yc * yc, axis=-1, keepdims=True)
    return yc * lax.rsqrt(var + LN_EPS) * g + b


CONV_ROWS = 64


def _conv_kernel(a_ref, g_ref, hist_ref, wdw_ref, bdw_ref, gcn_ref, bcn_ref, out_ref, nb_ref,
                 ubuf, shifted, *, tl):
    head = CONV_HIST + 2
    li = pl.program_id(1)

    @pl.when(li == 0)
    def _():
        ubuf[0:2, :] = jnp.zeros((2, CONV_WIDTH), F32)
        ubuf[2:head, :] = hist_ref[0]

    ubuf[head:head + tl, :] = a_ref[...] * jax.nn.sigmoid(g_ref[...])
    span = shifted.shape[1]
    for k in range(1, SUBLANES):
        shifted[k - 1] = ubuf[k:k + span, :]
    for r0 in range(0, tl, CONV_ROWS):
        acc = jnp.zeros((CONV_ROWS, CONV_WIDTH), F32) + bdw_ref[...]
        for j in range(CONV_K):
            lo, k = divmod(2 + j, SUBLANES)
            lo = lo * SUBLANES + r0
            src = ubuf if k == 0 else shifted.at[k - 1]
            acc = acc + wdw_ref[j:j + 1, :] * src[lo:lo + CONV_ROWS, :]
        y = _layer_norm_rows(acc, gcn_ref[...], bcn_ref[...])
        out_ref[r0:r0 + CONV_ROWS, :] = y * jax.nn.sigmoid(y)

    @pl.when(li == pl.num_programs(1) - 1)
    def _():
        nb_ref[0] = ubuf[2 + tl:head + tl, :]

    ubuf[2:head, :] = ubuf[2 + tl:head + tl, :]


def _conv_group(z, hist, w_dw, b_dw, g_cn, b_cn, batch, seq, tl, name):
    nl = seq // tl
    row = lambda b, l: (b * nl + l, 0)
    vec = pl.BlockSpec((1, CONV_WIDTH), lambda b, l: (0, 0))
    return pl.pallas_call(
        functools.partial(_conv_kernel, tl=tl),
        out_shape=(jax.ShapeDtypeStruct((batch * seq, CONV_WIDTH), F32),
                   jax.ShapeDtypeStruct((batch, CONV_HIST, CONV_WIDTH), F32)),
        grid=(batch, nl),
        in_specs=[
            pl.BlockSpec((tl, CONV_WIDTH), lambda b, l: (b * nl + l, Z_CONV_A // CONV_WIDTH)),
            pl.BlockSpec((tl, CONV_WIDTH), lambda b, l: (b * nl + l, Z_CONV_G // CONV_WIDTH)),
            pl.BlockSpec((1, CONV_HIST, CONV_WIDTH), lambda b, l: (b, 0, 0)),
            pl.BlockSpec((CONV_K, CONV_WIDTH), lambda b, l: (0, 0)),
            vec, vec, vec,
        ],
        out_specs=(pl.BlockSpec((tl, CONV_WIDTH), row),
                   pl.BlockSpec((1, CONV_HIST, CONV_WIDTH), lambda b, l: (b, 0, 0))),
        scratch_shapes=[pltpu.VMEM((CONV_HIST + 2 + tl, CONV_WIDTH), F32),
                        pltpu.VMEM((SUBLANES - 1, CONV_HIST + 2 + tl - SUBLANES, CONV_WIDTH), F32)],
        compiler_params=_params("arbitrary", "arbitrary"),
        name=name,
    )(z, z, hist, w_dw, b_dw, g_cn, b_cn)


def _conv_step_kernel(a_ref, g_ref, hist_ref, wdw_ref, bdw_ref, gcn_ref, bcn_ref, out_ref, nb_ref,
                      *, steps):
    u = [a_ref[:, t, :] * jax.nn.sigmoid(g_ref[:, t, :]) for t in range(steps)]
    full = lambda r: hist_ref[r] if r < CONV_HIST else u[r - CONV_HIST]
    for t in range(steps):
        acc = bdw_ref[...] + wdw_ref[0:1, :] * full(t)
        for j in range(1, CONV_K):
            acc = acc + wdw_ref[j:j + 1, :] * full(t + j)
        y = _layer_norm_rows(acc, gcn_ref[...], bcn_ref[...])
        out_ref[:, t, :] = y * jax.nn.sigmoid(y)
    for r in range(CONV_HIST):
        nb_ref[r] = full(r + steps)


def _conv_step_group(z3, hist_t, w_dw, b_dw, g_cn, b_cn, steps, bb, name):
    batch = z3.shape[0]
    vec = pl.BlockSpec((1, CONV_WIDTH), lambda b: (0, 0))
    hist_spec = pl.BlockSpec((CONV_HIST, bb, CONV_WIDTH), lambda b: (0, b, 0))
    return pl.pallas_call(
        functools.partial(_conv_step_kernel, steps=steps),
        out_shape=(jax.ShapeDtypeStruct((batch, steps, CONV_WIDTH), F32),
                   jax.ShapeDtypeStruct((CONV_HIST, batch, CONV_WIDTH), F32)),
        grid=(batch // bb,),
        in_specs=[
            pl.BlockSpec((bb, SAMPLE_PAD, CONV_WIDTH), lambda b: (b, 0, Z_CONV_A // CONV_WIDTH)),
            pl.BlockSpec((bb, SAMPLE_PAD, CONV_WIDTH), lambda b: (b, 0, Z_CONV_G // CONV_WIDTH)),
            hist_spec,
            pl.BlockSpec((CONV_K, CONV_WIDTH), lambda b: (0, 0)),
            vec, vec, vec,
        ],
        out_specs=(pl.BlockSpec((bb, steps, CONV_WIDTH), lambda b: (b, 0, 0)), hist_spec),
        compiler_params=_params("arbitrary"),
        name=name,
    )(z3, z3, hist_t, w_dw, b_dw, g_cn, b_cn)


def _col_to_row(col, eye):
    n = col.shape[0]
    return jnp.sum(jnp.where(eye, jnp.broadcast_to(col, (n, n)), 0.0), axis=0, keepdims=True)


def _mlstm_kernel(q_ref, k_ref, v_ref, o_ref, gate_ref, c0_ref, n0_ref, m0_ref, gmh_ref,
                  h_ref, c_ref, n_ref, m_ref, *, cl, valid, bb, single_chunk):
    if single_chunk:
        c_in, n_in, m_in = c0_ref, n0_ref, m0_ref
    else:
        c_in, n_in, m_in = c_ref, n_ref, m_ref

        @pl.when(pl.program_id(1) == 0)
        def _():
            c_ref[...] = c0_ref[...]
            n_ref[...] = n0_ref[...]
            m_ref[...] = m0_ref[...]

    rows = lax.broadcasted_iota(jnp.int32, (cl, cl), 0)
    cols = lax.broadcasted_iota(jnp.int32, (cl, cl), 1)
    eye = rows == cols
    tril = rows >= cols
    row_id = lax.broadcasted_iota(jnp.int32, (cl, 1), 0)
    dh = MLSTM_HEAD_DIM
    for bi, h in [(bi, h) for bi in range(bb) for h in range(MLSTM_HEADS)]:
        rs = slice(bi * cl, (bi + 1) * cl)
        sl = slice(h * dh, (h + 1) * dh)
        gates = gate_ref[rs, :]
        q = q_ref[rs, sl]
        k = k_ref[rs, sl] * (dh ** -0.5)
        v = v_ref[rs, sl]
        ip = gates[:, h:h + 1]
        fp = gates[:, MLSTM_HEADS + h:MLSTM_HEADS + h + 1]
        lf = jnp.minimum(fp, 0.0) - jnp.log1p(jnp.exp(-jnp.abs(fp)))
        if valid < cl:
            ip = jnp.where(row_id < valid, ip, -jnp.inf)
            lf = jnp.where(row_id < valid, lf, 0.0)
        lf_row = _col_to_row(lf, eye)
        ip_row = _col_to_row(ip, eye)
        b_col = jnp.sum(jnp.where(tril, jnp.broadcast_to(lf_row, (cl, cl)), 0.0), axis=1, keepdims=True)
        b_row = _col_to_row(b_col, eye)
        m_prev = m_in[bi, h:h + 1, 0:1]
        log_inter = b_col + m_prev
        log_intra = jnp.where(tril, b_col - b_row + ip_row, -jnp.inf)
        m_t = jnp.maximum(log_inter, jnp.max(log_intra, axis=1, keepdims=True))
        w_inter = jnp.exp(log_inter - m_t)
        qb = q.astype(BF16)
        kb = k.astype(BF16)
        vb = v.astype(BF16)
        s = lax.dot_general(qb, kb, (((1,), (1,)), ((), ())), preferred_element_type=F32)
        s = s * jnp.exp(log_intra - m_t)
        c_old = c_in[bi, h]
        n_old = n_in[bi, h:h + 1, :]
        num = (w_inter * jnp.dot(qb, c_old.astype(BF16), preferred_element_type=F32)
               + jnp.dot(s.astype(BF16), vb, preferred_element_type=F32))
        den = w_inter * jnp.sum(q * n_old, axis=1, keepdims=True) + jnp.sum(s, axis=1, keepdims=True)
        hh = num / jnp.maximum(jnp.abs(den), jnp.exp(-m_t))
        m_new = m_t[cl - 1:cl, :]
        b_last = b_col[cl - 1:cl, :]
        decay = jnp.exp(b_last + m_prev - m_new)
        w_s = jnp.exp(b_last - b_col + ip - m_new)
        kw = k * w_s
        c_ref[bi, h] = decay * c_old + lax.dot_general(
            kw.astype(BF16), vb, (((0,), (0,)), ((), ())), preferred_element_type=F32)
        n_ref[bi, h:h + 1, :] = decay * n_old + jnp.sum(kw, axis=0, keepdims=True)
        m_ref[bi, h:h + 1, :] = jnp.broadcast_to(m_new, (1, LANES))
        mu = jnp.mean(hh, axis=-1, keepdims=True)
        hc = hh - mu
        var = jnp.mean(hc * hc, axis=-1, keepdims=True)
        hn = hc * lax.rsqrt(var + LN_EPS) * gmh_ref[:, sl]
        h_ref[rs, sl] = hn * jax.nn.sigmoid(o_ref[rs, sl])
    if single_chunk:
        m_ref[:, MLSTM_HEADS:, :] = jnp.zeros((bb, SUBLANES - MLSTM_HEADS, LANES), F32)


def _mlstm_group(z_main, z_tail, c0, n0, m0, g_mh, batch, seq, cl, valid, bb, name):
    nc = seq // cl
    assert bb == 1 or nc == 1, "several sequences per step only for single-chunk sequences"
    rows = bb * cl
    zcol = lambda off: pl.BlockSpec((rows, MLSTM_WIDTH), lambda b, c: (b * nc + c, off // MLSTM_WIDTH))
    state = lambda shape: pl.BlockSpec((bb,) + shape, lambda b, c: (b,) + (0,) * len(shape))
    c_shape = (MLSTM_HEADS, MLSTM_HEAD_DIM, MLSTM_HEAD_DIM)
    n_shape = (MLSTM_HEADS, MLSTM_HEAD_DIM)
    m_shape = (SUBLANES, LANES)
    return pl.pallas_call(
        functools.partial(_mlstm_kernel, cl=cl, valid=valid, bb=bb, single_chunk=nc == 1),
        out_shape=(jax.ShapeDtypeStruct((batch * seq, MLSTM_WIDTH), F32),
                   jax.ShapeDtypeStruct((batch,) + c_shape, F32),
                   jax.ShapeDtypeStruct((batch,) + n_shape, F32),
                   jax.ShapeDtypeStruct((batch,) + m_shape, F32)),
        grid=(batch // bb, nc),
        in_specs=[
            zcol(Z_Q), zcol(Z_K), zcol(Z_V), zcol(Z_O),
            pl.BlockSpec((rows, LANES), lambda b, c: (b * nc + c, ZT_GATE // LANES)),
            state(c_shape), state(n_shape), state(m_shape),
            pl.BlockSpec((1, MLSTM_WIDTH), lambda b, c: (0, 0)),
        ],
        out_specs=(pl.BlockSpec((rows, MLSTM_WIDTH), lambda b, c: (b * nc + c, 0)),
                   state(c_shape), state(n_shape), state(m_shape)),
        compiler_params=_params("arbitrary", "arbitrary"),
        name=name,
    )(z_main, z_main, z_main, z_main, z_tail, c0, n0, m0, g_mh)


def _softmax_rows(s):
    e = jnp.exp(s - jnp.max(s, axis=-1, keepdims=True))
    return e / jnp.sum(e, axis=-1, keepdims=True)


_NT_DIMS = (((1,), (1,)), ((), ()))


def _memattn_head_kernel(q_ref, k_ref, v_ref, o_ref):
    s = lax.dot_general(q_ref[...].astype(BF16), k_ref[...].astype(BF16), _NT_DIMS,
                        preferred_element_type=F32) * (MEM_HEAD_DIM ** -0.5)
    p = _softmax_rows(s)
    o_ref[...] = jnp.dot(p.astype(BF16), v_ref[...].astype(BF16), preferred_element_type=F32)


def _memattn_heads(z_tail, kv, batch, seq, tq, name):
    nq = seq // tq
    dh = MEM_HEAD_DIM
    return pl.pallas_call(
        _memattn_head_kernel,
        out_shape=jax.ShapeDtypeStruct((batch * seq, MEM_WIDTH), F32),
        grid=(batch, MEM_HEADS, nq),
        in_specs=[pl.BlockSpec((tq, dh), lambda b, h, i: (b * nq + i, ZT_QM // dh + h)),
                  pl.BlockSpec((N_MEM, dh), lambda b, h, i: (b, h)),
                  pl.BlockSpec((N_MEM, dh), lambda b, h, i: (b, MEM_HEADS + h))],
        out_specs=pl.BlockSpec((tq, dh), lambda b, h, i: (b * nq + i, h)),
        compiler_params=_params("arbitrary", "arbitrary", "arbitrary"),
        name=name,
    )(z_tail, kv, kv)


def _memattn_packed_kernel(q_ref, k_ref, v_ref, o_ref, *, tq, bb):
    nh, dh = MEM_HEADS, MEM_HEAD_DIM
    shape = (nh * tq, N_MEM * nh)
    row_head = lax.broadcasted_iota(jnp.int32, shape, 0) // tq
    col_head = lax.broadcasted_iota(jnp.int32, shape, 1) % nh
    same_head = row_head == col_head
    for bi in range(bb):
        rs = slice(bi * tq, (bi + 1) * tq)
        q = jnp.concatenate([q_ref[rs, h * dh:(h + 1) * dh] for h in range(nh)], axis=0)
        s = lax.dot_general(q.astype(BF16), k_ref[bi].astype(BF16), _NT_DIMS,
                            preferred_element_type=F32) * (dh ** -0.5)
        p = _softmax_rows(jnp.where(same_head, s, -jnp.inf))
        o = jnp.dot(p.astype(BF16), v_ref[bi].astype(BF16), preferred_element_type=F32)
        for h in range(nh):
            o_ref[rs, h * dh:(h + 1) * dh] = o[h * tq:(h + 1) * tq, :]


def _memattn_packed(z_tail, mk, mv, batch, tq, bb, name):
    rows = bb * tq
    kv = pl.BlockSpec((bb, N_MEM * MEM_HEADS, MEM_HEAD_DIM), lambda b: (b, 0, 0))
    return pl.pallas_call(
        functools.partial(_memattn_packed_kernel, tq=tq, bb=bb),
        out_shape=jax.ShapeDtypeStruct((batch * tq, MEM_WIDTH), F32),
        grid=(batch // bb,),
        in_specs=[pl.BlockSpec((rows, MEM_WIDTH), lambda b: (b, ZT_QM // MEM_WIDTH)), kv, kv],
        out_specs=pl.BlockSpec((rows, MEM_WIDTH), lambda b: (b, 0)),
        compiler_params=_params("arbitrary"),
        name=name,
    )(z_tail, mk, mv)


def _route(logits):
    lane = lax.broadcasted_iota(jnp.int32, logits.shape, 1).astype(F32)
    neg = -jnp.inf
    first = lambda mask: jnp.min(jnp.where(mask, lane, float(LANES)), axis=1, keepdims=True)
    is_g = lane < N_GROUPS
    gl = jnp.where(is_g, logits, neg)
    g_max = jnp.max(gl, axis=1, keepdims=True)
    g_sel = first(gl == g_max)
    g_w = 1.0 / jnp.sum(jnp.exp(gl - g_max), axis=1, keepdims=True)
    lo = N_GROUPS + g_sel * EXPERTS_PER_GROUP
    in_grp = (lane >= lo) & (lane < lo + EXPERTS_PER_GROUP)
    el = jnp.where(in_grp, logits, neg)
    v1 = jnp.max(el, axis=1, keepdims=True)
    i1 = first(in_grp & (el == v1))
    rest = in_grp & (lane != i1)
    el2 = jnp.where(rest, logits, neg)
    v2 = jnp.max(el2, axis=1, keepdims=True)
    i2 = first(rest & (el2 == v2))
    t = jnp.exp(v2 - v1)
    w1 = g_w / (1.0 + t)
    w2 = g_w * t / (1.0 + t)
    lane_i = lax.broadcasted_iota(jnp.int32, logits.shape, 1)
    e1 = (i1 - N_GROUPS).astype(jnp.int32)
    e2 = (i2 - N_GROUPS).astype(jnp.int32)
    eid = jnp.where(lane_i == 0, e1, jnp.where(lane_i == 1, e2, 0))
    ew = jnp.where(lane_i == 0, w1, jnp.where(lane_i == 1, w2, 0.0))
    return eid, ew


def _outproj_kernel(cp, hp, mp, xp, cs, hs, ms, xs, wout, g1, b1, wr, br,
                    x1_ref, x1p_ref, eid_ref, ew_ref, *, n_prompt):
    def rows(c, h, m, x, rs):
        groups = jnp.concatenate(
            [c[rs, :].astype(BF16), h[rs, :].astype(BF16), m[rs, :].astype(BF16)], axis=1)
        mix = jnp.dot(groups, wout[...], preferred_element_type=F32)
        x1 = _layer_norm_rows(ALPHA * x[rs, :] + mix, g1[...], b1[...])
        x1_ref[rs, :] = x1
        xh = x1.astype(BF16)
        xh_f32 = xh.astype(F32)
        bits = pltpu.bitcast(xh_f32, jnp.int32)
        x1p_ref[rs, :] = bits[:, :D_MODEL // 2] | lax.shift_right_logical(bits[:, D_MODEL // 2:], 16)
        xl = (x1 - xh_f32).astype(BF16)
        n = x1.shape[0]
        cross = jnp.dot(jnp.concatenate([xh, xl], axis=0), wr[...], preferred_element_type=F32)
        logits = ((cross[:n, :ROUTER_LANES] + cross[:n, ROUTER_LANES:])
                  + (cross[n:, :ROUTER_LANES] + cross[n:, ROUTER_LANES:])) + br[...]
        eid, ew = _route(logits)
        eid_ref[rs, :] = eid
        ew_ref[rs, :] = ew

    def body(c, h, m, x):
        rows(c, h, m, x, slice(None))

    i = pl.program_id(0)
    pl.when(i < n_prompt)(lambda: body(cp, hp, mp, xp))
    pl.when(i >= n_prompt)(lambda: body(cs, hs, ms, xs))


def _outproj_router(prompt, sample, w_out_b, g1, b1, wr, br):
    tm = OUT_TM
    tp = prompt[0].shape[0]
    ts = sample[0].shape[0]
    n_p, n_s = tp // tm, ts // tm
    total = tp + ts
    widths = (CONV_WIDTH, MLSTM_WIDTH, MEM_WIDTH, D_MODEL)
    p_specs = [pl.BlockSpec((tm, w), lambda i: (jnp.minimum(i, n_p - 1), 0)) for w in widths]
    s_specs = [pl.BlockSpec((tm, w), lambda i: (jnp.maximum(i - n_p, 0), 0)) for w in widths]
    full = lambda shape: pl.BlockSpec(shape, lambda i: (0, 0))
    row = lambda w: pl.BlockSpec((tm, w), lambda i: (i, 0))
    return pl.pallas_call(
        functools.partial(_outproj_kernel, n_prompt=n_p),
        out_shape=(jax.ShapeDtypeStruct((total, D_MODEL), F32),
                   jax.ShapeDtypeStruct((total, D_MODEL // 2), jnp.int32),
                   jax.ShapeDtypeStruct((total, ROUTER_LANES), jnp.int32),
                   jax.ShapeDtypeStruct((total, ROUTER_LANES), F32)),
        grid=(n_p + n_s,),
        in_specs=p_specs + s_specs + [
            full((D_MODEL, D_MODEL)), full((1, D_MODEL)), full((1, D_MODEL)),
            full((D_MODEL, 2 * ROUTER_LANES)), full((1, ROUTER_LANES)),
        ],
        out_specs=(row(D_MODEL), row(D_MODEL // 2), row(ROUTER_LANES), row(ROUTER_LANES)),
        compiler_params=_params("arbitrary"),
        name="outproj_router",
    )(*prompt, *sample, w_out_b, g1, b1, wr, br)


def _moe_kernel(stok_ref, sdst_ref, base_ref, nval_ref, first_ref, count_ref,
                x_hbm, wg_ref, wu_ref, wd_ref, y_hbm, wgb, wub, wdb, xbuf, ybuf, gsem, ssem,
                *, dump_row):
    e = pl.program_id(0)
    last = pl.num_programs(0) - 1
    rc = MOE_ROWS
    ns = MOE_SLOTS

    def gather_start(base, slot):
        for r in range(rc):
            pltpu.make_async_copy(x_hbm.at[pl.ds(stok_ref[base + r], 1)],
                                  xbuf.at[slot, pl.ds(r, 1)], gsem.at[slot]).start(priority=ROW_DMA_PRIORITY)

    def gather_wait(slot):
        pltpu.make_async_copy(x_hbm.at[pl.ds(0, rc)], xbuf.at[slot], gsem.at[slot]).wait()

    def scatter_start(base, n_valid, slot):
        for r in range(rc):
            dst = jnp.where(r < n_valid, sdst_ref[base + r], dump_row + slot * rc + r)
            pltpu.make_async_copy(ybuf.at[slot, pl.ds(r, 1)], y_hbm.at[pl.ds(dst, 1)],
                                  ssem.at[slot]).start(priority=ROW_DMA_PRIORITY)

    def scatter_wait(slot):
        pltpu.make_async_copy(ybuf.at[slot], y_hbm.at[pl.ds(0, rc)], ssem.at[slot]).wait()

    @pl.when(e == 0)
    def _():
        ybuf[...] = jnp.zeros(ybuf.shape, F32)
        for slot in range(ns):
            pltpu.make_async_copy(ybuf.at[slot], y_hbm.at[pl.ds(dump_row + slot * rc, rc)],
                                  ssem.at[slot]).start()
            scatter_wait(slot)
        gather_start(base_ref[1], 0)
        gather_start(base_ref[2], 1)

    wgb[...] = wg_ref[0].astype(BF16)
    wub[...] = wu_ref[0].astype(BF16)
    wdb[...] = wd_ref[0].astype(BF16)
    first = first_ref[e]

    def chunk_on(slot, g):
        prv = (slot - 1) % ns
        gather_wait(slot)

        @pl.when(g >= ns - 1)
        def _():
            scatter_wait(slot)

        gather_start(base_ref[g + 3], prv)
        scatter_start(base_ref[g], nval_ref[g], prv)
        packed = xbuf[slot]
        x = jnp.concatenate(
            [pltpu.bitcast(packed & jnp.int32(-65536), F32).astype(BF16),
             pltpu.bitcast(packed << 16, F32).astype(BF16)], axis=1)
        hg = jnp.dot(x, wgb[...], preferred_element_type=F32)
        hu = jnp.dot(x, wub[...], preferred_element_type=F32)
        hid = (hg * jax.nn.sigmoid(hg) * hu).astype(BF16)
        ybuf[slot] = jnp.dot(hid, wdb[...], preferred_element_type=F32)

    def chunk(c, carry):
        g = first + c
        cur = lax.rem(g, ns)
        for slot in range(ns):
            pl.when(cur == slot)(functools.partial(chunk_on, slot, g))
        return carry

    lax.fori_loop(0, count_ref[e], chunk, 0)

    @pl.when(e == last)
    def _():
        g_end = first + count_ref[e]
        cur = lax.rem(g_end, ns)
        for slot in range(ns):
            @pl.when(cur == slot)
            def _():
                gather_wait(slot)
                gather_wait((slot + 1) % ns)
                scatter_start(base_ref[g_end], nval_ref[g_end], (slot - 1) % ns)

        for slot in range(ns):
            scatter_wait(slot)


def _moe(x1, tables, w_gate, w_up, w_down):
    total = x1.shape[0]
    dump_row = 2 * total
    wspec = lambda shape: pl.BlockSpec((1,) + shape, lambda e, *_: (e, 0, 0))
    grid_spec = pltpu.PrefetchScalarGridSpec(
        num_scalar_prefetch=len(tables),
        grid=(N_EXPERTS,),
        in_specs=[pl.BlockSpec(memory_space=pl.ANY),
                  wspec((D_MODEL, D_EXPERT)), wspec((D_MODEL, D_EXPERT)), wspec((D_EXPERT, D_MODEL))],
        out_specs=pl.BlockSpec(memory_space=pl.ANY),
        scratch_shapes=[
            pltpu.VMEM((D_MODEL, D_EXPERT), BF16), pltpu.VMEM((D_MODEL, D_EXPERT), BF16),
            pltpu.VMEM((D_EXPERT, D_MODEL), BF16),
            pltpu.VMEM((MOE_SLOTS, MOE_ROWS, D_MODEL // 2), jnp.int32),
            pltpu.VMEM((MOE_SLOTS, MOE_ROWS, D_MODEL), F32),
            pltpu.SemaphoreType.DMA((MOE_SLOTS,)), pltpu.SemaphoreType.DMA((MOE_SLOTS,)),
        ],
    )
    return pl.pallas_call(
        functools.partial(_moe_kernel, dump_row=dump_row),
        out_shape=jax.ShapeDtypeStruct((2 * total + MOE_SLOTS * MOE_ROWS, D_MODEL), F32),
        grid_spec=grid_spec,
        compiler_params=_params("arbitrary"),
        name="moe_experts",
    )(*tables, x1, w_gate, w_up, w_down)


def _final_kernel(x1_ref, y0_ref, y1_ref, ew_ref, g2, b2, o_ref):
    ew = ew_ref[...]
    ffn = ew[:, 0:1] * y0_ref[...] + ew[:, 1:2] * y1_ref[...]
    o_ref[...] = _layer_norm_rows(ALPHA * x1_ref[...] + ffn, g2[...], b2[...])


def _final(x1, y, ew, g2, b2, row0, rows, name):
    tm = OUT_TM
    total = x1.shape[0]
    off = row0 // tm
    k1 = total // tm
    vec = pl.BlockSpec((1, D_MODEL), lambda i: (0, 0))
    return pl.pallas_call(
        _final_kernel,
        out_shape=jax.ShapeDtypeStruct((rows, D_MODEL), F32),
        grid=(rows // tm,),
        in_specs=[pl.BlockSpec((tm, D_MODEL), lambda i: (i + off, 0)),
                  pl.BlockSpec((tm, D_MODEL), lambda i: (i + off, 0)),
                  pl.BlockSpec((tm, D_MODEL), lambda i: (i + off + k1, 0)),
                  pl.BlockSpec((tm, ROUTER_LANES), lambda i: (i + off, 0)),
                  vec, vec],
        out_specs=pl.BlockSpec((tm, D_MODEL), lambda i: (i, 0)),
        compiler_params=_params("arbitrary"),
        name=name,
    )(x1, y, y, ew, g2, b2)


def _dispatch_tables(eid, total):
    flat_e = eid[:, :2].reshape(-1)
    order = jnp.argsort(flat_e, stable=True).astype(jnp.int32)
    stok = order >> 1
    sdst = (order & 1) * total + stok
    experts = jnp.arange(N_EXPERTS, dtype=jnp.int32)
    cnt = jnp.sum((flat_e[:, None] == experts[None, :]).astype(jnp.int32), axis=0)
    start = jnp.cumsum(cnt) - cnt
    pad = jnp.zeros((MOE_ROWS,), jnp.int32)
    n_chunks = jnp.maximum((cnt + MOE_ROWS - 1) // MOE_ROWS, 1)
    first = jnp.cumsum(n_chunks) - n_chunks
    n_entries = flat_e.shape[0] // MOE_ROWS + N_EXPERTS + MOE_SLOTS + 1
    g = jnp.arange(n_entries, dtype=jnp.int32) - 1
    owner = jnp.sum((g[:, None] >= (first + n_chunks)[None, :]).astype(jnp.int32), axis=1)
    real = (g >= 0) & (owner < N_EXPERTS)
    pick = (owner[:, None] == experts[None, :]).astype(jnp.int32)
    sel = lambda v: jnp.sum(pick * v[None, :], axis=1)
    local = (g - sel(first)) * MOE_ROWS
    base = jnp.where(real, sel(start) + local, 0)
    n_valid = jnp.where(real, sel(cnt) - local, 0)
    return (jnp.concatenate([stok, pad]), jnp.concatenate([sdst, pad]), base, n_valid, first, n_chunks)


def kernel(x_prompt, x_sample, mem_prompt, state_conv, state_mlstm_C, state_mlstm_n, state_mlstm_m,
           cache_mem_k, cache_mem_v, w_in, b_in, w_dw, b_dw, g_cn, b_cn, g_mh, w_mk, w_mv, w_out,
           g_ln1, b_ln1, w_rg, b_rg, w_re, b_re, w_gate, w_up, w_down, g_ln2, b_ln2):
    bp, sp, _ = x_prompt.shape
    bs, ss, _ = x_sample.shape
    tp, ts = bp * sp, bs * ss

    gate_hi = Z_MAIN_COLS + N_GATE_COLS
    tail = lambda wt: jnp.concatenate(
        [wt[gate_hi:], wt[Z_MAIN_COLS:gate_hi],
         jnp.zeros((LANES - N_GATE_COLS,) + wt.shape[1:], wt.dtype)], axis=0)
    w_in_t = jnp.transpose(w_in[0]).astype(BF16)
    w_tail_t = tail(w_in_t)
    b_tail = tail(b_in[0])[None, :]
    w_kv_b = jnp.concatenate([w_mk[0], w_mv[0]], axis=-1).astype(BF16)
    w_out_b = w_out[0].astype(BF16)
    w_r = jnp.concatenate([w_rg[0], w_re[0],
                           jnp.zeros((D_MODEL, ROUTER_LANES - N_GROUPS - N_EXPERTS), F32)], axis=-1)
    w_r_hi = w_r.astype(BF16)
    w_r_lo = (w_r - w_r_hi.astype(F32)).astype(BF16)
    w_r_split = jnp.concatenate([w_r_hi, w_r_lo], axis=1)
    b_r = jnp.concatenate([b_rg[0], b_re[0],
                           jnp.zeros((ROUTER_LANES - N_GROUPS - N_EXPERTS,), F32)])[None, :]
    row = lambda a: a[0][None, :]

    xs_pad = jnp.pad(x_sample, ((0, 0), (0, SAMPLE_PAD - ss), (0, 0))).reshape(bs * SAMPLE_PAD, D_MODEL)
    zm_p, zt_p = _inproj(x_prompt.reshape(tp, D_MODEL), w_in_t, b_in, w_tail_t, b_tail, "inproj_prompt")
    zm_s, zt_s = _inproj(xs_pad, w_in_t, b_in, w_tail_t, b_tail, "inproj_sample")

    kv = _matmul(mem_prompt.reshape(bp * N_MEM, D_MODEL), w_kv_b, KV_TN, "mem_kv")
    mk_p = kv[:, :MEM_WIDTH].reshape(bp, N_MEM, MEM_HEADS, MEM_HEAD_DIM)
    mv_p = kv[:, MEM_WIDTH:].reshape(bp, N_MEM, MEM_HEADS, MEM_HEAD_DIM)

    conv_args = (w_dw[0], row(b_dw), row(g_cn), row(b_cn))
    conv_p, buf_p = _conv_group(zm_p, jnp.zeros((bp, CONV_HIST, CONV_WIDTH), F32), *conv_args,
                                bp, sp, 256, "conv_prompt")
    conv_s, buf_s_t = _conv_step_group(
        zm_s.reshape(bs, SAMPLE_PAD, Z_MAIN_COLS), jnp.transpose(state_conv[0], (1, 0, 2)),
        *conv_args, ss, 32, "conv_sample")
    buf_s = jnp.transpose(buf_s_t, (1, 0, 2))

    m_tile = lambda m: jnp.broadcast_to(
        jnp.pad(m, ((0, 0), (0, SUBLANES - MLSTM_HEADS)))[:, :, None], (m.shape[0], SUBLANES, LANES))
    g_mh_r = row(g_mh)
    h_p, c_p, n_p, m_p = _mlstm_group(
        zm_p, zt_p, jnp.zeros((bp, MLSTM_HEADS, MLSTM_HEAD_DIM, MLSTM_HEAD_DIM), F32),
        jnp.zeros((bp, MLSTM_HEADS, MLSTM_HEAD_DIM), F32), jnp.zeros((bp, SUBLANES, LANES), F32),
        g_mh_r, bp, sp, MLSTM_CHUNK, MLSTM_CHUNK, 1, "mlstm_prompt")
    h_s, c_s, n_s, m_s = _mlstm_group(
        zm_s, zt_s, state_mlstm_C[0], state_mlstm_n[0], m_tile(state_mlstm_m[0]),
        g_mh_r, bs, SAMPLE_PAD, SAMPLE_PAD, ss, 8, "mlstm_sample")

    mem_p = _memattn_heads(zt_p, kv, bp, sp, sp, "memattn_prompt")
    packed = lambda c: c[0].reshape(bs, N_MEM * MEM_HEADS, MEM_HEAD_DIM)
    mem_s = _memattn_packed(zt_s, packed(cache_mem_k), packed(cache_mem_v), bs, SAMPLE_PAD, 8,
                            "memattn_sample")

    compact = lambda a: a.reshape(bs, SAMPLE_PAD, a.shape[-1])[:, :ss].reshape(ts, a.shape[-1])
    x1, x1_packed, eid, ew = _outproj_router(
        (conv_p, h_p, mem_p, x_prompt.reshape(tp, D_MODEL)),
        (conv_s.reshape(ts, CONV_WIDTH), compact(h_s), compact(mem_s), x_sample.reshape(ts, D_MODEL)),
        w_out_b, row(g_ln1), row(b_ln1), w_r_split, b_r)

    total = tp + ts
    y = _moe(x1_packed, _dispatch_tables(eid, total), w_gate[0], w_up[0], w_down[0])
    g2, b2 = row(g_ln2), row(b_ln2)
    y_p = _final(x1, y, ew, g2, b2, 0, tp, "final_prompt").reshape(bp, sp, D_MODEL)
    y_s = _final(x1, y, ew, g2, b2, tp, ts, "final_sample").reshape(bs, ss, D_MODEL)

    return (y_p, y_s, buf_p[None], buf_s[None], c_p[None], c_s[None], n_p[None], n_s[None],
            m_p[:, :MLSTM_HEADS, 0][None], m_s[:, :MLSTM_HEADS, 0][None], mk_p[None], mv_p[None])
```

```python
import functools

import jax
import jax.numpy as jnp
from jax import lax
from jax.experimental import pallas as pl
from jax.experimental.pallas import tpu as pltpu

F32 = jnp.float32
BF16 = jnp.bfloat16

D_MODEL = 2048
CONV_WIDTH = 512
CONV_K = 31
CONV_HIST = CONV_K - 1
MLSTM_HEADS = 4
MLSTM_HEAD_DIM = 256
MLSTM_WIDTH = MLSTM_HEADS * MLSTM_HEAD_DIM
MLSTM_CHUNK = 128
MEM_HEADS = 4
MEM_HEAD_DIM = 128
MEM_WIDTH = MEM_HEADS * MEM_HEAD_DIM
N_MEM = 256
N_GROUPS = 8
EXPERTS_PER_GROUP = 8
N_EXPERTS = N_GROUPS * EXPERTS_PER_GROUP
D_EXPERT = 512
LN_EPS = 1e-5
DEPTH = 1
ALPHA = (2 * DEPTH) ** 0.25

LANES = 128
SUBLANES = 8
VMEM_LIMIT_BYTES = 56 * 1024 * 1024

Z_CONV_A = 0
Z_CONV_G = CONV_WIDTH
Z_Q = 2 * CONV_WIDTH
Z_K = Z_Q + MLSTM_WIDTH
Z_V = Z_K + MLSTM_WIDTH
Z_O = Z_V + MLSTM_WIDTH
Z_MAIN_COLS = Z_O + MLSTM_WIDTH
N_GATE_COLS = 2 * MLSTM_HEADS
ZT_QM = 0
ZT_GATE = MEM_WIDTH
Z_TAIL_COLS = MEM_WIDTH + LANES
INPROJ_TM = 1024
INPROJ_TN = 1024
KV_TN = 256

SAMPLE_PAD = SUBLANES
ROUTER_LANES = LANES
MOE_ROWS = 128
MOE_SLOTS = 4
ROW_DMA_PRIORITY = 1
OUT_TM = 256


def _params(*sem):
    return pltpu.CompilerParams(dimension_semantics=sem, vmem_limit_bytes=VMEM_LIMIT_BYTES)


def _inproj_kernel(x_ref, wm_ref, bm_ref, wt_ref, bt_ref, zm_ref, zt_ref, xb_ref, *, n_main):
    j = pl.program_id(1)

    @pl.when(j == 0)
    def _():
        xb_ref[...] = x_ref[...].astype(BF16)

    nt = (((1,), (1,)), ((), ()))

    @pl.when(j < n_main)
    def _():
        zm_ref[...] = lax.dot_general(xb_ref[...], wm_ref[...], nt,
                                      preferred_element_type=F32) + bm_ref[...]

    @pl.when(j == n_main)
    def _():
        zt_ref[...] = lax.dot_general(xb_ref[...], wt_ref[...], nt,
                                      preferred_element_type=F32) + bt_ref[...]


def _inproj(x, w_in_t, b_in, w_tail_t, b_tail, name):
    t, k = x.shape
    tm, tn = INPROJ_TM, INPROJ_TN
    n_main = Z_MAIN_COLS // tn
    main_col = lambda j: jnp.minimum(j, n_main - 1)
    return pl.pallas_call(
        functools.partial(_inproj_kernel, n_main=n_main),
        out_shape=(jax.ShapeDtypeStruct((t, Z_MAIN_COLS), F32),
                   jax.ShapeDtypeStruct((t, Z_TAIL_COLS), F32)),
        grid=(t // tm, n_main + 1),
        in_specs=[
            pl.BlockSpec((tm, k), lambda i, j: (i, 0)),
            pl.BlockSpec((tn, k), lambda i, j: (main_col(j), 0)),
            pl.BlockSpec((1, tn), lambda i, j: (0, main_col(j))),
            pl.BlockSpec((Z_TAIL_COLS, k), lambda i, j: (0, 0)),
            pl.BlockSpec((1, Z_TAIL_COLS), lambda i, j: (0, 0)),
        ],
        out_specs=(pl.BlockSpec((tm, tn), lambda i, j: (i, main_col(j))),
                   pl.BlockSpec((tm, Z_TAIL_COLS), lambda i, j: (i, 0))),
        scratch_shapes=[pltpu.VMEM((tm, k), BF16)],
        compiler_params=_params("arbitrary", "arbitrary"),
        name=name,
    )(x, w_in_t, b_in, w_tail_t, b_tail)


def _matmul_kernel(x_ref, w_ref, o_ref):
    o_ref[...] = jnp.dot(x_ref[...].astype(BF16), w_ref[...], preferred_element_type=F32)


def _matmul(x, w_bf16, tn, name):
    t, k = x.shape
    n = w_bf16.shape[1]
    return pl.pallas_call(
        _matmul_kernel,
        out_shape=jax.ShapeDtypeStruct((t, n), F32),
        grid=(n // tn,),
        in_specs=[pl.BlockSpec((t, k), lambda j: (0, 0)), pl.BlockSpec((k, tn), lambda j: (0, j))],
        out_specs=pl.BlockSpec((t, tn), lambda j: (0, j)),
        compiler_params=_params("arbitrary"),
        name=name,
    )(x, w_bf16)


def _layer_norm_rows(y, g, b):
    mu = jnp.mean(y, axis=-1, keepdims=True)
    yc = y - mu
    var = jnp.mean(yc * yc, axis=-1, keepdims=True)
    return yc * lax.rsqrt(var + LN_EPS) * g + b


CONV_ROWS = 64


def _conv_kernel(a_ref, g_ref, hist_ref, wdw_ref, bdw_ref, gcn_ref, bcn_ref, out_ref, nb_ref,
                 ubuf, shifted, *, tl):
    head = CONV_HIST + 2
    li = pl.program_id(1)

    @pl.when(li == 0)
    def _():
        ubuf[0:2, :] = jnp.zeros((2, CONV_WIDTH), F32)
        ubuf[2:head, :] = hist_ref[0]

    ubuf[head:head + tl, :] = a_ref[...] * jax.nn.sigmoid(g_ref[...])
    span = shifted.shape[1]
    for k in range(1, SUBLANES):
        shifted[k - 1] = ubuf[k:k + span, :]
    for r0 in range(0, tl, CONV_ROWS):
        acc = jnp.zeros((CONV_ROWS, CONV_WIDTH), F32) + bdw_ref[...]
        for j in range(CONV_K):
            lo, k = divmod(2 + j, SUBLANES)
            lo = lo * SUBLANES + r0
            src = ubuf if k == 0 else shifted.at[k - 1]
            acc = acc + wdw_ref[j:j + 1, :] * src[lo:lo + CONV_ROWS, :]
        y = _layer_norm_rows(acc, gcn_ref[...], bcn_ref[...])
        out_ref[r0:r0 + CONV_ROWS, :] = y * jax.nn.sigmoid(y)

    @pl.when(li == pl.num_programs(1) - 1)
    def _():
        nb_ref[0] = ubuf[2 + tl:head + tl, :]

    ubuf[2:head, :] = ubuf[2 + tl:head + tl, :]


def _conv_group(z, hist, w_dw, b_dw, g_cn, b_cn, batch, seq, tl, name):
    nl = seq // tl
    row = lambda b, l: (b * nl + l, 0)
    vec = pl.BlockSpec((1, CONV_WIDTH), lambda b, l: (0, 0))
    return pl.pallas_call(
        functools.partial(_conv_kernel, tl=tl),
        out_shape=(jax.ShapeDtypeStruct((batch * seq, CONV_WIDTH), F32),
                   jax.ShapeDtypeStruct((batch, CONV_HIST, CONV_WIDTH), F32)),
        grid=(batch, nl),
        in_specs=[
            pl.BlockSpec((tl, CONV_WIDTH), lambda b, l: (b * nl + l, Z_CONV_A // CONV_WIDTH)),
            pl.BlockSpec((tl, CONV_WIDTH), lambda b, l: (b * nl + l, Z_CONV_G // CONV_WIDTH)),
            pl.BlockSpec((1, CONV_HIST, CONV_WIDTH), lambda b, l: (b, 0, 0)),
            pl.BlockSpec((CONV_K, CONV_WIDTH), lambda b, l: (0, 0)),
            vec, vec, vec,
        ],
        out_specs=(pl.BlockSpec((tl, CONV_WIDTH), row),
                   pl.BlockSpec((1, CONV_HIST, CONV_WIDTH), lambda b, l: (b, 0, 0))),
        scratch_shapes=[pltpu.VMEM((CONV_HIST + 2 + tl, CONV_WIDTH), F32),
                        pltpu.VMEM((SUBLANES - 1, CONV_HIST + 2 + tl - SUBLANES, CONV_WIDTH), F32)],
        compiler_params=_params("arbitrary", "arbitrary"),
        name=name,
    )(z, z, hist, w_dw, b_dw, g_cn, b_cn)


def _conv_step_kernel(a_ref, g_ref, hist_ref, wdw_ref, bdw_ref, gcn_ref, bcn_ref, out_ref, nb_ref,
                      *, steps):
    u = [a_ref[:, t, :] * jax.nn.sigmoid(g_ref[:, t, :]) for t in range(steps)]
    full = lambda r: hist_ref[r] if r < CONV_HIST else u[r - CONV_HIST]
    for t in range(steps):
        acc = bdw_ref[...] + wdw_ref[0:1, :] * full(t)
        for j in range(1, CONV_K):
            acc = acc + wdw_ref[j:j + 1, :] * full(t + j)
        y = _layer_norm_rows(acc, gcn_ref[...], bcn_ref[...])
        out_ref[:, t, :] = y * jax.nn.sigmoid(y)
    for r in range(CONV_HIST):
        nb_ref[r] = full(r + steps)


def _conv_step_group(z3, hist_t, w_dw, b_dw, g_cn, b_cn, steps, bb, name):
    batch = z3.shape[0]
    vec = pl.BlockSpec((1, CONV_WIDTH), lambda b: (0, 0))
    hist_spec = pl.BlockSpec((CONV_HIST, bb, CONV_WIDTH), lambda b: (0, b, 0))
    return pl.pallas_call(
        functools.partial(_conv_step_kernel, steps=steps),
        out_shape=(jax.ShapeDtypeStruct((batch, steps, CONV_WIDTH), F32),
                   jax.ShapeDtypeStruct((CONV_HIST, batch, CONV_WIDTH), F32)),
        grid=(batch // bb,),
        in_specs=[
            pl.BlockSpec((bb, SAMPLE_PAD, CONV_WIDTH), lambda b: (b, 0, Z_CONV_A // CONV_WIDTH)),
            pl.BlockSpec((bb, SAMPLE_PAD, CONV_WIDTH), lambda b: (b, 0, Z_CONV_G // CONV_WIDTH)),
            hist_spec,
            pl.BlockSpec((CONV_K, CONV_WIDTH), lambda b: (0, 0)),
            vec, vec, vec,
        ],
        out_specs=(pl.BlockSpec((bb, steps, CONV_WIDTH), lambda b: (b, 0, 0)), hist_spec),
        compiler_params=_params("arbitrary"),
        name=name,
    )(z3, z3, hist_t, w_dw, b_dw, g_cn, b_cn)


def _col_to_row(col, eye):
    n = col.shape[0]
    return jnp.sum(jnp.where(eye, jnp.broadcast_to(col, (n, n)), 0.0), axis=0, keepdims=True)


def _mlstm_kernel(q_ref, k_ref, v_ref, o_ref, gate_ref, c0_ref, n0_ref, m0_ref, gmh_ref,
                  h_ref, c_ref, n_ref, m_ref, *, cl, valid, bb, single_chunk):
    if single_chunk:
        c_in, n_in, m_in = c0_ref, n0_ref, m0_ref
    else:
        c_in, n_in, m_in = c_ref, n_ref, m_ref

        @pl.when(pl.program_id(1) == 0)
        def _():
            c_ref[...] = c0_ref[...]
            n_ref[...] = n0_ref[...]
            m_ref[...] = m0_ref[...]

    rows = lax.broadcasted_iota(jnp.int32, (cl, cl), 0)
    cols = lax.broadcasted_iota(jnp.int32, (cl, cl), 1)
    eye = rows == cols
    tril = rows >= cols
    row_id = lax.broadcasted_iota(jnp.int32, (cl, 1), 0)
    dh = MLSTM_HEAD_DIM
    for bi, h in [(bi, h) for bi in range(bb) for h in range(MLSTM_HEADS)]:
        sl = slice(h * dh, (h + 1) * dh)
        if h == 0:
            gates = gate_ref[bi]
            log_sig = jnp.minimum(gates, 0.0) - jnp.log1p(jnp.exp(-jnp.abs(gates)))
        q = q_ref[bi, :, sl]
        k = k_ref[bi, :, sl] * (dh ** -0.5)
        v = v_ref[bi, :, sl]
        ip = gates[:, h:h + 1]
        lf = log_sig[:, MLSTM_HEADS + h:MLSTM_HEADS + h + 1]
        if valid < cl:
            ip = jnp.where(row_id < valid, ip, -jnp.inf)
            lf = jnp.where(row_id < valid, lf, 0.0)
        lf_row = _col_to_row(lf, eye)
        ip_row = _col_to_row(ip, eye)
        b_col = jnp.sum(jnp.where(tril, jnp.broadcast_to(lf_row, (cl, cl)), 0.0), axis=1, keepdims=True)
        b_row = _col_to_row(b_col, eye)
        m_prev = m_in[bi, h:h + 1, 0:1]
        log_inter = b_col + m_prev
        log_intra = jnp.where(tril, b_col - b_row + ip_row, -jnp.inf)
        m_t = jnp.maximum(log_inter, jnp.max(log_intra, axis=1, keepdims=True))
        w_inter = jnp.exp(log_inter - m_t)
        qb = q.astype(BF16)
        kb = k.astype(BF16)
        vb = v.astype(BF16)
        s = lax.dot_general(qb, kb, (((1,), (1,)), ((), ())), preferred_element_type=F32)
        s = s * jnp.exp(log_intra - m_t)
        c_old = c_in[bi, h]
        n_old = n_in[bi, h:h + 1, :]
        num = (w_inter * jnp.dot(qb, c_old.astype(BF16), preferred_element_type=F32)
               + jnp.dot(s.astype(BF16), vb, preferred_element_type=F32))
        den = w_inter * jnp.sum(q * n_old, axis=1, keepdims=True) + jnp.sum(s, axis=1, keepdims=True)
        hh = num / jnp.maximum(jnp.abs(den), jnp.exp(-m_t))
        m_new = m_t[cl - 1:cl, :]
        b_last = b_col[cl - 1:cl, :]
        decay = jnp.exp(b_last + m_prev - m_new)
        w_s = jnp.exp(b_last - b_col + ip - m_new)
        kw = k * w_s
        c_ref[bi, h] = decay * c_old + lax.dot_general(
            kw.astype(BF16), vb, (((0,), (0,)), ((), ())), preferred_element_type=F32)
        n_ref[bi, h:h + 1, :] = decay * n_old + jnp.sum(kw, axis=0, keepdims=True)
        m_ref[bi, h:h + 1, :] = jnp.broadcast_to(m_new, (1, LANES))
        mu = jnp.mean(hh, axis=-1, keepdims=True)
        hc = hh - mu
        var = jnp.mean(hc * hc, axis=-1, keepdims=True)
        hn = hc * lax.rsqrt(var + LN_EPS) * gmh_ref[:, sl]
        h_ref[bi, :, sl] = hn * jax.nn.sigmoid(o_ref[bi, :, sl])
    if single_chunk:
        m_ref[:, MLSTM_HEADS:, :] = jnp.zeros((bb, SUBLANES - MLSTM_HEADS, LANES), F32)


def _mlstm_group(z_main, z_tail, c0, n0, m0, g_mh, batch, seq, cl, valid, bb, name):
    nc = seq // cl
    zcol = lambda off: pl.BlockSpec((bb, cl, MLSTM_WIDTH), lambda b, c: (b, c, off // MLSTM_WIDTH))
    state = lambda shape: pl.BlockSpec((bb,) + shape, lambda b, c: (b,) + (0,) * len(shape))
    c_shape = (MLSTM_HEADS, MLSTM_HEAD_DIM, MLSTM_HEAD_DIM)
    n_shape = (MLSTM_HEADS, MLSTM_HEAD_DIM)
    m_shape = (SUBLANES, LANES)
    return pl.pallas_call(
        functools.partial(_mlstm_kernel, cl=cl, valid=valid, bb=bb, single_chunk=nc == 1),
        out_shape=(jax.ShapeDtypeStruct((batch, seq, MLSTM_WIDTH), F32),
                   jax.ShapeDtypeStruct((batch,) + c_shape, F32),
                   jax.ShapeDtypeStruct((batch,) + n_shape, F32),
                   jax.ShapeDtypeStruct((batch,) + m_shape, F32)),
        grid=(batch // bb, nc),
        in_specs=[
            zcol(Z_Q), zcol(Z_K), zcol(Z_V), zcol(Z_O),
            pl.BlockSpec((bb, cl, LANES), lambda b, c: (b, c, ZT_GATE // LANES)),
            state(c_shape), state(n_shape), state(m_shape),
            pl.BlockSpec((1, MLSTM_WIDTH), lambda b, c: (0, 0)),
        ],
        out_specs=(pl.BlockSpec((bb, cl, MLSTM_WIDTH), lambda b, c: (b, c, 0)),
                   state(c_shape), state(n_shape), state(m_shape)),
        compiler_params=_params("arbitrary", "arbitrary"),
        name=name,
    )(z_main, z_main, z_main, z_main, z_tail, c0, n0, m0, g_mh)


def _softmax_rows(s):
    e = jnp.exp(s - jnp.max(s, axis=-1, keepdims=True))
    return e / jnp.sum(e, axis=-1, keepdims=True)


_NT_DIMS = (((1,), (1,)), ((), ()))


def _memattn_head_kernel(q_ref, k_ref, v_ref, o_ref):
    s = lax.dot_general(q_ref[...].astype(BF16), k_ref[...].astype(BF16), _NT_DIMS,
                        preferred_element_type=F32) * (MEM_HEAD_DIM ** -0.5)
    p = _softmax_rows(s)
    o_ref[...] = jnp.dot(p.astype(BF16), v_ref[...].astype(BF16), preferred_element_type=F32)


def _memattn_heads(z_tail, kv, batch, seq, tq, name):
    nq = seq // tq
    dh = MEM_HEAD_DIM
    return pl.pallas_call(
        _memattn_head_kernel,
        out_shape=jax.ShapeDtypeStruct((batch * seq, MEM_WIDTH), F32),
        grid=(batch, MEM_HEADS, nq),
        in_specs=[pl.BlockSpec((tq, dh), lambda b, h, i: (b * nq + i, ZT_QM // dh + h)),
                  pl.BlockSpec((N_MEM, dh), lambda b, h, i: (b, h)),
                  pl.BlockSpec((N_MEM, dh), lambda b, h, i: (b, MEM_HEADS + h))],
        out_specs=pl.BlockSpec((tq, dh), lambda b, h, i: (b * nq + i, h)),
        compiler_params=_params("arbitrary", "arbitrary", "arbitrary"),
        name=name,
    )(z_tail, kv, kv)


def _memattn_packed_kernel(q_ref, k_ref, v_ref, o_ref, *, tq, bb):
    nh, dh = MEM_HEADS, MEM_HEAD_DIM
    shape = (nh * tq, N_MEM * nh)
    row_head = lax.broadcasted_iota(jnp.int32, shape, 0) // tq
    col_head = lax.broadcasted_iota(jnp.int32, shape, 1) % nh
    same_head = row_head == col_head
    for bi in range(bb):
        rs = slice(bi * tq, (bi + 1) * tq)
        q = jnp.concatenate([q_ref[rs, h * dh:(h + 1) * dh] for h in range(nh)], axis=0)
        s = lax.dot_general(q.astype(BF16), k_ref[bi].astype(BF16), _NT_DIMS,
                            preferred_element_type=F32) * (dh ** -0.5)
        p = _softmax_rows(jnp.where(same_head, s, -jnp.inf))
        o = jnp.dot(p.astype(BF16), v_ref[bi].astype(BF16), preferred_element_type=F32)
        for h in range(nh):
            o_ref[rs, h * dh:(h + 1) * dh] = o[h * tq:(h + 1) * tq, :]


def _memattn_packed(z_tail, mk, mv, batch, tq, bb, name):
    rows = bb * tq
    kv = pl.BlockSpec((bb, N_MEM * MEM_HEADS, MEM_HEAD_DIM), lambda b: (b, 0, 0))
    return pl.pallas_call(
        functools.partial(_memattn_packed_kernel, tq=tq, bb=bb),
        out_shape=jax.ShapeDtypeStruct((batch * tq, MEM_WIDTH), F32),
        grid=(batch // bb,),
        in_specs=[pl.BlockSpec((rows, MEM_WIDTH), lambda b: (b, ZT_QM // MEM_WIDTH)), kv, kv],
        out_specs=pl.BlockSpec((rows, MEM_WIDTH), lambda b: (b, 0)),
        compiler_params=_params("arbitrary"),
        name=name,
    )(z_tail, mk, mv)


def _route(logits):
    lane = lax.broadcasted_iota(jnp.int32, logits.shape, 1).astype(F32)
    neg = -jnp.inf
    first = lambda mask: jnp.min(jnp.where(mask, lane, float(LANES)), axis=1, keepdims=True)
    is_g = lane < N_GROUPS
    gl = jnp.where(is_g, logits, neg)
    g_max = jnp.max(gl, axis=1, keepdims=True)
    g_sel = first(gl == g_max)
    g_w = 1.0 / jnp.sum(jnp.exp(gl - g_max), axis=1, keepdims=True)
    lo = N_GROUPS + g_sel * EXPERTS_PER_GROUP
    in_grp = (lane >= lo) & (lane < lo + EXPERTS_PER_GROUP)
    el = jnp.where(in_grp, logits, neg)
    v1 = jnp.max(el, axis=1, keepdims=True)
    i1 = first(in_grp & (el == v1))
    rest = in_grp & (lane != i1)
    el2 = jnp.where(rest, logits, neg)
    v2 = jnp.max(el2, axis=1, keepdims=True)
    i2 = first(rest & (el2 == v2))
    t = jnp.exp(v2 - v1)
    w1 = g_w / (1.0 + t)
    w2 = g_w * t / (1.0 + t)
    lane_i = lax.broadcasted_iota(jnp.int32, logits.shape, 1)
    e1 = (i1 - N_GROUPS).astype(jnp.int32)
    e2 = (i2 - N_GROUPS).astype(jnp.int32)
    eid = jnp.where(lane_i == 0, e1, jnp.where(lane_i == 1, e2, 0))
    ew = jnp.where(lane_i == 0, w1, jnp.where(lane_i == 1, w2, 0.0))
    return eid, ew


def _outproj_kernel(cp, hp, mp, xp, cs, hs, ms, xs, wout, g1, b1, wr, br,
                    x1_ref, x1p_ref, eid_ref, ew_ref, *, n_prompt):
    def rows(c, h, m, x, rs):
        groups = jnp.concatenate(
            [c[rs, :].astype(BF16), h[rs, :].astype(BF16), m[rs, :].astype(BF16)], axis=1)
        mix = jnp.dot(groups, wout[...], preferred_element_type=F32)
        x1 = _layer_norm_rows(ALPHA * x[rs, :] + mix, g1[...], b1[...])
        x1_ref[rs, :] = x1
        xh = x1.astype(BF16)
        xh_f32 = xh.astype(F32)
        bits = pltpu.bitcast(xh_f32, jnp.int32)
        x1p_ref[rs, :] = bits[:, :D_MODEL // 2] | lax.shift_right_logical(bits[:, D_MODEL // 2:], 16)
        xl = (x1 - xh_f32).astype(BF16)
        n = x1.shape[0]
        cross = jnp.dot(jnp.concatenate([xh, xl], axis=0), wr[...], preferred_element_type=F32)
        logits = ((cross[:n, :ROUTER_LANES] + cross[:n, ROUTER_LANES:])
                  + (cross[n:, :ROUTER_LANES] + cross[n:, ROUTER_LANES:])) + br[...]
        eid, ew = _route(logits)
        eid_ref[rs, :] = eid
        ew_ref[rs, :] = ew

    def body(c, h, m, x):
        rows(c, h, m, x, slice(None))

    i = pl.program_id(0)
    pl.when(i < n_prompt)(lambda: body(cp, hp, mp, xp))
    pl.when(i >= n_prompt)(lambda: body(cs, hs, ms, xs))


def _outproj_router(prompt, sample, w_out_b, g1, b1, wr, br):
    tm = OUT_TM
    tp = prompt[0].shape[0]
    ts = sample[0].shape[0]
    n_p, n_s = tp // tm, ts // tm
    total = tp + ts
    widths = (CONV_WIDTH, MLSTM_WIDTH, MEM_WIDTH, D_MODEL)
    p_specs = [pl.BlockSpec((tm, w), lambda i: (jnp.minimum(i, n_p - 1), 0)) for w in widths]
    s_specs = [pl.BlockSpec((tm, w), lambda i: (jnp.maximum(i - n_p, 0), 0)) for w in widths]
    full = lambda shape: pl.BlockSpec(shape, lambda i: (0, 0))
    row = lambda w: pl.BlockSpec((tm, w), lambda i: (i, 0))
    return pl.pallas_call(
        functools.partial(_outproj_kernel, n_prompt=n_p),
        out_shape=(jax.ShapeDtypeStruct((total, D_MODEL), F32),
                   jax.ShapeDtypeStruct((total, D_MODEL // 2), jnp.int32),
                   jax.ShapeDtypeStruct((total, ROUTER_LANES), jnp.int32),
                   jax.ShapeDtypeStruct((total, ROUTER_LANES), F32)),
        grid=(n_p + n_s,),
        in_specs=p_specs + s_specs + [
            full((D_MODEL, D_MODEL)), full((1, D_MODEL)), full((1, D_MODEL)),
            full((D_MODEL, 2 * ROUTER_LANES)), full((1, ROUTER_LANES)),
        ],
        out_specs=(row(D_MODEL), row(D_MODEL // 2), row(ROUTER_LANES), row(ROUTER_LANES)),
        compiler_params=_params("arbitrary"),
        name="outproj_router",
    )(*prompt, *sample, w_out_b, g1, b1, wr, br)


def _moe_kernel(stok_ref, sdst_ref, base_ref, nval_ref, first_ref, count_ref,
                x_hbm, wg_ref, wu_ref, wd_ref, y_hbm, wgb, wub, wdb, xbuf, ybuf, gsem, ssem,
                *, dump_row):
    e = pl.program_id(0)
    last = pl.num_programs(0) - 1
    rc = MOE_ROWS
    ns = MOE_SLOTS

    def gather_start(base, slot):
        for r in range(rc):
            pltpu.make_async_copy(x_hbm.at[pl.ds(stok_ref[base + r], 1)],
                                  xbuf.at[slot, pl.ds(r, 1)], gsem.at[slot]).start(priority=ROW_DMA_PRIORITY)

    def gather_wait(slot):
        pltpu.make_async_copy(x_hbm.at[pl.ds(0, rc)], xbuf.at[slot], gsem.at[slot]).wait()

    def scatter_start(base, n_valid, slot):
        for r in range(rc):
            dst = jnp.where(r < n_valid, sdst_ref[base + r], dump_row + slot * rc + r)
            pltpu.make_async_copy(ybuf.at[slot, pl.ds(r, 1)], y_hbm.at[pl.ds(dst, 1)],
                                  ssem.at[slot]).start(priority=ROW_DMA_PRIORITY)

    def scatter_wait(slot):
        pltpu.make_async_copy(ybuf.at[slot], y_hbm.at[pl.ds(0, rc)], ssem.at[slot]).wait()

    @pl.when(e == 0)
    def _():
        ybuf[...] = jnp.zeros(ybuf.shape, F32)
        for slot in range(ns):
            pltpu.make_async_copy(ybuf.at[slot], y_hbm.at[pl.ds(dump_row + slot * rc, rc)],
                                  ssem.at[slot]).start()
            scatter_wait(slot)
        for ahead in range(ns - 1):
            gather_start(base_ref[1 + ahead], ahead)

    wgb[...] = wg_ref[0].astype(BF16)
    wub[...] = wu_ref[0].astype(BF16)
    wdb[...] = wd_ref[0].astype(BF16)
    first = first_ref[e]

    def chunk_on(slot, g):
        prv = (slot - 1) % ns
        gather_wait(slot)

        @pl.when(g >= ns - 1)
        def _():
            scatter_wait(slot)

        gather_start(base_ref[g + ns], prv)
        scatter_start(base_ref[g], nval_ref[g], prv)
        packed = xbuf[slot]
        x = jnp.concatenate(
            [pltpu.bitcast(packed & jnp.int32(-65536), F32).astype(BF16),
             pltpu.bitcast(packed << 16, F32).astype(BF16)], axis=1)
        hg = jnp.dot(x, wgb[...], preferred_element_type=F32)
        hu = jnp.dot(x, wub[...], preferred_element_type=F32)
        hid = (hg * jax.nn.sigmoid(hg) * hu).astype(BF16)
        ybuf[slot] = jnp.dot(hid, wdb[...], preferred_element_type=F32)

    def chunk(c, carry):
        g = first + c
        cur = lax.rem(g, ns)
        for slot in range(ns):
            pl.when(cur == slot)(functools.partial(chunk_on, slot, g))
        return carry

    lax.fori_loop(0, count_ref[e], chunk, 0)

    @pl.when(e == last)
    def _():
        g_end = first + count_ref[e]
        cur = lax.rem(g_end, ns)
        for slot in range(ns):
            @pl.when(cur == slot)
            def _():
                for ahead in range(ns - 1):
                    gather_wait((slot + ahead) % ns)
                scatter_start(base_ref[g_end], nval_ref[g_end], (slot - 1) % ns)

        for slot in range(ns):
            scatter_wait(slot)


def _moe(x1, tables, w_gate, w_up, w_down):
    total = x1.shape[0]
    dump_row = 2 * total
    wspec = lambda shape: pl.BlockSpec((1,) + shape, lambda e, *_: (e, 0, 0))
    grid_spec = pltpu.PrefetchScalarGridSpec(
        num_scalar_prefetch=len(tables),
        grid=(N_EXPERTS,),
        in_specs=[pl.BlockSpec(memory_space=pl.ANY),
                  wspec((D_MODEL, D_EXPERT)), wspec((D_MODEL, D_EXPERT)), wspec((D_EXPERT, D_MODEL))],
        out_specs=pl.BlockSpec(memory_space=pl.ANY),
        scratch_shapes=[
            pltpu.VMEM((D_MODEL, D_EXPERT), BF16), pltpu.VMEM((D_MODEL, D_EXPERT), BF16),
            pltpu.VMEM((D_EXPERT, D_MODEL), BF16),
            pltpu.VMEM((MOE_SLOTS, MOE_ROWS, D_MODEL // 2), jnp.int32),
            pltpu.VMEM((MOE_SLOTS, MOE_ROWS, D_MODEL), F32),
            pltpu.SemaphoreType.DMA((MOE_SLOTS,)), pltpu.SemaphoreType.DMA((MOE_SLOTS,)),
        ],
    )
    return pl.pallas_call(
        functools.partial(_moe_kernel, dump_row=dump_row),
        out_shape=jax.ShapeDtypeStruct((2 * total + MOE_SLOTS * MOE_ROWS, D_MODEL), F32),
        grid_spec=grid_spec,
        compiler_params=_params("arbitrary"),
        name="moe_experts",
    )(*tables, x1, w_gate, w_up, w_down)


def _final_kernel(x1_ref, y0_ref, y1_ref, ew_ref, g2, b2, o_ref):
    ew = ew_ref[...]
    ffn = ew[:, 0:1] * y0_ref[...] + ew[:, 1:2] * y1_ref[...]
    o_ref[...] = _layer_norm_rows(ALPHA * x1_ref[...] + ffn, g2[...], b2[...])


def _final(x1, y, ew, g2, b2, row0, rows, name):
    tm = OUT_TM
    total = x1.shape[0]
    off = row0 // tm
    k1 = total // tm
    vec = pl.BlockSpec((1, D_MODEL), lambda i: (0, 0))
    return pl.pallas_call(
        _final_kernel,
        out_shape=jax.ShapeDtypeStruct((rows, D_MODEL), F32),
        grid=(rows // tm,),
        in_specs=[pl.BlockSpec((tm, D_MODEL), lambda i: (i + off, 0)),
                  pl.BlockSpec((tm, D_MODEL), lambda i: (i + off, 0)),
                  pl.BlockSpec((tm, D_MODEL), lambda i: (i + off + k1, 0)),
                  pl.BlockSpec((tm, ROUTER_LANES), lambda i: (i + off, 0)),
                  vec, vec],
        out_specs=pl.BlockSpec((tm, D_MODEL), lambda i: (i, 0)),
        compiler_params=_params("arbitrary"),
        name=name,
    )(x1, y, y, ew, g2, b2)


def _dispatch_tables(eid, total):
    flat_e = eid[:, :2].reshape(-1)
    order = jnp.argsort(flat_e, stable=True).astype(jnp.int32)
    stok = order >> 1
    sdst = (order & 1) * total + stok
    experts = jnp.arange(N_EXPERTS, dtype=jnp.int32)
    cnt = jnp.sum((flat_e[:, None] == experts[None, :]).astype(jnp.int32), axis=0)
    start = jnp.cumsum(cnt) - cnt
    pad = jnp.zeros((MOE_ROWS,), jnp.int32)
    n_chunks = jnp.maximum((cnt + MOE_ROWS - 1) // MOE_ROWS, 1)
    first = jnp.cumsum(n_chunks) - n_chunks
    n_entries = flat_e.shape[0] // MOE_ROWS + N_EXPERTS + MOE_SLOTS + 1
    g = jnp.arange(n_entries, dtype=jnp.int32) - 1
    owner = jnp.sum((g[:, None] >= (first + n_chunks)[None, :]).astype(jnp.int32), axis=1)
    real = (g >= 0) & (owner < N_EXPERTS)
    pick = (owner[:, None] == experts[None, :]).astype(jnp.int32)
    sel = lambda v: jnp.sum(pick * v[None, :], axis=1)
    local = (g - sel(first)) * MOE_ROWS
    base = jnp.where(real, sel(start) + local, 0)
    n_valid = jnp.where(real, sel(cnt) - local, 0)
    return (jnp.concatenate([stok, pad]), jnp.concatenate([sdst, pad]), base, n_valid, first, n_chunks)


def kernel(x_prompt, x_sample, mem_prompt, state_conv, state_mlstm_C, state_mlstm_n, state_mlstm_m,
           cache_mem_k, cache_mem_v, w_in, b_in, w_dw, b_dw, g_cn, b_cn, g_mh, w_mk, w_mv, w_out,
           g_ln1, b_ln1, w_rg, b_rg, w_re, b_re, w_gate, w_up, w_down, g_ln2, b_ln2):
    bp, sp, _ = x_prompt.shape
    bs, ss, _ = x_sample.shape
    tp, ts = bp * sp, bs * ss

    gate_hi = Z_MAIN_COLS + N_GATE_COLS
    tail = lambda wt: jnp.concatenate(
        [wt[gate_hi:], wt[Z_MAIN_COLS:gate_hi],
         jnp.zeros((LANES - N_GATE_COLS,) + wt.shape[1:], wt.dtype)], axis=0)
    w_in_t = jnp.transpose(w_in[0]).astype(BF16)
    w_tail_t = tail(w_in_t)
    b_tail = tail(b_in[0])[None, :]
    w_kv_b = jnp.concatenate([w_mk[0], w_mv[0]], axis=-1).astype(BF16)
    w_out_b = w_out[0].astype(BF16)
    w_r = jnp.concatenate([w_rg[0], w_re[0],
                           jnp.zeros((D_MODEL, ROUTER_LANES - N_GROUPS - N_EXPERTS), F32)], axis=-1)
    w_r_hi = w_r.astype(BF16)
    w_r_lo = (w_r - w_r_hi.astype(F32)).astype(BF16)
    w_r_split = jnp.concatenate([w_r_hi, w_r_lo], axis=1)
    b_r = jnp.concatenate([b_rg[0], b_re[0],
                           jnp.zeros((ROUTER_LANES - N_GROUPS - N_EXPERTS,), F32)])[None, :]
    row = lambda a: a[0][None, :]

    xs_pad = jnp.pad(x_sample, ((0, 0), (0, SAMPLE_PAD - ss), (0, 0))).reshape(bs * SAMPLE_PAD, D_MODEL)
    zm_p, zt_p = _inproj(x_prompt.reshape(tp, D_MODEL), w_in_t, b_in, w_tail_t, b_tail, "inproj_prompt")
    zm_s, zt_s = _inproj(xs_pad, w_in_t, b_in, w_tail_t, b_tail, "inproj_sample")

    kv = _matmul(mem_prompt.reshape(bp * N_MEM, D_MODEL), w_kv_b, KV_TN, "mem_kv")
    mk_p = kv[:, :MEM_WIDTH].reshape(bp, N_MEM, MEM_HEADS, MEM_HEAD_DIM)
    mv_p = kv[:, MEM_WIDTH:].reshape(bp, N_MEM, MEM_HEADS, MEM_HEAD_DIM)

    conv_args = (w_dw[0], row(b_dw), row(g_cn), row(b_cn))
    conv_p, buf_p = _conv_group(zm_p, jnp.zeros((bp, CONV_HIST, CONV_WIDTH), F32), *conv_args,
                                bp, sp, 256, "conv_prompt")
    conv_s, buf_s_t = _conv_step_group(
        zm_s.reshape(bs, SAMPLE_PAD, Z_MAIN_COLS), jnp.transpose(state_conv[0], (1, 0, 2)),
        *conv_args, ss, 32, "conv_sample")
    buf_s = jnp.transpose(buf_s_t, (1, 0, 2))

    m_tile = lambda m: jnp.broadcast_to(
        jnp.pad(m, ((0, 0), (0, SUBLANES - MLSTM_HEADS)))[:, :, None], (m.shape[0], SUBLANES, LANES))
    g_mh_r = row(g_mh)
    seqs = lambda z, b: z.reshape(b, z.shape[0] // b, z.shape[1])
    h_p, c_p, n_p, m_p = _mlstm_group(
        seqs(zm_p, bp), seqs(zt_p, bp),
        jnp.zeros((bp, MLSTM_HEADS, MLSTM_HEAD_DIM, MLSTM_HEAD_DIM), F32),
        jnp.zeros((bp, MLSTM_HEADS, MLSTM_HEAD_DIM), F32), jnp.zeros((bp, SUBLANES, LANES), F32),
        g_mh_r, bp, sp, MLSTM_CHUNK, MLSTM_CHUNK, 1, "mlstm_prompt")
    h_s, c_s, n_s, m_s = _mlstm_group(
        seqs(zm_s, bs), seqs(zt_s, bs), state_mlstm_C[0], state_mlstm_n[0], m_tile(state_mlstm_m[0]),
        g_mh_r, bs, SAMPLE_PAD, SAMPLE_PAD, ss, 8, "mlstm_sample")
    h_p = h_p.reshape(tp, MLSTM_WIDTH)

    mem_p = _memattn_heads(zt_p, kv, bp, sp, sp, "memattn_prompt")
    packed = lambda c: c[0].reshape(bs, N_MEM * MEM_HEADS, MEM_HEAD_DIM)
    mem_s = _memattn_packed(zt_s, packed(cache_mem_k), packed(cache_mem_v), bs, SAMPLE_PAD, 8,
                            "memattn_sample")

    compact = lambda a: a.reshape(bs, SAMPLE_PAD, a.shape[-1])[:, :ss].reshape(ts, a.shape[-1])
    x1, x1_packed, eid, ew = _outproj_router(
        (conv_p, h_p, mem_p, x_prompt.reshape(tp, D_MODEL)),
        (conv_s.reshape(ts, CONV_WIDTH), h_s[:, :ss].reshape(ts, MLSTM_WIDTH), compact(mem_s),
         x_sample.reshape(ts, D_MODEL)),
        w_out_b, row(g_ln1), row(b_ln1), w_r_split, b_r)

    total = tp + ts
    y = _moe(x1_packed, _dispatch_tables(eid, total), w_gate[0], w_up[0], w_down[0])
    g2, b2 = row(g_ln2), row(b_ln2)
    y_p = _final(x1, y, ew, g2, b2, 0, tp, "final_prompt").reshape(bp, sp, D_MODEL)
    y_s = _final(x1, y, ew, g2, b2, tp, ts, "final_sample").reshape(bs, ss, D_MODEL)

    return (y_p, y_s, buf_p[None], buf_s[None], c_p[None], c_s[None], n_p[None], n_s[None],
            m_p[:, :MLSTM_HEADS, 0][None], m_s[:, :MLSTM_HEADS, 0][None], mk_p[None], mv_p[None])
```

```python
import functools

import jax
import jax.numpy as jnp
from jax import lax
from jax.experimental import pallas as pl
from jax.experimental.pallas import tpu as pltpu

F32 = jnp.float32
BF16 = jnp.bfloat16

D_MODEL = 2048
CONV_WIDTH = 512
CONV_K = 31
CONV_HIST = CONV_K - 1
MLSTM_HEADS = 4
MLSTM_HEAD_DIM = 256
MLSTM_WIDTH = MLSTM_HEADS * MLSTM_HEAD_DIM
MLSTM_CHUNK = 128
MEM_HEADS = 4
MEM_HEAD_DIM = 128
MEM_WIDTH = MEM_HEADS * MEM_HEAD_DIM
N_MEM = 256
N_GROUPS = 8
EXPERTS_PER_GROUP = 8
N_EXPERTS = N_GROUPS * EXPERTS_PER_GROUP
D_EXPERT = 512
LN_EPS = 1e-5
DEPTH = 1
ALPHA = (2 * DEPTH) ** 0.25

LANES = 128
SUBLANES = 8
VMEM_LIMIT_BYTES = 56 * 1024 * 1024

Z_CONV_A = 0
Z_CONV_G = CONV_WIDTH
Z_Q = 2 * CONV_WIDTH
Z_K = Z_Q + MLSTM_WIDTH
Z_V = Z_K + MLSTM_WIDTH
Z_O = Z_V + MLSTM_WIDTH
Z_MAIN_COLS = Z_O + MLSTM_WIDTH
N_GATE_COLS = 2 * MLSTM_HEADS
ZT_QM = 0
ZT_GATE = MEM_WIDTH
Z_TAIL_COLS = MEM_WIDTH + LANES
INPROJ_TM = 1024
INPROJ_TN = 1024
KV_TN = 256

SAMPLE_PAD = SUBLANES
ROUTER_LANES = LANES
MOE_ROWS = 160
MOE_SLOTS = 4
ROW_DMA_PRIORITY = 1
OUT_TM = 256


def _params(*sem):
    return pltpu.CompilerParams(dimension_semantics=sem, vmem_limit_bytes=VMEM_LIMIT_BYTES)


def _inproj_kernel(x_ref, wm_ref, bm_ref, wt_ref, bt_ref, zm_ref, zt_ref, xb_ref, *, n_main):
    j = pl.program_id(1)

    @pl.when(j == 0)
    def _():
        xb_ref[...] = x_ref[...].astype(BF16)

    nt = (((1,), (1,)), ((), ()))

    @pl.when(j < n_main)
    def _():
        zm_ref[...] = lax.dot_general(xb_ref[...], wm_ref[...], nt,
                                      preferred_element_type=F32) + bm_ref[...]

    @pl.when(j == n_main)
    def _():
        zt_ref[...] = lax.dot_general(xb_ref[...], wt_ref[...], nt,
                                      preferred_element_type=F32) + bt_ref[...]


def _inproj(x, w_in_t, b_in, w_tail_t, b_tail, name):
    t, k = x.shape
    tm, tn = INPROJ_TM, INPROJ_TN
    n_main = Z_MAIN_COLS // tn
    main_col = lambda j: jnp.minimum(j, n_main - 1)
    return pl.pallas_call(
        functools.partial(_inproj_kernel, n_main=n_main),
        out_shape=(jax.ShapeDtypeStruct((t, Z_MAIN_COLS), F32),
                   jax.ShapeDtypeStruct((t, Z_TAIL_COLS), F32)),
        grid=(t // tm, n_main + 1),
        in_specs=[
            pl.BlockSpec((tm, k), lambda i, j: (i, 0)),
            pl.BlockSpec((tn, k), lambda i, j: (main_col(j), 0)),
            pl.BlockSpec((1, tn), lambda i, j: (0, main_col(j))),
            pl.BlockSpec((Z_TAIL_COLS, k), lambda i, j: (0, 0)),
            pl.BlockSpec((1, Z_TAIL_COLS), lambda i, j: (0, 0)),
        ],
        out_specs=(pl.BlockSpec((tm, tn), lambda i, j: (i, main_col(j))),
                   pl.BlockSpec((tm, Z_TAIL_COLS), lambda i, j: (i, 0))),
        scratch_shapes=[pltpu.VMEM((tm, k), BF16)],
        compiler_params=_params("arbitrary", "arbitrary"),
        name=name,
    )(x, w_in_t, b_in, w_tail_t, b_tail)


def _matmul_kernel(x_ref, w_ref, o_ref):
    o_ref[...] = jnp.dot(x_ref[...].astype(BF16), w_ref[...], preferred_element_type=F32)


def _matmul(x, w_bf16, tn, name):
    t, k = x.shape
    n = w_bf16.shape[1]
    return pl.pallas_call(
        _matmul_kernel,
        out_shape=jax.ShapeDtypeStruct((t, n), F32),
        grid=(n // tn,),
        in_specs=[pl.BlockSpec((t, k), lambda j: (0, 0)), pl.BlockSpec((k, tn), lambda j: (0, j))],
        out_specs=pl.BlockSpec((t, tn), lambda j: (0, j)),
        compiler_params=_params("arbitrary"),
        name=name,
    )(x, w_bf16)


def _layer_norm_rows(y, g, b):
    mu = jnp.mean(y, axis=-1, keepdims=True)
    yc = y - mu
    var = jnp.mean(yc * yc, axis=-1, keepdims=True)
    return yc * lax.rsqrt(var + LN_EPS) * g + b


CONV_ROWS = 64


def _conv_kernel(a_ref, g_ref, hist_ref, wdw_ref, bdw_ref, gcn_ref, bcn_ref, out_ref, nb_ref,
                 ubuf, shifted, *, tl):
    head = CONV_HIST + 2
    li = pl.program_id(1)

    @pl.when(li == 0)
    def _():
        ubuf[0:2, :] = jnp.zeros((2, CONV_WIDTH), F32)
        ubuf[2:head, :] = hist_ref[0]

    ubuf[head:head + tl, :] = a_ref[...] * jax.nn.sigmoid(g_ref[...])
    span = shifted.shape[1]
    for k in range(1, SUBLANES):
        shifted[k - 1] = ubuf[k:k + span, :]
    for r0 in range(0, tl, CONV_ROWS):
        acc = jnp.zeros((CONV_ROWS, CONV_WIDTH), F32) + bdw_ref[...]
        for j in range(CONV_K):
            lo, k = divmod(2 + j, SUBLANES)
            lo = lo * SUBLANES + r0
            src = ubuf if k == 0 else shifted.at[k - 1]
            acc = acc + wdw_ref[j:j + 1, :] * src[lo:lo + CONV_ROWS, :]
        y = _layer_norm_rows(acc, gcn_ref[...], bcn_ref[...])
        out_ref[r0:r0 + CONV_ROWS, :] = y * jax.nn.sigmoid(y)

    @pl.when(li == pl.num_programs(1) - 1)
    def _():
        nb_ref[0] = ubuf[2 + tl:head + tl, :]

    ubuf[2:head, :] = ubuf[2 + tl:head + tl, :]


def _conv_group(z, hist, w_dw, b_dw, g_cn, b_cn, batch, seq, tl, name):
    nl = seq // tl
    row = lambda b, l: (b * nl + l, 0)
    vec = pl.BlockSpec((1, CONV_WIDTH), lambda b, l: (0, 0))
    return pl.pallas_call(
        functools.partial(_conv_kernel, tl=tl),
        out_shape=(jax.ShapeDtypeStruct((batch * seq, CONV_WIDTH), F32),
                   jax.ShapeDtypeStruct((batch, CONV_HIST, CONV_WIDTH), F32)),
        grid=(batch, nl),
        in_specs=[
            pl.BlockSpec((tl, CONV_WIDTH), lambda b, l: (b * nl + l, Z_CONV_A // CONV_WIDTH)),
            pl.BlockSpec((tl, CONV_WIDTH), lambda b, l: (b * nl + l, Z_CONV_G // CONV_WIDTH)),
            pl.BlockSpec((1, CONV_HIST, CONV_WIDTH), lambda b, l: (b, 0, 0)),
            pl.BlockSpec((CONV_K, CONV_WIDTH), lambda b, l: (0, 0)),
            vec, vec, vec,
        ],
        out_specs=(pl.BlockSpec((tl, CONV_WIDTH), row),
                   pl.BlockSpec((1, CONV_HIST, CONV_WIDTH), lambda b, l: (b, 0, 0))),
        scratch_shapes=[pltpu.VMEM((CONV_HIST + 2 + tl, CONV_WIDTH), F32),
                        pltpu.VMEM((SUBLANES - 1, CONV_HIST + 2 + tl - SUBLANES, CONV_WIDTH), F32)],
        compiler_params=_params("arbitrary", "arbitrary"),
        name=name,
    )(z, z, hist, w_dw, b_dw, g_cn, b_cn)


def _conv_step_kernel(a_ref, g_ref, hist_ref, wdw_ref, bdw_ref, gcn_ref, bcn_ref, out_ref, nb_ref,
                      *, steps):
    u = [a_ref[:, t, :] * jax.nn.sigmoid(g_ref[:, t, :]) for t in range(steps)]
    full = lambda r: hist_ref[r] if r < CONV_HIST else u[r - CONV_HIST]
    for t in range(steps):
        acc = bdw_ref[...] + wdw_ref[0:1, :] * full(t)
        for j in range(1, CONV_K):
            acc = acc + wdw_ref[j:j + 1, :] * full(t + j)
        y = _layer_norm_rows(acc, gcn_ref[...], bcn_ref[...])
        out_ref[:, t, :] = y * jax.nn.sigmoid(y)
    for r in range(CONV_HIST):
        nb_ref[r] = full(r + steps)


def _conv_step_group(z3, hist_t, w_dw, b_dw, g_cn, b_cn, steps, bb, name):
    batch = z3.shape[0]
    vec = pl.BlockSpec((1, CONV_WIDTH), lambda b: (0, 0))
    hist_spec = pl.BlockSpec((CONV_HIST, bb, CONV_WIDTH), lambda b: (0, b, 0))
    return pl.pallas_call(
        functools.partial(_conv_step_kernel, steps=steps),
        out_shape=(jax.ShapeDtypeStruct((batch, steps, CONV_WIDTH), F32),
                   jax.ShapeDtypeStruct((CONV_HIST, batch, CONV_WIDTH), F32)),
        grid=(batch // bb,),
        in_specs=[
            pl.BlockSpec((bb, SAMPLE_PAD, CONV_WIDTH), lambda b: (b, 0, Z_CONV_A // CONV_WIDTH)),
            pl.BlockSpec((bb, SAMPLE_PAD, CONV_WIDTH), lambda b: (b, 0, Z_CONV_G // CONV_WIDTH)),
            hist_spec,
            pl.BlockSpec((CONV_K, CONV_WIDTH), lambda b: (0, 0)),
            vec, vec, vec,
        ],
        out_specs=(pl.BlockSpec((bb, steps, CONV_WIDTH), lambda b: (b, 0, 0)), hist_spec),
        compiler_params=_params("arbitrary"),
        name=name,
    )(z3, z3, hist_t, w_dw, b_dw, g_cn, b_cn)


def _col_to_row(col, eye):
    n = col.shape[0]
    return jnp.sum(jnp.where(eye, jnp.broadcast_to(col, (n, n)), 0.0), axis=0, keepdims=True)


def _mlstm_kernel(q_ref, k_ref, v_ref, o_ref, gate_ref, c0_ref, n0_ref, m0_ref, gmh_ref,
                  h_ref, c_ref, n_ref, m_ref, *, cl, valid, bb, single_chunk):
    if single_chunk:
        c_in, n_in, m_in = c0_ref, n0_ref, m0_ref
    else:
        c_in, n_in, m_in = c_ref, n_ref, m_ref

        @pl.when(pl.program_id(1) == 0)
        def _():
            c_ref[...] = c0_ref[...]
            n_ref[...] = n0_ref[...]
            m_ref[...] = m0_ref[...]

    rows = lax.broadcasted_iota(jnp.int32, (cl, cl), 0)
    cols = lax.broadcasted_iota(jnp.int32, (cl, cl), 1)
    eye = rows == cols
    tril = rows >= cols
    row_id = lax.broadcasted_iota(jnp.int32, (cl, 1), 0)
    dh = MLSTM_HEAD_DIM
    for bi, h in [(bi, h) for bi in range(bb) for h in range(MLSTM_HEADS)]:
        sl = slice(h * dh, (h + 1) * dh)
        if h == 0:
            gates = gate_ref[bi]
            log_sig = jnp.minimum(gates, 0.0) - jnp.log1p(jnp.exp(-jnp.abs(gates)))
        q = q_ref[bi, :, sl]
        k = k_ref[bi, :, sl] * (dh ** -0.5)
        v = v_ref[bi, :, sl]
        ip = gates[:, h:h + 1]
        lf = log_sig[:, MLSTM_HEADS + h:MLSTM_HEADS + h + 1]
        if valid < cl:
            ip = jnp.where(row_id < valid, ip, -jnp.inf)
            lf = jnp.where(row_id < valid, lf, 0.0)
        lf_row = _col_to_row(lf, eye)
        ip_row = _col_to_row(ip, eye)
        b_col = jnp.sum(jnp.where(tril, jnp.broadcast_to(lf_row, (cl, cl)), 0.0), axis=1, keepdims=True)
        b_row = _col_to_row(b_col, eye)
        m_prev = m_in[bi, h:h + 1, 0:1]
        log_inter = b_col + m_prev
        log_intra = jnp.where(tril, b_col - b_row + ip_row, -jnp.inf)
        m_t = jnp.maximum(log_inter, jnp.max(log_intra, axis=1, keepdims=True))
        w_inter = jnp.exp(log_inter - m_t)
        qb = q.astype(BF16)
        kb = k.astype(BF16)
        vb = v.astype(BF16)
        s = lax.dot_general(qb, kb, (((1,), (1,)), ((), ())), preferred_element_type=F32)
        s = s * jnp.exp(log_intra - m_t)
        c_old = c_in[bi, h]
        n_old = n_in[bi, h:h + 1, :]
        num = (w_inter * jnp.dot(qb, c_old.astype(BF16), preferred_element_type=F32)
               + jnp.dot(s.astype(BF16), vb, preferred_element_type=F32))
        den = w_inter * jnp.sum(q * n_old, axis=1, keepdims=True) + jnp.sum(s, axis=1, keepdims=True)
        hh = num / jnp.maximum(jnp.abs(den), jnp.exp(-m_t))
        m_new = m_t[cl - 1:cl, :]
        b_last = b_col[cl - 1:cl, :]
        decay = jnp.exp(b_last + m_prev - m_new)
        w_s = jnp.exp(b_last - b_col + ip - m_new)
        kw = k * w_s
        c_ref[bi, h] = decay * c_old + lax.dot_general(
            kw.astype(BF16), vb, (((0,), (0,)), ((), ())), preferred_element_type=F32)
        n_ref[bi, h:h + 1, :] = decay * n_old + jnp.sum(kw, axis=0, keepdims=True)
        m_ref[bi, h:h + 1, :] = jnp.broadcast_to(m_new, (1, LANES))
        mu = jnp.mean(hh, axis=-1, keepdims=True)
        hc = hh - mu
        var = jnp.mean(hc * hc, axis=-1, keepdims=True)
        hn = hc * lax.rsqrt(var + LN_EPS) * gmh_ref[:, sl]
        h_ref[bi, :, sl] = hn * jax.nn.sigmoid(o_ref[bi, :, sl])
    if single_chunk:
        m_ref[:, MLSTM_HEADS:, :] = jnp.zeros((bb, SUBLANES - MLSTM_HEADS, LANES), F32)


def _mlstm_group(z_main, z_tail, c0, n0, m0, g_mh, batch, seq, cl, valid, bb, name):
    nc = seq // cl
    zcol = lambda off: pl.BlockSpec((bb, cl, MLSTM_WIDTH), lambda b, c: (b, c, off // MLSTM_WIDTH))
    state = lambda shape: pl.BlockSpec((bb,) + shape, lambda b, c: (b,) + (0,) * len(shape))
    c_shape = (MLSTM_HEADS, MLSTM_HEAD_DIM, MLSTM_HEAD_DIM)
    n_shape = (MLSTM_HEADS, MLSTM_HEAD_DIM)
    m_shape = (SUBLANES, LANES)
    return pl.pallas_call(
        functools.partial(_mlstm_kernel, cl=cl, valid=valid, bb=bb, single_chunk=nc == 1),
        out_shape=(jax.ShapeDtypeStruct((batch, seq, MLSTM_WIDTH), F32),
                   jax.ShapeDtypeStruct((batch,) + c_shape, F32),
                   jax.ShapeDtypeStruct((batch,) + n_shape, F32),
                   jax.ShapeDtypeStruct((batch,) + m_shape, F32)),
        grid=(batch // bb, nc),
        in_specs=[
            zcol(Z_Q), zcol(Z_K), zcol(Z_V), zcol(Z_O),
            pl.BlockSpec((bb, cl, LANES), lambda b, c: (b, c, ZT_GATE // LANES)),
            state(c_shape), state(n_shape), state(m_shape),
            pl.BlockSpec((1, MLSTM_WIDTH), lambda b, c: (0, 0)),
        ],
        out_specs=(pl.BlockSpec((bb, cl, MLSTM_WIDTH), lambda b, c: (b, c, 0)),
                   state(c_shape), state(n_shape), state(m_shape)),
        compiler_params=_params("arbitrary", "arbitrary"),
        name=name,
    )(z_main, z_main, z_main, z_main, z_tail, c0, n0, m0, g_mh)


def _softmax_rows(s):
    e = jnp.exp(s - jnp.max(s, axis=-1, keepdims=True))
    return e / jnp.sum(e, axis=-1, keepdims=True)


_NT_DIMS = (((1,), (1,)), ((), ()))


def _memattn_head_kernel(q_ref, k_ref, v_ref, o_ref):
    s = lax.dot_general(q_ref[...].astype(BF16), k_ref[...].astype(BF16), _NT_DIMS,
                        preferred_element_type=F32) * (MEM_HEAD_DIM ** -0.5)
    p = _softmax_rows(s)
    o_ref[...] = jnp.dot(p.astype(BF16), v_ref[...].astype(BF16), preferred_element_type=F32)


def _memattn_heads(z_tail, kv, batch, seq, tq, name):
    nq = seq // tq
    dh = MEM_HEAD_DIM
    return pl.pallas_call(
        _memattn_head_kernel,
        out_shape=jax.ShapeDtypeStruct((batch * seq, MEM_WIDTH), F32),
        grid=(batch, MEM_HEADS, nq),
        in_specs=[pl.BlockSpec((tq, dh), lambda b, h, i: (b * nq + i, ZT_QM // dh + h)),
                  pl.BlockSpec((N_MEM, dh), lambda b, h, i: (b, h)),
                  pl.BlockSpec((N_MEM, dh), lambda b, h, i: (b, MEM_HEADS + h))],
        out_specs=pl.BlockSpec((tq, dh), lambda b, h, i: (b * nq + i, h)),
        compiler_params=_params("arbitrary", "arbitrary", "arbitrary"),
        name=name,
    )(z_tail, kv, kv)


def _memattn_packed_kernel(q_ref, k_ref, v_ref, o_ref, *, tq, bb):
    nh, dh = MEM_HEADS, MEM_HEAD_DIM
    shape = (nh * tq, N_MEM * nh)
    row_head = lax.broadcasted_iota(jnp.int32, shape, 0) // tq
    col_head = lax.broadcasted_iota(jnp.int32, shape, 1) % nh
    same_head = row_head == col_head
    for bi in range(bb):
        rs = slice(bi * tq, (bi + 1) * tq)
        q = jnp.concatenate([q_ref[rs, h * dh:(h + 1) * dh] for h in range(nh)], axis=0)
        s = lax.dot_general(q.astype(BF16), k_ref[bi].astype(BF16), _NT_DIMS,
                            preferred_element_type=F32) * (dh ** -0.5)
        p = _softmax_rows(jnp.where(same_head, s, -jnp.inf))
        o = jnp.dot(p.astype(BF16), v_ref[bi].astype(BF16), preferred_element_type=F32)
        for h in range(nh):
            o_ref[rs, h * dh:(h + 1) * dh] = o[h * tq:(h + 1) * tq, :]


def _memattn_packed(z_tail, mk, mv, batch, tq, bb, name):
    rows = bb * tq
    kv = pl.BlockSpec((bb, N_MEM * MEM_HEADS, MEM_HEAD_DIM), lambda b: (b, 0, 0))
    return pl.pallas_call(
        functools.partial(_memattn_packed_kernel, tq=tq, bb=bb),
        out_shape=jax.ShapeDtypeStruct((batch * tq, MEM_WIDTH), F32),
        grid=(batch // bb,),
        in_specs=[pl.BlockSpec((rows, MEM_WIDTH), lambda b: (b, ZT_QM // MEM_WIDTH)), kv, kv],
        out_specs=pl.BlockSpec((rows, MEM_WIDTH), lambda b: (b, 0)),
        compiler_params=_params("arbitrary"),
        name=name,
    )(z_tail, mk, mv)


def _route(logits):
    lane = lax.broadcasted_iota(jnp.int32, logits.shape, 1).astype(F32)
    neg = -jnp.inf
    first = lambda mask: jnp.min(jnp.where(mask, lane, float(LANES)), axis=1, keepdims=True)
    is_g = lane < N_GROUPS
    gl = jnp.where(is_g, logits, neg)
    g_max = jnp.max(gl, axis=1, keepdims=True)
    g_sel = first(gl == g_max)
    g_w = 1.0 / jnp.sum(jnp.exp(gl - g_max), axis=1, keepdims=True)
    lo = N_GROUPS + g_sel * EXPERTS_PER_GROUP
    in_grp = (lane >= lo) & (lane < lo + EXPERTS_PER_GROUP)
    el = jnp.where(in_grp, logits, neg)
    v1 = jnp.max(el, axis=1, keepdims=True)
    i1 = first(in_grp & (el == v1))
    rest = in_grp & (lane != i1)
    el2 = jnp.where(rest, logits, neg)
    v2 = jnp.max(el2, axis=1, keepdims=True)
    i2 = first(rest & (el2 == v2))
    t = jnp.exp(v2 - v1)
    w1 = g_w / (1.0 + t)
    w2 = g_w * t / (1.0 + t)
    lane_i = lax.broadcasted_iota(jnp.int32, logits.shape, 1)
    e1 = (i1 - N_GROUPS).astype(jnp.int32)
    e2 = (i2 - N_GROUPS).astype(jnp.int32)
    eid = jnp.where(lane_i == 0, e1, jnp.where(lane_i == 1, e2, 0))
    ew = jnp.where(lane_i == 0, w1, jnp.where(lane_i == 1, w2, 0.0))
    return eid, ew


def _outproj_kernel(cp, hp, mp, xp, cs, hs, ms, xs, wout, g1, b1, wr, br,
                    x1_ref, x1p_ref, eid_ref, ew_ref, *, n_prompt):
    def rows(c, h, m, x, rs):
        groups = jnp.concatenate(
            [c[rs, :].astype(BF16), h[rs, :].astype(BF16), m[rs, :].astype(BF16)], axis=1)
        mix = jnp.dot(groups, wout[...], preferred_element_type=F32)
        x1 = _layer_norm_rows(ALPHA * x[rs, :] + mix, g1[...], b1[...])
        x1_ref[rs, :] = x1
        xh = x1.astype(BF16)
        xh_f32 = xh.astype(F32)
        bits = pltpu.bitcast(xh_f32, jnp.int32)
        x1p_ref[rs, :] = bits[:, :D_MODEL // 2] | lax.shift_right_logical(bits[:, D_MODEL // 2:], 16)
        xl = (x1 - xh_f32).astype(BF16)
        n = x1.shape[0]
        cross = jnp.dot(jnp.concatenate([xh, xl], axis=0), wr[...], preferred_element_type=F32)
        logits = ((cross[:n, :ROUTER_LANES] + cross[:n, ROUTER_LANES:])
                  + (cross[n:, :ROUTER_LANES] + cross[n:, ROUTER_LANES:])) + br[...]
        eid, ew = _route(logits)
        eid_ref[rs, :] = eid
        ew_ref[rs, :] = ew

    def body(c, h, m, x):
        rows(c, h, m, x, slice(None))

    i = pl.program_id(0)
    pl.when(i < n_prompt)(lambda: body(cp, hp, mp, xp))
    pl.when(i >= n_prompt)(lambda: body(cs, hs, ms, xs))


def _outproj_router(prompt, sample, w_out_b, g1, b1, wr, br):
    tm = OUT_TM
    tp = prompt[0].shape[0]
    ts = sample[0].shape[0]
    n_p, n_s = tp // tm, ts // tm
    total = tp + ts
    widths = (CONV_WIDTH, MLSTM_WIDTH, MEM_WIDTH, D_MODEL)
    p_specs = [pl.BlockSpec((tm, w), lambda i: (jnp.minimum(i, n_p - 1), 0)) for w in widths]
    s_specs = [pl.BlockSpec((tm, w), lambda i: (jnp.maximum(i - n_p, 0), 0)) for w in widths]
    full = lambda shape: pl.BlockSpec(shape, lambda i: (0, 0))
    row = lambda w: pl.BlockSpec((tm, w), lambda i: (i, 0))
    return pl.pallas_call(
        functools.partial(_outproj_kernel, n_prompt=n_p),
        out_shape=(jax.ShapeDtypeStruct((total, D_MODEL), F32),
                   jax.ShapeDtypeStruct((total, D_MODEL // 2), jnp.int32),
                   jax.ShapeDtypeStruct((total, ROUTER_LANES), jnp.int32),
                   jax.ShapeDtypeStruct((total, ROUTER_LANES), F32)),
        grid=(n_p + n_s,),
        in_specs=p_specs + s_specs + [
            full((D_MODEL, D_MODEL)), full((1, D_MODEL)), full((1, D_MODEL)),
            full((D_MODEL, 2 * ROUTER_LANES)), full((1, ROUTER_LANES)),
        ],
        out_specs=(row(D_MODEL), row(D_MODEL // 2), row(ROUTER_LANES), row(ROUTER_LANES)),
        compiler_params=_params("arbitrary"),
        name="outproj_router",
    )(*prompt, *sample, w_out_b, g1, b1, wr, br)


def _moe_kernel(stok_ref, sdst_ref, base_ref, nval_ref, first_ref, count_ref,
                x_hbm, wg_ref, wu_ref, wd_ref, y_hbm, wgb, wub, wdb, xbuf, ybuf, gsem, ssem,
                *, dump_row):
    e = pl.program_id(0)
    last = pl.num_programs(0) - 1
    rc = MOE_ROWS
    ns = MOE_SLOTS

    def gather_start(base, slot):
        for r in range(rc):
            pltpu.make_async_copy(x_hbm.at[pl.ds(stok_ref[base + r], 1)],
                                  xbuf.at[slot, pl.ds(r, 1)], gsem.at[slot]).start(priority=ROW_DMA_PRIORITY)

    def gather_wait(slot):
        pltpu.make_async_copy(x_hbm.at[pl.ds(0, rc)], xbuf.at[slot], gsem.at[slot]).wait()

    def scatter_start(base, n_valid, slot):
        for r in range(rc):
            dst = jnp.where(r < n_valid, sdst_ref[base + r], dump_row + slot * rc + r)
            pltpu.make_async_copy(ybuf.at[slot, pl.ds(r, 1)], y_hbm.at[pl.ds(dst, 1)],
                                  ssem.at[slot]).start(priority=ROW_DMA_PRIORITY)

    def scatter_wait(slot):
        pltpu.make_async_copy(ybuf.at[slot], y_hbm.at[pl.ds(0, rc)], ssem.at[slot]).wait()

    @pl.when(e == 0)
    def _():
        ybuf[...] = jnp.zeros(ybuf.shape, F32)
        for slot in range(ns):
            pltpu.make_async_copy(ybuf.at[slot], y_hbm.at[pl.ds(dump_row + slot * rc, rc)],
                                  ssem.at[slot]).start()
            scatter_wait(slot)
        for ahead in range(ns - 1):
            gather_start(base_ref[1 + ahead], ahead)

    wgb[...] = wg_ref[0].astype(BF16)
    wub[...] = wu_ref[0].astype(BF16)
    wdb[...] = wd_ref[0].astype(BF16)
    first = first_ref[e]

    def chunk_on(slot, g):
        prv = (slot - 1) % ns
        gather_wait(slot)

        @pl.when(g >= ns - 1)
        def _():
            scatter_wait(slot)

        gather_start(base_ref[g + ns], prv)
        scatter_start(base_ref[g], nval_ref[g], prv)
        packed = xbuf[slot]
        x = jnp.concatenate(
            [pltpu.bitcast(packed & jnp.int32(-65536), F32).astype(BF16),
             pltpu.bitcast(packed << 16, F32).astype(BF16)], axis=1)
        hg = jnp.dot(x, wgb[...], preferred_element_type=F32)
        hu = jnp.dot(x, wub[...], preferred_element_type=F32)
        hid = (hg * jax.nn.sigmoid(hg) * hu).astype(BF16)
        ybuf[slot] = jnp.dot(hid, wdb[...], preferred_element_type=F32)

    def chunk(c, carry):
        g = first + c
        cur = lax.rem(g, ns)
        for slot in range(ns):
            pl.when(cur == slot)(functools.partial(chunk_on, slot, g))
        return carry

    lax.fori_loop(0, count_ref[e], chunk, 0)

    @pl.when(e == last)
    def _():
        g_end = first + count_ref[e]
        cur = lax.rem(g_end, ns)
        for slot in range(ns):
            @pl.when(cur == slot)
            def _():
                for ahead in range(ns - 1):
                    gather_wait((slot + ahead) % ns)
                scatter_start(base_ref[g_end], nval_ref[g_end], (slot - 1) % ns)

        for slot in range(ns):
            scatter_wait(slot)


def _moe(x1, tables, w_gate, w_up, w_down):
    total = x1.shape[0]
    dump_row = 2 * total
    wspec = lambda shape: pl.BlockSpec((1,) + shape, lambda e, *_: (e, 0, 0))
    grid_spec = pltpu.PrefetchScalarGridSpec(
        num_scalar_prefetch=len(tables),
        grid=(N_EXPERTS,),
        in_specs=[pl.BlockSpec(memory_space=pl.ANY),
                  wspec((D_MODEL, D_EXPERT)), wspec((D_MODEL, D_EXPERT)), wspec((D_EXPERT, D_MODEL))],
        out_specs=pl.BlockSpec(memory_space=pl.ANY),
        scratch_shapes=[
            pltpu.VMEM((D_MODEL, D_EXPERT), BF16), pltpu.VMEM((D_MODEL, D_EXPERT), BF16),
            pltpu.VMEM((D_EXPERT, D_MODEL), BF16),
            pltpu.VMEM((MOE_SLOTS, MOE_ROWS, D_MODEL // 2), jnp.int32),
            pltpu.VMEM((MOE_SLOTS, MOE_ROWS, D_MODEL), F32),
            pltpu.SemaphoreType.DMA((MOE_SLOTS,)), pltpu.SemaphoreType.DMA((MOE_SLOTS,)),
        ],
    )
    return pl.pallas_call(
        functools.partial(_moe_kernel, dump_row=dump_row),
        out_shape=jax.ShapeDtypeStruct((2 * total + MOE_SLOTS * MOE_ROWS, D_MODEL), F32),
        grid_spec=grid_spec,
        compiler_params=_params("arbitrary"),
        name="moe_experts",
    )(*tables, x1, w_gate, w_up, w_down)


def _final_kernel(x1_ref, y0_ref, y1_ref, ew_ref, g2, b2, o_ref):
    ew = ew_ref[...]
    ffn = ew[:, 0:1] * y0_ref[...] + ew[:, 1:2] * y1_ref[...]
    o_ref[...] = _layer_norm_rows(ALPHA * x1_ref[...] + ffn, g2[...], b2[...])


def _final(x1, y, ew, g2, b2, row0, rows, name):
    tm = OUT_TM
    total = x1.shape[0]
    off = row0 // tm
    k1 = total // tm
    vec = pl.BlockSpec((1, D_MODEL), lambda i: (0, 0))
    return pl.pallas_call(
        _final_kernel,
        out_shape=jax.ShapeDtypeStruct((rows, D_MODEL), F32),
        grid=(rows // tm,),
        in_specs=[pl.BlockSpec((tm, D_MODEL), lambda i: (i + off, 0)),
                  pl.BlockSpec((tm, D_MODEL), lambda i: (i + off, 0)),
                  pl.BlockSpec((tm, D_MODEL), lambda i: (i + off + k1, 0)),
                  pl.BlockSpec((tm, ROUTER_LANES), lambda i: (i + off, 0)),
                  vec, vec],
        out_specs=pl.BlockSpec((tm, D_MODEL), lambda i: (i, 0)),
        compiler_params=_params("arbitrary"),
        name=name,
    )(x1, y, y, ew, g2, b2)


def _dispatch_tables(eid, total):
    flat_e = eid[:, :2].reshape(-1)
    order = jnp.argsort(flat_e, stable=True).astype(jnp.int32)
    stok = order >> 1
    sdst = (order & 1) * total + stok
    experts = jnp.arange(N_EXPERTS, dtype=jnp.int32)
    cnt = jnp.sum((flat_e[:, None] == experts[None, :]).astype(jnp.int32), axis=0)
    start = jnp.cumsum(cnt) - cnt
    pad = jnp.zeros((MOE_ROWS,), jnp.int32)
    n_chunks = jnp.maximum((cnt + MOE_ROWS - 1) // MOE_ROWS, 1)
    first = jnp.cumsum(n_chunks) - n_chunks
    n_entries = flat_e.shape[0] // MOE_ROWS + N_EXPERTS + MOE_SLOTS + 1
    g = jnp.arange(n_entries, dtype=jnp.int32) - 1
    owner = jnp.sum((g[:, None] >= (first + n_chunks)[None, :]).astype(jnp.int32), axis=1)
    real = (g >= 0) & (owner < N_EXPERTS)
    pick = (owner[:, None] == experts[None, :]).astype(jnp.int32)
    sel = lambda v: jnp.sum(pick * v[None, :], axis=1)
    local = (g - sel(first)) * MOE_ROWS
    base = jnp.where(real, sel(start) + local, 0)
    n_valid = jnp.where(real, sel(cnt) - local, 0)
    return (jnp.concatenate([stok, pad]), jnp.concatenate([sdst, pad]), base, n_valid, first, n_chunks)


def kernel(x_prompt, x_sample, mem_prompt, state_conv, state_mlstm_C, state_mlstm_n, state_mlstm_m,
           cache_mem_k, cache_mem_v, w_in, b_in, w_dw, b_dw, g_cn, b_cn, g_mh, w_mk, w_mv, w_out,
           g_ln1, b_ln1, w_rg, b_rg, w_re, b_re, w_gate, w_up, w_down, g_ln2, b_ln2):
    bp, sp, _ = x_prompt.shape
    bs, ss, _ = x_sample.shape
    tp, ts = bp * sp, bs * ss

    gate_hi = Z_MAIN_COLS + N_GATE_COLS
    tail = lambda wt: jnp.concatenate(
        [wt[gate_hi:], wt[Z_MAIN_COLS:gate_hi],
         jnp.zeros((LANES - N_GATE_COLS,) + wt.shape[1:], wt.dtype)], axis=0)
    w_in_t = jnp.transpose(w_in[0]).astype(BF16)
    w_tail_t = tail(w_in_t)
    b_tail = tail(b_in[0])[None, :]
    w_kv_b = jnp.concatenate([w_mk[0], w_mv[0]], axis=-1).astype(BF16)
    w_out_b = w_out[0].astype(BF16)
    w_r = jnp.concatenate([w_rg[0], w_re[0],
                           jnp.zeros((D_MODEL, ROUTER_LANES - N_GROUPS - N_EXPERTS), F32)], axis=-1)
    w_r_hi = w_r.astype(BF16)
    w_r_lo = (w_r - w_r_hi.astype(F32)).astype(BF16)
    w_r_split = jnp.concatenate([w_r_hi, w_r_lo], axis=1)
    b_r = jnp.concatenate([b_rg[0], b_re[0],
                           jnp.zeros((ROUTER_LANES - N_GROUPS - N_EXPERTS,), F32)])[None, :]
    row = lambda a: a[0][None, :]

    xs_pad = jnp.pad(x_sample, ((0, 0), (0, SAMPLE_PAD - ss), (0, 0))).reshape(bs * SAMPLE_PAD, D_MODEL)
    zm_p, zt_p = _inproj(x_prompt.reshape(tp, D_MODEL), w_in_t, b_in, w_tail_t, b_tail, "inproj_prompt")
    zm_s, zt_s = _inproj(xs_pad, w_in_t, b_in, w_tail_t, b_tail, "inproj_sample")

    kv = _matmul(mem_prompt.reshape(bp * N_MEM, D_MODEL), w_kv_b, KV_TN, "mem_kv")
    mk_p = kv[:, :MEM_WIDTH].reshape(bp, N_MEM, MEM_HEADS, MEM_HEAD_DIM)
    mv_p = kv[:, MEM_WIDTH:].reshape(bp, N_MEM, MEM_HEADS, MEM_HEAD_DIM)

    conv_args = (w_dw[0], row(b_dw), row(g_cn), row(b_cn))
    conv_p, buf_p = _conv_group(zm_p, jnp.zeros((bp, CONV_HIST, CONV_WIDTH), F32), *conv_args,
                                bp, sp, 256, "conv_prompt")
    conv_s, buf_s_t = _conv_step_group(
        zm_s.reshape(bs, SAMPLE_PAD, Z_MAIN_COLS), jnp.transpose(state_conv[0], (1, 0, 2)),
        *conv_args, ss, 32, "conv_sample")
    buf_s = jnp.transpose(buf_s_t, (1, 0, 2))

    m_tile = lambda m: jnp.broadcast_to(
        jnp.pad(m, ((0, 0), (0, SUBLANES - MLSTM_HEADS)))[:, :, None], (m.shape[0], SUBLANES, LANES))
    g_mh_r = row(g_mh)
    seqs = lambda z, b: z.reshape(b, z.shape[0] // b, z.shape[1])
    h_p, c_p, n_p, m_p = _mlstm_group(
        seqs(zm_p, bp), seqs(zt_p, bp),
        jnp.zeros((bp, MLSTM_HEADS, MLSTM_HEAD_DIM, MLSTM_HEAD_DIM), F32),
        jnp.zeros((bp, MLSTM_HEADS, MLSTM_HEAD_DIM), F32), jnp.zeros((bp, SUBLANES, LANES), F32),
        g_mh_r, bp, sp, MLSTM_CHUNK, MLSTM_CHUNK, 1, "mlstm_prompt")
    h_s, c_s, n_s, m_s = _mlstm_group(
        seqs(zm_s, bs), seqs(zt_s, bs), state_mlstm_C[0], state_mlstm_n[0], m_tile(state_mlstm_m[0]),
        g_mh_r, bs, SAMPLE_PAD, SAMPLE_PAD, ss, 8, "mlstm_sample")
    h_p = h_p.reshape(tp, MLSTM_WIDTH)

    mem_p = _memattn_heads(zt_p, kv, bp, sp, sp, "memattn_prompt")
    packed = lambda c: c[0].reshape(bs, N_MEM * MEM_HEADS, MEM_HEAD_DIM)
    mem_s = _memattn_packed(zt_s, packed(cache_mem_k), packed(cache_mem_v), bs, SAMPLE_PAD, 8,
                            "memattn_sample")

    compact = lambda a: a.reshape(bs, SAMPLE_PAD, a.shape[-1])[:, :ss].reshape(ts, a.shape[-1])
    x1, x1_packed, eid, ew = _outproj_router(
        (conv_p, h_p, mem_p, x_prompt.reshape(tp, D_MODEL)),
        (conv_s.reshape(ts, CONV_WIDTH), h_s[:, :ss].reshape(ts, MLSTM_WIDTH), compact(mem_s),
         x_sample.reshape(ts, D_MODEL)),
        w_out_b, row(g_ln1), row(b_ln1), w_r_split, b_r)

    total = tp + ts
    y = _moe(x1_packed, _dispatch_tables(eid, total), w_gate[0], w_up[0], w_down[0])
    g2, b2 = row(g_ln2), row(b_ln2)
    y_p = _final(x1, y, ew, g2, b2, 0, tp, "final_prompt").reshape(bp, sp, D_MODEL)
    y_s = _final(x1, y, ew, g2, b2, tp, ts, "final_sample").reshape(bs, ss, D_MODEL)

    return (y_p, y_s, buf_p[None], buf_s[None], c_p[None], c_s[None], n_p[None], n_s[None],
            m_p[:, :MLSTM_HEADS, 0][None], m_s[:, :MLSTM_HEADS, 0][None], mk_p[None], mv_p[None])
```

```python
import functools

import jax
import jax.numpy as jnp
from jax import lax
from jax.experimental import pallas as pl
from jax.experimental.pallas import tpu as pltpu

F32 = jnp.float32
BF16 = jnp.bfloat16

D_MODEL = 2048
CONV_WIDTH = 512
CONV_K = 31
CONV_HIST = CONV_K - 1
MLSTM_HEADS = 4
MLSTM_HEAD_DIM = 256
MLSTM_WIDTH = MLSTM_HEADS * MLSTM_HEAD_DIM
MLSTM_CHUNK = 128
MEM_HEADS = 4
MEM_HEAD_DIM = 128
MEM_WIDTH = MEM_HEADS * MEM_HEAD_DIM
N_MEM = 256
N_GROUPS = 8
EXPERTS_PER_GROUP = 8
N_EXPERTS = N_GROUPS * EXPERTS_PER_GROUP
D_EXPERT = 512
LN_EPS = 1e-5
DEPTH = 1
ALPHA = (2 * DEPTH) ** 0.25

LANES = 128
SUBLANES = 8
VMEM_LIMIT_BYTES = 56 * 1024 * 1024

Z_CONV_A = 0
Z_CONV_G = CONV_WIDTH
Z_Q = 2 * CONV_WIDTH
Z_K = Z_Q + MLSTM_WIDTH
Z_V = Z_K + MLSTM_WIDTH
Z_O = Z_V + MLSTM_WIDTH
Z_MAIN_COLS = Z_O + MLSTM_WIDTH
N_GATE_COLS = 2 * MLSTM_HEADS
ZT_QM = 0
ZT_GATE = MEM_WIDTH
Z_TAIL_COLS = MEM_WIDTH + LANES
INPROJ_TM = 1024
INPROJ_TN = 1024
KV_TN = 256

SAMPLE_PAD = SUBLANES
MLSTM_STAGE_PAIRS = 8
ROUTER_LANES = LANES
MOE_ROWS = 160
MOE_SLOTS = 4
ROW_DMA_PRIORITY = 1
OUT_TM = 256


def _params(*sem):
    return pltpu.CompilerParams(dimension_semantics=sem, vmem_limit_bytes=VMEM_LIMIT_BYTES)


def _inproj_kernel(x_ref, wm_ref, bm_ref, wt_ref, bt_ref, zm_ref, zt_ref, xb_ref, *, n_main):
    j = pl.program_id(1)

    @pl.when(j == 0)
    def _():
        xb_ref[...] = x_ref[...].astype(BF16)

    nt = (((1,), (1,)), ((), ()))

    @pl.when(j < n_main)
    def _():
        zm_ref[...] = lax.dot_general(xb_ref[...], wm_ref[...], nt,
                                      preferred_element_type=F32) + bm_ref[...]

    @pl.when(j == n_main)
    def _():
        zt_ref[...] = lax.dot_general(xb_ref[...], wt_ref[...], nt,
                                      preferred_element_type=F32) + bt_ref[...]


def _inproj(x, w_in_t, b_in, w_tail_t, b_tail, name):
    t, k = x.shape
    tm, tn = INPROJ_TM, INPROJ_TN
    n_main = Z_MAIN_COLS // tn
    main_col = lambda j: jnp.minimum(j, n_main - 1)
    return pl.pallas_call(
        functools.partial(_inproj_kernel, n_main=n_main),
        out_shape=(jax.ShapeDtypeStruct((t, Z_MAIN_COLS), F32),
                   jax.ShapeDtypeStruct((t, Z_TAIL_COLS), F32)),
        grid=(t // tm, n_main + 1),
        in_specs=[
            pl.BlockSpec((tm, k), lambda i, j: (i, 0)),
            pl.BlockSpec((tn, k), lambda i, j: (main_col(j), 0)),
            pl.BlockSpec((1, tn), lambda i, j: (0, main_col(j))),
            pl.BlockSpec((Z_TAIL_COLS, k), lambda i, j: (0, 0)),
            pl.BlockSpec((1, Z_TAIL_COLS), lambda i, j: (0, 0)),
        ],
        out_specs=(pl.BlockSpec((tm, tn), lambda i, j: (i, main_col(j))),
                   pl.BlockSpec((tm, Z_TAIL_COLS), lambda i, j: (i, 0))),
        scratch_shapes=[pltpu.VMEM((tm, k), BF16)],
        compiler_params=_params("arbitrary", "arbitrary"),
        name=name,
    )(x, w_in_t, b_in, w_tail_t, b_tail)


def _matmul_kernel(x_ref, w_ref, o_ref):
    o_ref[...] = jnp.dot(x_ref[...].astype(BF16), w_ref[...], preferred_element_type=F32)


def _matmul(x, w_bf16, tn, name):
    t, k = x.shape
    n = w_bf16.shape[1]
    return pl.pallas_call(
        _matmul_kernel,
        out_shape=jax.ShapeDtypeStruct((t, n), F32),
        grid=(n // tn,),
        in_specs=[pl.BlockSpec((t, k), lambda j: (0, 0)), pl.BlockSpec((k, tn), lambda j: (0, j))],
        out_specs=pl.BlockSpec((t, tn), lambda j: (0, j)),
        compiler_params=_params("arbitrary"),
        name=name,
    )(x, w_bf16)


def _layer_norm_rows(y, g, b):
    mu = jnp.mean(y, axis=-1, keepdims=True)
    yc = y - mu
    var = jnp.mean(yc * yc, axis=-1, keepdims=True)
    return yc * lax.rsqrt(var + LN_EPS) * g + b


CONV_ROWS = 64


def _conv_kernel(a_ref, g_ref, hist_ref, wdw_ref, bdw_ref, gcn_ref, bcn_ref, out_ref, nb_ref,
                 ubuf, shifted, *, tl):
    head = CONV_HIST + 2
    li = pl.program_id(1)

    @pl.when(li == 0)
    def _():
        ubuf[0:2, :] = jnp.zeros((2, CONV_WIDTH), F32)
        ubuf[2:head, :] = hist_ref[0]

    ubuf[head:head + tl, :] = a_ref[...] * jax.nn.sigmoid(g_ref[...])
    span = shifted.shape[1]
    for k in range(1, SUBLANES):
        shifted[k - 1] = ubuf[k:k + span, :]
    for r0 in range(0, tl, CONV_ROWS):
        acc = jnp.zeros((CONV_ROWS, CONV_WIDTH), F32) + bdw_ref[...]
        for j in range(CONV_K):
            lo, k = divmod(2 + j, SUBLANES)
            lo = lo * SUBLANES + r0
            src = ubuf if k == 0 else shifted.at[k - 1]
            acc = acc + wdw_ref[j:j + 1, :] * src[lo:lo + CONV_ROWS, :]
        y = _layer_norm_rows(acc, gcn_ref[...], bcn_ref[...])
        out_ref[r0:r0 + CONV_ROWS, :] = y * jax.nn.sigmoid(y)

    @pl.when(li == pl.num_programs(1) - 1)
    def _():
        nb_ref[0] = ubuf[2 + tl:head + tl, :]

    ubuf[2:head, :] = ubuf[2 + tl:head + tl, :]


def _conv_group(z, hist, w_dw, b_dw, g_cn, b_cn, batch, seq, tl, name):
    nl = seq // tl
    row = lambda b, l: (b * nl + l, 0)
    vec = pl.BlockSpec((1, CONV_WIDTH), lambda b, l: (0, 0))
    return pl.pallas_call(
        functools.partial(_conv_kernel, tl=tl),
        out_shape=(jax.ShapeDtypeStruct((batch * seq, CONV_WIDTH), F32),
                   jax.ShapeDtypeStruct((batch, CONV_HIST, CONV_WIDTH), F32)),
        grid=(batch, nl),
        in_specs=[
            pl.BlockSpec((tl, CONV_WIDTH), lambda b, l: (b * nl + l, Z_CONV_A // CONV_WIDTH)),
            pl.BlockSpec((tl, CONV_WIDTH), lambda b, l: (b * nl + l, Z_CONV_G // CONV_WIDTH)),
            pl.BlockSpec((1, CONV_HIST, CONV_WIDTH), lambda b, l: (b, 0, 0)),
            pl.BlockSpec((CONV_K, CONV_WIDTH), lambda b, l: (0, 0)),
            vec, vec, vec,
        ],
        out_specs=(pl.BlockSpec((tl, CONV_WIDTH), row),
                   pl.BlockSpec((1, CONV_HIST, CONV_WIDTH), lambda b, l: (b, 0, 0))),
        scratch_shapes=[pltpu.VMEM((CONV_HIST + 2 + tl, CONV_WIDTH), F32),
                        pltpu.VMEM((SUBLANES - 1, CONV_HIST + 2 + tl - SUBLANES, CONV_WIDTH), F32)],
        compiler_params=_params("arbitrary", "arbitrary"),
        name=name,
    )(z, z, hist, w_dw, b_dw, g_cn, b_cn)


def _conv_step_kernel(a_ref, g_ref, hist_ref, wdw_ref, bdw_ref, gcn_ref, bcn_ref, out_ref, nb_ref,
                      *, steps):
    u = [a_ref[:, t, :] * jax.nn.sigmoid(g_ref[:, t, :]) for t in range(steps)]
    full = lambda r: hist_ref[r] if r < CONV_HIST else u[r - CONV_HIST]
    for t in range(steps):
        acc = bdw_ref[...] + wdw_ref[0:1, :] * full(t)
        for j in range(1, CONV_K):
            acc = acc + wdw_ref[j:j + 1, :] * full(t + j)
        y = _layer_norm_rows(acc, gcn_ref[...], bcn_ref[...])
        out_ref[:, t, :] = y * jax.nn.sigmoid(y)
    for r in range(CONV_HIST):
        nb_ref[r] = full(r + steps)


def _conv_step_group(z3, hist_t, w_dw, b_dw, g_cn, b_cn, steps, bb, name):
    batch = z3.shape[0]
    vec = pl.BlockSpec((1, CONV_WIDTH), lambda b: (0, 0))
    hist_spec = pl.BlockSpec((CONV_HIST, bb, CONV_WIDTH), lambda b: (0, b, 0))
    return pl.pallas_call(
        functools.partial(_conv_step_kernel, steps=steps),
        out_shape=(jax.ShapeDtypeStruct((batch, steps, CONV_WIDTH), F32),
                   jax.ShapeDtypeStruct((CONV_HIST, batch, CONV_WIDTH), F32)),
        grid=(batch // bb,),
        in_specs=[
            pl.BlockSpec((bb, SAMPLE_PAD, CONV_WIDTH), lambda b: (b, 0, Z_CONV_A // CONV_WIDTH)),
            pl.BlockSpec((bb, SAMPLE_PAD, CONV_WIDTH), lambda b: (b, 0, Z_CONV_G // CONV_WIDTH)),
            hist_spec,
            pl.BlockSpec((CONV_K, CONV_WIDTH), lambda b: (0, 0)),
            vec, vec, vec,
        ],
        out_specs=(pl.BlockSpec((bb, steps, CONV_WIDTH), lambda b: (b, 0, 0)), hist_spec),
        compiler_params=_params("arbitrary"),
        name=name,
    )(z3, z3, hist_t, w_dw, b_dw, g_cn, b_cn)


def _col_to_row(col, eye):
    n = col.shape[0]
    return jnp.sum(jnp.where(eye, jnp.broadcast_to(col, (n, n)), 0.0), axis=0, keepdims=True)


def _mlstm_kernel(q_ref, k_ref, v_ref, o_ref, gate_ref, c0_ref, n0_ref, m0_ref, gmh_ref,
                  h_ref, c_ref, n_ref, m_ref, *, cl, valid, bb, single_chunk):
    if single_chunk:
        c_in, n_in, m_in = c0_ref, n0_ref, m0_ref
    else:
        c_in, n_in, m_in = c_ref, n_ref, m_ref

        @pl.when(pl.program_id(1) == 0)
        def _():
            c_ref[...] = c0_ref[...]
            n_ref[...] = n0_ref[...]
            m_ref[...] = m0_ref[...]

    rows = lax.broadcasted_iota(jnp.int32, (cl, cl), 0)
    cols = lax.broadcasted_iota(jnp.int32, (cl, cl), 1)
    eye = rows == cols
    tril = rows >= cols
    row_id = lax.broadcasted_iota(jnp.int32, (cl, 1), 0)
    nh, dh = MLSTM_HEADS, MLSTM_HEAD_DIM
    nt_dims = (((1,), (1,)), ((), ()))
    tn_dims = (((0,), (0,)), ((), ()))
    each = lambda f, *lists: [f(*args) for args in zip(*lists)]
    all_pairs = [(bi, h) for bi in range(bb) for h in range(nh)]
    for g0 in range(0, len(all_pairs), MLSTM_STAGE_PAIRS):
        pairs = all_pairs[g0:g0 + MLSTM_STAGE_PAIRS]
        cols_of = [slice(h * dh, (h + 1) * dh) for _, h in pairs]
        gates = {bi: gate_ref[bi] for bi in sorted({bi for bi, _ in pairs})}
        log_sig = {bi: jnp.minimum(g, 0.0) - jnp.log1p(jnp.exp(-jnp.abs(g))) for bi, g in gates.items()}
        ip = each(lambda p: gates[p[0]][:, p[1]:p[1] + 1], pairs)
        lf = each(lambda p: log_sig[p[0]][:, nh + p[1]:nh + p[1] + 1], pairs)
        if valid < cl:
            ip = each(lambda x: jnp.where(row_id < valid, x, -jnp.inf), ip)
            lf = each(lambda x: jnp.where(row_id < valid, x, 0.0), lf)
        lf_row = each(lambda x: _col_to_row(x, eye), lf)
        ip_row = each(lambda x: _col_to_row(x, eye), ip)
        b_col = each(lambda r: jnp.sum(jnp.where(tril, jnp.broadcast_to(r, (cl, cl)), 0.0),
                                       axis=1, keepdims=True), lf_row)
        b_row = each(lambda x: _col_to_row(x, eye), b_col)
        m_prev = each(lambda p: m_in[p[0], p[1]:p[1] + 1, 0:1], pairs)
        log_inter = each(lambda b, m: b + m, b_col, m_prev)
        log_intra = each(lambda bc, br, ir: jnp.where(tril, bc - br + ir, -jnp.inf), b_col, b_row, ip_row)
        m_t = each(lambda le, la: jnp.maximum(le, jnp.max(la, axis=1, keepdims=True)), log_inter, log_intra)
        w_inter = each(lambda le, m: jnp.exp(le - m), log_inter, m_t)
        intra = each(lambda la, m: jnp.exp(la - m), log_intra, m_t)
        m_new = each(lambda m: m[cl - 1:cl, :], m_t)
        b_last = each(lambda b: b[cl - 1:cl, :], b_col)
        decay = each(lambda bl, mp, mn: jnp.exp(bl + mp - mn), b_last, m_prev, m_new)
        w_s = each(lambda bl, b, i, mn: jnp.exp(bl - b + i - mn), b_last, b_col, ip, m_new)
        q = each(lambda p, sl: q_ref[p[0], :, sl], pairs, cols_of)
        k = each(lambda p, sl: k_ref[p[0], :, sl] * (dh ** -0.5), pairs, cols_of)
        vb = each(lambda p, sl: v_ref[p[0], :, sl].astype(BF16), pairs, cols_of)
        qb = each(lambda x: x.astype(BF16), q)
        kb = each(lambda x: x.astype(BF16), k)
        s = each(lambda a, b: lax.dot_general(a, b, nt_dims, preferred_element_type=F32), qb, kb)
        s = each(lambda x, d: x * d, s, intra)
        c_old = each(lambda p: c_in[p[0], p[1]], pairs)
        n_old = each(lambda p: n_in[p[0], p[1]:p[1] + 1, :], pairs)
        inter = each(lambda a, c: jnp.dot(a, c.astype(BF16), preferred_element_type=F32), qb, c_old)
        local = each(lambda x, v: jnp.dot(x.astype(BF16), v, preferred_element_type=F32), s, vb)
        num = each(lambda w, a, b: w * a + b, w_inter, inter, local)
        den = each(lambda w, x, n, ss: w * jnp.sum(x * n, axis=1, keepdims=True)
                   + jnp.sum(ss, axis=1, keepdims=True), w_inter, q, n_old, s)
        hh = each(lambda a, d, m: a / jnp.maximum(jnp.abs(d), jnp.exp(-m)), num, den, m_t)
        kw = each(lambda x, w: x * w, k, w_s)
        outer = each(lambda a, v: lax.dot_general(a.astype(BF16), v, tn_dims, preferred_element_type=F32),
                     kw, vb)
        for i, (bi, h) in enumerate(pairs):
            c_ref[bi, h] = decay[i] * c_old[i] + outer[i]
            n_ref[bi, h:h + 1, :] = decay[i] * n_old[i] + jnp.sum(kw[i], axis=0, keepdims=True)
            m_ref[bi, h:h + 1, :] = jnp.broadcast_to(m_new[i], (1, LANES))
        mu = each(lambda x: jnp.mean(x, axis=-1, keepdims=True), hh)
        hc = each(lambda x, m: x - m, hh, mu)
        var = each(lambda x: jnp.mean(x * x, axis=-1, keepdims=True), hc)
        for i, (bi, h) in enumerate(pairs):
            sl = cols_of[i]
            hn = hc[i] * lax.rsqrt(var[i] + LN_EPS) * gmh_ref[:, sl]
            h_ref[bi, :, sl] = hn * jax.nn.sigmoid(o_ref[bi, :, sl])
    if single_chunk:
        m_ref[:, MLSTM_HEADS:, :] = jnp.zeros((bb, SUBLANES - MLSTM_HEADS, LANES), F32)


def _mlstm_group(z_main, z_tail, c0, n0, m0, g_mh, batch, seq, cl, valid, bb, name):
    nc = seq // cl
    zcol = lambda off: pl.BlockSpec((bb, cl, MLSTM_WIDTH), lambda b, c: (b, c, off // MLSTM_WIDTH))
    state = lambda shape: pl.BlockSpec((bb,) + shape, lambda b, c: (b,) + (0,) * len(shape))
    c_shape = (MLSTM_HEADS, MLSTM_HEAD_DIM, MLSTM_HEAD_DIM)
    n_shape = (MLSTM_HEADS, MLSTM_HEAD_DIM)
    m_shape = (SUBLANES, LANES)
    return pl.pallas_call(
        functools.partial(_mlstm_kernel, cl=cl, valid=valid, bb=bb, single_chunk=nc == 1),
        out_shape=(jax.ShapeDtypeStruct((batch, seq, MLSTM_WIDTH), F32),
                   jax.ShapeDtypeStruct((batch,) + c_shape, F32),
                   jax.ShapeDtypeStruct((batch,) + n_shape, F32),
                   jax.ShapeDtypeStruct((batch,) + m_shape, F32)),
        grid=(batch // bb, nc),
        in_specs=[
            zcol(Z_Q), zcol(Z_K), zcol(Z_V), zcol(Z_O),
            pl.BlockSpec((bb, cl, LANES), lambda b, c: (b, c, ZT_GATE // LANES)),
            state(c_shape), state(n_shape), state(m_shape),
            pl.BlockSpec((1, MLSTM_WIDTH), lambda b, c: (0, 0)),
        ],
        out_specs=(pl.BlockSpec((bb, cl, MLSTM_WIDTH), lambda b, c: (b, c, 0)),
                   state(c_shape), state(n_shape), state(m_shape)),
        compiler_params=_params("arbitrary", "arbitrary"),
        name=name,
    )(z_main, z_main, z_main, z_main, z_tail, c0, n0, m0, g_mh)


def _softmax_rows(s):
    e = jnp.exp(s - jnp.max(s, axis=-1, keepdims=True))
    return e / jnp.sum(e, axis=-1, keepdims=True)


_NT_DIMS = (((1,), (1,)), ((), ()))


def _memattn_head_kernel(q_ref, k_ref, v_ref, o_ref):
    s = lax.dot_general(q_ref[...].astype(BF16), k_ref[...].astype(BF16), _NT_DIMS,
                        preferred_element_type=F32) * (MEM_HEAD_DIM ** -0.5)
    p = _softmax_rows(s)
    o_ref[...] = jnp.dot(p.astype(BF16), v_ref[...].astype(BF16), preferred_element_type=F32)


def _memattn_heads(z_tail, kv, batch, seq, tq, name):
    nq = seq // tq
    dh = MEM_HEAD_DIM
    return pl.pallas_call(
        _memattn_head_kernel,
        out_shape=jax.ShapeDtypeStruct((batch * seq, MEM_WIDTH), F32),
        grid=(batch, MEM_HEADS, nq),
        in_specs=[pl.BlockSpec((tq, dh), lambda b, h, i: (b * nq + i, ZT_QM // dh + h)),
                  pl.BlockSpec((N_MEM, dh), lambda b, h, i: (b, h)),
                  pl.BlockSpec((N_MEM, dh), lambda b, h, i: (b, MEM_HEADS + h))],
        out_specs=pl.BlockSpec((tq, dh), lambda b, h, i: (b * nq + i, h)),
        compiler_params=_params("arbitrary", "arbitrary", "arbitrary"),
        name=name,
    )(z_tail, kv, kv)


def _memattn_packed_kernel(q_ref, k_ref, v_ref, o_ref, *, tq, bb):
    nh, dh = MEM_HEADS, MEM_HEAD_DIM
    shape = (nh * tq, N_MEM * nh)
    row_head = lax.broadcasted_iota(jnp.int32, shape, 0) // tq
    col_head = lax.broadcasted_iota(jnp.int32, shape, 1) % nh
    same_head = row_head == col_head
    scores = []
    for bi in range(bb):
        rs = slice(bi * tq, (bi + 1) * tq)
        q = jnp.concatenate([q_ref[rs, h * dh:(h + 1) * dh] for h in range(nh)], axis=0)
        scores.append(lax.dot_general(q.astype(BF16), k_ref[bi].astype(BF16), _NT_DIMS,
                                      preferred_element_type=F32) * (dh ** -0.5))
    probs = [_softmax_rows(jnp.where(same_head, s, -jnp.inf)).astype(BF16) for s in scores]
    outs = [jnp.dot(p, v_ref[bi].astype(BF16), preferred_element_type=F32) for bi, p in enumerate(probs)]
    for bi, o in enumerate(outs):
        rs = slice(bi * tq, (bi + 1) * tq)
        for h in range(nh):
            o_ref[rs, h * dh:(h + 1) * dh] = o[h * tq:(h + 1) * tq, :]


def _memattn_packed(z_tail, mk, mv, batch, tq, bb, name):
    rows = bb * tq
    kv = pl.BlockSpec((bb, N_MEM * MEM_HEADS, MEM_HEAD_DIM), lambda b: (b, 0, 0))
    return pl.pallas_call(
        functools.partial(_memattn_packed_kernel, tq=tq, bb=bb),
        out_shape=jax.ShapeDtypeStruct((batch * tq, MEM_WIDTH), F32),
        grid=(batch // bb,),
        in_specs=[pl.BlockSpec((rows, MEM_WIDTH), lambda b: (b, ZT_QM // MEM_WIDTH)), kv, kv],
        out_specs=pl.BlockSpec((rows, MEM_WIDTH), lambda b: (b, 0)),
        compiler_params=_params("arbitrary"),
        name=name,
    )(z_tail, mk, mv)


def _route(logits):
    lane = lax.broadcasted_iota(jnp.int32, logits.shape, 1).astype(F32)
    neg = -jnp.inf
    first = lambda mask: jnp.min(jnp.where(mask, lane, float(LANES)), axis=1, keepdims=True)
    is_g = lane < N_GROUPS
    gl = jnp.where(is_g, logits, neg)
    g_max = jnp.max(gl, axis=1, keepdims=True)
    g_sel = first(gl == g_max)
    g_w = 1.0 / jnp.sum(jnp.exp(gl - g_max), axis=1, keepdims=True)
    lo = N_GROUPS + g_sel * EXPERTS_PER_GROUP
    in_grp = (lane >= lo) & (lane < lo + EXPERTS_PER_GROUP)
    el = jnp.where(in_grp, logits, neg)
    v1 = jnp.max(el, axis=1, keepdims=True)
    i1 = first(in_grp & (el == v1))
    rest = in_grp & (lane != i1)
    el2 = jnp.where(rest, logits, neg)
    v2 = jnp.max(el2, axis=1, keepdims=True)
    i2 = first(rest & (el2 == v2))
    t = jnp.exp(v2 - v1)
    w1 = g_w / (1.0 + t)
    w2 = g_w * t / (1.0 + t)
    lane_i = lax.broadcasted_iota(jnp.int32, logits.shape, 1)
    e1 = (i1 - N_GROUPS).astype(jnp.int32)
    e2 = (i2 - N_GROUPS).astype(jnp.int32)
    eid = jnp.where(lane_i == 0, e1, jnp.where(lane_i == 1, e2, 0))
    ew = jnp.where(lane_i == 0, w1, jnp.where(lane_i == 1, w2, 0.0))
    return eid, ew


def _outproj_kernel(cp, hp, mp, xp, cs, hs, ms, xs, wout, g1, b1, wr, br,
                    x1_ref, x1p_ref, eid_ref, ew_ref, *, n_prompt):
    def rows(c, h, m, x, rs):
        groups = jnp.concatenate(
            [c[rs, :].astype(BF16), h[rs, :].astype(BF16), m[rs, :].astype(BF16)], axis=1)
        mix = jnp.dot(groups, wout[...], preferred_element_type=F32)
        x1 = _layer_norm_rows(ALPHA * x[rs, :] + mix, g1[...], b1[...])
        x1_ref[rs, :] = x1
        xh = x1.astype(BF16)
        xh_f32 = xh.astype(F32)
        bits = pltpu.bitcast(xh_f32, jnp.int32)
        x1p_ref[rs, :] = bits[:, :D_MODEL // 2] | lax.shift_right_logical(bits[:, D_MODEL // 2:], 16)
        xl = (x1 - xh_f32).astype(BF16)
        n = x1.shape[0]
        cross = jnp.dot(jnp.concatenate([xh, xl], axis=0), wr[...], preferred_element_type=F32)
        logits = ((cross[:n, :ROUTER_LANES] + cross[:n, ROUTER_LANES:])
                  + (cross[n:, :ROUTER_LANES] + cross[n:, ROUTER_LANES:])) + br[...]
        eid, ew = _route(logits)
        eid_ref[rs, :] = eid
        ew_ref[rs, :] = ew

    def body(c, h, m, x):
        rows(c, h, m, x, slice(None))

    i = pl.program_id(0)
    pl.when(i < n_prompt)(lambda: body(cp, hp, mp, xp))
    pl.when(i >= n_prompt)(lambda: body(cs, hs, ms, xs))


def _outproj_router(prompt, sample, w_out_b, g1, b1, wr, br):
    tm = OUT_TM
    tp = prompt[0].shape[0]
    ts = sample[0].shape[0]
    n_p, n_s = tp // tm, ts // tm
    total = tp + ts
    widths = (CONV_WIDTH, MLSTM_WIDTH, MEM_WIDTH, D_MODEL)
    p_specs = [pl.BlockSpec((tm, w), lambda i: (jnp.minimum(i, n_p - 1), 0)) for w in widths]
    s_specs = [pl.BlockSpec((tm, w), lambda i: (jnp.maximum(i - n_p, 0), 0)) for w in widths]
    full = lambda shape: pl.BlockSpec(shape, lambda i: (0, 0))
    row = lambda w: pl.BlockSpec((tm, w), lambda i: (i, 0))
    return pl.pallas_call(
        functools.partial(_outproj_kernel, n_prompt=n_p),
        out_shape=(jax.ShapeDtypeStruct((total, D_MODEL), F32),
                   jax.ShapeDtypeStruct((total, D_MODEL // 2), jnp.int32),
                   jax.ShapeDtypeStruct((total, ROUTER_LANES), jnp.int32),
                   jax.ShapeDtypeStruct((total, ROUTER_LANES), F32)),
        grid=(n_p + n_s,),
        in_specs=p_specs + s_specs + [
            full((D_MODEL, D_MODEL)), full((1, D_MODEL)), full((1, D_MODEL)),
            full((D_MODEL, 2 * ROUTER_LANES)), full((1, ROUTER_LANES)),
        ],
        out_specs=(row(D_MODEL), row(D_MODEL // 2), row(ROUTER_LANES), row(ROUTER_LANES)),
        compiler_params=_params("arbitrary"),
        name="outproj_router",
    )(*prompt, *sample, w_out_b, g1, b1, wr, br)


def _moe_kernel(stok_ref, sdst_ref, base_ref, nval_ref, first_ref, count_ref,
                x_hbm, wg_ref, wu_ref, wd_ref, y_hbm, wgb, wub, wdb, xbuf, ybuf, gsem, ssem,
                *, dump_row):
    e = pl.program_id(0)
    last = pl.num_programs(0) - 1
    rc = MOE_ROWS
    ns = MOE_SLOTS

    def gather_start(base, slot):
        for r in range(rc):
            pltpu.make_async_copy(x_hbm.at[pl.ds(stok_ref[base + r], 1)],
                                  xbuf.at[slot, pl.ds(r, 1)], gsem.at[slot]).start(priority=ROW_DMA_PRIORITY)

    def gather_wait(slot):
        pltpu.make_async_copy(x_hbm.at[pl.ds(0, rc)], xbuf.at[slot], gsem.at[slot]).wait()

    def scatter_start(base, n_valid, slot):
        for r in range(rc):
            dst = jnp.where(r < n_valid, sdst_ref[base + r], dump_row + slot * rc + r)
            pltpu.make_async_copy(ybuf.at[slot, pl.ds(r, 1)], y_hbm.at[pl.ds(dst, 1)],
                                  ssem.at[slot]).start(priority=ROW_DMA_PRIORITY)

    def scatter_wait(slot):
        pltpu.make_async_copy(ybuf.at[slot], y_hbm.at[pl.ds(0, rc)], ssem.at[slot]).wait()

    @pl.when(e == 0)
    def _():
        ybuf[...] = jnp.zeros(ybuf.shape, F32)
        for slot in range(ns):
            pltpu.make_async_copy(ybuf.at[slot], y_hbm.at[pl.ds(dump_row + slot * rc, rc)],
                                  ssem.at[slot]).start()
            scatter_wait(slot)
        for ahead in range(ns - 1):
            gather_start(base_ref[1 + ahead], ahead)

    wgb[...] = wg_ref[0].astype(BF16)
    wub[...] = wu_ref[0].astype(BF16)
    wdb[...] = wd_ref[0].astype(BF16)
    first = first_ref[e]

    def chunk_on(slot, g):
        prv = (slot - 1) % ns
        gather_wait(slot)

        @pl.when(g >= ns - 1)
        def _():
            scatter_wait(slot)

        gather_start(base_ref[g + ns], prv)
        scatter_start(base_ref[g], nval_ref[g], prv)
        packed = xbuf[slot]
        x = jnp.concatenate(
            [pltpu.bitcast(packed & jnp.int32(-65536), F32).astype(BF16),
             pltpu.bitcast(packed << 16, F32).astype(BF16)], axis=1)
        hg = jnp.dot(x, wgb[...], preferred_element_type=F32)
        hu = jnp.dot(x, wub[...], preferred_element_type=F32)
        hid = (hg * jax.nn.sigmoid(hg) * hu).astype(BF16)
        ybuf[slot] = jnp.dot(hid, wdb[...], preferred_element_type=F32)

    def chunk(c, carry):
        g = first + c
        cur = lax.rem(g, ns)
        for slot in range(ns):
            pl.when(cur == slot)(functools.partial(chunk_on, slot, g))
        return carry

    lax.fori_loop(0, count_ref[e], chunk, 0)

    @pl.when(e == last)
    def _():
        g_end = first + count_ref[e]
        cur = lax.rem(g_end, ns)
        for slot in range(ns):
            @pl.when(cur == slot)
            def _():
                for ahead in range(ns - 1):
                    gather_wait((slot + ahead) % ns)
                scatter_start(base_ref[g_end], nval_ref[g_end], (slot - 1) % ns)

        for slot in range(ns):
            scatter_wait(slot)


def _moe(x1, tables, w_gate, w_up, w_down):
    total = x1.shape[0]
    dump_row = 2 * total
    wspec = lambda shape: pl.BlockSpec((1,) + shape, lambda e, *_: (e, 0, 0))
    grid_spec = pltpu.PrefetchScalarGridSpec(
        num_scalar_prefetch=len(tables),
        grid=(N_EXPERTS,),
        in_specs=[pl.BlockSpec(memory_space=pl.ANY),
                  wspec((D_MODEL, D_EXPERT)), wspec((D_MODEL, D_EXPERT)), wspec((D_EXPERT, D_MODEL))],
        out_specs=pl.BlockSpec(memory_space=pl.ANY),
        scratch_shapes=[
            pltpu.VMEM((D_MODEL, D_EXPERT), BF16), pltpu.VMEM((D_MODEL, D_EXPERT), BF16),
            pltpu.VMEM((D_EXPERT, D_MODEL), BF16),
            pltpu.VMEM((MOE_SLOTS, MOE_ROWS, D_MODEL // 2), jnp.int32),
            pltpu.VMEM((MOE_SLOTS, MOE_ROWS, D_MODEL), F32),
            pltpu.SemaphoreType.DMA((MOE_SLOTS,)), pltpu.SemaphoreType.DMA((MOE_SLOTS,)),
        ],
    )
    return pl.pallas_call(
        functools.partial(_moe_kernel, dump_row=dump_row),
        out_shape=jax.ShapeDtypeStruct((2 * total + MOE_SLOTS * MOE_ROWS, D_MODEL), F32),
        grid_spec=grid_spec,
        compiler_params=_params("arbitrary"),
        name="moe_experts",
    )(*tables, x1, w_gate, w_up, w_down)


def _final_kernel(x1_ref, y0_ref, y1_ref, ew_ref, g2, b2, o_ref):
    ew = ew_ref[...]
    ffn = ew[:, 0:1] * y0_ref[...] + ew[:, 1:2] * y1_ref[...]
    o_ref[...] = _layer_norm_rows(ALPHA * x1_ref[...] + ffn, g2[...], b2[...])


def _final(x1, y, ew, g2, b2, row0, rows, name):
    tm = OUT_TM
    total = x1.shape[0]
    off = row0 // tm
    k1 = total // tm
    vec = pl.BlockSpec((1, D_MODEL), lambda i: (0, 0))
    return pl.pallas_call(
        _final_kernel,
        out_shape=jax.ShapeDtypeStruct((rows, D_MODEL), F32),
        grid=(rows // tm,),
        in_specs=[pl.BlockSpec((tm, D_MODEL), lambda i: (i + off, 0)),
                  pl.BlockSpec((tm, D_MODEL), lambda i: (i + off, 0)),
                  pl.BlockSpec((tm, D_MODEL), lambda i: (i + off + k1, 0)),
                  pl.BlockSpec((tm, ROUTER_LANES), lambda i: (i + off, 0)),
                  vec, vec],
        out_specs=pl.BlockSpec((tm, D_MODEL), lambda i: (i, 0)),
        compiler_params=_params("arbitrary"),
        name=name,
    )(x1, y, y, ew, g2, b2)


def _dispatch_tables(eid, total):
    flat_e = eid[:, :2].reshape(-1)
    order = jnp.argsort(flat_e, stable=True).astype(jnp.int32)
    stok = order >> 1
    sdst = (order & 1) * total + stok
    experts = jnp.arange(N_EXPERTS, dtype=jnp.int32)
    cnt = jnp.sum((flat_e[:, None] == experts[None, :]).astype(jnp.int32), axis=0)
    start = jnp.cumsum(cnt) - cnt
    pad = jnp.zeros((MOE_ROWS,), jnp.int32)
    n_chunks = jnp.maximum((cnt + MOE_ROWS - 1) // MOE_ROWS, 1)
    first = jnp.cumsum(n_chunks) - n_chunks
    n_entries = flat_e.shape[0] // MOE_ROWS + N_EXPERTS + MOE_SLOTS + 1
    g = jnp.arange(n_entries, dtype=jnp.int32) - 1
    owner = jnp.sum((g[:, None] >= (first + n_chunks)[None, :]).astype(jnp.int32), axis=1)
    real = (g >= 0) & (owner < N_EXPERTS)
    pick = (owner[:, None] == experts[None, :]).astype(jnp.int32)
    sel = lambda v: jnp.sum(pick * v[None, :], axis=1)
    local = (g - sel(first)) * MOE_ROWS
    base = jnp.where(real, sel(start) + local, 0)
    n_valid = jnp.where(real, sel(cnt) - local, 0)
    return (jnp.concatenate([stok, pad]), jnp.concatenate([sdst, pad]), base, n_valid, first, n_chunks)


def kernel(x_prompt, x_sample, mem_prompt, state_conv, state_mlstm_C, state_mlstm_n, state_mlstm_m,
           cache_mem_k, cache_mem_v, w_in, b_in, w_dw, b_dw, g_cn, b_cn, g_mh, w_mk, w_mv, w_out,
           g_ln1, b_ln1, w_rg, b_rg, w_re, b_re, w_gate, w_up, w_down, g_ln2, b_ln2):
    bp, sp, _ = x_prompt.shape
    bs, ss, _ = x_sample.shape
    tp, ts = bp * sp, bs * ss

    gate_hi = Z_MAIN_COLS + N_GATE_COLS
    tail = lambda wt: jnp.concatenate(
        [wt[gate_hi:], wt[Z_MAIN_COLS:gate_hi],
         jnp.zeros((LANES - N_GATE_COLS,) + wt.shape[1:], wt.dtype)], axis=0)
    w_in_t = jnp.transpose(w_in[0]).astype(BF16)
    w_tail_t = tail(w_in_t)
    b_tail = tail(b_in[0])[None, :]
    w_kv_b = jnp.concatenate([w_mk[0], w_mv[0]], axis=-1).astype(BF16)
    w_out_b = w_out[0].astype(BF16)
    w_r = jnp.concatenate([w_rg[0], w_re[0],
                           jnp.zeros((D_MODEL, ROUTER_LANES - N_GROUPS - N_EXPERTS), F32)], axis=-1)
    w_r_hi = w_r.astype(BF16)
    w_r_lo = (w_r - w_r_hi.astype(F32)).astype(BF16)
    w_r_split = jnp.concatenate([w_r_hi, w_r_lo], axis=1)
    b_r = jnp.concatenate([b_rg[0], b_re[0],
                           jnp.zeros((ROUTER_LANES - N_GROUPS - N_EXPERTS,), F32)])[None, :]
    row = lambda a: a[0][None, :]

    xs_pad = jnp.pad(x_sample, ((0, 0), (0, SAMPLE_PAD - ss), (0, 0))).reshape(bs * SAMPLE_PAD, D_MODEL)
    zm_p, zt_p = _inproj(x_prompt.reshape(tp, D_MODEL), w_in_t, b_in, w_tail_t, b_tail, "inproj_prompt")
    zm_s, zt_s = _inproj(xs_pad, w_in_t, b_in, w_tail_t, b_tail, "inproj_sample")

    kv = _matmul(mem_prompt.reshape(bp * N_MEM, D_MODEL), w_kv_b, KV_TN, "mem_kv")
    mk_p = kv[:, :MEM_WIDTH].reshape(bp, N_MEM, MEM_HEADS, MEM_HEAD_DIM)
    mv_p = kv[:, MEM_WIDTH:].reshape(bp, N_MEM, MEM_HEADS, MEM_HEAD_DIM)

    conv_args = (w_dw[0], row(b_dw), row(g_cn), row(b_cn))
    conv_p, buf_p = _conv_group(zm_p, jnp.zeros((bp, CONV_HIST, CONV_WIDTH), F32), *conv_args,
                                bp, sp, 256, "conv_prompt")
    conv_s, buf_s_t = _conv_step_group(
        zm_s.reshape(bs, SAMPLE_PAD, Z_MAIN_COLS), jnp.transpose(state_conv[0], (1, 0, 2)),
        *conv_args, ss, 32, "conv_sample")
    buf_s = jnp.transpose(buf_s_t, (1, 0, 2))

    m_tile = lambda m: jnp.broadcast_to(
        jnp.pad(m, ((0, 0), (0, SUBLANES - MLSTM_HEADS)))[:, :, None], (m.shape[0], SUBLANES, LANES))
    g_mh_r = row(g_mh)
    seqs = lambda z, b: z.reshape(b, z.shape[0] // b, z.shape[1])
    h_p, c_p, n_p, m_p = _mlstm_group(
        seqs(zm_p, bp), seqs(zt_p, bp),
        jnp.zeros((bp, MLSTM_HEADS, MLSTM_HEAD_DIM, MLSTM_HEAD_DIM), F32),
        jnp.zeros((bp, MLSTM_HEADS, MLSTM_HEAD_DIM), F32), jnp.zeros((bp, SUBLANES, LANES), F32),
        g_mh_r, bp, sp, MLSTM_CHUNK, MLSTM_CHUNK, 1, "mlstm_prompt")
    h_s, c_s, n_s, m_s = _mlstm_group(
        seqs(zm_s, bs), seqs(zt_s, bs), state_mlstm_C[0], state_mlstm_n[0], m_tile(state_mlstm_m[0]),
        g_mh_r, bs, SAMPLE_PAD, SAMPLE_PAD, ss, 8, "mlstm_sample")
    h_p = h_p.reshape(tp, MLSTM_WIDTH)

    mem_p = _memattn_heads(zt_p, kv, bp, sp, sp, "memattn_prompt")
    packed = lambda c: c[0].reshape(bs, N_MEM * MEM_HEADS, MEM_HEAD_DIM)
    mem_s = _memattn_packed(zt_s, packed(cache_mem_k), packed(cache_mem_v), bs, SAMPLE_PAD, 8,
                            "memattn_sample")

    compact = lambda a: a.reshape(bs, SAMPLE_PAD, a.shape[-1])[:, :ss].reshape(ts, a.shape[-1])
    x1, x1_packed, eid, ew = _outproj_router(
        (conv_p, h_p, mem_p, x_prompt.reshape(tp, D_MODEL)),
        (conv_s.reshape(ts, CONV_WIDTH), h_s[:, :ss].reshape(ts, MLSTM_WIDTH), compact(mem_s),
         x_sample.reshape(ts, D_MODEL)),
        w_out_b, row(g_ln1), row(b_ln1), w_r_split, b_r)

    total = tp + ts
    y = _moe(x1_packed, _dispatch_tables(eid, total), w_gate[0], w_up[0], w_down[0])
    g2, b2 = row(g_ln2), row(b_ln2)
    y_p = _final(x1, y, ew, g2, b2, 0, tp, "final_prompt").reshape(bp, sp, D_MODEL)
    y_s = _final(x1, y, ew, g2, b2, tp, ts, "final_sample").reshape(bs, ss, D_MODEL)

    return (y_p, y_s, buf_p[None], buf_s[None], c_p[None], c_s[None], n_p[None], n_s[None],
            m_p[:, :MLSTM_HEADS, 0][None], m_s[:, :MLSTM_HEADS, 0][None], mk_p[None], mv_p[None])
```

```python
import functools

import jax
import jax.numpy as jnp
from jax import lax
from jax.experimental import pallas as pl
from jax.experimental.pallas import tpu as pltpu

F32 = jnp.float32
BF16 = jnp.bfloat16

D_MODEL = 2048
CONV_WIDTH = 512
CONV_K = 31
CONV_HIST = CONV_K - 1
MLSTM_HEADS = 4
MLSTM_HEAD_DIM = 256
MLSTM_WIDTH = MLSTM_HEADS * MLSTM_HEAD_DIM
MLSTM_CHUNK = 128
MEM_HEADS = 4
MEM_HEAD_DIM = 128
MEM_WIDTH = MEM_HEADS * MEM_HEAD_DIM
N_MEM = 256
N_GROUPS = 8
EXPERTS_PER_GROUP = 8
N_EXPERTS = N_GROUPS * EXPERTS_PER_GROUP
D_EXPERT = 512
LN_EPS = 1e-5
DEPTH = 1
ALPHA = (2 * DEPTH) ** 0.25

LANES = 128
SUBLANES = 8
VMEM_LIMIT_BYTES = 56 * 1024 * 1024

Z_CONV_A = 0
Z_CONV_G = CONV_WIDTH
Z_Q = 2 * CONV_WIDTH
Z_K = Z_Q + MLSTM_WIDTH
Z_V = Z_K + MLSTM_WIDTH
Z_O = Z_V + MLSTM_WIDTH
Z_MAIN_COLS = Z_O + MLSTM_WIDTH
N_GATE_COLS = 2 * MLSTM_HEADS
ZT_QM = 0
ZT_GATE = MEM_WIDTH
Z_TAIL_COLS = MEM_WIDTH + LANES
INPROJ_TM = 1024
INPROJ_TN = 1024
KV_TN = 256

SAMPLE_PAD = SUBLANES
MLSTM_STAGE_PAIRS = 8
ROUTER_LANES = LANES
MOE_ROWS = 160
MOE_SLOTS = 4
ROW_DMA_PRIORITY = 1
OUT_TM = 256
FINAL_TM = 512


def _params(*sem):
    return pltpu.CompilerParams(dimension_semantics=sem, vmem_limit_bytes=VMEM_LIMIT_BYTES)


def _inproj_kernel(x_ref, wm_ref, bm_ref, wt_ref, bt_ref, zm_ref, zt_ref, xb_ref, *, n_main):
    j = pl.program_id(1)

    @pl.when(j == 0)
    def _():
        xb_ref[...] = x_ref[...].astype(BF16)

    nt = (((1,), (1,)), ((), ()))

    @pl.when(j < n_main)
    def _():
        zm_ref[...] = lax.dot_general(xb_ref[...], wm_ref[...], nt,
                                      preferred_element_type=F32) + bm_ref[...]

    @pl.when(j == n_main)
    def _():
        zt_ref[...] = lax.dot_general(xb_ref[...], wt_ref[...], nt,
                                      preferred_element_type=F32) + bt_ref[...]


def _inproj(x, w_in_t, b_in, w_tail_t, b_tail, name):
    t, k = x.shape
    tm, tn = INPROJ_TM, INPROJ_TN
    n_main = Z_MAIN_COLS // tn
    main_col = lambda j: jnp.minimum(j, n_main - 1)
    return pl.pallas_call(
        functools.partial(_inproj_kernel, n_main=n_main),
        out_shape=(jax.ShapeDtypeStruct((t, Z_MAIN_COLS), F32),
                   jax.ShapeDtypeStruct((t, Z_TAIL_COLS), F32)),
        grid=(t // tm, n_main + 1),
        in_specs=[
            pl.BlockSpec((tm, k), lambda i, j: (i, 0)),
            pl.BlockSpec((tn, k), lambda i, j: (main_col(j), 0)),
            pl.BlockSpec((1, tn), lambda i, j: (0, main_col(j))),
            pl.BlockSpec((Z_TAIL_COLS, k), lambda i, j: (0, 0)),
            pl.BlockSpec((1, Z_TAIL_COLS), lambda i, j: (0, 0)),
        ],
        out_specs=(pl.BlockSpec((tm, tn), lambda i, j: (i, main_col(j))),
                   pl.BlockSpec((tm, Z_TAIL_COLS), lambda i, j: (i, 0))),
        scratch_shapes=[pltpu.VMEM((tm, k), BF16)],
        compiler_params=_params("arbitrary", "arbitrary"),
        name=name,
    )(x, w_in_t, b_in, w_tail_t, b_tail)


def _matmul_kernel(x_ref, w_ref, o_ref):
    o_ref[...] = jnp.dot(x_ref[...].astype(BF16), w_ref[...], preferred_element_type=F32)


def _matmul(x, w_bf16, tn, name):
    t, k = x.shape
    n = w_bf16.shape[1]
    return pl.pallas_call(
        _matmul_kernel,
        out_shape=jax.ShapeDtypeStruct((t, n), F32),
        grid=(n // tn,),
        in_specs=[pl.BlockSpec((t, k), lambda j: (0, 0)), pl.BlockSpec((k, tn), lambda j: (0, j))],
        out_specs=pl.BlockSpec((t, tn), lambda j: (0, j)),
        compiler_params=_params("arbitrary"),
        name=name,
    )(x, w_bf16)


def _layer_norm_rows(y, g, b):
    mu = jnp.mean(y, axis=-1, keepdims=True)
    yc = y - mu
    var = jnp.mean(yc * yc, axis=-1, keepdims=True)
    return yc * lax.rsqrt(var + LN_EPS) * g + b


CONV_ROWS = 64


def _conv_kernel(a_ref, g_ref, hist_ref, wdw_ref, bdw_ref, gcn_ref, bcn_ref, out_ref, nb_ref,
                 ubuf, shifted, *, tl):
    head = CONV_HIST + 2
    li = pl.program_id(1)

    @pl.when(li == 0)
    def _():
        ubuf[0:2, :] = jnp.zeros((2, CONV_WIDTH), F32)
        ubuf[2:head, :] = hist_ref[0]

    ubuf[head:head + tl, :] = a_ref[...] * jax.nn.sigmoid(g_ref[...])
    span = shifted.shape[1]
    for k in range(1, SUBLANES):
        shifted[k - 1] = ubuf[k:k + span, :]
    for r0 in range(0, tl, CONV_ROWS):
        acc = jnp.zeros((CONV_ROWS, CONV_WIDTH), F32) + bdw_ref[...]
        for j in range(CONV_K):
            lo, k = divmod(2 + j, SUBLANES)
            lo = lo * SUBLANES + r0
            src = ubuf if k == 0 else shifted.at[k - 1]
            acc = acc + wdw_ref[j:j + 1, :] * src[lo:lo + CONV_ROWS, :]
        y = _layer_norm_rows(acc, gcn_ref[...], bcn_ref[...])
        out_ref[r0:r0 + CONV_ROWS, :] = y * jax.nn.sigmoid(y)

    @pl.when(li == pl.num_programs(1) - 1)
    def _():
        nb_ref[0] = ubuf[2 + tl:head + tl, :]

    ubuf[2:head, :] = ubuf[2 + tl:head + tl, :]


def _conv_group(z, hist, w_dw, b_dw, g_cn, b_cn, batch, seq, tl, name):
    nl = seq // tl
    row = lambda b, l: (b * nl + l, 0)
    vec = pl.BlockSpec((1, CONV_WIDTH), lambda b, l: (0, 0))
    return pl.pallas_call(
        functools.partial(_conv_kernel, tl=tl),
        out_shape=(jax.ShapeDtypeStruct((batch * seq, CONV_WIDTH), F32),
                   jax.ShapeDtypeStruct((batch, CONV_HIST, CONV_WIDTH), F32)),
        grid=(batch, nl),
        in_specs=[
            pl.BlockSpec((tl, CONV_WIDTH), lambda b, l: (b * nl + l, Z_CONV_A // CONV_WIDTH)),
            pl.BlockSpec((tl, CONV_WIDTH), lambda b, l: (b * nl + l, Z_CONV_G // CONV_WIDTH)),
            pl.BlockSpec((1, CONV_HIST, CONV_WIDTH), lambda b, l: (b, 0, 0)),
            pl.BlockSpec((CONV_K, CONV_WIDTH), lambda b, l: (0, 0)),
            vec, vec, vec,
        ],
        out_specs=(pl.BlockSpec((tl, CONV_WIDTH), row),
                   pl.BlockSpec((1, CONV_HIST, CONV_WIDTH), lambda b, l: (b, 0, 0))),
        scratch_shapes=[pltpu.VMEM((CONV_HIST + 2 + tl, CONV_WIDTH), F32),
                        pltpu.VMEM((SUBLANES - 1, CONV_HIST + 2 + tl - SUBLANES, CONV_WIDTH), F32)],
        compiler_params=_params("arbitrary", "arbitrary"),
        name=name,
    )(z, z, hist, w_dw, b_dw, g_cn, b_cn)


def _conv_step_kernel(a_ref, g_ref, hist_ref, wdw_ref, bdw_ref, gcn_ref, bcn_ref, out_ref, nb_ref,
                      *, steps):
    u = [a_ref[:, t, :] * jax.nn.sigmoid(g_ref[:, t, :]) for t in range(steps)]
    full = lambda r: hist_ref[r] if r < CONV_HIST else u[r - CONV_HIST]
    for t in range(steps):
        acc = bdw_ref[...] + wdw_ref[0:1, :] * full(t)
        for j in range(1, CONV_K):
            acc = acc + wdw_ref[j:j + 1, :] * full(t + j)
        y = _layer_norm_rows(acc, gcn_ref[...], bcn_ref[...])
        out_ref[:, t, :] = y * jax.nn.sigmoid(y)
    for r in range(CONV_HIST):
        nb_ref[r] = full(r + steps)


def _conv_step_group(z3, hist_t, w_dw, b_dw, g_cn, b_cn, steps, bb, name):
    batch = z3.shape[0]
    vec = pl.BlockSpec((1, CONV_WIDTH), lambda b: (0, 0))
    hist_spec = pl.BlockSpec((CONV_HIST, bb, CONV_WIDTH), lambda b: (0, b, 0))
    return pl.pallas_call(
        functools.partial(_conv_step_kernel, steps=steps),
        out_shape=(jax.ShapeDtypeStruct((batch, steps, CONV_WIDTH), F32),
                   jax.ShapeDtypeStruct((CONV_HIST, batch, CONV_WIDTH), F32)),
        grid=(batch // bb,),
        in_specs=[
            pl.BlockSpec((bb, SAMPLE_PAD, CONV_WIDTH), lambda b: (b, 0, Z_CONV_A // CONV_WIDTH)),
            pl.BlockSpec((bb, SAMPLE_PAD, CONV_WIDTH), lambda b: (b, 0, Z_CONV_G // CONV_WIDTH)),
            hist_spec,
            pl.BlockSpec((CONV_K, CONV_WIDTH), lambda b: (0, 0)),
            vec, vec, vec,
        ],
        out_specs=(pl.BlockSpec((bb, steps, CONV_WIDTH), lambda b: (b, 0, 0)), hist_spec),
        compiler_params=_params("arbitrary"),
        name=name,
    )(z3, z3, hist_t, w_dw, b_dw, g_cn, b_cn)


def _col_to_row(col, eye):
    n = col.shape[0]
    return jnp.sum(jnp.where(eye, jnp.broadcast_to(col, (n, n)), 0.0), axis=0, keepdims=True)


def _mlstm_kernel(q_ref, k_ref, v_ref, o_ref, gate_ref, c0_ref, n0_ref, m0_ref, gmh_ref,
                  h_ref, c_ref, n_ref, m_ref, *, cl, valid, bb, single_chunk):
    if single_chunk:
        c_in, n_in, m_in = c0_ref, n0_ref, m0_ref
    else:
        c_in, n_in, m_in = c_ref, n_ref, m_ref

        @pl.when(pl.program_id(1) == 0)
        def _():
            c_ref[...] = c0_ref[...]
            n_ref[...] = n0_ref[...]
            m_ref[...] = m0_ref[...]

    rows = lax.broadcasted_iota(jnp.int32, (cl, cl), 0)
    cols = lax.broadcasted_iota(jnp.int32, (cl, cl), 1)
    eye = rows == cols
    tril = rows >= cols
    row_id = lax.broadcasted_iota(jnp.int32, (cl, 1), 0)
    nh, dh = MLSTM_HEADS, MLSTM_HEAD_DIM
    nt_dims = (((1,), (1,)), ((), ()))
    tn_dims = (((0,), (0,)), ((), ()))
    each = lambda f, *lists: [f(*args) for args in zip(*lists)]
    all_pairs = [(bi, h) for bi in range(bb) for h in range(nh)]
    for g0 in range(0, len(all_pairs), MLSTM_STAGE_PAIRS):
        pairs = all_pairs[g0:g0 + MLSTM_STAGE_PAIRS]
        cols_of = [slice(h * dh, (h + 1) * dh) for _, h in pairs]
        gates = {bi: gate_ref[bi] for bi in sorted({bi for bi, _ in pairs})}
        log_sig = {bi: jnp.minimum(g, 0.0) - jnp.log1p(jnp.exp(-jnp.abs(g))) for bi, g in gates.items()}
        ip = each(lambda p: gates[p[0]][:, p[1]:p[1] + 1], pairs)
        lf = each(lambda p: log_sig[p[0]][:, nh + p[1]:nh + p[1] + 1], pairs)
        if valid < cl:
            ip = each(lambda x: jnp.where(row_id < valid, x, -jnp.inf), ip)
            lf = each(lambda x: jnp.where(row_id < valid, x, 0.0), lf)
        lf_row = each(lambda x: _col_to_row(x, eye), lf)
        ip_row = each(lambda x: _col_to_row(x, eye), ip)
        b_col = each(lambda r: jnp.sum(jnp.where(tril, jnp.broadcast_to(r, (cl, cl)), 0.0),
                                       axis=1, keepdims=True), lf_row)
        b_row = each(lambda x: _col_to_row(x, eye), b_col)
        m_prev = each(lambda p: m_in[p[0], p[1]:p[1] + 1, 0:1], pairs)
        log_inter = each(lambda b, m: b + m, b_col, m_prev)
        log_intra = each(lambda bc, br, ir: jnp.where(tril, bc - br + ir, -jnp.inf), b_col, b_row, ip_row)
        m_t = each(lambda le, la: jnp.maximum(le, jnp.max(la, axis=1, keepdims=True)), log_inter, log_intra)
        w_inter = each(lambda le, m: jnp.exp(le - m), log_inter, m_t)
        intra = each(lambda la, m: jnp.exp(la - m), log_intra, m_t)
        m_new = each(lambda m: m[cl - 1:cl, :], m_t)
        b_last = each(lambda b: b[cl - 1:cl, :], b_col)
        decay = each(lambda bl, mp, mn: jnp.exp(bl + mp - mn), b_last, m_prev, m_new)
        w_s = each(lambda bl, b, i, mn: jnp.exp(bl - b + i - mn), b_last, b_col, ip, m_new)
        q = each(lambda p, sl: q_ref[p[0], :, sl], pairs, cols_of)
        k = each(lambda p, sl: k_ref[p[0], :, sl] * (dh ** -0.5), pairs, cols_of)
        vb = each(lambda p, sl: v_ref[p[0], :, sl].astype(BF16), pairs, cols_of)
        qb = each(lambda x: x.astype(BF16), q)
        kb = each(lambda x: x.astype(BF16), k)
        s = each(lambda a, b: lax.dot_general(a, b, nt_dims, preferred_element_type=F32), qb, kb)
        s = each(lambda x, d: x * d, s, intra)
        c_old = each(lambda p: c_in[p[0], p[1]], pairs)
        n_old = each(lambda p: n_in[p[0], p[1]:p[1] + 1, :], pairs)
        inter = each(lambda a, c: jnp.dot(a, c.astype(BF16), preferred_element_type=F32), qb, c_old)
        local = each(lambda x, v: jnp.dot(x.astype(BF16), v, preferred_element_type=F32), s, vb)
        num = each(lambda w, a, b: w * a + b, w_inter, inter, local)
        den = each(lambda w, x, n, ss: w * jnp.sum(x * n, axis=1, keepdims=True)
                   + jnp.sum(ss, axis=1, keepdims=True), w_inter, q, n_old, s)
        hh = each(lambda a, d, m: a / jnp.maximum(jnp.abs(d), jnp.exp(-m)), num, den, m_t)
        kw = each(lambda x, w: x * w, k, w_s)
        outer = each(lambda a, v: lax.dot_general(a.astype(BF16), v, tn_dims, preferred_element_type=F32),
                     kw, vb)
        for i, (bi, h) in enumerate(pairs):
            c_ref[bi, h] = decay[i] * c_old[i] + outer[i]
            n_ref[bi, h:h + 1, :] = decay[i] * n_old[i] + jnp.sum(kw[i], axis=0, keepdims=True)
            m_ref[bi, h:h + 1, :] = jnp.broadcast_to(m_new[i], (1, LANES))
        mu = each(lambda x: jnp.mean(x, axis=-1, keepdims=True), hh)
        hc = each(lambda x, m: x - m, hh, mu)
        var = each(lambda x: jnp.mean(x * x, axis=-1, keepdims=True), hc)
        for i, (bi, h) in enumerate(pairs):
            sl = cols_of[i]
            hn = hc[i] * lax.rsqrt(var[i] + LN_EPS) * gmh_ref[:, sl]
            h_ref[bi, :, sl] = hn * jax.nn.sigmoid(o_ref[bi, :, sl])
    if single_chunk:
        m_ref[:, MLSTM_HEADS:, :] = jnp.zeros((bb, SUBLANES - MLSTM_HEADS, LANES), F32)


def _mlstm_group(z_main, z_tail, c0, n0, m0, g_mh, batch, seq, cl, valid, bb, name):
    nc = seq // cl
    zcol = lambda off: pl.BlockSpec((bb, cl, MLSTM_WIDTH), lambda b, c: (b, c, off // MLSTM_WIDTH))
    state = lambda shape: pl.BlockSpec((bb,) + shape, lambda b, c: (b,) + (0,) * len(shape))
    c_shape = (MLSTM_HEADS, MLSTM_HEAD_DIM, MLSTM_HEAD_DIM)
    n_shape = (MLSTM_HEADS, MLSTM_HEAD_DIM)
    m_shape = (SUBLANES, LANES)
    return pl.pallas_call(
        functools.partial(_mlstm_kernel, cl=cl, valid=valid, bb=bb, single_chunk=nc == 1),
        out_shape=(jax.ShapeDtypeStruct((batch, seq, MLSTM_WIDTH), F32),
                   jax.ShapeDtypeStruct((batch,) + c_shape, F32),
                   jax.ShapeDtypeStruct((batch,) + n_shape, F32),
                   jax.ShapeDtypeStruct((batch,) + m_shape, F32)),
        grid=(batch // bb, nc),
        in_specs=[
            zcol(Z_Q), zcol(Z_K), zcol(Z_V), zcol(Z_O),
            pl.BlockSpec((bb, cl, LANES), lambda b, c: (b, c, ZT_GATE // LANES)),
            state(c_shape), state(n_shape), state(m_shape),
            pl.BlockSpec((1, MLSTM_WIDTH), lambda b, c: (0, 0)),
        ],
        out_specs=(pl.BlockSpec((bb, cl, MLSTM_WIDTH), lambda b, c: (b, c, 0)),
                   state(c_shape), state(n_shape), state(m_shape)),
        compiler_params=_params("arbitrary", "arbitrary"),
        name=name,
    )(z_main, z_main, z_main, z_main, z_tail, c0, n0, m0, g_mh)


def _softmax_rows(s):
    e = jnp.exp(s - jnp.max(s, axis=-1, keepdims=True))
    return e / jnp.sum(e, axis=-1, keepdims=True)


_NT_DIMS = (((1,), (1,)), ((), ()))


def _memattn_head_kernel(q_ref, k_ref, v_ref, o_ref):
    s = lax.dot_general(q_ref[...].astype(BF16), k_ref[...].astype(BF16), _NT_DIMS,
                        preferred_element_type=F32) * (MEM_HEAD_DIM ** -0.5)
    p = _softmax_rows(s)
    o_ref[...] = jnp.dot(p.astype(BF16), v_ref[...].astype(BF16), preferred_element_type=F32)


def _memattn_heads(z_tail, kv, batch, seq, tq, name):
    nq = seq // tq
    dh = MEM_HEAD_DIM
    return pl.pallas_call(
        _memattn_head_kernel,
        out_shape=jax.ShapeDtypeStruct((batch * seq, MEM_WIDTH), F32),
        grid=(batch, MEM_HEADS, nq),
        in_specs=[pl.BlockSpec((tq, dh), lambda b, h, i: (b * nq + i, ZT_QM // dh + h)),
                  pl.BlockSpec((N_MEM, dh), lambda b, h, i: (b, h)),
                  pl.BlockSpec((N_MEM, dh), lambda b, h, i: (b, MEM_HEADS + h))],
        out_specs=pl.BlockSpec((tq, dh), lambda b, h, i: (b * nq + i, h)),
        compiler_params=_params("arbitrary", "arbitrary", "arbitrary"),
        name=name,
    )(z_tail, kv, kv)


def _memattn_packed_kernel(q_ref, k_ref, v_ref, o_ref, *, tq, bb):
    nh, dh = MEM_HEADS, MEM_HEAD_DIM
    shape = (nh * tq, N_MEM * nh)
    row_head = lax.broadcasted_iota(jnp.int32, shape, 0) // tq
    col_head = lax.broadcasted_iota(jnp.int32, shape, 1) % nh
    same_head = row_head == col_head
    scores = []
    for bi in range(bb):
        rs = slice(bi * tq, (bi + 1) * tq)
        q = jnp.concatenate([q_ref[rs, h * dh:(h + 1) * dh] for h in range(nh)], axis=0)
        scores.append(lax.dot_general(q.astype(BF16), k_ref[bi].astype(BF16), _NT_DIMS,
                                      preferred_element_type=F32) * (dh ** -0.5))
    probs = [_softmax_rows(jnp.where(same_head, s, -jnp.inf)).astype(BF16) for s in scores]
    outs = [jnp.dot(p, v_ref[bi].astype(BF16), preferred_element_type=F32) for bi, p in enumerate(probs)]
    for bi, o in enumerate(outs):
        rs = slice(bi * tq, (bi + 1) * tq)
        for h in range(nh):
            o_ref[rs, h * dh:(h + 1) * dh] = o[h * tq:(h + 1) * tq, :]


def _memattn_packed(z_tail, mk, mv, batch, tq, bb, name):
    rows = bb * tq
    kv = pl.BlockSpec((bb, N_MEM * MEM_HEADS, MEM_HEAD_DIM), lambda b: (b, 0, 0))
    return pl.pallas_call(
        functools.partial(_memattn_packed_kernel, tq=tq, bb=bb),
        out_shape=jax.ShapeDtypeStruct((batch * tq, MEM_WIDTH), F32),
        grid=(batch // bb,),
        in_specs=[pl.BlockSpec((rows, MEM_WIDTH), lambda b: (b, ZT_QM // MEM_WIDTH)), kv, kv],
        out_specs=pl.BlockSpec((rows, MEM_WIDTH), lambda b: (b, 0)),
        compiler_params=_params("arbitrary"),
        name=name,
    )(z_tail, mk, mv)


def _route(logits):
    lane = lax.broadcasted_iota(jnp.int32, logits.shape, 1).astype(F32)
    neg = -jnp.inf
    first = lambda mask: jnp.min(jnp.where(mask, lane, float(LANES)), axis=1, keepdims=True)
    is_g = lane < N_GROUPS
    gl = jnp.where(is_g, logits, neg)
    g_max = jnp.max(gl, axis=1, keepdims=True)
    g_sel = first(gl == g_max)
    g_w = 1.0 / jnp.sum(jnp.exp(gl - g_max), axis=1, keepdims=True)
    lo = N_GROUPS + g_sel * EXPERTS_PER_GROUP
    in_grp = (lane >= lo) & (lane < lo + EXPERTS_PER_GROUP)
    el = jnp.where(in_grp, logits, neg)
    v1 = jnp.max(el, axis=1, keepdims=True)
    i1 = first(in_grp & (el == v1))
    rest = in_grp & (lane != i1)
    el2 = jnp.where(rest, logits, neg)
    v2 = jnp.max(el2, axis=1, keepdims=True)
    i2 = first(rest & (el2 == v2))
    t = jnp.exp(v2 - v1)
    w1 = g_w / (1.0 + t)
    w2 = g_w * t / (1.0 + t)
    lane_i = lax.broadcasted_iota(jnp.int32, logits.shape, 1)
    e1 = (i1 - N_GROUPS).astype(jnp.int32)
    e2 = (i2 - N_GROUPS).astype(jnp.int32)
    eid = jnp.where(lane_i == 0, e1, jnp.where(lane_i == 1, e2, 0))
    ew = jnp.where(lane_i == 0, w1, jnp.where(lane_i == 1, w2, 0.0))
    return eid, ew


def _outproj_kernel(cp, hp, mp, xp, cs, hs, ms, xs, wout, g1, b1, wr, br,
                    x1_ref, x1p_ref, eid_ref, ew_ref, *, n_prompt):
    def rows(c, h, m, x, rs):
        groups = jnp.concatenate(
            [c[rs, :].astype(BF16), h[rs, :].astype(BF16), m[rs, :].astype(BF16)], axis=1)
        mix = jnp.dot(groups, wout[...], preferred_element_type=F32)
        x1 = _layer_norm_rows(ALPHA * x[rs, :] + mix, g1[...], b1[...])
        x1_ref[rs, :] = x1
        xh = x1.astype(BF16)
        xh_f32 = xh.astype(F32)
        bits = pltpu.bitcast(xh_f32, jnp.int32)
        x1p_ref[rs, :] = bits[:, :D_MODEL // 2] | lax.shift_right_logical(bits[:, D_MODEL // 2:], 16)
        xl = (x1 - xh_f32).astype(BF16)
        n = x1.shape[0]
        cross = jnp.dot(jnp.concatenate([xh, xl], axis=0), wr[...], preferred_element_type=F32)
        logits = ((cross[:n, :ROUTER_LANES] + cross[:n, ROUTER_LANES:])
                  + (cross[n:, :ROUTER_LANES] + cross[n:, ROUTER_LANES:])) + br[...]
        eid, ew = _route(logits)
        eid_ref[rs, :] = eid
        ew_ref[rs, :] = ew

    def body(c, h, m, x):
        rows(c, h, m, x, slice(None))

    i = pl.program_id(0)
    pl.when(i < n_prompt)(lambda: body(cp, hp, mp, xp))
    pl.when(i >= n_prompt)(lambda: body(cs, hs, ms, xs))


def _outproj_router(prompt, sample, w_out_b, g1, b1, wr, br):
    tm = OUT_TM
    tp = prompt[0].shape[0]
    ts = sample[0].shape[0]
    n_p, n_s = tp // tm, ts // tm
    total = tp + ts
    widths = (CONV_WIDTH, MLSTM_WIDTH, MEM_WIDTH, D_MODEL)
    p_specs = [pl.BlockSpec((tm, w), lambda i: (jnp.minimum(i, n_p - 1), 0)) for w in widths]
    s_specs = [pl.BlockSpec((tm, w), lambda i: (jnp.maximum(i - n_p, 0), 0)) for w in widths]
    full = lambda shape: pl.BlockSpec(shape, lambda i: (0, 0))
    row = lambda w: pl.BlockSpec((tm, w), lambda i: (i, 0))
    return pl.pallas_call(
        functools.partial(_outproj_kernel, n_prompt=n_p),
        out_shape=(jax.ShapeDtypeStruct((total, D_MODEL), F32),
                   jax.ShapeDtypeStruct((total, D_MODEL // 2), jnp.int32),
                   jax.ShapeDtypeStruct((total, ROUTER_LANES), jnp.int32),
                   jax.ShapeDtypeStruct((total, ROUTER_LANES), F32)),
        grid=(n_p + n_s,),
        in_specs=p_specs + s_specs + [
            full((D_MODEL, D_MODEL)), full((1, D_MODEL)), full((1, D_MODEL)),
            full((D_MODEL, 2 * ROUTER_LANES)), full((1, ROUTER_LANES)),
        ],
        out_specs=(row(D_MODEL), row(D_MODEL // 2), row(ROUTER_LANES), row(ROUTER_LANES)),
        compiler_params=_params("arbitrary"),
        name="outproj_router",
    )(*prompt, *sample, w_out_b, g1, b1, wr, br)


def _moe_kernel(stok_ref, sdst_ref, base_ref, nval_ref, first_ref, count_ref,
                x_hbm, wg_ref, wu_ref, wd_ref, y_hbm, wgb, wub, wdb, xbuf, ybuf, gsem, ssem,
                *, dump_row):
    e = pl.program_id(0)
    last = pl.num_programs(0) - 1
    rc = MOE_ROWS
    ns = MOE_SLOTS

    def gather_start(base, slot):
        for r in range(rc):
            pltpu.make_async_copy(x_hbm.at[pl.ds(stok_ref[base + r], 1)],
                                  xbuf.at[slot, pl.ds(r, 1)], gsem.at[slot]).start(priority=ROW_DMA_PRIORITY)

    def gather_wait(slot):
        pltpu.make_async_copy(x_hbm.at[pl.ds(0, rc)], xbuf.at[slot], gsem.at[slot]).wait()

    def scatter_start(base, n_valid, slot):
        for r in range(rc):
            dst = jnp.where(r < n_valid, sdst_ref[base + r], dump_row + slot * rc + r)
            pltpu.make_async_copy(ybuf.at[slot, pl.ds(r, 1)], y_hbm.at[pl.ds(dst, 1)],
                                  ssem.at[slot]).start(priority=ROW_DMA_PRIORITY)

    def scatter_wait(slot):
        pltpu.make_async_copy(ybuf.at[slot], y_hbm.at[pl.ds(0, rc)], ssem.at[slot]).wait()

    @pl.when(e == 0)
    def _():
        ybuf[...] = jnp.zeros(ybuf.shape, F32)
        for slot in range(ns):
            pltpu.make_async_copy(ybuf.at[slot], y_hbm.at[pl.ds(dump_row + slot * rc, rc)],
                                  ssem.at[slot]).start()
            scatter_wait(slot)
        for ahead in range(ns - 1):
            gather_start(base_ref[1 + ahead], ahead)

    wgb[...] = wg_ref[0].astype(BF16)
    wub[...] = wu_ref[0].astype(BF16)
    wdb[...] = wd_ref[0].astype(BF16)
    first = first_ref[e]

    def chunk_on(slot, g):
        prv = (slot - 1) % ns
        gather_wait(slot)

        @pl.when(g >= ns - 1)
        def _():
            scatter_wait(slot)

        gather_start(base_ref[g + ns], prv)
        scatter_start(base_ref[g], nval_ref[g], prv)
        packed = xbuf[slot]
        x = jnp.concatenate(
            [pltpu.bitcast(packed & jnp.int32(-65536), F32).astype(BF16),
             pltpu.bitcast(packed << 16, F32).astype(BF16)], axis=1)
        hg = jnp.dot(x, wgb[...], preferred_element_type=F32)
        hu = jnp.dot(x, wub[...], preferred_element_type=F32)
        hid = (hg * jax.nn.sigmoid(hg) * hu).astype(BF16)
        ybuf[slot] = jnp.dot(hid, wdb[...], preferred_element_type=F32)

    def chunk(c, carry):
        g = first + c
        cur = lax.rem(g, ns)
        for slot in range(ns):
            pl.when(cur == slot)(functools.partial(chunk_on, slot, g))
        return carry

    lax.fori_loop(0, count_ref[e], chunk, 0)

    @pl.when(e == last)
    def _():
        g_end = first + count_ref[e]
        cur = lax.rem(g_end, ns)
        for slot in range(ns):
            @pl.when(cur == slot)
            def _():
                for ahead in range(ns - 1):
                    gather_wait((slot + ahead) % ns)
                scatter_start(base_ref[g_end], nval_ref[g_end], (slot - 1) % ns)

        for slot in range(ns):
            scatter_wait(slot)


def _moe(x1, tables, w_gate, w_up, w_down):
    total = x1.shape[0]
    dump_row = 2 * total
    wspec = lambda shape: pl.BlockSpec((1,) + shape, lambda e, *_: (e, 0, 0))
    grid_spec = pltpu.PrefetchScalarGridSpec(
        num_scalar_prefetch=len(tables),
        grid=(N_EXPERTS,),
        in_specs=[pl.BlockSpec(memory_space=pl.ANY),
                  wspec((D_MODEL, D_EXPERT)), wspec((D_MODEL, D_EXPERT)), wspec((D_EXPERT, D_MODEL))],
        out_specs=pl.BlockSpec(memory_space=pl.ANY),
        scratch_shapes=[
            pltpu.VMEM((D_MODEL, D_EXPERT), BF16), pltpu.VMEM((D_MODEL, D_EXPERT), BF16),
            pltpu.VMEM((D_EXPERT, D_MODEL), BF16),
            pltpu.VMEM((MOE_SLOTS, MOE_ROWS, D_MODEL // 2), jnp.int32),
            pltpu.VMEM((MOE_SLOTS, MOE_ROWS, D_MODEL), F32),
            pltpu.SemaphoreType.DMA((MOE_SLOTS,)), pltpu.SemaphoreType.DMA((MOE_SLOTS,)),
        ],
    )
    return pl.pallas_call(
        functools.partial(_moe_kernel, dump_row=dump_row),
        out_shape=jax.ShapeDtypeStruct((2 * total + MOE_SLOTS * MOE_ROWS, D_MODEL), F32),
        grid_spec=grid_spec,
        compiler_params=_params("arbitrary"),
        name="moe_experts",
    )(*tables, x1, w_gate, w_up, w_down)


def _final_kernel(x1_ref, y0_ref, y1_ref, ew_ref, g2, b2, o_ref):
    ew = ew_ref[...]
    ffn = ew[:, 0:1] * y0_ref[...] + ew[:, 1:2] * y1_ref[...]
    o_ref[...] = _layer_norm_rows(ALPHA * x1_ref[...] + ffn, g2[...], b2[...])


def _final(x1, y, ew, g2, b2, row0, rows, name):
    tm = FINAL_TM
    total = x1.shape[0]
    off = row0 // tm
    k1 = total // tm
    vec = pl.BlockSpec((1, D_MODEL), lambda i: (0, 0))
    return pl.pallas_call(
        _final_kernel,
        out_shape=jax.ShapeDtypeStruct((rows, D_MODEL), F32),
        grid=(rows // tm,),
        in_specs=[pl.BlockSpec((tm, D_MODEL), lambda i: (i + off, 0)),
                  pl.BlockSpec((tm, D_MODEL), lambda i: (i + off, 0)),
                  pl.BlockSpec((tm, D_MODEL), lambda i: (i + off + k1, 0)),
                  pl.BlockSpec((tm, ROUTER_LANES), lambda i: (i + off, 0)),
                  vec, vec],
        out_specs=pl.BlockSpec((tm, D_MODEL), lambda i: (i, 0)),
        compiler_params=_params("arbitrary"),
        name=name,
    )(x1, y, y, ew, g2, b2)


def _dispatch_tables(eid, total):
    flat_e = eid[:, :2].reshape(-1)
    order = jnp.argsort(flat_e, stable=True).astype(jnp.int32)
    stok = order >> 1
    sdst = (order & 1) * total + stok
    experts = jnp.arange(N_EXPERTS, dtype=jnp.int32)
    cnt = jnp.sum((flat_e[:, None] == experts[None, :]).astype(jnp.int32), axis=0)
    start = jnp.cumsum(cnt) - cnt
    pad = jnp.zeros((MOE_ROWS,), jnp.int32)
    n_chunks = jnp.maximum((cnt + MOE_ROWS - 1) // MOE_ROWS, 1)
    first = jnp.cumsum(n_chunks) - n_chunks
    n_entries = flat_e.shape[0] // MOE_ROWS + N_EXPERTS + MOE_SLOTS + 1
    g = jnp.arange(n_entries, dtype=jnp.int32) - 1
    owner = jnp.sum((g[:, None] >= (first + n_chunks)[None, :]).astype(jnp.int32), axis=1)
    real = (g >= 0) & (owner < N_EXPERTS)
    pick = (owner[:, None] == experts[None, :]).astype(jnp.int32)
    sel = lambda v: jnp.sum(pick * v[None, :], axis=1)
    local = (g - sel(first)) * MOE_ROWS
    base = jnp.where(real, sel(start) + local, 0)
    n_valid = jnp.where(real, sel(cnt) - local, 0)
    return (jnp.concatenate([stok, pad]), jnp.concatenate([sdst, pad]), base, n_valid, first, n_chunks)


def kernel(x_prompt, x_sample, mem_prompt, state_conv, state_mlstm_C, state_mlstm_n, state_mlstm_m,
           cache_mem_k, cache_mem_v, w_in, b_in, w_dw, b_dw, g_cn, b_cn, g_mh, w_mk, w_mv, w_out,
           g_ln1, b_ln1, w_rg, b_rg, w_re, b_re, w_gate, w_up, w_down, g_ln2, b_ln2):
    bp, sp, _ = x_prompt.shape
    bs, ss, _ = x_sample.shape
    tp, ts = bp * sp, bs * ss

    gate_hi = Z_MAIN_COLS + N_GATE_COLS
    tail = lambda wt: jnp.concatenate(
        [wt[gate_hi:], wt[Z_MAIN_COLS:gate_hi],
         jnp.zeros((LANES - N_GATE_COLS,) + wt.shape[1:], wt.dtype)], axis=0)
    w_in_t = jnp.transpose(w_in[0]).astype(BF16)
    w_tail_t = tail(w_in_t)
    b_tail = tail(b_in[0])[None, :]
    w_kv_b = jnp.concatenate([w_mk[0], w_mv[0]], axis=-1).astype(BF16)
    w_out_b = w_out[0].astype(BF16)
    w_r = jnp.concatenate([w_rg[0], w_re[0],
                           jnp.zeros((D_MODEL, ROUTER_LANES - N_GROUPS - N_EXPERTS), F32)], axis=-1)
    w_r_hi = w_r.astype(BF16)
    w_r_lo = (w_r - w_r_hi.astype(F32)).astype(BF16)
    w_r_split = jnp.concatenate([w_r_hi, w_r_lo], axis=1)
    b_r = jnp.concatenate([b_rg[0], b_re[0],
                           jnp.zeros((ROUTER_LANES - N_GROUPS - N_EXPERTS,), F32)])[None, :]
    row = lambda a: a[0][None, :]

    xs_pad = jnp.pad(x_sample, ((0, 0), (0, SAMPLE_PAD - ss), (0, 0))).reshape(bs * SAMPLE_PAD, D_MODEL)
    zm_p, zt_p = _inproj(x_prompt.reshape(tp, D_MODEL), w_in_t, b_in, w_tail_t, b_tail, "inproj_prompt")
    zm_s, zt_s = _inproj(xs_pad, w_in_t, b_in, w_tail_t, b_tail, "inproj_sample")

    kv = _matmul(mem_prompt.reshape(bp * N_MEM, D_MODEL), w_kv_b, KV_TN, "mem_kv")
    mk_p = kv[:, :MEM_WIDTH].reshape(bp, N_MEM, MEM_HEADS, MEM_HEAD_DIM)
    mv_p = kv[:, MEM_WIDTH:].reshape(bp, N_MEM, MEM_HEADS, MEM_HEAD_DIM)

    conv_args = (w_dw[0], row(b_dw), row(g_cn), row(b_cn))
    conv_p, buf_p = _conv_group(zm_p, jnp.zeros((bp, CONV_HIST, CONV_WIDTH), F32), *conv_args,
                                bp, sp, 512, "conv_prompt")
    conv_s, buf_s_t = _conv_step_group(
        zm_s.reshape(bs, SAMPLE_PAD, Z_MAIN_COLS), jnp.transpose(state_conv[0], (1, 0, 2)),
        *conv_args, ss, 32, "conv_sample")
    buf_s = jnp.transpose(buf_s_t, (1, 0, 2))

    m_tile = lambda m: jnp.broadcast_to(
        jnp.pad(m, ((0, 0), (0, SUBLANES - MLSTM_HEADS)))[:, :, None], (m.shape[0], SUBLANES, LANES))
    g_mh_r = row(g_mh)
    seqs = lambda z, b: z.reshape(b, z.shape[0] // b, z.shape[1])
    h_p, c_p, n_p, m_p = _mlstm_group(
        seqs(zm_p, bp), seqs(zt_p, bp),
        jnp.zeros((bp, MLSTM_HEADS, MLSTM_HEAD_DIM, MLSTM_HEAD_DIM), F32),
        jnp.zeros((bp, MLSTM_HEADS, MLSTM_HEAD_DIM), F32), jnp.zeros((bp, SUBLANES, LANES), F32),
        g_mh_r, bp, sp, MLSTM_CHUNK, MLSTM_CHUNK, 2, "mlstm_prompt")
    h_s, c_s, n_s, m_s = _mlstm_group(
        seqs(zm_s, bs), seqs(zt_s, bs), state_mlstm_C[0], state_mlstm_n[0], m_tile(state_mlstm_m[0]),
        g_mh_r, bs, SAMPLE_PAD, SAMPLE_PAD, ss, 8, "mlstm_sample")
    h_p = h_p.reshape(tp, MLSTM_WIDTH)

    mem_p = _memattn_heads(zt_p, kv, bp, sp, sp, "memattn_prompt")
    packed = lambda c: c[0].reshape(bs, N_MEM * MEM_HEADS, MEM_HEAD_DIM)
    mem_s = _memattn_packed(zt_s, packed(cache_mem_k), packed(cache_mem_v), bs, SAMPLE_PAD, 8,
                            "memattn_sample")

    compact = lambda a: a.reshape(bs, SAMPLE_PAD, a.shape[-1])[:, :ss].reshape(ts, a.shape[-1])
    x1, x1_packed, eid, ew = _outproj_router(
        (conv_p, h_p, mem_p, x_prompt.reshape(tp, D_MODEL)),
        (conv_s.reshape(ts, CONV_WIDTH), h_s[:, :ss].reshape(ts, MLSTM_WIDTH), compact(mem_s),
         x_sample.reshape(ts, D_MODEL)),
        w_out_b, row(g_ln1), row(b_ln1), w_r_split, b_r)

    total = tp + ts
    y = _moe(x1_packed, _dispatch_tables(eid, total), w_gate[0], w_up[0], w_down[0])
    g2, b2 = row(g_ln2), row(b_ln2)
    y_p = _final(x1, y, ew, g2, b2, 0, tp, "final_prompt").reshape(bp, sp, D_MODEL)
    y_s = _final(x1, y, ew, g2, b2, tp, ts, "final_sample").reshape(bs, ss, D_MODEL)

    return (y_p, y_s, buf_p[None], buf_s[None], c_p[None], c_s[None], n_p[None], n_s[None],
            m_p[:, :MLSTM_HEADS, 0][None], m_s[:, :MLSTM_HEADS, 0][None], mk_p[None], mv_p[None])
```

```python
import functools

import jax
import jax.numpy as jnp
from jax import lax
from jax.experimental import pallas as pl
from jax.experimental.pallas import tpu as pltpu

F32 = jnp.float32
BF16 = jnp.bfloat16

D_MODEL = 2048
CONV_WIDTH = 512
CONV_K = 31
CONV_HIST = CONV_K - 1
MLSTM_HEADS = 4
MLSTM_HEAD_DIM = 256
MLSTM_WIDTH = MLSTM_HEADS * MLSTM_HEAD_DIM
MLSTM_CHUNK = 128
MEM_HEADS = 4
MEM_HEAD_DIM = 128
MEM_WIDTH = MEM_HEADS * MEM_HEAD_DIM
N_MEM = 256
N_GROUPS = 8
EXPERTS_PER_GROUP = 8
N_EXPERTS = N_GROUPS * EXPERTS_PER_GROUP
D_EXPERT = 512
LN_EPS = 1e-5
DEPTH = 1
ALPHA = (2 * DEPTH) ** 0.25

LANES = 128
SUBLANES = 8
VMEM_LIMIT_BYTES = 56 * 1024 * 1024

Z_CONV_A = 0
Z_CONV_G = CONV_WIDTH
Z_Q = 2 * CONV_WIDTH
Z_K = Z_Q + MLSTM_WIDTH
Z_V = Z_K + MLSTM_WIDTH
Z_O = Z_V + MLSTM_WIDTH
Z_MAIN_COLS = Z_O + MLSTM_WIDTH
N_GATE_COLS = 2 * MLSTM_HEADS
ZT_QM = 0
ZT_GATE = MEM_WIDTH
Z_TAIL_COLS = MEM_WIDTH + LANES
INPROJ_TM = 1024
INPROJ_TN = 1024
KV_TN = 256

SAMPLE_PAD = SUBLANES
MLSTM_STAGE_PAIRS = 8
ROUTER_LANES = LANES
MOE_ROWS = 160
MOE_SLOTS = 4
ROW_DMA_PRIORITY = 1
OUT_TM = 256
FINAL_TM = 512


def _params(*sem):
    return pltpu.CompilerParams(dimension_semantics=sem, vmem_limit_bytes=VMEM_LIMIT_BYTES)


def _inproj_kernel(x_ref, wm_ref, bm_ref, wt_ref, bt_ref, zm_ref, zt_ref, xb_ref, *, n_main):
    j = pl.program_id(1)

    @pl.when(j == 0)
    def _():
        xb_ref[...] = x_ref[...].astype(BF16)

    nt = (((1,), (1,)), ((), ()))

    @pl.when(j < n_main)
    def _():
        zm_ref[...] = lax.dot_general(xb_ref[...], wm_ref[...], nt,
                                      preferred_element_type=F32) + bm_ref[...]

    @pl.when(j == n_main)
    def _():
        zt_ref[...] = lax.dot_general(xb_ref[...], wt_ref[...], nt,
                                      preferred_element_type=F32) + bt_ref[...]


def _inproj(x, w_in_t, b_in, w_tail_t, b_tail, name):
    t, k = x.shape
    tm, tn = INPROJ_TM, INPROJ_TN
    n_main = Z_MAIN_COLS // tn
    main_col = lambda j: jnp.minimum(j, n_main - 1)
    return pl.pallas_call(
        functools.partial(_inproj_kernel, n_main=n_main),
        out_shape=(jax.ShapeDtypeStruct((t, Z_MAIN_COLS), F32),
                   jax.ShapeDtypeStruct((t, Z_TAIL_COLS), F32)),
        grid=(t // tm, n_main + 1),
        in_specs=[
            pl.BlockSpec((tm, k), lambda i, j: (i, 0)),
            pl.BlockSpec((tn, k), lambda i, j: (main_col(j), 0)),
            pl.BlockSpec((1, tn), lambda i, j: (0, main_col(j))),
            pl.BlockSpec((Z_TAIL_COLS, k), lambda i, j: (0, 0)),
            pl.BlockSpec((1, Z_TAIL_COLS), lambda i, j: (0, 0)),
        ],
        out_specs=(pl.BlockSpec((tm, tn), lambda i, j: (i, main_col(j))),
                   pl.BlockSpec((tm, Z_TAIL_COLS), lambda i, j: (i, 0))),
        scratch_shapes=[pltpu.VMEM((tm, k), BF16)],
        compiler_params=_params("arbitrary", "arbitrary"),
        name=name,
    )(x, w_in_t, b_in, w_tail_t, b_tail)


def _matmul_kernel(x_ref, w_ref, o_ref):
    o_ref[...] = jnp.dot(x_ref[...].astype(BF16), w_ref[...], preferred_element_type=F32)


def _matmul(x, w_bf16, tn, name):
    t, k = x.shape
    n = w_bf16.shape[1]
    return pl.pallas_call(
        _matmul_kernel,
        out_shape=jax.ShapeDtypeStruct((t, n), F32),
        grid=(n // tn,),
        in_specs=[pl.BlockSpec((t, k), lambda j: (0, 0)), pl.BlockSpec((k, tn), lambda j: (0, j))],
        out_specs=pl.BlockSpec((t, tn), lambda j: (0, j)),
        compiler_params=_params("arbitrary"),
        name=name,
    )(x, w_bf16)


def _layer_norm_rows(y, g, b):
    mu = jnp.mean(y, axis=-1, keepdims=True)
    yc = y - mu
    var = jnp.mean(yc * yc, axis=-1, keepdims=True)
    return yc * lax.rsqrt(var + LN_EPS) * g + b


CONV_ROWS = 64


def _conv_kernel(a_ref, g_ref, hist_ref, wdw_ref, bdw_ref, gcn_ref, bcn_ref, out_ref, nb_ref,
                 ubuf, shifted, *, tl):
    head = CONV_HIST + 2
    li = pl.program_id(1)

    @pl.when(li == 0)
    def _():
        ubuf[0:2, :] = jnp.zeros((2, CONV_WIDTH), F32)
        ubuf[2:head, :] = hist_ref[0]

    ubuf[head:head + tl, :] = a_ref[...] * jax.nn.sigmoid(g_ref[...])
    span = shifted.shape[1]
    for k in range(1, SUBLANES):
        shifted[k - 1] = ubuf[k:k + span, :]
    for r0 in range(0, tl, CONV_ROWS):
        acc = jnp.zeros((CONV_ROWS, CONV_WIDTH), F32) + bdw_ref[...]
        for j in range(CONV_K):
            lo, k = divmod(2 + j, SUBLANES)
            lo = lo * SUBLANES + r0
            src = ubuf if k == 0 else shifted.at[k - 1]
            acc = acc + wdw_ref[j:j + 1, :] * src[lo:lo + CONV_ROWS, :]
        y = _layer_norm_rows(acc, gcn_ref[...], bcn_ref[...])
        out_ref[r0:r0 + CONV_ROWS, :] = y * jax.nn.sigmoid(y)

    @pl.when(li == pl.num_programs(1) - 1)
    def _():
        nb_ref[0] = ubuf[2 + tl:head + tl, :]

    ubuf[2:head, :] = ubuf[2 + tl:head + tl, :]


def _conv_group(z, hist, w_dw, b_dw, g_cn, b_cn, batch, seq, tl, name):
    nl = seq // tl
    row = lambda b, l: (b * nl + l, 0)
    vec = pl.BlockSpec((1, CONV_WIDTH), lambda b, l: (0, 0))
    return pl.pallas_call(
        functools.partial(_conv_kernel, tl=tl),
        out_shape=(jax.ShapeDtypeStruct((batch * seq, CONV_WIDTH), F32),
                   jax.ShapeDtypeStruct((batch, CONV_HIST, CONV_WIDTH), F32)),
        grid=(batch, nl),
        in_specs=[
            pl.BlockSpec((tl, CONV_WIDTH), lambda b, l: (b * nl + l, Z_CONV_A // CONV_WIDTH)),
            pl.BlockSpec((tl, CONV_WIDTH), lambda b, l: (b * nl + l, Z_CONV_G // CONV_WIDTH)),
            pl.BlockSpec((1, CONV_HIST, CONV_WIDTH), lambda b, l: (b, 0, 0)),
            pl.BlockSpec((CONV_K, CONV_WIDTH), lambda b, l: (0, 0)),
            vec, vec, vec,
        ],
        out_specs=(pl.BlockSpec((tl, CONV_WIDTH), row),
                   pl.BlockSpec((1, CONV_HIST, CONV_WIDTH), lambda b, l: (b, 0, 0))),
        scratch_shapes=[pltpu.VMEM((CONV_HIST + 2 + tl, CONV_WIDTH), F32),
                        pltpu.VMEM((SUBLANES - 1, CONV_HIST + 2 + tl - SUBLANES, CONV_WIDTH), F32)],
        compiler_params=_params("arbitrary", "arbitrary"),
        name=name,
    )(z, z, hist, w_dw, b_dw, g_cn, b_cn)


def _conv_step_kernel(a_ref, g_ref, hist_ref, wdw_ref, bdw_ref, gcn_ref, bcn_ref, out_ref, nb_ref,
                      *, steps):
    u = [a_ref[:, t, :] * jax.nn.sigmoid(g_ref[:, t, :]) for t in range(steps)]
    full = lambda r: hist_ref[r] if r < CONV_HIST else u[r - CONV_HIST]
    for t in range(steps):
        acc = bdw_ref[...] + wdw_ref[0:1, :] * full(t)
        for j in range(1, CONV_K):
            acc = acc + wdw_ref[j:j + 1, :] * full(t + j)
        y = _layer_norm_rows(acc, gcn_ref[...], bcn_ref[...])
        out_ref[:, t, :] = y * jax.nn.sigmoid(y)
    for r in range(CONV_HIST):
        nb_ref[r] = full(r + steps)


def _conv_step_group(z3, hist_t, w_dw, b_dw, g_cn, b_cn, steps, bb, name):
    batch = z3.shape[0]
    vec = pl.BlockSpec((1, CONV_WIDTH), lambda b: (0, 0))
    hist_spec = pl.BlockSpec((CONV_HIST, bb, CONV_WIDTH), lambda b: (0, b, 0))
    return pl.pallas_call(
        functools.partial(_conv_step_kernel, steps=steps),
        out_shape=(jax.ShapeDtypeStruct((batch, steps, CONV_WIDTH), F32),
                   jax.ShapeDtypeStruct((CONV_HIST, batch, CONV_WIDTH), F32)),
        grid=(batch // bb,),
        in_specs=[
            pl.BlockSpec((bb, SAMPLE_PAD, CONV_WIDTH), lambda b: (b, 0, Z_CONV_A // CONV_WIDTH)),
            pl.BlockSpec((bb, SAMPLE_PAD, CONV_WIDTH), lambda b: (b, 0, Z_CONV_G // CONV_WIDTH)),
            hist_spec,
            pl.BlockSpec((CONV_K, CONV_WIDTH), lambda b: (0, 0)),
            vec, vec, vec,
        ],
        out_specs=(pl.BlockSpec((bb, steps, CONV_WIDTH), lambda b: (b, 0, 0)), hist_spec),
        compiler_params=_params("arbitrary"),
        name=name,
    )(z3, z3, hist_t, w_dw, b_dw, g_cn, b_cn)


def _col_to_row(col, eye):
    n = col.shape[0]
    return jnp.sum(jnp.where(eye, jnp.broadcast_to(col, (n, n)), 0.0), axis=0, keepdims=True)


def _mlstm_kernel(q_ref, k_ref, v_ref, o_ref, gate_ref, c0_ref, n0_ref, m0_ref, gmh_ref,
                  h_ref, c_ref, n_ref, m_ref, *, cl, valid, bb, single_chunk):
    if single_chunk:
        c_in, n_in, m_in = c0_ref, n0_ref, m0_ref
    else:
        c_in, n_in, m_in = c_ref, n_ref, m_ref

        @pl.when(pl.program_id(1) == 0)
        def _():
            c_ref[...] = c0_ref[...]
            n_ref[...] = n0_ref[...]
            m_ref[...] = m0_ref[...]

    rows = lax.broadcasted_iota(jnp.int32, (cl, cl), 0)
    cols = lax.broadcasted_iota(jnp.int32, (cl, cl), 1)
    eye = rows == cols
    tril = rows >= cols
    row_id = lax.broadcasted_iota(jnp.int32, (cl, 1), 0)
    nh, dh = MLSTM_HEADS, MLSTM_HEAD_DIM
    nt_dims = (((1,), (1,)), ((), ()))
    tn_dims = (((0,), (0,)), ((), ()))
    each = lambda f, *lists: [f(*args) for args in zip(*lists)]
    all_pairs = [(bi, h) for bi in range(bb) for h in range(nh)]
    for g0 in range(0, len(all_pairs), MLSTM_STAGE_PAIRS):
        pairs = all_pairs[g0:g0 + MLSTM_STAGE_PAIRS]
        cols_of = [slice(h * dh, (h + 1) * dh) for _, h in pairs]
        gates = {bi: gate_ref[bi] for bi in sorted({bi for bi, _ in pairs})}
        log_sig = {bi: jnp.minimum(g, 0.0) - jnp.log1p(jnp.exp(-jnp.abs(g))) for bi, g in gates.items()}
        ip = each(lambda p: gates[p[0]][:, p[1]:p[1] + 1], pairs)
        lf = each(lambda p: log_sig[p[0]][:, nh + p[1]:nh + p[1] + 1], pairs)
        if valid < cl:
            ip = each(lambda x: jnp.where(row_id < valid, x, -jnp.inf), ip)
            lf = each(lambda x: jnp.where(row_id < valid, x, 0.0), lf)
        lf_row = each(lambda x: _col_to_row(x, eye), lf)
        ip_row = each(lambda x: _col_to_row(x, eye), ip)
        b_col = each(lambda r: jnp.sum(jnp.where(tril, jnp.broadcast_to(r, (cl, cl)), 0.0),
                                       axis=1, keepdims=True), lf_row)
        b_row = each(lambda x: _col_to_row(x, eye), b_col)
        m_prev = each(lambda p: m_in[p[0], p[1]:p[1] + 1, 0:1], pairs)
        log_inter = each(lambda b, m: b + m, b_col, m_prev)
        log_intra = each(lambda bc, br, ir: jnp.where(tril, bc - br + ir, -jnp.inf), b_col, b_row, ip_row)
        m_t = each(lambda le, la: jnp.maximum(le, jnp.max(la, axis=1, keepdims=True)), log_inter, log_intra)
        w_inter = each(lambda le, m: jnp.exp(le - m), log_inter, m_t)
        intra = each(lambda la, m: jnp.exp(la - m), log_intra, m_t)
        m_new = each(lambda m: m[cl - 1:cl, :], m_t)
        b_last = each(lambda b: b[cl - 1:cl, :], b_col)
        decay = each(lambda bl, mp, mn: jnp.exp(bl + mp - mn), b_last, m_prev, m_new)
        w_s = each(lambda bl, b, i, mn: jnp.exp(bl - b + i - mn), b_last, b_col, ip, m_new)
        q = each(lambda p, sl: q_ref[p[0], :, sl], pairs, cols_of)
        k = each(lambda p, sl: k_ref[p[0], :, sl] * (dh ** -0.5), pairs, cols_of)
        vb = each(lambda p, sl: v_ref[p[0], :, sl].astype(BF16), pairs, cols_of)
        qb = each(lambda x: x.astype(BF16), q)
        kb = each(lambda x: x.astype(BF16), k)
        s = each(lambda a, b: lax.dot_general(a, b, nt_dims, preferred_element_type=F32), qb, kb)
        s = each(lambda x, d: x * d, s, intra)
        c_old = each(lambda p: c_in[p[0], p[1]], pairs)
        n_old = each(lambda p: n_in[p[0], p[1]:p[1] + 1, :], pairs)
        inter = each(lambda a, c: jnp.dot(a, c.astype(BF16), preferred_element_type=F32), qb, c_old)
        local = each(lambda x, v: jnp.dot(x.astype(BF16), v, preferred_element_type=F32), s, vb)
        num = each(lambda w, a, b: w * a + b, w_inter, inter, local)
        den = each(lambda w, x, n, ss: w * jnp.sum(x * n, axis=1, keepdims=True)
                   + jnp.sum(ss, axis=1, keepdims=True), w_inter, q, n_old, s)
        hh = each(lambda a, d, m: a / jnp.maximum(jnp.abs(d), jnp.exp(-m)), num, den, m_t)
        kw = each(lambda x, w: x * w, k, w_s)
        outer = each(lambda a, v: lax.dot_general(a.astype(BF16), v, tn_dims, preferred_element_type=F32),
                     kw, vb)
        for i, (bi, h) in enumerate(pairs):
            c_ref[bi, h] = decay[i] * c_old[i] + outer[i]
            n_ref[bi, h:h + 1, :] = decay[i] * n_old[i] + jnp.sum(kw[i], axis=0, keepdims=True)
            m_ref[bi, h:h + 1, :] = jnp.broadcast_to(m_new[i], (1, LANES))
        mu = each(lambda x: jnp.mean(x, axis=-1, keepdims=True), hh)
        hc = each(lambda x, m: x - m, hh, mu)
        var = each(lambda x: jnp.mean(x * x, axis=-1, keepdims=True), hc)
        for i, (bi, h) in enumerate(pairs):
            sl = cols_of[i]
            hn = hc[i] * lax.rsqrt(var[i] + LN_EPS) * gmh_ref[:, sl]
            h_ref[bi, :, sl] = hn * jax.nn.sigmoid(o_ref[bi, :, sl])
    if single_chunk:
        m_ref[:, MLSTM_HEADS:, :] = jnp.zeros((bb, SUBLANES - MLSTM_HEADS, LANES), F32)


def _mlstm_group(z_main, z_tail, c0, n0, m0, g_mh, batch, seq, cl, valid, bb, name):
    nc = seq // cl
    zcol = lambda off: pl.BlockSpec((bb, cl, MLSTM_WIDTH), lambda b, c: (b, c, off // MLSTM_WIDTH))
    state = lambda shape: pl.BlockSpec((bb,) + shape, lambda b, c: (b,) + (0,) * len(shape))
    c_shape = (MLSTM_HEADS, MLSTM_HEAD_DIM, MLSTM_HEAD_DIM)
    n_shape = (MLSTM_HEADS, MLSTM_HEAD_DIM)
    m_shape = (SUBLANES, LANES)
    return pl.pallas_call(
        functools.partial(_mlstm_kernel, cl=cl, valid=valid, bb=bb, single_chunk=nc == 1),
        out_shape=(jax.ShapeDtypeStruct((batch, seq, MLSTM_WIDTH), F32),
                   jax.ShapeDtypeStruct((batch,) + c_shape, F32),
                   jax.ShapeDtypeStruct((batch,) + n_shape, F32),
                   jax.ShapeDtypeStruct((batch,) + m_shape, F32)),
        grid=(batch // bb, nc),
        in_specs=[
            zcol(Z_Q), zcol(Z_K), zcol(Z_V), zcol(Z_O),
            pl.BlockSpec((bb, cl, LANES), lambda b, c: (b, c, ZT_GATE // LANES)),
            state(c_shape), state(n_shape), state(m_shape),
            pl.BlockSpec((1, MLSTM_WIDTH), lambda b, c: (0, 0)),
        ],
        out_specs=(pl.BlockSpec((bb, cl, MLSTM_WIDTH), lambda b, c: (b, c, 0)),
                   state(c_shape), state(n_shape), state(m_shape)),
        compiler_params=_params("arbitrary", "arbitrary"),
        name=name,
    )(z_main, z_main, z_main, z_main, z_tail, c0, n0, m0, g_mh)


def _softmax_rows(s):
    e = jnp.exp(s - jnp.max(s, axis=-1, keepdims=True))
    return e / jnp.sum(e, axis=-1, keepdims=True)


_NT_DIMS = (((1,), (1,)), ((), ()))


def _memattn_head_kernel(q_ref, k_ref, v_ref, o_ref):
    s = lax.dot_general(q_ref[...].astype(BF16), k_ref[...].astype(BF16), _NT_DIMS,
                        preferred_element_type=F32) * (MEM_HEAD_DIM ** -0.5)
    p = _softmax_rows(s)
    o_ref[...] = jnp.dot(p.astype(BF16), v_ref[...].astype(BF16), preferred_element_type=F32)


def _memattn_heads(z_tail, kv, batch, seq, tq, name):
    nq = seq // tq
    dh = MEM_HEAD_DIM
    return pl.pallas_call(
        _memattn_head_kernel,
        out_shape=jax.ShapeDtypeStruct((batch * seq, MEM_WIDTH), F32),
        grid=(batch, MEM_HEADS, nq),
        in_specs=[pl.BlockSpec((tq, dh), lambda b, h, i: (b * nq + i, ZT_QM // dh + h)),
                  pl.BlockSpec((N_MEM, dh), lambda b, h, i: (b, h)),
                  pl.BlockSpec((N_MEM, dh), lambda b, h, i: (b, MEM_HEADS + h))],
        out_specs=pl.BlockSpec((tq, dh), lambda b, h, i: (b * nq + i, h)),
        compiler_params=_params("arbitrary", "arbitrary", "arbitrary"),
        name=name,
    )(z_tail, kv, kv)


def _memattn_packed_kernel(q_ref, k_ref, v_ref, o_ref, *, tq, bb):
    nh, dh = MEM_HEADS, MEM_HEAD_DIM
    shape = (nh * tq, N_MEM * nh)
    row_head = lax.broadcasted_iota(jnp.int32, shape, 0) // tq
    col_head = lax.broadcasted_iota(jnp.int32, shape, 1) % nh
    same_head = row_head == col_head
    scores = []
    for bi in range(bb):
        rs = slice(bi * tq, (bi + 1) * tq)
        q = jnp.concatenate([q_ref[rs, h * dh:(h + 1) * dh] for h in range(nh)], axis=0)
        scores.append(lax.dot_general(q.astype(BF16), k_ref[bi].astype(BF16), _NT_DIMS,
                                      preferred_element_type=F32) * (dh ** -0.5))
    probs = [_softmax_rows(jnp.where(same_head, s, -jnp.inf)).astype(BF16) for s in scores]
    outs = [jnp.dot(p, v_ref[bi].astype(BF16), preferred_element_type=F32) for bi, p in enumerate(probs)]
    for bi, o in enumerate(outs):
        rs = slice(bi * tq, (bi + 1) * tq)
        for h in range(nh):
            o_ref[rs, h * dh:(h + 1) * dh] = o[h * tq:(h + 1) * tq, :]


def _memattn_packed(z_tail, mk, mv, batch, tq, bb, name):
    rows = bb * tq
    kv = pl.BlockSpec((bb, N_MEM * MEM_HEADS, MEM_HEAD_DIM), lambda b: (b, 0, 0))
    return pl.pallas_call(
        functools.partial(_memattn_packed_kernel, tq=tq, bb=bb),
        out_shape=jax.ShapeDtypeStruct((batch * tq, MEM_WIDTH), F32),
        grid=(batch // bb,),
        in_specs=[pl.BlockSpec((rows, MEM_WIDTH), lambda b: (b, ZT_QM // MEM_WIDTH)), kv, kv],
        out_specs=pl.BlockSpec((rows, MEM_WIDTH), lambda b: (b, 0)),
        compiler_params=_params("arbitrary"),
        name=name,
    )(z_tail, mk, mv)


def _route(logits):
    lane = lax.broadcasted_iota(jnp.int32, logits.shape, 1).astype(F32)
    neg = -jnp.inf
    first = lambda mask: jnp.min(jnp.where(mask, lane, float(LANES)), axis=1, keepdims=True)
    is_g = lane < N_GROUPS
    gl = jnp.where(is_g, logits, neg)
    g_max = jnp.max(gl, axis=1, keepdims=True)
    g_sel = first(gl == g_max)
    g_w = 1.0 / jnp.sum(jnp.exp(gl - g_max), axis=1, keepdims=True)
    lo = N_GROUPS + g_sel * EXPERTS_PER_GROUP
    in_grp = (lane >= lo) & (lane < lo + EXPERTS_PER_GROUP)
    el = jnp.where(in_grp, logits, neg)
    v1 = jnp.max(el, axis=1, keepdims=True)
    i1 = first(in_grp & (el == v1))
    rest = in_grp & (lane != i1)
    el2 = jnp.where(rest, logits, neg)
    v2 = jnp.max(el2, axis=1, keepdims=True)
    i2 = first(rest & (el2 == v2))
    t = jnp.exp(v2 - v1)
    w1 = g_w / (1.0 + t)
    w2 = g_w * t / (1.0 + t)
    lane_i = lax.broadcasted_iota(jnp.int32, logits.shape, 1)
    e1 = (i1 - N_GROUPS).astype(jnp.int32)
    e2 = (i2 - N_GROUPS).astype(jnp.int32)
    eid = jnp.where(lane_i == 0, e1, jnp.where(lane_i == 1, e2, 0))
    ew = jnp.where(lane_i == 0, w1, jnp.where(lane_i == 1, w2, 0.0))
    return eid, ew


def _outproj_kernel(cp, hp, mp, xp, cs, hs, ms, xs, wout, g1, b1, wr, br,
                    x1_ref, x1p_ref, eid_ref, ew_ref, *, n_prompt):
    def rows(c, h, m, x, rs):
        groups = jnp.concatenate(
            [c[rs, :].astype(BF16), h[rs, :].astype(BF16), m[rs, :].astype(BF16)], axis=1)
        mix = jnp.dot(groups, wout[...], preferred_element_type=F32)
        x1 = _layer_norm_rows(ALPHA * x[rs, :] + mix, g1[...], b1[...])
        x1_ref[rs, :] = x1
        xh = x1.astype(BF16)
        xh_f32 = xh.astype(F32)
        bits = pltpu.bitcast(xh_f32, jnp.int32)
        x1p_ref[rs, :] = bits[:, :D_MODEL // 2] | lax.shift_right_logical(bits[:, D_MODEL // 2:], 16)
        xl = (x1 - xh_f32).astype(BF16)
        n = x1.shape[0]
        cross = jnp.dot(jnp.concatenate([xh, xl], axis=0), wr[...], preferred_element_type=F32)
        logits = ((cross[:n, :ROUTER_LANES] + cross[:n, ROUTER_LANES:])
                  + (cross[n:, :ROUTER_LANES] + cross[n:, ROUTER_LANES:])) + br[...]
        eid, ew = _route(logits)
        eid_ref[rs, :] = eid
        ew_ref[rs, :] = ew

    def body(c, h, m, x):
        rows(c, h, m, x, slice(None))

    i = pl.program_id(0)
    pl.when(i < n_prompt)(lambda: body(cp, hp, mp, xp))
    pl.when(i >= n_prompt)(lambda: body(cs, hs, ms, xs))


def _outproj_router(prompt, sample, w_out_b, g1, b1, wr, br):
    tm = OUT_TM
    tp = prompt[0].shape[0]
    ts = sample[0].shape[0]
    n_p, n_s = tp // tm, ts // tm
    total = tp + ts
    widths = (CONV_WIDTH, MLSTM_WIDTH, MEM_WIDTH, D_MODEL)
    p_specs = [pl.BlockSpec((tm, w), lambda i: (jnp.minimum(i, n_p - 1), 0)) for w in widths]
    s_specs = [pl.BlockSpec((tm, w), lambda i: (jnp.maximum(i - n_p, 0), 0)) for w in widths]
    full = lambda shape: pl.BlockSpec(shape, lambda i: (0, 0))
    row = lambda w: pl.BlockSpec((tm, w), lambda i: (i, 0))
    return pl.pallas_call(
        functools.partial(_outproj_kernel, n_prompt=n_p),
        out_shape=(jax.ShapeDtypeStruct((total, D_MODEL), F32),
                   jax.ShapeDtypeStruct((total, D_MODEL // 2), jnp.int32),
                   jax.ShapeDtypeStruct((total, ROUTER_LANES), jnp.int32),
                   jax.ShapeDtypeStruct((total, ROUTER_LANES), F32)),
        grid=(n_p + n_s,),
        in_specs=p_specs + s_specs + [
            full((D_MODEL, D_MODEL)), full((1, D_MODEL)), full((1, D_MODEL)),
            full((D_MODEL, 2 * ROUTER_LANES)), full((1, ROUTER_LANES)),
        ],
        out_specs=(row(D_MODEL), row(D_MODEL // 2), row(ROUTER_LANES), row(ROUTER_LANES)),
        compiler_params=_params("arbitrary"),
        name="outproj_router",
    )(*prompt, *sample, w_out_b, g1, b1, wr, br)


def _moe_kernel(stok_ref, sdst_ref, base_ref, nval_ref, first_ref, count_ref,
                x_hbm, wg_ref, wu_ref, wd_ref, y_hbm, wgb, wub, wdb, xbuf, ybuf, gsem, ssem,
                *, dump_row):
    e = pl.program_id(0)
    last = pl.num_programs(0) - 1
    rc = MOE_ROWS
    ns = MOE_SLOTS

    def gather_start(base, slot):
        for r in range(rc):
            pltpu.make_async_copy(x_hbm.at[pl.ds(stok_ref[base + r], 1)],
                                  xbuf.at[slot, pl.ds(r, 1)], gsem.at[slot]).start(priority=ROW_DMA_PRIORITY)

    def gather_wait(slot):
        pltpu.make_async_copy(x_hbm.at[pl.ds(0, rc)], xbuf.at[slot], gsem.at[slot]).wait()

    def scatter_start(base, n_valid, slot):
        for r in range(rc):
            dst = jnp.where(r < n_valid, sdst_ref[base + r], dump_row + slot * rc + r)
            pltpu.make_async_copy(ybuf.at[slot, pl.ds(r, 1)], y_hbm.at[pl.ds(dst, 1)],
                                  ssem.at[slot]).start(priority=ROW_DMA_PRIORITY)

    def scatter_wait(slot):
        pltpu.make_async_copy(ybuf.at[slot], y_hbm.at[pl.ds(0, rc)], ssem.at[slot]).wait()

    @pl.when(e == 0)
    def _():
        ybuf[...] = jnp.zeros(ybuf.shape, F32)
        for slot in range(ns):
            pltpu.make_async_copy(ybuf.at[slot], y_hbm.at[pl.ds(dump_row + slot * rc, rc)],
                                  ssem.at[slot]).start()
            scatter_wait(slot)
        for ahead in range(ns - 1):
            gather_start(base_ref[1 + ahead], ahead)

    wgb[...] = wg_ref[0].astype(BF16)
    wub[...] = wu_ref[0].astype(BF16)
    wdb[...] = wd_ref[0].astype(BF16)
    first = first_ref[e]

    def chunk_on(slot, g):
        prv = (slot - 1) % ns
        gather_wait(slot)

        @pl.when(g >= ns - 1)
        def _():
            scatter_wait(slot)

        gather_start(base_ref[g + ns], prv)
        scatter_start(base_ref[g], nval_ref[g], prv)
        packed = xbuf[slot]
        x = jnp.concatenate(
            [pltpu.bitcast(packed & jnp.int32(-65536), F32).astype(BF16),
             pltpu.bitcast(packed << 16, F32).astype(BF16)], axis=1)
        hg = jnp.dot(x, wgb[...], preferred_element_type=F32)
        hu = jnp.dot(x, wub[...], preferred_element_type=F32)
        hid = (hg * jax.nn.sigmoid(hg) * hu).astype(BF16)
        ybuf[slot] = jnp.dot(hid, wdb[...], preferred_element_type=F32)

    def chunk(c, carry):
        g = first + c
        cur = lax.rem(g, ns)
        for slot in range(ns):
            pl.when(cur == slot)(functools.partial(chunk_on, slot, g))
        return carry

    lax.fori_loop(0, count_ref[e], chunk, 0)

    @pl.when(e == last)
    def _():
        g_end = first + count_ref[e]
        cur = lax.rem(g_end, ns)
        for slot in range(ns):
            @pl.when(cur == slot)
            def _():
                for ahead in range(ns - 1):
                    gather_wait((slot + ahead) % ns)
                scatter_start(base_ref[g_end], nval_ref[g_end], (slot - 1) % ns)

        for slot in range(ns):
            scatter_wait(slot)


def _moe(x1, tables, w_gate, w_up, w_down):
    total = x1.shape[0]
    dump_row = 2 * total
    wspec = lambda shape: pl.BlockSpec((1,) + shape, lambda e, *_: (e, 0, 0))
    grid_spec = pltpu.PrefetchScalarGridSpec(
        num_scalar_prefetch=len(tables),
        grid=(N_EXPERTS,),
        in_specs=[pl.BlockSpec(memory_space=pl.ANY),
                  wspec((D_MODEL, D_EXPERT)), wspec((D_MODEL, D_EXPERT)), wspec((D_EXPERT, D_MODEL))],
        out_specs=pl.BlockSpec(memory_space=pl.ANY),
        scratch_shapes=[
            pltpu.VMEM((D_MODEL, D_EXPERT), BF16), pltpu.VMEM((D_MODEL, D_EXPERT), BF16),
            pltpu.VMEM((D_EXPERT, D_MODEL), BF16),
            pltpu.VMEM((MOE_SLOTS, MOE_ROWS, D_MODEL // 2), jnp.int32),
            pltpu.VMEM((MOE_SLOTS, MOE_ROWS, D_MODEL), F32),
            pltpu.SemaphoreType.DMA((MOE_SLOTS,)), pltpu.SemaphoreType.DMA((MOE_SLOTS,)),
        ],
    )
    return pl.pallas_call(
        functools.partial(_moe_kernel, dump_row=dump_row),
        out_shape=jax.ShapeDtypeStruct((2 * total + MOE_SLOTS * MOE_ROWS, D_MODEL), F32),
        grid_spec=grid_spec,
        compiler_params=_params("arbitrary"),
        name="moe_experts",
    )(*tables, x1, w_gate, w_up, w_down)


def _final_kernel(x1_ref, y0_ref, y1_ref, ew_ref, g2, b2, o_ref):
    ew = ew_ref[...]
    ffn = ew[:, 0:1] * y0_ref[...] + ew[:, 1:2] * y1_ref[...]
    o_ref[...] = _layer_norm_rows(ALPHA * x1_ref[...] + ffn, g2[...], b2[...])


def _final(x1, y, ew, g2, b2, row0, rows, name):
    tm = FINAL_TM
    total = x1.shape[0]
    off = row0 // tm
    k1 = total // tm
    vec = pl.BlockSpec((1, D_MODEL), lambda i: (0, 0))
    return pl.pallas_call(
        _final_kernel,
        out_shape=jax.ShapeDtypeStruct((rows, D_MODEL), F32),
        grid=(rows // tm,),
        in_specs=[pl.BlockSpec((tm, D_MODEL), lambda i: (i + off, 0)),
                  pl.BlockSpec((tm, D_MODEL), lambda i: (i + off, 0)),
                  pl.BlockSpec((tm, D_MODEL), lambda i: (i + off + k1, 0)),
                  pl.BlockSpec((tm, ROUTER_LANES), lambda i: (i + off, 0)),
                  vec, vec],
        out_specs=pl.BlockSpec((tm, D_MODEL), lambda i: (i, 0)),
        compiler_params=_params("arbitrary"),
        name=name,
    )(x1, y, y, ew, g2, b2)


def _dispatch_tables(eid, total):
    flat_e = eid[:, :2].reshape(-1)
    order = jnp.argsort(flat_e, stable=True).astype(jnp.int32)
    stok = order >> 1
    sdst = (order & 1) * total + stok
    experts = jnp.arange(N_EXPERTS, dtype=jnp.int32)
    cnt = jnp.sum((flat_e[:, None] == experts[None, :]).astype(jnp.int32), axis=0)
    start = jnp.cumsum(cnt) - cnt
    pad = jnp.zeros((MOE_ROWS,), jnp.int32)
    n_chunks = jnp.maximum((cnt + MOE_ROWS - 1) // MOE_ROWS, 1)
    first = jnp.cumsum(n_chunks) - n_chunks
    n_entries = flat_e.shape[0] // MOE_ROWS + N_EXPERTS + MOE_SLOTS + 1
    g = jnp.arange(n_entries, dtype=jnp.int32) - 1
    owner = jnp.sum((g[:, None] >= (first + n_chunks)[None, :]).astype(jnp.int32), axis=1)
    real = (g >= 0) & (owner < N_EXPERTS)
    pick = (owner[:, None] == experts[None, :]).astype(jnp.int32)
    sel = lambda v: jnp.sum(pick * v[None, :], axis=1)
    local = (g - sel(first)) * MOE_ROWS
    base = jnp.where(real, sel(start) + local, 0)
    n_valid = jnp.where(real, sel(cnt) - local, 0)
    return (jnp.concatenate([stok, pad]), jnp.concatenate([sdst, pad]), base, n_valid, first, n_chunks)


def kernel(x_prompt, x_sample, mem_prompt, state_conv, state_mlstm_C, state_mlstm_n, state_mlstm_m,
           cache_mem_k, cache_mem_v, w_in, b_in, w_dw, b_dw, g_cn, b_cn, g_mh, w_mk, w_mv, w_out,
           g_ln1, b_ln1, w_rg, b_rg, w_re, b_re, w_gate, w_up, w_down, g_ln2, b_ln2):
    bp, sp, _ = x_prompt.shape
    bs, ss, _ = x_sample.shape
    tp, ts = bp * sp, bs * ss

    gate_hi = Z_MAIN_COLS + N_GATE_COLS
    tail = lambda wt: jnp.concatenate(
        [wt[gate_hi:], wt[Z_MAIN_COLS:gate_hi],
         jnp.zeros((LANES - N_GATE_COLS,) + wt.shape[1:], wt.dtype)], axis=0)
    w_in_t = jnp.transpose(w_in[0]).astype(BF16)
    w_tail_t = tail(w_in_t)
    b_tail = tail(b_in[0])[None, :]
    w_kv_b = jnp.concatenate([w_mk[0], w_mv[0]], axis=-1).astype(BF16)
    w_out_b = w_out[0].astype(BF16)
    w_r = jnp.concatenate([w_rg[0], w_re[0],
                           jnp.zeros((D_MODEL, ROUTER_LANES - N_GROUPS - N_EXPERTS), F32)], axis=-1)
    w_r_hi = w_r.astype(BF16)
    w_r_lo = (w_r - w_r_hi.astype(F32)).astype(BF16)
    w_r_split = jnp.concatenate([w_r_hi, w_r_lo], axis=1)
    b_r = jnp.concatenate([b_rg[0], b_re[0],
                           jnp.zeros((ROUTER_LANES - N_GROUPS - N_EXPERTS,), F32)])[None, :]
    row = lambda a: a[0][None, :]

    xs_pad = jnp.pad(x_sample, ((0, 0), (0, SAMPLE_PAD - ss), (0, 0))).reshape(bs * SAMPLE_PAD, D_MODEL)
    zm_p, zt_p = _inproj(x_prompt.reshape(tp, D_MODEL), w_in_t, b_in, w_tail_t, b_tail, "inproj_prompt")
    zm_s, zt_s = _inproj(xs_pad, w_in_t, b_in, w_tail_t, b_tail, "inproj_sample")

    kv = _matmul(mem_prompt.reshape(bp * N_MEM, D_MODEL), w_kv_b, KV_TN, "mem_kv")
    mk_p = kv[:, :MEM_WIDTH].reshape(bp, N_MEM, MEM_HEADS, MEM_HEAD_DIM)
    mv_p = kv[:, MEM_WIDTH:].reshape(bp, N_MEM, MEM_HEADS, MEM_HEAD_DIM)

    conv_args = (w_dw[0], row(b_dw), row(g_cn), row(b_cn))
    conv_p, buf_p = _conv_group(zm_p, jnp.zeros((bp, CONV_HIST, CONV_WIDTH), F32), *conv_args,
                                bp, sp, 512, "conv_prompt")
    conv_s, buf_s_t = _conv_step_group(
        zm_s.reshape(bs, SAMPLE_PAD, Z_MAIN_COLS), jnp.transpose(state_conv[0], (1, 0, 2)),
        *conv_args, ss, 32, "conv_sample")
    buf_s = jnp.transpose(buf_s_t, (1, 0, 2))

    m_tile = lambda m: jnp.broadcast_to(
        jnp.pad(m, ((0, 0), (0, SUBLANES - MLSTM_HEADS)))[:, :, None], (m.shape[0], SUBLANES, LANES))
    g_mh_r = row(g_mh)
    seqs = lambda z, b: z.reshape(b, z.shape[0] // b, z.shape[1])
    h_p, c_p, n_p, m_p = _mlstm_group(
        seqs(zm_p, bp), seqs(zt_p, bp),
        jnp.zeros((bp, MLSTM_HEADS, MLSTM_HEAD_DIM, MLSTM_HEAD_DIM), F32),
        jnp.zeros((bp, MLSTM_HEADS, MLSTM_HEAD_DIM), F32), jnp.zeros((bp, SUBLANES, LANES), F32),
        g_mh_r, bp, sp, MLSTM_CHUNK, MLSTM_CHUNK, 4, "mlstm_prompt")
    h_s, c_s, n_s, m_s = _mlstm_group(
        seqs(zm_s, bs), seqs(zt_s, bs), state_mlstm_C[0], state_mlstm_n[0], m_tile(state_mlstm_m[0]),
        g_mh_r, bs, SAMPLE_PAD, SAMPLE_PAD, ss, 8, "mlstm_sample")
    h_p = h_p.reshape(tp, MLSTM_WIDTH)

    mem_p = _memattn_heads(zt_p, kv, bp, sp, sp, "memattn_prompt")
    packed = lambda c: c[0].reshape(bs, N_MEM * MEM_HEADS, MEM_HEAD_DIM)
    mem_s = _memattn_packed(zt_s, packed(cache_mem_k), packed(cache_mem_v), bs, SAMPLE_PAD, 8,
                            "memattn_sample")

    compact = lambda a: a.reshape(bs, SAMPLE_PAD, a.shape[-1])[:, :ss].reshape(ts, a.shape[-1])
    x1, x1_packed, eid, ew = _outproj_router(
        (conv_p, h_p, mem_p, x_prompt.reshape(tp, D_MODEL)),
        (conv_s.reshape(ts, CONV_WIDTH), h_s[:, :ss].reshape(ts, MLSTM_WIDTH), compact(mem_s),
         x_sample.reshape(ts, D_MODEL)),
        w_out_b, row(g_ln1), row(b_ln1), w_r_split, b_r)

    total = tp + ts
    y = _moe(x1_packed, _dispatch_tables(eid, total), w_gate[0], w_up[0], w_down[0])
    g2, b2 = row(g_ln2), row(b_ln2)
    y_p = _final(x1, y, ew, g2, b2, 0, tp, "final_prompt").reshape(bp, sp, D_MODEL)
    y_s = _final(x1, y, ew, g2, b2, tp, ts, "final_sample").reshape(bs, ss, D_MODEL)

    return (y_p, y_s, buf_p[None], buf_s[None], c_p[None], c_s[None], n_p[None], n_s[None],
            m_p[:, :MLSTM_HEADS, 0][None], m_s[:, :MLSTM_HEADS, 0][None], mk_p[None], mv_p[None])
```

```python
import functools

import jax
import jax.numpy as jnp
from jax import lax
from jax.experimental import pallas as pl
from jax.experimental.pallas import tpu as pltpu

F32 = jnp.float32
BF16 = jnp.bfloat16

D_MODEL = 2048
CONV_WIDTH = 512
CONV_K = 31
CONV_HIST = CONV_K - 1
MLSTM_HEADS = 4
MLSTM_HEAD_DIM = 256
MLSTM_WIDTH = MLSTM_HEADS * MLSTM_HEAD_DIM
MLSTM_CHUNK = 128
MEM_HEADS = 4
MEM_HEAD_DIM = 128
MEM_WIDTH = MEM_HEADS * MEM_HEAD_DIM
N_MEM = 256
N_GROUPS = 8
EXPERTS_PER_GROUP = 8
N_EXPERTS = N_GROUPS * EXPERTS_PER_GROUP
D_EXPERT = 512
LN_EPS = 1e-5
DEPTH = 1
ALPHA = (2 * DEPTH) ** 0.25

LANES = 128
SUBLANES = 8
VMEM_LIMIT_BYTES = 56 * 1024 * 1024

Z_CONV_A = 0
Z_CONV_G = CONV_WIDTH
Z_Q = 2 * CONV_WIDTH
Z_K = Z_Q + MLSTM_WIDTH
Z_V = Z_K + MLSTM_WIDTH
Z_O = Z_V + MLSTM_WIDTH
Z_MAIN_COLS = Z_O + MLSTM_WIDTH
N_GATE_COLS = 2 * MLSTM_HEADS
ZT_QM = 0
ZT_GATE = MEM_WIDTH
Z_TAIL_COLS = MEM_WIDTH + LANES
INPROJ_TM = 1024
INPROJ_TN = 1024
KV_TN = 256

SAMPLE_PAD = SUBLANES
MLSTM_STAGE_PAIRS = 8
ROUTER_LANES = LANES
MOE_ROWS = 160
MOE_SLOTS = 4
assert N_EXPERTS >= MOE_SLOTS
OUT_TM = 256
FINAL_TM = 512


def _params(*sem):
    return pltpu.CompilerParams(dimension_semantics=sem, vmem_limit_bytes=VMEM_LIMIT_BYTES)


def _inproj_kernel(x_ref, wm_ref, bm_ref, wt_ref, bt_ref, zm_ref, zt_ref, xb_ref, *, n_main):
    j = pl.program_id(1)

    @pl.when(j == 0)
    def _():
        xb_ref[...] = x_ref[...].astype(BF16)

    nt = (((1,), (1,)), ((), ()))

    @pl.when(j < n_main)
    def _():
        zm_ref[...] = lax.dot_general(xb_ref[...], wm_ref[...], nt,
                                      preferred_element_type=F32) + bm_ref[...]

    @pl.when(j == n_main)
    def _():
        zt_ref[...] = lax.dot_general(xb_ref[...], wt_ref[...], nt,
                                      preferred_element_type=F32) + bt_ref[...]


def _inproj(x, w_in_t, b_in, w_tail_t, b_tail, name):
    t, k = x.shape
    tm, tn = INPROJ_TM, INPROJ_TN
    n_main = Z_MAIN_COLS // tn
    main_col = lambda j: jnp.minimum(j, n_main - 1)
    return pl.pallas_call(
        functools.partial(_inproj_kernel, n_main=n_main),
        out_shape=(jax.ShapeDtypeStruct((t, Z_MAIN_COLS), F32),
                   jax.ShapeDtypeStruct((t, Z_TAIL_COLS), F32)),
        grid=(t // tm, n_main + 1),
        in_specs=[
            pl.BlockSpec((tm, k), lambda i, j: (i, 0)),
            pl.BlockSpec((tn, k), lambda i, j: (main_col(j), 0)),
            pl.BlockSpec((1, tn), lambda i, j: (0, main_col(j))),
            pl.BlockSpec((Z_TAIL_COLS, k), lambda i, j: (0, 0)),
            pl.BlockSpec((1, Z_TAIL_COLS), lambda i, j: (0, 0)),
        ],
        out_specs=(pl.BlockSpec((tm, tn), lambda i, j: (i, main_col(j))),
                   pl.BlockSpec((tm, Z_TAIL_COLS), lambda i, j: (i, 0))),
        scratch_shapes=[pltpu.VMEM((tm, k), BF16)],
        compiler_params=_params("arbitrary", "arbitrary"),
        name=name,
    )(x, w_in_t, b_in, w_tail_t, b_tail)


def _matmul_kernel(x_ref, w_ref, o_ref):
    o_ref[...] = jnp.dot(x_ref[...].astype(BF16), w_ref[...], preferred_element_type=F32)


def _matmul(x, w_bf16, tn, name):
    t, k = x.shape
    n = w_bf16.shape[1]
    return pl.pallas_call(
        _matmul_kernel,
        out_shape=jax.ShapeDtypeStruct((t, n), F32),
        grid=(n // tn,),
        in_specs=[pl.BlockSpec((t, k), lambda j: (0, 0)), pl.BlockSpec((k, tn), lambda j: (0, j))],
        out_specs=pl.BlockSpec((t, tn), lambda j: (0, j)),
        compiler_params=_params("arbitrary"),
        name=name,
    )(x, w_bf16)


def _layer_norm_rows(y, g, b):
    mu = jnp.mean(y, axis=-1, keepdims=True)
    yc = y - mu
    var = jnp.mean(yc * yc, axis=-1, keepdims=True)
    return yc * lax.rsqrt(var + LN_EPS) * g + b


CONV_ROWS = 64


def _conv_kernel(a_ref, g_ref, hist_ref, wdw_ref, bdw_ref, gcn_ref, bcn_ref, out_ref, nb_ref,
                 ubuf, shifted, *, tl):
    head = CONV_HIST + 2
    li = pl.program_id(1)

    @pl.when(li == 0)
    def _():
        ubuf[0:2, :] = jnp.zeros((2, CONV_WIDTH), F32)
        ubuf[2:head, :] = hist_ref[0]

    ubuf[head:head + tl, :] = a_ref[...] * jax.nn.sigmoid(g_ref[...])
    span = shifted.shape[1]
    for k in range(1, SUBLANES):
        shifted[k - 1] = ubuf[k:k + span, :]
    for r0 in range(0, tl, CONV_ROWS):
        acc = jnp.zeros((CONV_ROWS, CONV_WIDTH), F32) + bdw_ref[...]
        for j in range(CONV_K):
            lo, k = divmod(2 + j, SUBLANES)
            lo = lo * SUBLANES + r0
            src = ubuf if k == 0 else shifted.at[k - 1]
            acc = acc + wdw_ref[j:j + 1, :] * src[lo:lo + CONV_ROWS, :]
        y = _layer_norm_rows(acc, gcn_ref[...], bcn_ref[...])
        out_ref[r0:r0 + CONV_ROWS, :] = y * jax.nn.sigmoid(y)

    @pl.when(li == pl.num_programs(1) - 1)
    def _():
        nb_ref[0] = ubuf[2 + tl:head + tl, :]

    ubuf[2:head, :] = ubuf[2 + tl:head + tl, :]


def _conv_group(z, hist, w_dw, b_dw, g_cn, b_cn, batch, seq, tl, name):
    nl = seq // tl
    row = lambda b, l: (b * nl + l, 0)
    vec = pl.BlockSpec((1, CONV_WIDTH), lambda b, l: (0, 0))
    return pl.pallas_call(
        functools.partial(_conv_kernel, tl=tl),
        out_shape=(jax.ShapeDtypeStruct((batch * seq, CONV_WIDTH), F32),
                   jax.ShapeDtypeStruct((batch, CONV_HIST, CONV_WIDTH), F32)),
        grid=(batch, nl),
        in_specs=[
            pl.BlockSpec((tl, CONV_WIDTH), lambda b, l: (b * nl + l, Z_CONV_A // CONV_WIDTH)),
            pl.BlockSpec((tl, CONV_WIDTH), lambda b, l: (b * nl + l, Z_CONV_G // CONV_WIDTH)),
            pl.BlockSpec((1, CONV_HIST, CONV_WIDTH), lambda b, l: (b, 0, 0)),
            pl.BlockSpec((CONV_K, CONV_WIDTH), lambda b, l: (0, 0)),
            vec, vec, vec,
        ],
        out_specs=(pl.BlockSpec((tl, CONV_WIDTH), row),
                   pl.BlockSpec((1, CONV_HIST, CONV_WIDTH), lambda b, l: (b, 0, 0))),
        scratch_shapes=[pltpu.VMEM((CONV_HIST + 2 + tl, CONV_WIDTH), F32),
                        pltpu.VMEM((SUBLANES - 1, CONV_HIST + 2 + tl - SUBLANES, CONV_WIDTH), F32)],
        compiler_params=_params("arbitrary", "arbitrary"),
        name=name,
    )(z, z, hist, w_dw, b_dw, g_cn, b_cn)


def _conv_step_kernel(a_ref, g_ref, hist_ref, wdw_ref, bdw_ref, gcn_ref, bcn_ref, out_ref, nb_ref,
                      *, steps):
    u = [a_ref[:, t, :] * jax.nn.sigmoid(g_ref[:, t, :]) for t in range(steps)]
    full = lambda r: hist_ref[r] if r < CONV_HIST else u[r - CONV_HIST]
    for t in range(steps):
        acc = bdw_ref[...] + wdw_ref[0:1, :] * full(t)
        for j in range(1, CONV_K):
            acc = acc + wdw_ref[j:j + 1, :] * full(t + j)
        y = _layer_norm_rows(acc, gcn_ref[...], bcn_ref[...])
        out_ref[:, t, :] = y * jax.nn.sigmoid(y)
    for r in range(CONV_HIST):
        nb_ref[r] = full(r + steps)


def _conv_step_group(z3, hist_t, w_dw, b_dw, g_cn, b_cn, steps, bb, name):
    batch = z3.shape[0]
    vec = pl.BlockSpec((1, CONV_WIDTH), lambda b: (0, 0))
    hist_spec = pl.BlockSpec((CONV_HIST, bb, CONV_WIDTH), lambda b: (0, b, 0))
    return pl.pallas_call(
        functools.partial(_conv_step_kernel, steps=steps),
        out_shape=(jax.ShapeDtypeStruct((batch, steps, CONV_WIDTH), F32),
                   jax.ShapeDtypeStruct((CONV_HIST, batch, CONV_WIDTH), F32)),
        grid=(batch // bb,),
        in_specs=[
            pl.BlockSpec((bb, SAMPLE_PAD, CONV_WIDTH), lambda b: (b, 0, Z_CONV_A // CONV_WIDTH)),
            pl.BlockSpec((bb, SAMPLE_PAD, CONV_WIDTH), lambda b: (b, 0, Z_CONV_G // CONV_WIDTH)),
            hist_spec,
            pl.BlockSpec((CONV_K, CONV_WIDTH), lambda b: (0, 0)),
            vec, vec, vec,
        ],
        out_specs=(pl.BlockSpec((bb, steps, CONV_WIDTH), lambda b: (b, 0, 0)), hist_spec),
        compiler_params=_params("arbitrary"),
        name=name,
    )(z3, z3, hist_t, w_dw, b_dw, g_cn, b_cn)


def _col_to_row(col, eye):
    n = col.shape[0]
    return jnp.sum(jnp.where(eye, jnp.broadcast_to(col, (n, n)), 0.0), axis=0, keepdims=True)


def _mlstm_kernel(q_ref, k_ref, v_ref, o_ref, gate_ref, c0_ref, n0_ref, m0_ref, gmh_ref,
                  h_ref, c_ref, n_ref, m_ref, *, cl, valid, bb, single_chunk):
    if single_chunk:
        c_in, n_in, m_in = c0_ref, n0_ref, m0_ref
    else:
        c_in, n_in, m_in = c_ref, n_ref, m_ref

        @pl.when(pl.program_id(1) == 0)
        def _():
            c_ref[...] = c0_ref[...]
            n_ref[...] = n0_ref[...]
            m_ref[...] = m0_ref[...]

    rows = lax.broadcasted_iota(jnp.int32, (cl, cl), 0)
    cols = lax.broadcasted_iota(jnp.int32, (cl, cl), 1)
    eye = rows == cols
    tril = rows >= cols
    row_id = lax.broadcasted_iota(jnp.int32, (cl, 1), 0)
    nh, dh = MLSTM_HEADS, MLSTM_HEAD_DIM
    nt_dims = (((1,), (1,)), ((), ()))
    tn_dims = (((0,), (0,)), ((), ()))
    each = lambda f, *lists: [f(*args) for args in zip(*lists)]
    all_pairs = [(bi, h) for bi in range(bb) for h in range(nh)]
    for g0 in range(0, len(all_pairs), MLSTM_STAGE_PAIRS):
        pairs = all_pairs[g0:g0 + MLSTM_STAGE_PAIRS]
        cols_of = [slice(h * dh, (h + 1) * dh) for _, h in pairs]
        gates = {bi: gate_ref[bi] for bi in sorted({bi for bi, _ in pairs})}
        log_sig = {bi: jnp.minimum(g, 0.0) - jnp.log1p(jnp.exp(-jnp.abs(g))) for bi, g in gates.items()}
        ip = each(lambda p: gates[p[0]][:, p[1]:p[1] + 1], pairs)
        lf = each(lambda p: log_sig[p[0]][:, nh + p[1]:nh + p[1] + 1], pairs)
        if valid < cl:
            ip = each(lambda x: jnp.where(row_id < valid, x, -jnp.inf), ip)
            lf = each(lambda x: jnp.where(row_id < valid, x, 0.0), lf)
        lf_row = each(lambda x: _col_to_row(x, eye), lf)
        ip_row = each(lambda x: _col_to_row(x, eye), ip)
        b_col = each(lambda r: jnp.sum(jnp.where(tril, jnp.broadcast_to(r, (cl, cl)), 0.0),
                                       axis=1, keepdims=True), lf_row)
        b_row = each(lambda x: _col_to_row(x, eye), b_col)
        m_prev = each(lambda p: m_in[p[0], p[1]:p[1] + 1, 0:1], pairs)
        log_inter = each(lambda b, m: b + m, b_col, m_prev)
        log_intra = each(lambda bc, br, ir: jnp.where(tril, bc - br + ir, -jnp.inf), b_col, b_row, ip_row)
        m_t = each(lambda le, la: jnp.maximum(le, jnp.max(la, axis=1, keepdims=True)), log_inter, log_intra)
        w_inter = each(lambda le, m: jnp.exp(le - m), log_inter, m_t)
        intra = each(lambda la, m: jnp.exp(la - m), log_intra, m_t)
        m_new = each(lambda m: m[cl - 1:cl, :], m_t)
        b_last = each(lambda b: b[cl - 1:cl, :], b_col)
        decay = each(lambda bl, mp, mn: jnp.exp(bl + mp - mn), b_last, m_prev, m_new)
        w_s = each(lambda bl, b, i, mn: jnp.exp(bl - b + i - mn), b_last, b_col, ip, m_new)
        q = each(lambda p, sl: q_ref[p[0], :, sl], pairs, cols_of)
        k = each(lambda p, sl: k_ref[p[0], :, sl] * (dh ** -0.5), pairs, cols_of)
        vb = each(lambda p, sl: v_ref[p[0], :, sl].astype(BF16), pairs, cols_of)
        qb = each(lambda x: x.astype(BF16), q)
        kb = each(lambda x: x.astype(BF16), k)
        s = each(lambda a, b: lax.dot_general(a, b, nt_dims, preferred_element_type=F32), qb, kb)
        s = each(lambda x, d: x * d, s, intra)
        c_old = each(lambda p: c_in[p[0], p[1]], pairs)
        n_old = each(lambda p: n_in[p[0], p[1]:p[1] + 1, :], pairs)
        inter = each(lambda a, c: jnp.dot(a, c.astype(BF16), preferred_element_type=F32), qb, c_old)
        local = each(lambda x, v: jnp.dot(x.astype(BF16), v, preferred_element_type=F32), s, vb)
        num = each(lambda w, a, b: w * a + b, w_inter, inter, local)
        den = each(lambda w, x, n, ss: w * jnp.sum(x * n, axis=1, keepdims=True)
                   + jnp.sum(ss, axis=1, keepdims=True), w_inter, q, n_old, s)
        hh = each(lambda a, d, m: a / jnp.maximum(jnp.abs(d), jnp.exp(-m)), num, den, m_t)
        kw = each(lambda x, w: x * w, k, w_s)
        outer = each(lambda a, v: lax.dot_general(a.astype(BF16), v, tn_dims, preferred_element_type=F32),
                     kw, vb)
        for i, (bi, h) in enumerate(pairs):
            c_ref[bi, h] = decay[i] * c_old[i] + outer[i]
            n_ref[bi, h:h + 1, :] = decay[i] * n_old[i] + jnp.sum(kw[i], axis=0, keepdims=True)
            m_ref[bi, h:h + 1, :] = jnp.broadcast_to(m_new[i], (1, LANES))
        mu = each(lambda x: jnp.mean(x, axis=-1, keepdims=True), hh)
        hc = each(lambda x, m: x - m, hh, mu)
        var = each(lambda x: jnp.mean(x * x, axis=-1, keepdims=True), hc)
        for i, (bi, h) in enumerate(pairs):
            sl = cols_of[i]
            hn = hc[i] * lax.rsqrt(var[i] + LN_EPS) * gmh_ref[:, sl]
            h_ref[bi, :, sl] = hn * jax.nn.sigmoid(o_ref[bi, :, sl])
    if single_chunk:
        m_ref[:, MLSTM_HEADS:, :] = jnp.zeros((bb, SUBLANES - MLSTM_HEADS, LANES), F32)


def _mlstm_group(z_main, z_tail, c0, n0, m0, g_mh, batch, seq, cl, valid, bb, name):
    nc = seq // cl
    zcol = lambda off: pl.BlockSpec((bb, cl, MLSTM_WIDTH), lambda b, c: (b, c, off // MLSTM_WIDTH))
    state = lambda shape: pl.BlockSpec((bb,) + shape, lambda b, c: (b,) + (0,) * len(shape))
    c_shape = (MLSTM_HEADS, MLSTM_HEAD_DIM, MLSTM_HEAD_DIM)
    n_shape = (MLSTM_HEADS, MLSTM_HEAD_DIM)
    m_shape = (SUBLANES, LANES)
    return pl.pallas_call(
        functools.partial(_mlstm_kernel, cl=cl, valid=valid, bb=bb, single_chunk=nc == 1),
        out_shape=(jax.ShapeDtypeStruct((batch, seq, MLSTM_WIDTH), F32),
                   jax.ShapeDtypeStruct((batch,) + c_shape, F32),
                   jax.ShapeDtypeStruct((batch,) + n_shape, F32),
                   jax.ShapeDtypeStruct((batch,) + m_shape, F32)),
        grid=(batch // bb, nc),
        in_specs=[
            zcol(Z_Q), zcol(Z_K), zcol(Z_V), zcol(Z_O),
            pl.BlockSpec((bb, cl, LANES), lambda b, c: (b, c, ZT_GATE // LANES)),
            state(c_shape), state(n_shape), state(m_shape),
            pl.BlockSpec((1, MLSTM_WIDTH), lambda b, c: (0, 0)),
        ],
        out_specs=(pl.BlockSpec((bb, cl, MLSTM_WIDTH), lambda b, c: (b, c, 0)),
                   state(c_shape), state(n_shape), state(m_shape)),
        compiler_params=_params("arbitrary", "arbitrary"),
        name=name,
    )(z_main, z_main, z_main, z_main, z_tail, c0, n0, m0, g_mh)


def _softmax_rows(s):
    e = jnp.exp(s - jnp.max(s, axis=-1, keepdims=True))
    return e / jnp.sum(e, axis=-1, keepdims=True)


_NT_DIMS = (((1,), (1,)), ((), ()))


def _memattn_head_kernel(q_ref, k_ref, v_ref, o_ref):
    s = lax.dot_general(q_ref[...].astype(BF16), k_ref[...].astype(BF16), _NT_DIMS,
                        preferred_element_type=F32) * (MEM_HEAD_DIM ** -0.5)
    p = _softmax_rows(s)
    o_ref[...] = jnp.dot(p.astype(BF16), v_ref[...].astype(BF16), preferred_element_type=F32)


def _memattn_heads(z_tail, kv, batch, seq, tq, name):
    nq = seq // tq
    dh = MEM_HEAD_DIM
    return pl.pallas_call(
        _memattn_head_kernel,
        out_shape=jax.ShapeDtypeStruct((batch * seq, MEM_WIDTH), F32),
        grid=(batch, MEM_HEADS, nq),
        in_specs=[pl.BlockSpec((tq, dh), lambda b, h, i: (b * nq + i, ZT_QM // dh + h)),
                  pl.BlockSpec((N_MEM, dh), lambda b, h, i: (b, h)),
                  pl.BlockSpec((N_MEM, dh), lambda b, h, i: (b, MEM_HEADS + h))],
        out_specs=pl.BlockSpec((tq, dh), lambda b, h, i: (b * nq + i, h)),
        compiler_params=_params("arbitrary", "arbitrary", "arbitrary"),
        name=name,
    )(z_tail, kv, kv)


def _memattn_packed_kernel(q_ref, k_ref, v_ref, o_ref, *, tq, bb):
    nh, dh = MEM_HEADS, MEM_HEAD_DIM
    shape = (nh * tq, N_MEM * nh)
    row_head = lax.broadcasted_iota(jnp.int32, shape, 0) // tq
    col_head = lax.broadcasted_iota(jnp.int32, shape, 1) % nh
    same_head = row_head == col_head
    scores = []
    for bi in range(bb):
        rs = slice(bi * tq, (bi + 1) * tq)
        q = jnp.concatenate([q_ref[rs, h * dh:(h + 1) * dh] for h in range(nh)], axis=0)
        scores.append(lax.dot_general(q.astype(BF16), k_ref[bi].astype(BF16), _NT_DIMS,
                                      preferred_element_type=F32) * (dh ** -0.5))
    probs = [_softmax_rows(jnp.where(same_head, s, -jnp.inf)).astype(BF16) for s in scores]
    outs = [jnp.dot(p, v_ref[bi].astype(BF16), preferred_element_type=F32) for bi, p in enumerate(probs)]
    for bi, o in enumerate(outs):
        rs = slice(bi * tq, (bi + 1) * tq)
        for h in range(nh):
            o_ref[rs, h * dh:(h + 1) * dh] = o[h * tq:(h + 1) * tq, :]


def _memattn_packed(z_tail, mk, mv, batch, tq, bb, name):
    rows = bb * tq
    kv = pl.BlockSpec((bb, N_MEM * MEM_HEADS, MEM_HEAD_DIM), lambda b: (b, 0, 0))
    return pl.pallas_call(
        functools.partial(_memattn_packed_kernel, tq=tq, bb=bb),
        out_shape=jax.ShapeDtypeStruct((batch * tq, MEM_WIDTH), F32),
        grid=(batch // bb,),
        in_specs=[pl.BlockSpec((rows, MEM_WIDTH), lambda b: (b, ZT_QM // MEM_WIDTH)), kv, kv],
        out_specs=pl.BlockSpec((rows, MEM_WIDTH), lambda b: (b, 0)),
        compiler_params=_params("arbitrary"),
        name=name,
    )(z_tail, mk, mv)


def _route(logits):
    lane = lax.broadcasted_iota(jnp.int32, logits.shape, 1).astype(F32)
    neg = -jnp.inf
    first = lambda mask: jnp.min(jnp.where(mask, lane, float(LANES)), axis=1, keepdims=True)
    is_g = lane < N_GROUPS
    gl = jnp.where(is_g, logits, neg)
    g_max = jnp.max(gl, axis=1, keepdims=True)
    g_sel = first(gl == g_max)
    g_w = 1.0 / jnp.sum(jnp.exp(gl - g_max), axis=1, keepdims=True)
    lo = N_GROUPS + g_sel * EXPERTS_PER_GROUP
    in_grp = (lane >= lo) & (lane < lo + EXPERTS_PER_GROUP)
    el = jnp.where(in_grp, logits, neg)
    v1 = jnp.max(el, axis=1, keepdims=True)
    i1 = first(in_grp & (el == v1))
    rest = in_grp & (lane != i1)
    el2 = jnp.where(rest, logits, neg)
    v2 = jnp.max(el2, axis=1, keepdims=True)
    i2 = first(rest & (el2 == v2))
    t = jnp.exp(v2 - v1)
    w1 = g_w / (1.0 + t)
    w2 = g_w * t / (1.0 + t)
    lane_i = lax.broadcasted_iota(jnp.int32, logits.shape, 1)
    e1 = (i1 - N_GROUPS).astype(jnp.int32)
    e2 = (i2 - N_GROUPS).astype(jnp.int32)
    eid = jnp.where(lane_i == 0, e1, jnp.where(lane_i == 1, e2, 0))
    ew = jnp.where(lane_i == 0, w1, jnp.where(lane_i == 1, w2, 0.0))
    return eid, ew


def _outproj_kernel(cp, hp, mp, xp, cs, hs, ms, xs, wout, g1, b1, wr, br,
                    x1_ref, x1p_ref, eid_ref, ew_ref, *, n_prompt):
    def rows(c, h, m, x, rs):
        groups = jnp.concatenate(
            [c[rs, :].astype(BF16), h[rs, :].astype(BF16), m[rs, :].astype(BF16)], axis=1)
        mix = jnp.dot(groups, wout[...], preferred_element_type=F32)
        x1 = _layer_norm_rows(ALPHA * x[rs, :] + mix, g1[...], b1[...])
        x1_ref[rs, :] = x1
        xh = x1.astype(BF16)
        xh_f32 = xh.astype(F32)
        bits = pltpu.bitcast(xh_f32, jnp.int32)
        x1p_ref[rs, :] = bits[:, :D_MODEL // 2] | lax.shift_right_logical(bits[:, D_MODEL // 2:], 16)
        xl = (x1 - xh_f32).astype(BF16)
        n = x1.shape[0]
        cross = jnp.dot(jnp.concatenate([xh, xl], axis=0), wr[...], preferred_element_type=F32)
        logits = ((cross[:n, :ROUTER_LANES] + cross[:n, ROUTER_LANES:])
                  + (cross[n:, :ROUTER_LANES] + cross[n:, ROUTER_LANES:])) + br[...]
        eid, ew = _route(logits)
        eid_ref[rs, :] = eid
        ew_ref[rs, :] = ew

    def body(c, h, m, x):
        rows(c, h, m, x, slice(None))

    i = pl.program_id(0)
    pl.when(i < n_prompt)(lambda: body(cp, hp, mp, xp))
    pl.when(i >= n_prompt)(lambda: body(cs, hs, ms, xs))


def _outproj_router(prompt, sample, w_out_b, g1, b1, wr, br):
    tm = OUT_TM
    tp = prompt[0].shape[0]
    ts = sample[0].shape[0]
    n_p, n_s = tp // tm, ts // tm
    total = tp + ts
    widths = (CONV_WIDTH, MLSTM_WIDTH, MEM_WIDTH, D_MODEL)
    p_specs = [pl.BlockSpec((tm, w), lambda i: (jnp.minimum(i, n_p - 1), 0)) for w in widths]
    s_specs = [pl.BlockSpec((tm, w), lambda i: (jnp.maximum(i - n_p, 0), 0)) for w in widths]
    full = lambda shape: pl.BlockSpec(shape, lambda i: (0, 0))
    row = lambda w: pl.BlockSpec((tm, w), lambda i: (i, 0))
    return pl.pallas_call(
        functools.partial(_outproj_kernel, n_prompt=n_p),
        out_shape=(jax.ShapeDtypeStruct((total, D_MODEL), F32),
                   jax.ShapeDtypeStruct((total, D_MODEL // 2), jnp.int32),
                   jax.ShapeDtypeStruct((total, ROUTER_LANES), jnp.int32),
                   jax.ShapeDtypeStruct((total, ROUTER_LANES), F32)),
        grid=(n_p + n_s,),
        in_specs=p_specs + s_specs + [
            full((D_MODEL, D_MODEL)), full((1, D_MODEL)), full((1, D_MODEL)),
            full((D_MODEL, 2 * ROUTER_LANES)), full((1, ROUTER_LANES)),
        ],
        out_specs=(row(D_MODEL), row(D_MODEL // 2), row(ROUTER_LANES), row(ROUTER_LANES)),
        compiler_params=_params("arbitrary"),
        name="outproj_router",
    )(*prompt, *sample, w_out_b, g1, b1, wr, br)


def _moe_kernel(stok_ref, sdst_ref, base_ref, nval_ref, first_ref, count_ref,
                x_hbm, wg_ref, wu_ref, wd_ref, y_hbm, wgb, wub, wdb, xbuf, ybuf, gsem, ssem,
                *, dump_row):
    e = pl.program_id(0)
    last = pl.num_programs(0) - 1
    rc = MOE_ROWS
    ns = MOE_SLOTS

    def gather_start(base, slot):
        for r in range(rc):
            pltpu.make_async_copy(x_hbm.at[pl.ds(stok_ref[base + r], 1)],
                                  xbuf.at[slot, pl.ds(r, 1)], gsem.at[slot]).start()

    def gather_wait(slot):
        pltpu.make_async_copy(x_hbm.at[pl.ds(0, rc)], xbuf.at[slot], gsem.at[slot]).wait()

    def scatter_start(base, n_valid, slot):
        for r in range(rc):
            dst = jnp.where(r < n_valid, sdst_ref[base + r], dump_row + slot * rc + r)
            pltpu.make_async_copy(ybuf.at[slot, pl.ds(r, 1)], y_hbm.at[pl.ds(dst, 1)],
                                  ssem.at[slot]).start()

    def scatter_wait(slot):
        pltpu.make_async_copy(ybuf.at[slot], y_hbm.at[pl.ds(0, rc)], ssem.at[slot]).wait()

    @pl.when(e == 0)
    def _():
        ybuf[...] = jnp.zeros(ybuf.shape, F32)
        for slot in range(ns):
            pltpu.make_async_copy(ybuf.at[slot], y_hbm.at[pl.ds(dump_row + slot * rc, rc)],
                                  ssem.at[slot]).start()
            scatter_wait(slot)
        for ahead in range(ns - 1):
            gather_start(base_ref[1 + ahead], ahead)

    wgb[...] = wg_ref[0].astype(BF16)
    wub[...] = wu_ref[0].astype(BF16)
    wdb[...] = wd_ref[0].astype(BF16)
    first = first_ref[e]

    def chunk_on(slot, g):
        prv = (slot - 1) % ns
        gather_wait(slot)

        @pl.when(g >= ns - 1)
        def _():
            scatter_wait(slot)

        gather_start(base_ref[g + ns], prv)
        scatter_start(base_ref[g], nval_ref[g], prv)
        packed = xbuf[slot]
        x = jnp.concatenate(
            [pltpu.bitcast(packed & jnp.int32(-65536), F32).astype(BF16),
             pltpu.bitcast(packed << 16, F32).astype(BF16)], axis=1)
        hg = jnp.dot(x, wgb[...], preferred_element_type=F32)
        hu = jnp.dot(x, wub[...], preferred_element_type=F32)
        hid = (hg * jax.nn.sigmoid(hg) * hu).astype(BF16)
        ybuf[slot] = jnp.dot(hid, wdb[...], preferred_element_type=F32)

    def chunk(c, carry):
        g = first + c
        cur = lax.rem(g, ns)
        for slot in range(ns):
            pl.when(cur == slot)(functools.partial(chunk_on, slot, g))
        return carry

    lax.fori_loop(0, count_ref[e], chunk, 0)

    @pl.when(e == last)
    def _():
        g_end = first + count_ref[e]
        cur = lax.rem(g_end, ns)
        for slot in range(ns):
            @pl.when(cur == slot)
            def _():
                for ahead in range(ns - 1):
                    gather_wait((slot + ahead) % ns)
                scatter_start(base_ref[g_end], nval_ref[g_end], (slot - 1) % ns)

        for slot in range(ns):
            scatter_wait(slot)


def _moe(x1, tables, w_gate, w_up, w_down):
    total = x1.shape[0]
    dump_row = 2 * total
    wspec = lambda shape: pl.BlockSpec((1,) + shape, lambda e, *_: (e, 0, 0))
    grid_spec = pltpu.PrefetchScalarGridSpec(
        num_scalar_prefetch=len(tables),
        grid=(N_EXPERTS,),
        in_specs=[pl.BlockSpec(memory_space=pl.ANY),
                  wspec((D_MODEL, D_EXPERT)), wspec((D_MODEL, D_EXPERT)), wspec((D_EXPERT, D_MODEL))],
        out_specs=pl.BlockSpec(memory_space=pl.ANY),
        scratch_shapes=[
            pltpu.VMEM((D_MODEL, D_EXPERT), BF16), pltpu.VMEM((D_MODEL, D_EXPERT), BF16),
            pltpu.VMEM((D_EXPERT, D_MODEL), BF16),
            pltpu.VMEM((MOE_SLOTS, MOE_ROWS, D_MODEL // 2), jnp.int32),
            pltpu.VMEM((MOE_SLOTS, MOE_ROWS, D_MODEL), F32),
            pltpu.SemaphoreType.DMA((MOE_SLOTS,)), pltpu.SemaphoreType.DMA((MOE_SLOTS,)),
        ],
    )
    return pl.pallas_call(
        functools.partial(_moe_kernel, dump_row=dump_row),
        out_shape=jax.ShapeDtypeStruct((2 * total + MOE_SLOTS * MOE_ROWS, D_MODEL), F32),
        grid_spec=grid_spec,
        compiler_params=_params("arbitrary"),
        name="moe_experts",
    )(*tables, x1, w_gate, w_up, w_down)


def _final_kernel(x1_ref, y0_ref, y1_ref, ew_ref, g2, b2, o_ref):
    ew = ew_ref[...]
    ffn = ew[:, 0:1] * y0_ref[...] + ew[:, 1:2] * y1_ref[...]
    o_ref[...] = _layer_norm_rows(ALPHA * x1_ref[...] + ffn, g2[...], b2[...])


def _final(x1, y, ew, g2, b2, row0, rows, name):
    tm = FINAL_TM
    total = x1.shape[0]
    off = row0 // tm
    k1 = total // tm
    vec = pl.BlockSpec((1, D_MODEL), lambda i: (0, 0))
    return pl.pallas_call(
        _final_kernel,
        out_shape=jax.ShapeDtypeStruct((rows, D_MODEL), F32),
        grid=(rows // tm,),
        in_specs=[pl.BlockSpec((tm, D_MODEL), lambda i: (i + off, 0)),
                  pl.BlockSpec((tm, D_MODEL), lambda i: (i + off, 0)),
                  pl.BlockSpec((tm, D_MODEL), lambda i: (i + off + k1, 0)),
                  pl.BlockSpec((tm, ROUTER_LANES), lambda i: (i + off, 0)),
                  vec, vec],
        out_specs=pl.BlockSpec((tm, D_MODEL), lambda i: (i, 0)),
        compiler_params=_params("arbitrary"),
        name=name,
    )(x1, y, y, ew, g2, b2)


def _dispatch_tables(eid, total):
    flat_e = eid[:, :2].reshape(-1)
    order = jnp.argsort(flat_e, stable=True).astype(jnp.int32)
    stok = order >> 1
    sdst = (order & 1) * total + stok
    experts = jnp.arange(N_EXPERTS, dtype=jnp.int32)
    cnt = jnp.sum((flat_e[:, None] == experts[None, :]).astype(jnp.int32), axis=0)
    start = jnp.cumsum(cnt) - cnt
    pad = jnp.zeros((MOE_ROWS,), jnp.int32)
    n_chunks = jnp.maximum((cnt + MOE_ROWS - 1) // MOE_ROWS, 1)
    first = jnp.cumsum(n_chunks) - n_chunks
    n_entries = flat_e.shape[0] // MOE_ROWS + N_EXPERTS + MOE_SLOTS + 1
    g = jnp.arange(n_entries, dtype=jnp.int32) - 1
    owner = jnp.sum((g[:, None] >= (first + n_chunks)[None, :]).astype(jnp.int32), axis=1)
    real = (g >= 0) & (owner < N_EXPERTS)
    pick = (owner[:, None] == experts[None, :]).astype(jnp.int32)
    sel = lambda v: jnp.sum(pick * v[None, :], axis=1)
    local = (g - sel(first)) * MOE_ROWS
    base = jnp.where(real, sel(start) + local, 0)
    n_valid = jnp.where(real, sel(cnt) - local, 0)
    return (jnp.concatenate([stok, pad]), jnp.concatenate([sdst, pad]), base, n_valid, first, n_chunks)


def kernel(x_prompt, x_sample, mem_prompt, state_conv, state_mlstm_C, state_mlstm_n, state_mlstm_m,
           cache_mem_k, cache_mem_v, w_in, b_in, w_dw, b_dw, g_cn, b_cn, g_mh, w_mk, w_mv, w_out,
           g_ln1, b_ln1, w_rg, b_rg, w_re, b_re, w_gate, w_up, w_down, g_ln2, b_ln2):
    bp, sp, _ = x_prompt.shape
    bs, ss, _ = x_sample.shape
    tp, ts = bp * sp, bs * ss

    gate_hi = Z_MAIN_COLS + N_GATE_COLS
    tail = lambda wt: jnp.concatenate(
        [wt[gate_hi:], wt[Z_MAIN_COLS:gate_hi],
         jnp.zeros((LANES - N_GATE_COLS,) + wt.shape[1:], wt.dtype)], axis=0)
    w_in_t = jnp.transpose(w_in[0]).astype(BF16)
    w_tail_t = tail(w_in_t)
    b_tail = tail(b_in[0])[None, :]
    w_kv_b = jnp.concatenate([w_mk[0], w_mv[0]], axis=-1).astype(BF16)
    w_out_b = w_out[0].astype(BF16)
    w_r = jnp.concatenate([w_rg[0], w_re[0],
                           jnp.zeros((D_MODEL, ROUTER_LANES - N_GROUPS - N_EXPERTS), F32)], axis=-1)
    w_r_hi = w_r.astype(BF16)
    w_r_lo = (w_r - w_r_hi.astype(F32)).astype(BF16)
    w_r_split = jnp.concatenate([w_r_hi, w_r_lo], axis=1)
    b_r = jnp.concatenate([b_rg[0], b_re[0],
                           jnp.zeros((ROUTER_LANES - N_GROUPS - N_EXPERTS,), F32)])[None, :]
    row = lambda a: a[0][None, :]

    xs_pad = jnp.pad(x_sample, ((0, 0), (0, SAMPLE_PAD - ss), (0, 0))).reshape(bs * SAMPLE_PAD, D_MODEL)
    zm_p, zt_p = _inproj(x_prompt.reshape(tp, D_MODEL), w_in_t, b_in, w_tail_t, b_tail, "inproj_prompt")
    zm_s, zt_s = _inproj(xs_pad, w_in_t, b_in, w_tail_t, b_tail, "inproj_sample")

    kv = _matmul(mem_prompt.reshape(bp * N_MEM, D_MODEL), w_kv_b, KV_TN, "mem_kv")
    mk_p = kv[:, :MEM_WIDTH].reshape(bp, N_MEM, MEM_HEADS, MEM_HEAD_DIM)
    mv_p = kv[:, MEM_WIDTH:].reshape(bp, N_MEM, MEM_HEADS, MEM_HEAD_DIM)

    conv_args = (w_dw[0], row(b_dw), row(g_cn), row(b_cn))
    conv_p, buf_p = _conv_group(zm_p, jnp.zeros((bp, CONV_HIST, CONV_WIDTH), F32), *conv_args,
                                bp, sp, 512, "conv_prompt")
    conv_s, buf_s_t = _conv_step_group(
        zm_s.reshape(bs, SAMPLE_PAD, Z_MAIN_COLS), jnp.transpose(state_conv[0], (1, 0, 2)),
        *conv_args, ss, 32, "conv_sample")
    buf_s = jnp.transpose(buf_s_t, (1, 0, 2))

    m_tile = lambda m: jnp.broadcast_to(
        jnp.pad(m, ((0, 0), (0, SUBLANES - MLSTM_HEADS)))[:, :, None], (m.shape[0], SUBLANES, LANES))
    g_mh_r = row(g_mh)
    seqs = lambda z, b: z.reshape(b, z.shape[0] // b, z.shape[1])
    h_p, c_p, n_p, m_p = _mlstm_group(
        seqs(zm_p, bp), seqs(zt_p, bp),
        jnp.zeros((bp, MLSTM_HEADS, MLSTM_HEAD_DIM, MLSTM_HEAD_DIM), F32),
        jnp.zeros((bp, MLSTM_HEADS, MLSTM_HEAD_DIM), F32), jnp.zeros((bp, SUBLANES, LANES), F32),
        g_mh_r, bp, sp, MLSTM_CHUNK, MLSTM_CHUNK, 2, "mlstm_prompt")
    h_s, c_s, n_s, m_s = _mlstm_group(
        seqs(zm_s, bs), seqs(zt_s, bs), state_mlstm_C[0], state_mlstm_n[0], m_tile(state_mlstm_m[0]),
        g_mh_r, bs, SAMPLE_PAD, SAMPLE_PAD, ss, 8, "mlstm_sample")
    h_p = h_p.reshape(tp, MLSTM_WIDTH)

    mem_p = _memattn_heads(zt_p, kv, bp, sp, sp, "memattn_prompt")
    packed = lambda c: c[0].reshape(bs, N_MEM * MEM_HEADS, MEM_HEAD_DIM)
    mem_s = _memattn_packed(zt_s, packed(cache_mem_k), packed(cache_mem_v), bs, SAMPLE_PAD, 8,
                            "memattn_sample")

    compact = lambda a: a.reshape(bs, SAMPLE_PAD, a.shape[-1])[:, :ss].reshape(ts, a.shape[-1])
    x1, x1_packed, eid, ew = _outproj_router(
        (conv_p, h_p, mem_p, x_prompt.reshape(tp, D_MODEL)),
        (conv_s.reshape(ts, CONV_WIDTH), h_s[:, :ss].reshape(ts, MLSTM_WIDTH), compact(mem_s),
         x_sample.reshape(ts, D_MODEL)),
        w_out_b, row(g_ln1), row(b_ln1), w_r_split, b_r)

    total = tp + ts
    y = _moe(x1_packed, _dispatch_tables(eid, total), w_gate[0], w_up[0], w_down[0])
    g2, b2 = row(g_ln2), row(b_ln2)
    y_p = _final(x1, y, ew, g2, b2, 0, tp, "final_prompt").reshape(bp, sp, D_MODEL)
    y_s = _final(x1, y, ew, g2, b2, tp, ts, "final_sample").reshape(bs, ss, D_MODEL)

    return (y_p, y_s, buf_p[None], buf_s[None], c_p[None], c_s[None], n_p[None], n_s[None],
            m_p[:, :MLSTM_HEADS, 0][None], m_s[:, :MLSTM_HEADS, 0][None], mk_p[None], mv_p[None])
```

```python
import functools

import jax
import jax.numpy as jnp
from jax import lax
from jax.experimental import pallas as pl
from jax.experimental.pallas import tpu as pltpu

F32 = jnp.float32
BF16 = jnp.bfloat16

D_MODEL = 2048
CONV_WIDTH = 512
CONV_K = 31
CONV_HIST = CONV_K - 1
MLSTM_HEADS = 4
MLSTM_HEAD_DIM = 256
MLSTM_WIDTH = MLSTM_HEADS * MLSTM_HEAD_DIM
MLSTM_CHUNK = 128
MEM_HEADS = 4
MEM_HEAD_DIM = 128
MEM_WIDTH = MEM_HEADS * MEM_HEAD_DIM
N_MEM = 256
N_GROUPS = 8
EXPERTS_PER_GROUP = 8
N_EXPERTS = N_GROUPS * EXPERTS_PER_GROUP
D_EXPERT = 512
LN_EPS = 1e-5
DEPTH = 1
ALPHA = (2 * DEPTH) ** 0.25

LANES = 128
SUBLANES = 8
VMEM_LIMIT_BYTES = 56 * 1024 * 1024

Z_CONV_A = 0
Z_CONV_G = CONV_WIDTH
Z_Q = 2 * CONV_WIDTH
Z_K = Z_Q + MLSTM_WIDTH
Z_V = Z_K + MLSTM_WIDTH
Z_O = Z_V + MLSTM_WIDTH
Z_MAIN_COLS = Z_O + MLSTM_WIDTH
N_GATE_COLS = 2 * MLSTM_HEADS
ZT_QM = 0
ZT_GATE = MEM_WIDTH
Z_TAIL_COLS = MEM_WIDTH + LANES
INPROJ_TM = 1024
INPROJ_TN = 1024
KV_TN = 256

SAMPLE_PAD = SUBLANES
MLSTM_STAGE_PAIRS = 8
ROUTER_LANES = LANES
MOE_ROWS = 160
MOE_SLOTS = 4
assert N_EXPERTS >= MOE_SLOTS
OUT_TM = 512
FINAL_TM = 512


def _params(*sem):
    return pltpu.CompilerParams(dimension_semantics=sem, vmem_limit_bytes=VMEM_LIMIT_BYTES)


def _inproj_kernel(x_ref, wm_ref, bm_ref, wt_ref, bt_ref, zm_ref, zt_ref, xb_ref, *, n_main):
    j = pl.program_id(1)

    @pl.when(j == 0)
    def _():
        xb_ref[...] = x_ref[...].astype(BF16)

    nt = (((1,), (1,)), ((), ()))

    @pl.when(j < n_main)
    def _():
        zm_ref[...] = lax.dot_general(xb_ref[...], wm_ref[...], nt,
                                      preferred_element_type=F32) + bm_ref[...]

    @pl.when(j == n_main)
    def _():
        zt_ref[...] = lax.dot_general(xb_ref[...], wt_ref[...], nt,
                                      preferred_element_type=F32) + bt_ref[...]


def _inproj(x, w_in_t, b_in, w_tail_t, b_tail, name):
    t, k = x.shape
    tm, tn = INPROJ_TM, INPROJ_TN
    n_main = Z_MAIN_COLS // tn
    main_col = lambda j: jnp.minimum(j, n_main - 1)
    return pl.pallas_call(
        functools.partial(_inproj_kernel, n_main=n_main),
        out_shape=(jax.ShapeDtypeStruct((t, Z_MAIN_COLS), F32),
                   jax.ShapeDtypeStruct((t, Z_TAIL_COLS), F32)),
        grid=(t // tm, n_main + 1),
        in_specs=[
            pl.BlockSpec((tm, k), lambda i, j: (i, 0)),
            pl.BlockSpec((tn, k), lambda i, j: (main_col(j), 0)),
            pl.BlockSpec((1, tn), lambda i, j: (0, main_col(j))),
            pl.BlockSpec((Z_TAIL_COLS, k), lambda i, j: (0, 0)),
            pl.BlockSpec((1, Z_TAIL_COLS), lambda i, j: (0, 0)),
        ],
        out_specs=(pl.BlockSpec((tm, tn), lambda i, j: (i, main_col(j))),
                   pl.BlockSpec((tm, Z_TAIL_COLS), lambda i, j: (i, 0))),
        scratch_shapes=[pltpu.VMEM((tm, k), BF16)],
        compiler_params=_params("arbitrary", "arbitrary"),
        name=name,
    )(x, w_in_t, b_in, w_tail_t, b_tail)


def _matmul_kernel(x_ref, w_ref, o_ref):
    o_ref[...] = jnp.dot(x_ref[...].astype(BF16), w_ref[...], preferred_element_type=F32)


def _matmul(x, w_bf16, tn, name):
    t, k = x.shape
    n = w_bf16.shape[1]
    return pl.pallas_call(
        _matmul_kernel,
        out_shape=jax.ShapeDtypeStruct((t, n), F32),
        grid=(n // tn,),
        in_specs=[pl.BlockSpec((t, k), lambda j: (0, 0)), pl.BlockSpec((k, tn), lambda j: (0, j))],
        out_specs=pl.BlockSpec((t, tn), lambda j: (0, j)),
        compiler_params=_params("arbitrary"),
        name=name,
    )(x, w_bf16)


def _layer_norm_rows(y, g, b):
    mu = jnp.mean(y, axis=-1, keepdims=True)
    yc = y - mu
    var = jnp.mean(yc * yc, axis=-1, keepdims=True)
    return yc * lax.rsqrt(var + LN_EPS) * g + b


CONV_ROWS = 64


def _conv_kernel(a_ref, g_ref, hist_ref, wdw_ref, bdw_ref, gcn_ref, bcn_ref, out_ref, nb_ref,
                 ubuf, shifted, *, tl):
    head = CONV_HIST + 2
    li = pl.program_id(1)

    @pl.when(li == 0)
    def _():
        ubuf[0:2, :] = jnp.zeros((2, CONV_WIDTH), F32)
        ubuf[2:head, :] = hist_ref[0]

    ubuf[head:head + tl, :] = a_ref[...] * jax.nn.sigmoid(g_ref[...])
    span = shifted.shape[1]
    for k in range(1, SUBLANES):
        shifted[k - 1] = ubuf[k:k + span, :]
    for r0 in range(0, tl, CONV_ROWS):
        acc = jnp.zeros((CONV_ROWS, CONV_WIDTH), F32) + bdw_ref[...]
        for j in range(CONV_K):
            lo, k = divmod(2 + j, SUBLANES)
            lo = lo * SUBLANES + r0
            src = ubuf if k == 0 else shifted.at[k - 1]
            acc = acc + wdw_ref[j:j + 1, :] * src[lo:lo + CONV_ROWS, :]
        y = _layer_norm_rows(acc, gcn_ref[...], bcn_ref[...])
        out_ref[r0:r0 + CONV_ROWS, :] = y * jax.nn.sigmoid(y)

    @pl.when(li == pl.num_programs(1) - 1)
    def _():
        nb_ref[0] = ubuf[2 + tl:head + tl, :]

    ubuf[2:head, :] = ubuf[2 + tl:head + tl, :]


def _conv_group(z, hist, w_dw, b_dw, g_cn, b_cn, batch, seq, tl, name):
    nl = seq // tl
    row = lambda b, l: (b * nl + l, 0)
    vec = pl.BlockSpec((1, CONV_WIDTH), lambda b, l: (0, 0))
    return pl.pallas_call(
        functools.partial(_conv_kernel, tl=tl),
        out_shape=(jax.ShapeDtypeStruct((batch * seq, CONV_WIDTH), F32),
                   jax.ShapeDtypeStruct((batch, CONV_HIST, CONV_WIDTH), F32)),
        grid=(batch, nl),
        in_specs=[
            pl.BlockSpec((tl, CONV_WIDTH), lambda b, l: (b * nl + l, Z_CONV_A // CONV_WIDTH)),
            pl.BlockSpec((tl, CONV_WIDTH), lambda b, l: (b * nl + l, Z_CONV_G // CONV_WIDTH)),
            pl.BlockSpec((1, CONV_HIST, CONV_WIDTH), lambda b, l: (b, 0, 0)),
            pl.BlockSpec((CONV_K, CONV_WIDTH), lambda b, l: (0, 0)),
            vec, vec, vec,
        ],
        out_specs=(pl.BlockSpec((tl, CONV_WIDTH), row),
                   pl.BlockSpec((1, CONV_HIST, CONV_WIDTH), lambda b, l: (b, 0, 0))),
        scratch_shapes=[pltpu.VMEM((CONV_HIST + 2 + tl, CONV_WIDTH), F32),
                        pltpu.VMEM((SUBLANES - 1, CONV_HIST + 2 + tl - SUBLANES, CONV_WIDTH), F32)],
        compiler_params=_params("arbitrary", "arbitrary"),
        name=name,
    )(z, z, hist, w_dw, b_dw, g_cn, b_cn)


def _conv_step_kernel(a_ref, g_ref, hist_ref, wdw_ref, bdw_ref, gcn_ref, bcn_ref, out_ref, nb_ref,
                      *, steps):
    u = [a_ref[:, t, :] * jax.nn.sigmoid(g_ref[:, t, :]) for t in range(steps)]
    full = lambda r: hist_ref[r] if r < CONV_HIST else u[r - CONV_HIST]
    for t in range(steps):
        acc = bdw_ref[...] + wdw_ref[0:1, :] * full(t)
        for j in range(1, CONV_K):
            acc = acc + wdw_ref[j:j + 1, :] * full(t + j)
        y = _layer_norm_rows(acc, gcn_ref[...], bcn_ref[...])
        out_ref[:, t, :] = y * jax.nn.sigmoid(y)
    for r in range(CONV_HIST):
        nb_ref[r] = full(r + steps)


def _conv_step_group(z3, hist_t, w_dw, b_dw, g_cn, b_cn, steps, bb, name):
    batch = z3.shape[0]
    vec = pl.BlockSpec((1, CONV_WIDTH), lambda b: (0, 0))
    hist_spec = pl.BlockSpec((CONV_HIST, bb, CONV_WIDTH), lambda b: (0, b, 0))
    return pl.pallas_call(
        functools.partial(_conv_step_kernel, steps=steps),
        out_shape=(jax.ShapeDtypeStruct((batch, steps, CONV_WIDTH), F32),
                   jax.ShapeDtypeStruct((CONV_HIST, batch, CONV_WIDTH), F32)),
        grid=(batch // bb,),
        in_specs=[
            pl.BlockSpec((bb, SAMPLE_PAD, CONV_WIDTH), lambda b: (b, 0, Z_CONV_A // CONV_WIDTH)),
            pl.BlockSpec((bb, SAMPLE_PAD, CONV_WIDTH), lambda b: (b, 0, Z_CONV_G // CONV_WIDTH)),
            hist_spec,
            pl.BlockSpec((CONV_K, CONV_WIDTH), lambda b: (0, 0)),
            vec, vec, vec,
        ],
        out_specs=(pl.BlockSpec((bb, steps, CONV_WIDTH), lambda b: (b, 0, 0)), hist_spec),
        compiler_params=_params("arbitrary"),
        name=name,
    )(z3, z3, hist_t, w_dw, b_dw, g_cn, b_cn)


def _col_to_row(col, eye):
    n = col.shape[0]
    return jnp.sum(jnp.where(eye, jnp.broadcast_to(col, (n, n)), 0.0), axis=0, keepdims=True)


def _mlstm_kernel(q_ref, k_ref, v_ref, o_ref, gate_ref, c0_ref, n0_ref, m0_ref, gmh_ref,
                  h_ref, c_ref, n_ref, m_ref, *, cl, valid, bb, single_chunk):
    if single_chunk:
        c_in, n_in, m_in = c0_ref, n0_ref, m0_ref
    else:
        c_in, n_in, m_in = c_ref, n_ref, m_ref

        @pl.when(pl.program_id(1) == 0)
        def _():
            c_ref[...] = c0_ref[...]
            n_ref[...] = n0_ref[...]
            m_ref[...] = m0_ref[...]

    rows = lax.broadcasted_iota(jnp.int32, (cl, cl), 0)
    cols = lax.broadcasted_iota(jnp.int32, (cl, cl), 1)
    eye = rows == cols
    tril = rows >= cols
    row_id = lax.broadcasted_iota(jnp.int32, (cl, 1), 0)
    nh, dh = MLSTM_HEADS, MLSTM_HEAD_DIM
    nt_dims = (((1,), (1,)), ((), ()))
    tn_dims = (((0,), (0,)), ((), ()))
    each = lambda f, *lists: [f(*args) for args in zip(*lists)]
    all_pairs = [(bi, h) for bi in range(bb) for h in range(nh)]
    for g0 in range(0, len(all_pairs), MLSTM_STAGE_PAIRS):
        pairs = all_pairs[g0:g0 + MLSTM_STAGE_PAIRS]
        cols_of = [slice(h * dh, (h + 1) * dh) for _, h in pairs]
        gates = {bi: gate_ref[bi] for bi in sorted({bi for bi, _ in pairs})}
        log_sig = {bi: jnp.minimum(g, 0.0) - jnp.log1p(jnp.exp(-jnp.abs(g))) for bi, g in gates.items()}
        ip = each(lambda p: gates[p[0]][:, p[1]:p[1] + 1], pairs)
        lf = each(lambda p: log_sig[p[0]][:, nh + p[1]:nh + p[1] + 1], pairs)
        if valid < cl:
            ip = each(lambda x: jnp.where(row_id < valid, x, -jnp.inf), ip)
            lf = each(lambda x: jnp.where(row_id < valid, x, 0.0), lf)
        lf_row = each(lambda x: _col_to_row(x, eye), lf)
        ip_row = each(lambda x: _col_to_row(x, eye), ip)
        b_col = each(lambda r: jnp.sum(jnp.where(tril, jnp.broadcast_to(r, (cl, cl)), 0.0),
                                       axis=1, keepdims=True), lf_row)
        b_row = each(lambda x: _col_to_row(x, eye), b_col)
        m_prev = each(lambda p: m_in[p[0], p[1]:p[1] + 1, 0:1], pairs)
        log_inter = each(lambda b, m: b + m, b_col, m_prev)
        log_intra = each(lambda bc, br, ir: jnp.where(tril, bc - br + ir, -jnp.inf), b_col, b_row, ip_row)
        m_t = each(lambda le, la: jnp.maximum(le, jnp.max(la, axis=1, keepdims=True)), log_inter, log_intra)
        w_inter = each(lambda le, m: jnp.exp(le - m), log_inter, m_t)
        intra = each(lambda la, m: jnp.exp(la - m), log_intra, m_t)
        m_new = each(lambda m: m[cl - 1:cl, :], m_t)
        b_last = each(lambda b: b[cl - 1:cl, :], b_col)
        decay = each(lambda bl, mp, mn: jnp.exp(bl + mp - mn), b_last, m_prev, m_new)
        w_s = each(lambda bl, b, i, mn: jnp.exp(bl - b + i - mn), b_last, b_col, ip, m_new)
        q = each(lambda p, sl: q_ref[p[0], :, sl], pairs, cols_of)
        k = each(lambda p, sl: k_ref[p[0], :, sl] * (dh ** -0.5), pairs, cols_of)
        vb = each(lambda p, sl: v_ref[p[0], :, sl].astype(BF16), pairs, cols_of)
        qb = each(lambda x: x.astype(BF16), q)
        kb = each(lambda x: x.astype(BF16), k)
        s = each(lambda a, b: lax.dot_general(a, b, nt_dims, preferred_element_type=F32), qb, kb)
        s = each(lambda x, d: x * d, s, intra)
        c_old = each(lambda p: c_in[p[0], p[1]], pairs)
        n_old = each(lambda p: n_in[p[0], p[1]:p[1] + 1, :], pairs)
        inter = each(lambda a, c: jnp.dot(a, c.astype(BF16), preferred_element_type=F32), qb, c_old)
        local = each(lambda x, v: jnp.dot(x.astype(BF16), v, preferred_element_type=F32), s, vb)
        num = each(lambda w, a, b: w * a + b, w_inter, inter, local)
        den = each(lambda w, x, n, ss: w * jnp.sum(x * n, axis=1, keepdims=True)
                   + jnp.sum(ss, axis=1, keepdims=True), w_inter, q, n_old, s)
        hh = each(lambda a, d, m: a / jnp.maximum(jnp.abs(d), jnp.exp(-m)), num, den, m_t)
        kw = each(lambda x, w: x * w, k, w_s)
        outer = each(lambda a, v: lax.dot_general(a.astype(BF16), v, tn_dims, preferred_element_type=F32),
                     kw, vb)
        for i, (bi, h) in enumerate(pairs):
            c_ref[bi, h] = decay[i] * c_old[i] + outer[i]
            n_ref[bi, h:h + 1, :] = decay[i] * n_old[i] + jnp.sum(kw[i], axis=0, keepdims=True)
            m_ref[bi, h:h + 1, :] = jnp.broadcast_to(m_new[i], (1, LANES))
        mu = each(lambda x: jnp.mean(x, axis=-1, keepdims=True), hh)
        hc = each(lambda x, m: x - m, hh, mu)
        var = each(lambda x: jnp.mean(x * x, axis=-1, keepdims=True), hc)
        for i, (bi, h) in enumerate(pairs):
            sl = cols_of[i]
            hn = hc[i] * lax.rsqrt(var[i] + LN_EPS) * gmh_ref[:, sl]
            h_ref[bi, :, sl] = hn * jax.nn.sigmoid(o_ref[bi, :, sl])
    if single_chunk:
        m_ref[:, MLSTM_HEADS:, :] = jnp.zeros((bb, SUBLANES - MLSTM_HEADS, LANES), F32)


def _mlstm_group(z_main, z_tail, c0, n0, m0, g_mh, batch, seq, cl, valid, bb, name):
    nc = seq // cl
    zcol = lambda off: pl.BlockSpec((bb, cl, MLSTM_WIDTH), lambda b, c: (b, c, off // MLSTM_WIDTH))
    state = lambda shape: pl.BlockSpec((bb,) + shape, lambda b, c: (b,) + (0,) * len(shape))
    c_shape = (MLSTM_HEADS, MLSTM_HEAD_DIM, MLSTM_HEAD_DIM)
    n_shape = (MLSTM_HEADS, MLSTM_HEAD_DIM)
    m_shape = (SUBLANES, LANES)
    return pl.pallas_call(
        functools.partial(_mlstm_kernel, cl=cl, valid=valid, bb=bb, single_chunk=nc == 1),
        out_shape=(jax.ShapeDtypeStruct((batch, seq, MLSTM_WIDTH), F32),
                   jax.ShapeDtypeStruct((batch,) + c_shape, F32),
                   jax.ShapeDtypeStruct((batch,) + n_shape, F32),
                   jax.ShapeDtypeStruct((batch,) + m_shape, F32)),
        grid=(batch // bb, nc),
        in_specs=[
            zcol(Z_Q), zcol(Z_K), zcol(Z_V), zcol(Z_O),
            pl.BlockSpec((bb, cl, LANES), lambda b, c: (b, c, ZT_GATE // LANES)),
            state(c_shape), state(n_shape), state(m_shape),
            pl.BlockSpec((1, MLSTM_WIDTH), lambda b, c: (0, 0)),
        ],
        out_specs=(pl.BlockSpec((bb, cl, MLSTM_WIDTH), lambda b, c: (b, c, 0)),
                   state(c_shape), state(n_shape), state(m_shape)),
        compiler_params=_params("arbitrary", "arbitrary"),
        name=name,
    )(z_main, z_main, z_main, z_main, z_tail, c0, n0, m0, g_mh)


def _softmax_rows(s):
    e = jnp.exp(s - jnp.max(s, axis=-1, keepdims=True))
    return e / jnp.sum(e, axis=-1, keepdims=True)


_NT_DIMS = (((1,), (1,)), ((), ()))


def _memattn_head_kernel(q_ref, k_ref, v_ref, o_ref):
    s = lax.dot_general(q_ref[...].astype(BF16), k_ref[...].astype(BF16), _NT_DIMS,
                        preferred_element_type=F32) * (MEM_HEAD_DIM ** -0.5)
    p = _softmax_rows(s)
    o_ref[...] = jnp.dot(p.astype(BF16), v_ref[...].astype(BF16), preferred_element_type=F32)


def _memattn_heads(z_tail, kv, batch, seq, tq, name):
    nq = seq // tq
    dh = MEM_HEAD_DIM
    return pl.pallas_call(
        _memattn_head_kernel,
        out_shape=jax.ShapeDtypeStruct((batch * seq, MEM_WIDTH), F32),
        grid=(batch, MEM_HEADS, nq),
        in_specs=[pl.BlockSpec((tq, dh), lambda b, h, i: (b * nq + i, ZT_QM // dh + h)),
                  pl.BlockSpec((N_MEM, dh), lambda b, h, i: (b, h)),
                  pl.BlockSpec((N_MEM, dh), lambda b, h, i: (b, MEM_HEADS + h))],
        out_specs=pl.BlockSpec((tq, dh), lambda b, h, i: (b * nq + i, h)),
        compiler_params=_params("arbitrary", "arbitrary", "arbitrary"),
        name=name,
    )(z_tail, kv, kv)


def _memattn_packed_kernel(q_ref, k_ref, v_ref, o_ref, *, tq, bb):
    nh, dh = MEM_HEADS, MEM_HEAD_DIM
    shape = (nh * tq, N_MEM * nh)
    row_head = lax.broadcasted_iota(jnp.int32, shape, 0) // tq
    col_head = lax.broadcasted_iota(jnp.int32, shape, 1) % nh
    same_head = row_head == col_head
    scores = []
    for bi in range(bb):
        rs = slice(bi * tq, (bi + 1) * tq)
        q = jnp.concatenate([q_ref[rs, h * dh:(h + 1) * dh] for h in range(nh)], axis=0)
        scores.append(lax.dot_general(q.astype(BF16), k_ref[bi].astype(BF16), _NT_DIMS,
                                      preferred_element_type=F32) * (dh ** -0.5))
    probs = [_softmax_rows(jnp.where(same_head, s, -jnp.inf)).astype(BF16) for s in scores]
    outs = [jnp.dot(p, v_ref[bi].astype(BF16), preferred_element_type=F32) for bi, p in enumerate(probs)]
    for bi, o in enumerate(outs):
        rs = slice(bi * tq, (bi + 1) * tq)
        for h in range(nh):
            o_ref[rs, h * dh:(h + 1) * dh] = o[h * tq:(h + 1) * tq, :]


def _memattn_packed(z_tail, mk, mv, batch, tq, bb, name):
    rows = bb * tq
    kv = pl.BlockSpec((bb, N_MEM * MEM_HEADS, MEM_HEAD_DIM), lambda b: (b, 0, 0))
    return pl.pallas_call(
        functools.partial(_memattn_packed_kernel, tq=tq, bb=bb),
        out_shape=jax.ShapeDtypeStruct((batch * tq, MEM_WIDTH), F32),
        grid=(batch // bb,),
        in_specs=[pl.BlockSpec((rows, MEM_WIDTH), lambda b: (b, ZT_QM // MEM_WIDTH)), kv, kv],
        out_specs=pl.BlockSpec((rows, MEM_WIDTH), lambda b: (b, 0)),
        compiler_params=_params("arbitrary"),
        name=name,
    )(z_tail, mk, mv)


def _route(logits):
    lane = lax.broadcasted_iota(jnp.int32, logits.shape, 1).astype(F32)
    neg = -jnp.inf
    first = lambda mask: jnp.min(jnp.where(mask, lane, float(LANES)), axis=1, keepdims=True)
    is_g = lane < N_GROUPS
    gl = jnp.where(is_g, logits, neg)
    g_max = jnp.max(gl, axis=1, keepdims=True)
    g_sel = first(gl == g_max)
    g_w = 1.0 / jnp.sum(jnp.exp(gl - g_max), axis=1, keepdims=True)
    lo = N_GROUPS + g_sel * EXPERTS_PER_GROUP
    in_grp = (lane >= lo) & (lane < lo + EXPERTS_PER_GROUP)
    el = jnp.where(in_grp, logits, neg)
    v1 = jnp.max(el, axis=1, keepdims=True)
    i1 = first(in_grp & (el == v1))
    rest = in_grp & (lane != i1)
    el2 = jnp.where(rest, logits, neg)
    v2 = jnp.max(el2, axis=1, keepdims=True)
    i2 = first(rest & (el2 == v2))
    t = jnp.exp(v2 - v1)
    w1 = g_w / (1.0 + t)
    w2 = g_w * t / (1.0 + t)
    lane_i = lax.broadcasted_iota(jnp.int32, logits.shape, 1)
    e1 = (i1 - N_GROUPS).astype(jnp.int32)
    e2 = (i2 - N_GROUPS).astype(jnp.int32)
    eid = jnp.where(lane_i == 0, e1, jnp.where(lane_i == 1, e2, 0))
    ew = jnp.where(lane_i == 0, w1, jnp.where(lane_i == 1, w2, 0.0))
    return eid, ew


def _outproj_kernel(cp, hp, mp, xp, cs, hs, ms, xs, wout, g1, b1, wr, br,
                    x1_ref, x1p_ref, eid_ref, ew_ref, *, n_prompt):
    def rows(c, h, m, x, rs):
        groups = jnp.concatenate(
            [c[rs, :].astype(BF16), h[rs, :].astype(BF16), m[rs, :].astype(BF16)], axis=1)
        mix = jnp.dot(groups, wout[...], preferred_element_type=F32)
        x1 = _layer_norm_rows(ALPHA * x[rs, :] + mix, g1[...], b1[...])
        x1_ref[rs, :] = x1
        xh = x1.astype(BF16)
        xh_f32 = xh.astype(F32)
        bits = pltpu.bitcast(xh_f32, jnp.int32)
        x1p_ref[rs, :] = bits[:, :D_MODEL // 2] | lax.shift_right_logical(bits[:, D_MODEL // 2:], 16)
        xl = (x1 - xh_f32).astype(BF16)
        n = x1.shape[0]
        cross = jnp.dot(jnp.concatenate([xh, xl], axis=0), wr[...], preferred_element_type=F32)
        logits = ((cross[:n, :ROUTER_LANES] + cross[:n, ROUTER_LANES:])
                  + (cross[n:, :ROUTER_LANES] + cross[n:, ROUTER_LANES:])) + br[...]
        eid, ew = _route(logits)
        eid_ref[rs, :] = eid
        ew_ref[rs, :] = ew

    def body(c, h, m, x):
        rows(c, h, m, x, slice(None))

    i = pl.program_id(0)
    pl.when(i < n_prompt)(lambda: body(cp, hp, mp, xp))
    pl.when(i >= n_prompt)(lambda: body(cs, hs, ms, xs))


def _outproj_router(prompt, sample, w_out_b, g1, b1, wr, br):
    tm = OUT_TM
    tp = prompt[0].shape[0]
    ts = sample[0].shape[0]
    n_p, n_s = tp // tm, ts // tm
    total = tp + ts
    widths = (CONV_WIDTH, MLSTM_WIDTH, MEM_WIDTH, D_MODEL)
    p_specs = [pl.BlockSpec((tm, w), lambda i: (jnp.minimum(i, n_p - 1), 0)) for w in widths]
    once = pl.Buffered(1)
    s_specs = [pl.BlockSpec((tm, w), lambda i: (jnp.maximum(i - n_p, 0), 0),
                            pipeline_mode=once if n_s == 1 else None) for w in widths]
    full = lambda shape: pl.BlockSpec(shape, lambda i: (0, 0), pipeline_mode=once)
    row = lambda w: pl.BlockSpec((tm, w), lambda i: (i, 0))
    return pl.pallas_call(
        functools.partial(_outproj_kernel, n_prompt=n_p),
        out_shape=(jax.ShapeDtypeStruct((total, D_MODEL), F32),
                   jax.ShapeDtypeStruct((total, D_MODEL // 2), jnp.int32),
                   jax.ShapeDtypeStruct((total, ROUTER_LANES), jnp.int32),
                   jax.ShapeDtypeStruct((total, ROUTER_LANES), F32)),
        grid=(n_p + n_s,),
        in_specs=p_specs + s_specs + [
            full((D_MODEL, D_MODEL)), full((1, D_MODEL)), full((1, D_MODEL)),
            full((D_MODEL, 2 * ROUTER_LANES)), full((1, ROUTER_LANES)),
        ],
        out_specs=(row(D_MODEL), row(D_MODEL // 2), row(ROUTER_LANES), row(ROUTER_LANES)),
        compiler_params=_params("arbitrary"),
        name="outproj_router",
    )(*prompt, *sample, w_out_b, g1, b1, wr, br)


def _moe_kernel(stok_ref, sdst_ref, base_ref, nval_ref, first_ref, count_ref,
                x_hbm, wg_ref, wu_ref, wd_ref, y_hbm, wgb, wub, wdb, xbuf, ybuf, gsem, ssem,
                *, dump_row):
    e = pl.program_id(0)
    last = pl.num_programs(0) - 1
    rc = MOE_ROWS
    ns = MOE_SLOTS

    def gather_start(base, slot):
        for r in range(rc):
            pltpu.make_async_copy(x_hbm.at[pl.ds(stok_ref[base + r], 1)],
                                  xbuf.at[slot, pl.ds(r, 1)], gsem.at[slot]).start()

    def gather_wait(slot):
        pltpu.make_async_copy(x_hbm.at[pl.ds(0, rc)], xbuf.at[slot], gsem.at[slot]).wait()

    def scatter_start(base, n_valid, slot):
        for r in range(rc):
            dst = jnp.where(r < n_valid, sdst_ref[base + r], dump_row + slot * rc + r)
            pltpu.make_async_copy(ybuf.at[slot, pl.ds(r, 1)], y_hbm.at[pl.ds(dst, 1)],
                                  ssem.at[slot]).start()

    def scatter_wait(slot):
        pltpu.make_async_copy(ybuf.at[slot], y_hbm.at[pl.ds(0, rc)], ssem.at[slot]).wait()

    @pl.when(e == 0)
    def _():
        ybuf[...] = jnp.zeros(ybuf.shape, F32)
        for slot in range(ns):
            pltpu.make_async_copy(ybuf.at[slot], y_hbm.at[pl.ds(dump_row + slot * rc, rc)],
                                  ssem.at[slot]).start()
            scatter_wait(slot)
        for ahead in range(ns - 1):
            gather_start(base_ref[1 + ahead], ahead)

    wgb[...] = wg_ref[0].astype(BF16)
    wub[...] = wu_ref[0].astype(BF16)
    wdb[...] = wd_ref[0].astype(BF16)
    first = first_ref[e]

    def chunk_on(slot, g):
        prv = (slot - 1) % ns
        gather_wait(slot)

        @pl.when(g >= ns - 1)
        def _():
            scatter_wait(slot)

        gather_start(base_ref[g + ns], prv)
        scatter_start(base_ref[g], nval_ref[g], prv)
        packed = xbuf[slot]
        x = jnp.concatenate(
            [pltpu.bitcast(packed & jnp.int32(-65536), F32).astype(BF16),
             pltpu.bitcast(packed << 16, F32).astype(BF16)], axis=1)
        hg = jnp.dot(x, wgb[...], preferred_element_type=F32)
        hu = jnp.dot(x, wub[...], preferred_element_type=F32)
        hid = (hg * jax.nn.sigmoid(hg) * hu).astype(BF16)
        ybuf[slot] = jnp.dot(hid, wdb[...], preferred_element_type=F32)

    def chunk(c, carry):
        g = first + c
        cur = lax.rem(g, ns)
        for slot in range(ns):
            pl.when(cur == slot)(functools.partial(chunk_on, slot, g))
        return carry

    lax.fori_loop(0, count_ref[e], chunk, 0)

    @pl.when(e == last)
    def _():
        g_end = first + count_ref[e]
        cur = lax.rem(g_end, ns)
        for slot in range(ns):
            @pl.when(cur == slot)
            def _():
                for ahead in range(ns - 1):
                    gather_wait((slot + ahead) % ns)
                scatter_start(base_ref[g_end], nval_ref[g_end], (slot - 1) % ns)

        for slot in range(ns):
            scatter_wait(slot)


def _moe(x1, tables, w_gate, w_up, w_down):
    total = x1.shape[0]
    dump_row = 2 * total
    wspec = lambda shape: pl.BlockSpec((1,) + shape, lambda e, *_: (e, 0, 0))
    grid_spec = pltpu.PrefetchScalarGridSpec(
        num_scalar_prefetch=len(tables),
        grid=(N_EXPERTS,),
        in_specs=[pl.BlockSpec(memory_space=pl.ANY),
                  wspec((D_MODEL, D_EXPERT)), wspec((D_MODEL, D_EXPERT)), wspec((D_EXPERT, D_MODEL))],
        out_specs=pl.BlockSpec(memory_space=pl.ANY),
        scratch_shapes=[
            pltpu.VMEM((D_MODEL, D_EXPERT), BF16), pltpu.VMEM((D_MODEL, D_EXPERT), BF16),
            pltpu.VMEM((D_EXPERT, D_MODEL), BF16),
            pltpu.VMEM((MOE_SLOTS, MOE_ROWS, D_MODEL // 2), jnp.int32),
            pltpu.VMEM((MOE_SLOTS, MOE_ROWS, D_MODEL), F32),
            pltpu.SemaphoreType.DMA((MOE_SLOTS,)), pltpu.SemaphoreType.DMA((MOE_SLOTS,)),
        ],
    )
    return pl.pallas_call(
        functools.partial(_moe_kernel, dump_row=dump_row),
        out_shape=jax.ShapeDtypeStruct((2 * total + MOE_SLOTS * MOE_ROWS, D_MODEL), F32),
        grid_spec=grid_spec,
        compiler_params=_params("arbitrary"),
        name="moe_experts",
    )(*tables, x1, w_gate, w_up, w_down)


def _final_kernel(x1_ref, y0_ref, y1_ref, ew_ref, g2, b2, o_ref):
    ew = ew_ref[...]
    ffn = ew[:, 0:1] * y0_ref[...] + ew[:, 1:2] * y1_ref[...]
    o_ref[...] = _layer_norm_rows(ALPHA * x1_ref[...] + ffn, g2[...], b2[...])


def _final(x1, y, ew, g2, b2, row0, rows, name):
    tm = FINAL_TM
    total = x1.shape[0]
    off = row0 // tm
    k1 = total // tm
    vec = pl.BlockSpec((1, D_MODEL), lambda i: (0, 0))
    return pl.pallas_call(
        _final_kernel,
        out_shape=jax.ShapeDtypeStruct((rows, D_MODEL), F32),
        grid=(rows // tm,),
        in_specs=[pl.BlockSpec((tm, D_MODEL), lambda i: (i + off, 0)),
                  pl.BlockSpec((tm, D_MODEL), lambda i: (i + off, 0)),
                  pl.BlockSpec((tm, D_MODEL), lambda i: (i + off + k1, 0)),
                  pl.BlockSpec((tm, ROUTER_LANES), lambda i: (i + off, 0)),
                  vec, vec],
        out_specs=pl.BlockSpec((tm, D_MODEL), lambda i: (i, 0)),
        compiler_params=_params("arbitrary"),
        name=name,
    )(x1, y, y, ew, g2, b2)


def _dispatch_tables(eid, total):
    flat_e = eid[:, :2].reshape(-1)
    order = jnp.argsort(flat_e, stable=True).astype(jnp.int32)
    stok = order >> 1
    sdst = (order & 1) * total + stok
    experts = jnp.arange(N_EXPERTS, dtype=jnp.int32)
    cnt = jnp.sum((flat_e[:, None] == experts[None, :]).astype(jnp.int32), axis=0)
    start = jnp.cumsum(cnt) - cnt
    pad = jnp.zeros((MOE_ROWS,), jnp.int32)
    n_chunks = jnp.maximum((cnt + MOE_ROWS - 1) // MOE_ROWS, 1)
    first = jnp.cumsum(n_chunks) - n_chunks
    n_entries = flat_e.shape[0] // MOE_ROWS + N_EXPERTS + MOE_SLOTS + 1
    g = jnp.arange(n_entries, dtype=jnp.int32) - 1
    owner = jnp.sum((g[:, None] >= (first + n_chunks)[None, :]).astype(jnp.int32), axis=1)
    real = (g >= 0) & (owner < N_EXPERTS)
    pick = (owner[:, None] == experts[None, :]).astype(jnp.int32)
    sel = lambda v: jnp.sum(pick * v[None, :], axis=1)
    local = (g - sel(first)) * MOE_ROWS
    base = jnp.where(real, sel(start) + local, 0)
    n_valid = jnp.where(real, sel(cnt) - local, 0)
    return (jnp.concatenate([stok, pad]), jnp.concatenate([sdst, pad]), base, n_valid, first, n_chunks)


def kernel(x_prompt, x_sample, mem_prompt, state_conv, state_mlstm_C, state_mlstm_n, state_mlstm_m,
           cache_mem_k, cache_mem_v, w_in, b_in, w_dw, b_dw, g_cn, b_cn, g_mh, w_mk, w_mv, w_out,
           g_ln1, b_ln1, w_rg, b_rg, w_re, b_re, w_gate, w_up, w_down, g_ln2, b_ln2):
    bp, sp, _ = x_prompt.shape
    bs, ss, _ = x_sample.shape
    tp, ts = bp * sp, bs * ss

    gate_hi = Z_MAIN_COLS + N_GATE_COLS
    tail = lambda wt: jnp.concatenate(
        [wt[gate_hi:], wt[Z_MAIN_COLS:gate_hi],
         jnp.zeros((LANES - N_GATE_COLS,) + wt.shape[1:], wt.dtype)], axis=0)
    w_in_t = jnp.transpose(w_in[0]).astype(BF16)
    w_tail_t = tail(w_in_t)
    b_tail = tail(b_in[0])[None, :]
    w_kv_b = jnp.concatenate([w_mk[0], w_mv[0]], axis=-1).astype(BF16)
    w_out_b = w_out[0].astype(BF16)
    w_r = jnp.concatenate([w_rg[0], w_re[0],
                           jnp.zeros((D_MODEL, ROUTER_LANES - N_GROUPS - N_EXPERTS), F32)], axis=-1)
    w_r_hi = w_r.astype(BF16)
    w_r_lo = (w_r - w_r_hi.astype(F32)).astype(BF16)
    w_r_split = jnp.concatenate([w_r_hi, w_r_lo], axis=1)
    b_r = jnp.concatenate([b_rg[0], b_re[0],
                           jnp.zeros((ROUTER_LANES - N_GROUPS - N_EXPERTS,), F32)])[None, :]
    row = lambda a: a[0][None, :]

    xs_pad = jnp.pad(x_sample, ((0, 0), (0, SAMPLE_PAD - ss), (0, 0))).reshape(bs * SAMPLE_PAD, D_MODEL)
    zm_p, zt_p = _inproj(x_prompt.reshape(tp, D_MODEL), w_in_t, b_in, w_tail_t, b_tail, "inproj_prompt")
    zm_s, zt_s = _inproj(xs_pad, w_in_t, b_in, w_tail_t, b_tail, "inproj_sample")

    kv = _matmul(mem_prompt.reshape(bp * N_MEM, D_MODEL), w_kv_b, KV_TN, "mem_kv")
    mk_p = kv[:, :MEM_WIDTH].reshape(bp, N_MEM, MEM_HEADS, MEM_HEAD_DIM)
    mv_p = kv[:, MEM_WIDTH:].reshape(bp, N_MEM, MEM_HEADS, MEM_HEAD_DIM)

    conv_args = (w_dw[0], row(b_dw), row(g_cn), row(b_cn))
    conv_p, buf_p = _conv_group(zm_p, jnp.zeros((bp, CONV_HIST, CONV_WIDTH), F32), *conv_args,
                                bp, sp, 512, "conv_prompt")
    conv_s, buf_s_t = _conv_step_group(
        zm_s.reshape(bs, SAMPLE_PAD, Z_MAIN_COLS), jnp.transpose(state_conv[0], (1, 0, 2)),
        *conv_args, ss, 32, "conv_sample")
    buf_s = jnp.transpose(buf_s_t, (1, 0, 2))

    m_tile = lambda m: jnp.broadcast_to(
        jnp.pad(m, ((0, 0), (0, SUBLANES - MLSTM_HEADS)))[:, :, None], (m.shape[0], SUBLANES, LANES))
    g_mh_r = row(g_mh)
    seqs = lambda z, b: z.reshape(b, z.shape[0] // b, z.shape[1])
    h_p, c_p, n_p, m_p = _mlstm_group(
        seqs(zm_p, bp), seqs(zt_p, bp),
        jnp.zeros((bp, MLSTM_HEADS, MLSTM_HEAD_DIM, MLSTM_HEAD_DIM), F32),
        jnp.zeros((bp, MLSTM_HEADS, MLSTM_HEAD_DIM), F32), jnp.zeros((bp, SUBLANES, LANES), F32),
        g_mh_r, bp, sp, MLSTM_CHUNK, MLSTM_CHUNK, 2, "mlstm_prompt")
    h_s, c_s, n_s, m_s = _mlstm_group(
        seqs(zm_s, bs), seqs(zt_s, bs), state_mlstm_C[0], state_mlstm_n[0], m_tile(state_mlstm_m[0]),
        g_mh_r, bs, SAMPLE_PAD, SAMPLE_PAD, ss, 8, "mlstm_sample")
    h_p = h_p.reshape(tp, MLSTM_WIDTH)

    mem_p = _memattn_heads(zt_p, kv, bp, sp, sp, "memattn_prompt")
    packed = lambda c: c[0].reshape(bs, N_MEM * MEM_HEADS, MEM_HEAD_DIM)
    mem_s = _memattn_packed(zt_s, packed(cache_mem_k), packed(cache_mem_v), bs, SAMPLE_PAD, 8,
                            "memattn_sample")

    compact = lambda a: a.reshape(bs, SAMPLE_PAD, a.shape[-1])[:, :ss].reshape(ts, a.shape[-1])
    x1, x1_packed, eid, ew = _outproj_router(
        (conv_p, h_p, mem_p, x_prompt.reshape(tp, D_MODEL)),
        (conv_s.reshape(ts, CONV_WIDTH), h_s[:, :ss].reshape(ts, MLSTM_WIDTH), compact(mem_s),
         x_sample.reshape(ts, D_MODEL)),
        w_out_b, row(g_ln1), row(b_ln1), w_r_split, b_r)

    total = tp + ts
    y = _moe(x1_packed, _dispatch_tables(eid, total), w_gate[0], w_up[0], w_down[0])
    g2, b2 = row(g_ln2), row(b_ln2)
    y_p = _final(x1, y, ew, g2, b2, 0, tp, "final_prompt").reshape(bp, sp, D_MODEL)
    y_s = _final(x1, y, ew, g2, b2, tp, ts, "final_sample").reshape(bs, ss, D_MODEL)

    return (y_p, y_s, buf_p[None], buf_s[None], c_p[None], c_s[None], n_p[None], n_s[None],
            m_p[:, :MLSTM_HEADS, 0][None], m_s[:, :MLSTM_HEADS, 0][None], mk_p[None], mv_p[None])
```

```python
import functools

import jax
import jax.numpy as jnp
from jax import lax
from jax.experimental import pallas as pl
from jax.experimental.pallas import tpu as pltpu

F32 = jnp.float32
BF16 = jnp.bfloat16

D_MODEL = 2048
CONV_WIDTH = 512
CONV_K = 31
CONV_HIST = CONV_K - 1
MLSTM_HEADS = 4
MLSTM_HEAD_DIM = 256
MLSTM_WIDTH = MLSTM_HEADS * MLSTM_HEAD_DIM
MLSTM_CHUNK = 128
MEM_HEADS = 4
MEM_HEAD_DIM = 128
MEM_WIDTH = MEM_HEADS * MEM_HEAD_DIM
N_MEM = 256
N_GROUPS = 8
EXPERTS_PER_GROUP = 8
N_EXPERTS = N_GROUPS * EXPERTS_PER_GROUP
D_EXPERT = 512
LN_EPS = 1e-5
DEPTH = 1
ALPHA = (2 * DEPTH) ** 0.25

LANES = 128
SUBLANES = 8
VMEM_LIMIT_BYTES = 56 * 1024 * 1024

Z_CONV_A = 0
Z_CONV_G = CONV_WIDTH
Z_Q = 2 * CONV_WIDTH
Z_K = Z_Q + MLSTM_WIDTH
Z_V = Z_K + MLSTM_WIDTH
Z_O = Z_V + MLSTM_WIDTH
Z_MAIN_COLS = Z_O + MLSTM_WIDTH
N_GATE_COLS = 2 * MLSTM_HEADS
ZT_QM = 0
ZT_GATE = MEM_WIDTH
Z_TAIL_COLS = MEM_WIDTH + LANES
INPROJ_TM = 1024
INPROJ_TN = 1024
KV_TN = 256

SAMPLE_PAD = SUBLANES
MLSTM_STAGE_PAIRS = 8
ROUTER_LANES = LANES
MOE_ROWS = 160
MOE_SLOTS = 5
assert N_EXPERTS >= MOE_SLOTS
OUT_TM = 512
FINAL_TM = 512


def _params(*sem):
    return pltpu.CompilerParams(dimension_semantics=sem, vmem_limit_bytes=VMEM_LIMIT_BYTES)


def _inproj_kernel(x_ref, wm_ref, bm_ref, wt_ref, bt_ref, zm_ref, zt_ref, xb_ref, *, n_main):
    j = pl.program_id(1)

    @pl.when(j == 0)
    def _():
        xb_ref[...] = x_ref[...].astype(BF16)

    nt = (((1,), (1,)), ((), ()))

    @pl.when(j < n_main)
    def _():
        zm_ref[...] = lax.dot_general(xb_ref[...], wm_ref[...], nt,
                                      preferred_element_type=F32) + bm_ref[...]

    @pl.when(j == n_main)
    def _():
        zt_ref[...] = lax.dot_general(xb_ref[...], wt_ref[...], nt,
                                      preferred_element_type=F32) + bt_ref[...]


def _inproj(x, w_in_t, b_in, w_tail_t, b_tail, name):
    t, k = x.shape
    tm, tn = INPROJ_TM, INPROJ_TN
    n_main = Z_MAIN_COLS // tn
    main_col = lambda j: jnp.minimum(j, n_main - 1)
    return pl.pallas_call(
        functools.partial(_inproj_kernel, n_main=n_main),
        out_shape=(jax.ShapeDtypeStruct((t, Z_MAIN_COLS), F32),
                   jax.ShapeDtypeStruct((t, Z_TAIL_COLS), F32)),
        grid=(t // tm, n_main + 1),
        in_specs=[
            pl.BlockSpec((tm, k), lambda i, j: (i, 0)),
            pl.BlockSpec((tn, k), lambda i, j: (main_col(j), 0)),
            pl.BlockSpec((1, tn), lambda i, j: (0, main_col(j))),
            pl.BlockSpec((Z_TAIL_COLS, k), lambda i, j: (0, 0)),
            pl.BlockSpec((1, Z_TAIL_COLS), lambda i, j: (0, 0)),
        ],
        out_specs=(pl.BlockSpec((tm, tn), lambda i, j: (i, main_col(j))),
                   pl.BlockSpec((tm, Z_TAIL_COLS), lambda i, j: (i, 0))),
        scratch_shapes=[pltpu.VMEM((tm, k), BF16)],
        compiler_params=_params("arbitrary", "arbitrary"),
        name=name,
    )(x, w_in_t, b_in, w_tail_t, b_tail)


def _matmul_kernel(x_ref, w_ref, o_ref):
    o_ref[...] = jnp.dot(x_ref[...].astype(BF16), w_ref[...], preferred_element_type=F32)


def _matmul(x, w_bf16, tn, name):
    t, k = x.shape
    n = w_bf16.shape[1]
    return pl.pallas_call(
        _matmul_kernel,
        out_shape=jax.ShapeDtypeStruct((t, n), F32),
        grid=(n // tn,),
        in_specs=[pl.BlockSpec((t, k), lambda j: (0, 0)), pl.BlockSpec((k, tn), lambda j: (0, j))],
        out_specs=pl.BlockSpec((t, tn), lambda j: (0, j)),
        compiler_params=_params("arbitrary"),
        name=name,
    )(x, w_bf16)


def _layer_norm_rows(y, g, b):
    mu = jnp.mean(y, axis=-1, keepdims=True)
    yc = y - mu
    var = jnp.mean(yc * yc, axis=-1, keepdims=True)
    return yc * lax.rsqrt(var + LN_EPS) * g + b


CONV_ROWS = 64


def _conv_kernel(a_ref, g_ref, hist_ref, wdw_ref, bdw_ref, gcn_ref, bcn_ref, out_ref, nb_ref,
                 ubuf, shifted, *, tl):
    head = CONV_HIST + 2
    li = pl.program_id(1)

    @pl.when(li == 0)
    def _():
        ubuf[0:2, :] = jnp.zeros((2, CONV_WIDTH), F32)
        ubuf[2:head, :] = hist_ref[0]

    ubuf[head:head + tl, :] = a_ref[...] * jax.nn.sigmoid(g_ref[...])
    span = shifted.shape[1]
    for k in range(1, SUBLANES):
        shifted[k - 1] = ubuf[k:k + span, :]
    for r0 in range(0, tl, CONV_ROWS):
        acc = jnp.zeros((CONV_ROWS, CONV_WIDTH), F32) + bdw_ref[...]
        for j in range(CONV_K):
            lo, k = divmod(2 + j, SUBLANES)
            lo = lo * SUBLANES + r0
            src = ubuf if k == 0 else shifted.at[k - 1]
            acc = acc + wdw_ref[j:j + 1, :] * src[lo:lo + CONV_ROWS, :]
        y = _layer_norm_rows(acc, gcn_ref[...], bcn_ref[...])
        out_ref[r0:r0 + CONV_ROWS, :] = y * jax.nn.sigmoid(y)

    @pl.when(li == pl.num_programs(1) - 1)
    def _():
        nb_ref[0] = ubuf[2 + tl:head + tl, :]

    ubuf[2:head, :] = ubuf[2 + tl:head + tl, :]


def _conv_group(z, hist, w_dw, b_dw, g_cn, b_cn, batch, seq, tl, name):
    nl = seq // tl
    row = lambda b, l: (b * nl + l, 0)
    vec = pl.BlockSpec((1, CONV_WIDTH), lambda b, l: (0, 0))
    return pl.pallas_call(
        functools.partial(_conv_kernel, tl=tl),
        out_shape=(jax.ShapeDtypeStruct((batch * seq, CONV_WIDTH), F32),
                   jax.ShapeDtypeStruct((batch, CONV_HIST, CONV_WIDTH), F32)),
        grid=(batch, nl),
        in_specs=[
            pl.BlockSpec((tl, CONV_WIDTH), lambda b, l: (b * nl + l, Z_CONV_A // CONV_WIDTH)),
            pl.BlockSpec((tl, CONV_WIDTH), lambda b, l: (b * nl + l, Z_CONV_G // CONV_WIDTH)),
            pl.BlockSpec((1, CONV_HIST, CONV_WIDTH), lambda b, l: (b, 0, 0)),
            pl.BlockSpec((CONV_K, CONV_WIDTH), lambda b, l: (0, 0)),
            vec, vec, vec,
        ],
        out_specs=(pl.BlockSpec((tl, CONV_WIDTH), row),
                   pl.BlockSpec((1, CONV_HIST, CONV_WIDTH), lambda b, l: (b, 0, 0))),
        scratch_shapes=[pltpu.VMEM((CONV_HIST + 2 + tl, CONV_WIDTH), F32),
                        pltpu.VMEM((SUBLANES - 1, CONV_HIST + 2 + tl - SUBLANES, CONV_WIDTH), F32)],
        compiler_params=_params("arbitrary", "arbitrary"),
        name=name,
    )(z, z, hist, w_dw, b_dw, g_cn, b_cn)


def _conv_step_kernel(a_ref, g_ref, hist_ref, wdw_ref, bdw_ref, gcn_ref, bcn_ref, out_ref, nb_ref,
                      *, steps):
    u = [a_ref[:, t, :] * jax.nn.sigmoid(g_ref[:, t, :]) for t in range(steps)]
    full = lambda r: hist_ref[r] if r < CONV_HIST else u[r - CONV_HIST]
    for t in range(steps):
        acc = bdw_ref[...] + wdw_ref[0:1, :] * full(t)
        for j in range(1, CONV_K):
            acc = acc + wdw_ref[j:j + 1, :] * full(t + j)
        y = _layer_norm_rows(acc, gcn_ref[...], bcn_ref[...])
        out_ref[:, t, :] = y * jax.nn.sigmoid(y)
    for r in range(CONV_HIST):
        nb_ref[r] = full(r + steps)


def _conv_step_group(z3, hist_t, w_dw, b_dw, g_cn, b_cn, steps, bb, name):
    batch = z3.shape[0]
    vec = pl.BlockSpec((1, CONV_WIDTH), lambda b: (0, 0))
    hist_spec = pl.BlockSpec((CONV_HIST, bb, CONV_WIDTH), lambda b: (0, b, 0))
    return pl.pallas_call(
        functools.partial(_conv_step_kernel, steps=steps),
        out_shape=(jax.ShapeDtypeStruct((batch, steps, CONV_WIDTH), F32),
                   jax.ShapeDtypeStruct((CONV_HIST, batch, CONV_WIDTH), F32)),
        grid=(batch // bb,),
        in_specs=[
            pl.BlockSpec((bb, SAMPLE_PAD, CONV_WIDTH), lambda b: (b, 0, Z_CONV_A // CONV_WIDTH)),
            pl.BlockSpec((bb, SAMPLE_PAD, CONV_WIDTH), lambda b: (b, 0, Z_CONV_G // CONV_WIDTH)),
            hist_spec,
            pl.BlockSpec((CONV_K, CONV_WIDTH), lambda b: (0, 0)),
            vec, vec, vec,
        ],
        out_specs=(pl.BlockSpec((bb, steps, CONV_WIDTH), lambda b: (b, 0, 0)), hist_spec),
        compiler_params=_params("arbitrary"),
        name=name,
    )(z3, z3, hist_t, w_dw, b_dw, g_cn, b_cn)


def _col_to_row(col, eye):
    n = col.shape[0]
    return jnp.sum(jnp.where(eye, jnp.broadcast_to(col, (n, n)), 0.0), axis=0, keepdims=True)


def _mlstm_kernel(q_ref, k_ref, v_ref, o_ref, gate_ref, c0_ref, n0_ref, m0_ref, gmh_ref,
                  h_ref, c_ref, n_ref, m_ref, *, cl, valid, bb, single_chunk):
    if single_chunk:
        c_in, n_in, m_in = c0_ref, n0_ref, m0_ref
    else:
        c_in, n_in, m_in = c_ref, n_ref, m_ref

        @pl.when(pl.program_id(1) == 0)
        def _():
            c_ref[...] = c0_ref[...]
            n_ref[...] = n0_ref[...]
            m_ref[...] = m0_ref[...]

    rows = lax.broadcasted_iota(jnp.int32, (cl, cl), 0)
    cols = lax.broadcasted_iota(jnp.int32, (cl, cl), 1)
    eye = rows == cols
    tril = rows >= cols
    row_id = lax.broadcasted_iota(jnp.int32, (cl, 1), 0)
    nh, dh = MLSTM_HEADS, MLSTM_HEAD_DIM
    nt_dims = (((1,), (1,)), ((), ()))
    tn_dims = (((0,), (0,)), ((), ()))
    each = lambda f, *lists: [f(*args) for args in zip(*lists)]
    all_pairs = [(bi, h) for bi in range(bb) for h in range(nh)]
    for g0 in range(0, len(all_pairs), MLSTM_STAGE_PAIRS):
        pairs = all_pairs[g0:g0 + MLSTM_STAGE_PAIRS]
        cols_of = [slice(h * dh, (h + 1) * dh) for _, h in pairs]
        gates = {bi: gate_ref[bi] for bi in sorted({bi for bi, _ in pairs})}
        log_sig = {bi: jnp.minimum(g, 0.0) - jnp.log1p(jnp.exp(-jnp.abs(g))) for bi, g in gates.items()}
        ip = each(lambda p: gates[p[0]][:, p[1]:p[1] + 1], pairs)
        lf = each(lambda p: log_sig[p[0]][:, nh + p[1]:nh + p[1] + 1], pairs)
        if valid < cl:
            ip = each(lambda x: jnp.where(row_id < valid, x, -jnp.inf), ip)
            lf = each(lambda x: jnp.where(row_id < valid, x, 0.0), lf)
        lf_row = each(lambda x: _col_to_row(x, eye), lf)
        ip_row = each(lambda x: _col_to_row(x, eye), ip)
        b_col = each(lambda r: jnp.sum(jnp.where(tril, jnp.broadcast_to(r, (cl, cl)), 0.0),
                                       axis=1, keepdims=True), lf_row)
        b_row = each(lambda x: _col_to_row(x, eye), b_col)
        m_prev = each(lambda p: m_in[p[0], p[1]:p[1] + 1, 0:1], pairs)
        log_inter = each(lambda b, m: b + m, b_col, m_prev)
        log_intra = each(lambda bc, br, ir: jnp.where(tril, bc - br + ir, -jnp.inf), b_col, b_row, ip_row)
        m_t = each(lambda le, la: jnp.maximum(le, jnp.max(la, axis=1, keepdims=True)), log_inter, log_intra)
        w_inter = each(lambda le, m: jnp.exp(le - m), log_inter, m_t)
        intra = each(lambda la, m: jnp.exp(la - m), log_intra, m_t)
        m_new = each(lambda m: m[cl - 1:cl, :], m_t)
        b_last = each(lambda b: b[cl - 1:cl, :], b_col)
        decay = each(lambda bl, mp, mn: jnp.exp(bl + mp - mn), b_last, m_prev, m_new)
        w_s = each(lambda bl, b, i, mn: jnp.exp(bl - b + i - mn), b_last, b_col, ip, m_new)
        q = each(lambda p, sl: q_ref[p[0], :, sl], pairs, cols_of)
        k = each(lambda p, sl: k_ref[p[0], :, sl] * (dh ** -0.5), pairs, cols_of)
        vb = each(lambda p, sl: v_ref[p[0], :, sl].astype(BF16), pairs, cols_of)
        qb = each(lambda x: x.astype(BF16), q)
        kb = each(lambda x: x.astype(BF16), k)
        s = each(lambda a, b: lax.dot_general(a, b, nt_dims, preferred_element_type=F32), qb, kb)
        s = each(lambda x, d: x * d, s, intra)
        c_old = each(lambda p: c_in[p[0], p[1]], pairs)
        n_old = each(lambda p: n_in[p[0], p[1]:p[1] + 1, :], pairs)
        inter = each(lambda a, c: jnp.dot(a, c.astype(BF16), preferred_element_type=F32), qb, c_old)
        local = each(lambda x, v: jnp.dot(x.astype(BF16), v, preferred_element_type=F32), s, vb)
        num = each(lambda w, a, b: w * a + b, w_inter, inter, local)
        den = each(lambda w, x, n, ss: w * jnp.sum(x * n, axis=1, keepdims=True)
                   + jnp.sum(ss, axis=1, keepdims=True), w_inter, q, n_old, s)
        hh = each(lambda a, d, m: a / jnp.maximum(jnp.abs(d), jnp.exp(-m)), num, den, m_t)
        kw = each(lambda x, w: x * w, k, w_s)
        outer = each(lambda a, v: lax.dot_general(a.astype(BF16), v, tn_dims, preferred_element_type=F32),
                     kw, vb)
        for i, (bi, h) in enumerate(pairs):
            c_ref[bi, h] = decay[i] * c_old[i] + outer[i]
            n_ref[bi, h:h + 1, :] = decay[i] * n_old[i] + jnp.sum(kw[i], axis=0, keepdims=True)
            m_ref[bi, h:h + 1, :] = jnp.broadcast_to(m_new[i], (1, LANES))
        mu = each(lambda x: jnp.mean(x, axis=-1, keepdims=True), hh)
        hc = each(lambda x, m: x - m, hh, mu)
        var = each(lambda x: jnp.mean(x * x, axis=-1, keepdims=True), hc)
        for i, (bi, h) in enumerate(pairs):
            sl = cols_of[i]
            hn = hc[i] * lax.rsqrt(var[i] + LN_EPS) * gmh_ref[:, sl]
            h_ref[bi, :, sl] = hn * jax.nn.sigmoid(o_ref[bi, :, sl])
    if single_chunk:
        m_ref[:, MLSTM_HEADS:, :] = jnp.zeros((bb, SUBLANES - MLSTM_HEADS, LANES), F32)


def _mlstm_group(z_main, z_tail, c0, n0, m0, g_mh, batch, seq, cl, valid, bb, name):
    nc = seq // cl
    zcol = lambda off: pl.BlockSpec((bb, cl, MLSTM_WIDTH), lambda b, c: (b, c, off // MLSTM_WIDTH))
    state = lambda shape: pl.BlockSpec((bb,) + shape, lambda b, c: (b,) + (0,) * len(shape))
    c_shape = (MLSTM_HEADS, MLSTM_HEAD_DIM, MLSTM_HEAD_DIM)
    n_shape = (MLSTM_HEADS, MLSTM_HEAD_DIM)
    m_shape = (SUBLANES, LANES)
    return pl.pallas_call(
        functools.partial(_mlstm_kernel, cl=cl, valid=valid, bb=bb, single_chunk=nc == 1),
        out_shape=(jax.ShapeDtypeStruct((batch, seq, MLSTM_WIDTH), F32),
                   jax.ShapeDtypeStruct((batch,) + c_shape, F32),
                   jax.ShapeDtypeStruct((batch,) + n_shape, F32),
                   jax.ShapeDtypeStruct((batch,) + m_shape, F32)),
        grid=(batch // bb, nc),
        in_specs=[
            zcol(Z_Q), zcol(Z_K), zcol(Z_V), zcol(Z_O),
            pl.BlockSpec((bb, cl, LANES), lambda b, c: (b, c, ZT_GATE // LANES)),
            state(c_shape), state(n_shape), state(m_shape),
            pl.BlockSpec((1, MLSTM_WIDTH), lambda b, c: (0, 0)),
        ],
        out_specs=(pl.BlockSpec((bb, cl, MLSTM_WIDTH), lambda b, c: (b, c, 0)),
                   state(c_shape), state(n_shape), state(m_shape)),
        compiler_params=_params("arbitrary", "arbitrary"),
        name=name,
    )(z_main, z_main, z_main, z_main, z_tail, c0, n0, m0, g_mh)


def _softmax_rows(s):
    e = jnp.exp(s - jnp.max(s, axis=-1, keepdims=True))
    return e / jnp.sum(e, axis=-1, keepdims=True)


_NT_DIMS = (((1,), (1,)), ((), ()))


def _memattn_head_kernel(q_ref, k_ref, v_ref, o_ref):
    s = lax.dot_general(q_ref[...].astype(BF16), k_ref[...].astype(BF16), _NT_DIMS,
                        preferred_element_type=F32) * (MEM_HEAD_DIM ** -0.5)
    p = _softmax_rows(s)
    o_ref[...] = jnp.dot(p.astype(BF16), v_ref[...].astype(BF16), preferred_element_type=F32)


def _memattn_heads(z_tail, kv, batch, seq, tq, name):
    nq = seq // tq
    dh = MEM_HEAD_DIM
    return pl.pallas_call(
        _memattn_head_kernel,
        out_shape=jax.ShapeDtypeStruct((batch * seq, MEM_WIDTH), F32),
        grid=(batch, MEM_HEADS, nq),
        in_specs=[pl.BlockSpec((tq, dh), lambda b, h, i: (b * nq + i, ZT_QM // dh + h)),
                  pl.BlockSpec((N_MEM, dh), lambda b, h, i: (b, h)),
                  pl.BlockSpec((N_MEM, dh), lambda b, h, i: (b, MEM_HEADS + h))],
        out_specs=pl.BlockSpec((tq, dh), lambda b, h, i: (b * nq + i, h)),
        compiler_params=_params("arbitrary", "arbitrary", "arbitrary"),
        name=name,
    )(z_tail, kv, kv)


def _memattn_packed_kernel(q_ref, k_ref, v_ref, o_ref, *, tq, bb):
    nh, dh = MEM_HEADS, MEM_HEAD_DIM
    shape = (nh * tq, N_MEM * nh)
    row_head = lax.broadcasted_iota(jnp.int32, shape, 0) // tq
    col_head = lax.broadcasted_iota(jnp.int32, shape, 1) % nh
    same_head = row_head == col_head
    scores = []
    for bi in range(bb):
        rs = slice(bi * tq, (bi + 1) * tq)
        q = jnp.concatenate([q_ref[rs, h * dh:(h + 1) * dh] for h in range(nh)], axis=0)
        scores.append(lax.dot_general(q.astype(BF16), k_ref[bi].astype(BF16), _NT_DIMS,
                                      preferred_element_type=F32) * (dh ** -0.5))
    probs = [_softmax_rows(jnp.where(same_head, s, -jnp.inf)).astype(BF16) for s in scores]
    outs = [jnp.dot(p, v_ref[bi].astype(BF16), preferred_element_type=F32) for bi, p in enumerate(probs)]
    for bi, o in enumerate(outs):
        rs = slice(bi * tq, (bi + 1) * tq)
        for h in range(nh):
            o_ref[rs, h * dh:(h + 1) * dh] = o[h * tq:(h + 1) * tq, :]


def _memattn_packed(z_tail, mk, mv, batch, tq, bb, name):
    rows = bb * tq
    kv = pl.BlockSpec((bb, N_MEM * MEM_HEADS, MEM_HEAD_DIM), lambda b: (b, 0, 0))
    return pl.pallas_call(
        functools.partial(_memattn_packed_kernel, tq=tq, bb=bb),
        out_shape=jax.ShapeDtypeStruct((batch * tq, MEM_WIDTH), F32),
        grid=(batch // bb,),
        in_specs=[pl.BlockSpec((rows, MEM_WIDTH), lambda b: (b, ZT_QM // MEM_WIDTH)), kv, kv],
        out_specs=pl.BlockSpec((rows, MEM_WIDTH), lambda b: (b, 0)),
        compiler_params=_params("arbitrary"),
        name=name,
    )(z_tail, mk, mv)


def _route(logits):
    lane = lax.broadcasted_iota(jnp.int32, logits.shape, 1).astype(F32)
    neg = -jnp.inf
    first = lambda mask: jnp.min(jnp.where(mask, lane, float(LANES)), axis=1, keepdims=True)
    is_g = lane < N_GROUPS
    gl = jnp.where(is_g, logits, neg)
    g_max = jnp.max(gl, axis=1, keepdims=True)
    g_sel = first(gl == g_max)
    g_w = 1.0 / jnp.sum(jnp.exp(gl - g_max), axis=1, keepdims=True)
    lo = N_GROUPS + g_sel * EXPERTS_PER_GROUP
    in_grp = (lane >= lo) & (lane < lo + EXPERTS_PER_GROUP)
    el = jnp.where(in_grp, logits, neg)
    v1 = jnp.max(el, axis=1, keepdims=True)
    i1 = first(in_grp & (el == v1))
    rest = in_grp & (lane != i1)
    el2 = jnp.where(rest, logits, neg)
    v2 = jnp.max(el2, axis=1, keepdims=True)
    i2 = first(rest & (el2 == v2))
    t = jnp.exp(v2 - v1)
    w1 = g_w / (1.0 + t)
    w2 = g_w * t / (1.0 + t)
    lane_i = lax.broadcasted_iota(jnp.int32, logits.shape, 1)
    e1 = (i1 - N_GROUPS).astype(jnp.int32)
    e2 = (i2 - N_GROUPS).astype(jnp.int32)
    eid = jnp.where(lane_i == 0, e1, jnp.where(lane_i == 1, e2, 0))
    ew = jnp.where(lane_i == 0, w1, jnp.where(lane_i == 1, w2, 0.0))
    return eid, ew


def _outproj_kernel(cp, hp, mp, xp, cs, hs, ms, xs, wout, g1, b1, wr, br,
                    x1_ref, x1p_ref, eid_ref, ew_ref, *, n_prompt):
    def rows(c, h, m, x, rs):
        groups = jnp.concatenate(
            [c[rs, :].astype(BF16), h[rs, :].astype(BF16), m[rs, :].astype(BF16)], axis=1)
        mix = jnp.dot(groups, wout[...], preferred_element_type=F32)
        x1 = _layer_norm_rows(ALPHA * x[rs, :] + mix, g1[...], b1[...])
        x1_ref[rs, :] = x1
        xh = x1.astype(BF16)
        xh_f32 = xh.astype(F32)
        bits = pltpu.bitcast(xh_f32, jnp.int32)
        x1p_ref[rs, :] = bits[:, :D_MODEL // 2] | lax.shift_right_logical(bits[:, D_MODEL // 2:], 16)
        xl = (x1 - xh_f32).astype(BF16)
        n = x1.shape[0]
        cross = jnp.dot(jnp.concatenate([xh, xl], axis=0), wr[...], preferred_element_type=F32)
        logits = ((cross[:n, :ROUTER_LANES] + cross[:n, ROUTER_LANES:])
                  + (cross[n:, :ROUTER_LANES] + cross[n:, ROUTER_LANES:])) + br[...]
        eid, ew = _route(logits)
        eid_ref[rs, :] = eid
        ew_ref[rs, :] = ew

    def body(c, h, m, x):
        rows(c, h, m, x, slice(None))

    i = pl.program_id(0)
    pl.when(i < n_prompt)(lambda: body(cp, hp, mp, xp))
    pl.when(i >= n_prompt)(lambda: body(cs, hs, ms, xs))


def _outproj_router(prompt, sample, w_out_b, g1, b1, wr, br):
    tm = OUT_TM
    tp = prompt[0].shape[0]
    ts = sample[0].shape[0]
    n_p, n_s = tp // tm, ts // tm
    total = tp + ts
    widths = (CONV_WIDTH, MLSTM_WIDTH, MEM_WIDTH, D_MODEL)
    p_specs = [pl.BlockSpec((tm, w), lambda i: (jnp.minimum(i, n_p - 1), 0)) for w in widths]
    once = pl.Buffered(1)
    s_specs = [pl.BlockSpec((tm, w), lambda i: (jnp.maximum(i - n_p, 0), 0),
                            pipeline_mode=once if n_s == 1 else None) for w in widths]
    full = lambda shape: pl.BlockSpec(shape, lambda i: (0, 0), pipeline_mode=once)
    row = lambda w: pl.BlockSpec((tm, w), lambda i: (i, 0))
    return pl.pallas_call(
        functools.partial(_outproj_kernel, n_prompt=n_p),
        out_shape=(jax.ShapeDtypeStruct((total, D_MODEL), F32),
                   jax.ShapeDtypeStruct((total, D_MODEL // 2), jnp.int32),
                   jax.ShapeDtypeStruct((total, ROUTER_LANES), jnp.int32),
                   jax.ShapeDtypeStruct((total, ROUTER_LANES), F32)),
        grid=(n_p + n_s,),
        in_specs=p_specs + s_specs + [
            full((D_MODEL, D_MODEL)), full((1, D_MODEL)), full((1, D_MODEL)),
            full((D_MODEL, 2 * ROUTER_LANES)), full((1, ROUTER_LANES)),
        ],
        out_specs=(row(D_MODEL), row(D_MODEL // 2), row(ROUTER_LANES), row(ROUTER_LANES)),
        compiler_params=_params("arbitrary"),
        name="outproj_router",
    )(*prompt, *sample, w_out_b, g1, b1, wr, br)


def _moe_kernel(stok_ref, sdst_ref, base_ref, nval_ref, first_ref, count_ref,
                x_hbm, wg_ref, wu_ref, wd_ref, y_hbm, wgb, wub, wdb, xbuf, ybuf, gsem, ssem,
                *, dump_row):
    e = pl.program_id(0)
    last = pl.num_programs(0) - 1
    rc = MOE_ROWS
    ns = MOE_SLOTS

    def gather_start(base, slot):
        for r in range(rc):
            pltpu.make_async_copy(x_hbm.at[pl.ds(stok_ref[base + r], 1)],
                                  xbuf.at[slot, pl.ds(r, 1)], gsem.at[slot]).start()

    def gather_wait(slot):
        pltpu.make_async_copy(x_hbm.at[pl.ds(0, rc)], xbuf.at[slot], gsem.at[slot]).wait()

    def scatter_start(base, n_valid, slot):
        for r in range(rc):
            dst = jnp.where(r < n_valid, sdst_ref[base + r], dump_row + slot * rc + r)
            pltpu.make_async_copy(ybuf.at[slot, pl.ds(r, 1)], y_hbm.at[pl.ds(dst, 1)],
                                  ssem.at[slot]).start()

    def scatter_wait(slot):
        pltpu.make_async_copy(ybuf.at[slot], y_hbm.at[pl.ds(0, rc)], ssem.at[slot]).wait()

    @pl.when(e == 0)
    def _():
        ybuf[...] = jnp.zeros(ybuf.shape, F32)
        for slot in range(ns):
            pltpu.make_async_copy(ybuf.at[slot], y_hbm.at[pl.ds(dump_row + slot * rc, rc)],
                                  ssem.at[slot]).start()
            scatter_wait(slot)
        for ahead in range(ns - 1):
            gather_start(base_ref[1 + ahead], ahead)

    wgb[...] = wg_ref[0].astype(BF16)
    wub[...] = wu_ref[0].astype(BF16)
    wdb[...] = wd_ref[0].astype(BF16)
    first = first_ref[e]

    def chunk_on(slot, g):
        prv = (slot - 1) % ns
        gather_wait(slot)

        @pl.when(g >= ns - 1)
        def _():
            scatter_wait(slot)

        gather_start(base_ref[g + ns], prv)
        scatter_start(base_ref[g], nval_ref[g], prv)
        packed = xbuf[slot]
        x = jnp.concatenate(
            [pltpu.bitcast(packed & jnp.int32(-65536), F32).astype(BF16),
             pltpu.bitcast(packed << 16, F32).astype(BF16)], axis=1)
        hg = jnp.dot(x, wgb[...], preferred_element_type=F32)
        hu = jnp.dot(x, wub[...], preferred_element_type=F32)
        hid = (hg * jax.nn.sigmoid(hg) * hu).astype(BF16)
        ybuf[slot] = jnp.dot(hid, wdb[...], preferred_element_type=F32)

    def chunk(c, carry):
        g = first + c
        cur = lax.rem(g, ns)
        for slot in range(ns):
            pl.when(cur == slot)(functools.partial(chunk_on, slot, g))
        return carry

    lax.fori_loop(0, count_ref[e], chunk, 0)

    @pl.when(e == last)
    def _():
        g_end = first + count_ref[e]
        cur = lax.rem(g_end, ns)
        for slot in range(ns):
            @pl.when(cur == slot)
            def _():
                for ahead in range(ns - 1):
                    gather_wait((slot + ahead) % ns)
                scatter_start(base_ref[g_end], nval_ref[g_end], (slot - 1) % ns)

        for slot in range(ns):
            scatter_wait(slot)


def _moe(x1, tables, w_gate, w_up, w_down):
    total = x1.shape[0]
    dump_row = 2 * total
    wspec = lambda shape: pl.BlockSpec((1,) + shape, lambda e, *_: (e, 0, 0))
    grid_spec = pltpu.PrefetchScalarGridSpec(
        num_scalar_prefetch=len(tables),
        grid=(N_EXPERTS,),
        in_specs=[pl.BlockSpec(memory_space=pl.ANY),
                  wspec((D_MODEL, D_EXPERT)), wspec((D_MODEL, D_EXPERT)), wspec((D_EXPERT, D_MODEL))],
        out_specs=pl.BlockSpec(memory_space=pl.ANY),
        scratch_shapes=[
            pltpu.VMEM((D_MODEL, D_EXPERT), BF16), pltpu.VMEM((D_MODEL, D_EXPERT), BF16),
            pltpu.VMEM((D_EXPERT, D_MODEL), BF16),
            pltpu.VMEM((MOE_SLOTS, MOE_ROWS, D_MODEL // 2), jnp.int32),
            pltpu.VMEM((MOE_SLOTS, MOE_ROWS, D_MODEL), F32),
            pltpu.SemaphoreType.DMA((MOE_SLOTS,)), pltpu.SemaphoreType.DMA((MOE_SLOTS,)),
        ],
    )
    return pl.pallas_call(
        functools.partial(_moe_kernel, dump_row=dump_row),
        out_shape=jax.ShapeDtypeStruct((2 * total + MOE_SLOTS * MOE_ROWS, D_MODEL), F32),
        grid_spec=grid_spec,
        compiler_params=_params("arbitrary"),
        name="moe_experts",
    )(*tables, x1, w_gate, w_up, w_down)


def _final_kernel(x1_ref, y0_ref, y1_ref, ew_ref, g2, b2, o_ref):
    ew = ew_ref[...]
    ffn = ew[:, 0:1] * y0_ref[...] + ew[:, 1:2] * y1_ref[...]
    o_ref[...] = _layer_norm_rows(ALPHA * x1_ref[...] + ffn, g2[...], b2[...])


def _final(x1, y, ew, g2, b2, row0, rows, name):
    tm = FINAL_TM
    total = x1.shape[0]
    off = row0 // tm
    k1 = total // tm
    vec = pl.BlockSpec((1, D_MODEL), lambda i: (0, 0))
    return pl.pallas_call(
        _final_kernel,
        out_shape=jax.ShapeDtypeStruct((rows, D_MODEL), F32),
        grid=(rows // tm,),
        in_specs=[pl.BlockSpec((tm, D_MODEL), lambda i: (i + off, 0)),
                  pl.BlockSpec((tm, D_MODEL), lambda i: (i + off, 0)),
                  pl.BlockSpec((tm, D_MODEL), lambda i: (i + off + k1, 0)),
                  pl.BlockSpec((tm, ROUTER_LANES), lambda i: (i + off, 0)),
                  vec, vec],
        out_specs=pl.BlockSpec((tm, D_MODEL), lambda i: (i, 0)),
        compiler_params=_params("arbitrary"),
        name=name,
    )(x1, y, y, ew, g2, b2)


def _dispatch_tables(eid, total):
    flat_e = eid[:, :2].reshape(-1)
    order = jnp.argsort(flat_e, stable=True).astype(jnp.int32)
    stok = order >> 1
    sdst = (order & 1) * total + stok
    experts = jnp.arange(N_EXPERTS, dtype=jnp.int32)
    cnt = jnp.sum((flat_e[:, None] == experts[None, :]).astype(jnp.int32), axis=0)
    start = jnp.cumsum(cnt) - cnt
    pad = jnp.zeros((MOE_ROWS,), jnp.int32)
    n_chunks = jnp.maximum((cnt + MOE_ROWS - 1) // MOE_ROWS, 1)
    first = jnp.cumsum(n_chunks) - n_chunks
    n_entries = flat_e.shape[0] // MOE_ROWS + N_EXPERTS + MOE_SLOTS + 1
    g = jnp.arange(n_entries, dtype=jnp.int32) - 1
    owner = jnp.sum((g[:, None] >= (first + n_chunks)[None, :]).astype(jnp.int32), axis=1)
    real = (g >= 0) & (owner < N_EXPERTS)
    pick = (owner[:, None] == experts[None, :]).astype(jnp.int32)
    sel = lambda v: jnp.sum(pick * v[None, :], axis=1)
    local = (g - sel(first)) * MOE_ROWS
    base = jnp.where(real, sel(start) + local, 0)
    n_valid = jnp.where(real, sel(cnt) - local, 0)
    return (jnp.concatenate([stok, pad]), jnp.concatenate([sdst, pad]), base, n_valid, first, n_chunks)


def kernel(x_prompt, x_sample, mem_prompt, state_conv, state_mlstm_C, state_mlstm_n, state_mlstm_m,
           cache_mem_k, cache_mem_v, w_in, b_in, w_dw, b_dw, g_cn, b_cn, g_mh, w_mk, w_mv, w_out,
           g_ln1, b_ln1, w_rg, b_rg, w_re, b_re, w_gate, w_up, w_down, g_ln2, b_ln2):
    bp, sp, _ = x_prompt.shape
    bs, ss, _ = x_sample.shape
    tp, ts = bp * sp, bs * ss

    gate_hi = Z_MAIN_COLS + N_GATE_COLS
    tail = lambda wt: jnp.concatenate(
        [wt[gate_hi:], wt[Z_MAIN_COLS:gate_hi],
         jnp.zeros((LANES - N_GATE_COLS,) + wt.shape[1:], wt.dtype)], axis=0)
    w_in_t = jnp.transpose(w_in[0]).astype(BF16)
    w_tail_t = tail(w_in_t)
    b_tail = tail(b_in[0])[None, :]
    w_kv_b = jnp.concatenate([w_mk[0], w_mv[0]], axis=-1).astype(BF16)
    w_out_b = w_out[0].astype(BF16)
    w_r = jnp.concatenate([w_rg[0], w_re[0],
                           jnp.zeros((D_MODEL, ROUTER_LANES - N_GROUPS - N_EXPERTS), F32)], axis=-1)
    w_r_hi = w_r.astype(BF16)
    w_r_lo = (w_r - w_r_hi.astype(F32)).astype(BF16)
    w_r_split = jnp.concatenate([w_r_hi, w_r_lo], axis=1)
    b_r = jnp.concatenate([b_rg[0], b_re[0],
                           jnp.zeros((ROUTER_LANES - N_GROUPS - N_EXPERTS,), F32)])[None, :]
    row = lambda a: a[0][None, :]

    xs_pad = jnp.pad(x_sample, ((0, 0), (0, SAMPLE_PAD - ss), (0, 0))).reshape(bs * SAMPLE_PAD, D_MODEL)
    zm_p, zt_p = _inproj(x_prompt.reshape(tp, D_MODEL), w_in_t, b_in, w_tail_t, b_tail, "inproj_prompt")
    zm_s, zt_s = _inproj(xs_pad, w_in_t, b_in, w_tail_t, b_tail, "inproj_sample")

    kv = _matmul(mem_prompt.reshape(bp * N_MEM, D_MODEL), w_kv_b, KV_TN, "mem_kv")
    mk_p = kv[:, :MEM_WIDTH].reshape(bp, N_MEM, MEM_HEADS, MEM_HEAD_DIM)
    mv_p = kv[:, MEM_WIDTH:].reshape(bp, N_MEM, MEM_HEADS, MEM_HEAD_DIM)

    conv_args = (w_dw[0], row(b_dw), row(g_cn), row(b_cn))
    conv_p, buf_p = _conv_group(zm_p, jnp.zeros((bp, CONV_HIST, CONV_WIDTH), F32), *conv_args,
                                bp, sp, 512, "conv_prompt")
    conv_s, buf_s_t = _conv_step_group(
        zm_s.reshape(bs, SAMPLE_PAD, Z_MAIN_COLS), jnp.transpose(state_conv[0], (1, 0, 2)),
        *conv_args, ss, 32, "conv_sample")
    buf_s = jnp.transpose(buf_s_t, (1, 0, 2))

    m_tile = lambda m: jnp.broadcast_to(
        jnp.pad(m, ((0, 0), (0, SUBLANES - MLSTM_HEADS)))[:, :, None], (m.shape[0], SUBLANES, LANES))
    g_mh_r = row(g_mh)
    seqs = lambda z, b: z.reshape(b, z.shape[0] // b, z.shape[1])
    h_p, c_p, n_p, m_p = _mlstm_group(
        seqs(zm_p, bp), seqs(zt_p, bp),
        jnp.zeros((bp, MLSTM_HEADS, MLSTM_HEAD_DIM, MLSTM_HEAD_DIM), F32),
        jnp.zeros((bp, MLSTM_HEADS, MLSTM_HEAD_DIM), F32), jnp.zeros((bp, SUBLANES, LANES), F32),
        g_mh_r, bp, sp, MLSTM_CHUNK, MLSTM_CHUNK, 2, "mlstm_prompt")
    h_s, c_s, n_s, m_s = _mlstm_group(
        seqs(zm_s, bs), seqs(zt_s, bs), state_mlstm_C[0], state_mlstm_n[0], m_tile(state_mlstm_m[0]),
        g_mh_r, bs, SAMPLE_PAD, SAMPLE_PAD, ss, 8, "mlstm_sample")
    h_p = h_p.reshape(tp, MLSTM_WIDTH)

    mem_p = _memattn_heads(zt_p, kv, bp, sp, sp, "memattn_prompt")
    packed = lambda c: c[0].reshape(bs, N_MEM * MEM_HEADS, MEM_HEAD_DIM)
    mem_s = _memattn_packed(zt_s, packed(cache_mem_k), packed(cache_mem_v), bs, SAMPLE_PAD, 8,
                            "memattn_sample")

    compact = lambda a: a.reshape(bs, SAMPLE_PAD, a.shape[-1])[:, :ss].reshape(ts, a.shape[-1])
    x1, x1_packed, eid, ew = _outproj_router(
        (conv_p, h_p, mem_p, x_prompt.reshape(tp, D_MODEL)),
        (conv_s.reshape(ts, CONV_WIDTH), h_s[:, :ss].reshape(ts, MLSTM_WIDTH), compact(mem_s),
         x_sample.reshape(ts, D_MODEL)),
        w_out_b, row(g_ln1), row(b_ln1), w_r_split, b_r)

    total = tp + ts
    y = _moe(x1_packed, _dispatch_tables(eid, total), w_gate[0], w_up[0], w_down[0])
    g2, b2 = row(g_ln2), row(b_ln2)
    y_p = _final(x1, y, ew, g2, b2, 0, tp, "final_prompt").reshape(bp, sp, D_MODEL)
    y_s = _final(x1, y, ew, g2, b2, tp, ts, "final_sample").reshape(bs, ss, D_MODEL)

    return (y_p, y_s, buf_p[None], buf_s[None], c_p[None], c_s[None], n_p[None], n_s[None],
            m_p[:, :MLSTM_HEADS, 0][None], m_s[:, :MLSTM_HEADS, 0][None], mk_p[None], mv_p[None])
```

```python
import functools

import jax
import jax.numpy as jnp
from jax import lax
from jax.experimental import pallas as pl
from jax.experimental.pallas import tpu as pltpu

F32 = jnp.float32
BF16 = jnp.bfloat16

D_MODEL = 2048
CONV_WIDTH = 512
CONV_K = 31
CONV_HIST = CONV_K - 1
MLSTM_HEADS = 4
MLSTM_HEAD_DIM = 256
MLSTM_WIDTH = MLSTM_HEADS * MLSTM_HEAD_DIM
MLSTM_CHUNK = 128
MEM_HEADS = 4
MEM_HEAD_DIM = 128
MEM_WIDTH = MEM_HEADS * MEM_HEAD_DIM
N_MEM = 256
N_GROUPS = 8
EXPERTS_PER_GROUP = 8
N_EXPERTS = N_GROUPS * EXPERTS_PER_GROUP
D_EXPERT = 512
LN_EPS = 1e-5
DEPTH = 1
ALPHA = (2 * DEPTH) ** 0.25

LANES = 128
SUBLANES = 8
VMEM_LIMIT_BYTES = 56 * 1024 * 1024

Z_CONV_A = 0
Z_CONV_G = CONV_WIDTH
Z_Q = 2 * CONV_WIDTH
Z_K = Z_Q + MLSTM_WIDTH
Z_V = Z_K + MLSTM_WIDTH
Z_O = Z_V + MLSTM_WIDTH
Z_MAIN_COLS = Z_O + MLSTM_WIDTH
N_GATE_COLS = 2 * MLSTM_HEADS
ZT_QM = 0
ZT_GATE = MEM_WIDTH
Z_TAIL_COLS = MEM_WIDTH + LANES
INPROJ_TM = 1024
INPROJ_TN = 1024
KV_TN = 256

SAMPLE_PAD = SUBLANES
MLSTM_STAGE_PAIRS = 8
ROUTER_LANES = LANES
MOE_ROWS = 320
MOE_SLOTS = 4
assert N_EXPERTS >= MOE_SLOTS
OUT_TM = 512
FINAL_TM = 512


def _params(*sem):
    return pltpu.CompilerParams(dimension_semantics=sem, vmem_limit_bytes=VMEM_LIMIT_BYTES)


def _inproj_kernel(x_ref, wm_ref, bm_ref, wt_ref, bt_ref, zm_ref, zt_ref, xb_ref, *, n_main):
    j = pl.program_id(1)

    @pl.when(j == 0)
    def _():
        xb_ref[...] = x_ref[...].astype(BF16)

    nt = (((1,), (1,)), ((), ()))

    @pl.when(j < n_main)
    def _():
        zm_ref[...] = lax.dot_general(xb_ref[...], wm_ref[...], nt,
                                      preferred_element_type=F32) + bm_ref[...]

    @pl.when(j == n_main)
    def _():
        zt_ref[...] = lax.dot_general(xb_ref[...], wt_ref[...], nt,
                                      preferred_element_type=F32) + bt_ref[...]


def _inproj(x, w_in_t, b_in, w_tail_t, b_tail, name):
    t, k = x.shape
    tm, tn = INPROJ_TM, INPROJ_TN
    n_main = Z_MAIN_COLS // tn
    main_col = lambda j: jnp.minimum(j, n_main - 1)
    return pl.pallas_call(
        functools.partial(_inproj_kernel, n_main=n_main),
        out_shape=(jax.ShapeDtypeStruct((t, Z_MAIN_COLS), F32),
                   jax.ShapeDtypeStruct((t, Z_TAIL_COLS), F32)),
        grid=(t // tm, n_main + 1),
        in_specs=[
            pl.BlockSpec((tm, k), lambda i, j: (i, 0)),
            pl.BlockSpec((tn, k), lambda i, j: (main_col(j), 0)),
            pl.BlockSpec((1, tn), lambda i, j: (0, main_col(j))),
            pl.BlockSpec((Z_TAIL_COLS, k), lambda i, j: (0, 0)),
            pl.BlockSpec((1, Z_TAIL_COLS), lambda i, j: (0, 0)),
        ],
        out_specs=(pl.BlockSpec((tm, tn), lambda i, j: (i, main_col(j))),
                   pl.BlockSpec((tm, Z_TAIL_COLS), lambda i, j: (i, 0))),
        scratch_shapes=[pltpu.VMEM((tm, k), BF16)],
        compiler_params=_params("arbitrary", "arbitrary"),
        name=name,
    )(x, w_in_t, b_in, w_tail_t, b_tail)


def _matmul_kernel(x_ref, w_ref, o_ref):
    o_ref[...] = jnp.dot(x_ref[...].astype(BF16), w_ref[...], preferred_element_type=F32)


def _matmul(x, w_bf16, tn, name):
    t, k = x.shape
    n = w_bf16.shape[1]
    return pl.pallas_call(
        _matmul_kernel,
        out_shape=jax.ShapeDtypeStruct((t, n), F32),
        grid=(n // tn,),
        in_specs=[pl.BlockSpec((t, k), lambda j: (0, 0)), pl.BlockSpec((k, tn), lambda j: (0, j))],
        out_specs=pl.BlockSpec((t, tn), lambda j: (0, j)),
        compiler_params=_params("arbitrary"),
        name=name,
    )(x, w_bf16)


def _layer_norm_rows(y, g, b):
    mu = jnp.mean(y, axis=-1, keepdims=True)
    yc = y - mu
    var = jnp.mean(yc * yc, axis=-1, keepdims=True)
    return yc * lax.rsqrt(var + LN_EPS) * g + b


CONV_ROWS = 64


def _conv_kernel(a_ref, g_ref, hist_ref, wdw_ref, bdw_ref, gcn_ref, bcn_ref, out_ref, nb_ref,
                 ubuf, shifted, *, tl):
    head = CONV_HIST + 2
    li = pl.program_id(1)

    @pl.when(li == 0)
    def _():
        ubuf[0:2, :] = jnp.zeros((2, CONV_WIDTH), F32)
        ubuf[2:head, :] = hist_ref[0]

    ubuf[head:head + tl, :] = a_ref[...] * jax.nn.sigmoid(g_ref[...])
    span = shifted.shape[1]
    for k in range(1, SUBLANES):
        shifted[k - 1] = ubuf[k:k + span, :]
    for r0 in range(0, tl, CONV_ROWS):
        acc = jnp.zeros((CONV_ROWS, CONV_WIDTH), F32) + bdw_ref[...]
        for j in range(CONV_K):
            lo, k = divmod(2 + j, SUBLANES)
            lo = lo * SUBLANES + r0
            src = ubuf if k == 0 else shifted.at[k - 1]
            acc = acc + wdw_ref[j:j + 1, :] * src[lo:lo + CONV_ROWS, :]
        y = _layer_norm_rows(acc, gcn_ref[...], bcn_ref[...])
        out_ref[r0:r0 + CONV_ROWS, :] = y * jax.nn.sigmoid(y)

    @pl.when(li == pl.num_programs(1) - 1)
    def _():
        nb_ref[0] = ubuf[2 + tl:head + tl, :]

    ubuf[2:head, :] = ubuf[2 + tl:head + tl, :]


def _conv_group(z, hist, w_dw, b_dw, g_cn, b_cn, batch, seq, tl, name):
    nl = seq // tl
    row = lambda b, l: (b * nl + l, 0)
    vec = pl.BlockSpec((1, CONV_WIDTH), lambda b, l: (0, 0))
    return pl.pallas_call(
        functools.partial(_conv_kernel, tl=tl),
        out_shape=(jax.ShapeDtypeStruct((batch * seq, CONV_WIDTH), F32),
                   jax.ShapeDtypeStruct((batch, CONV_HIST, CONV_WIDTH), F32)),
        grid=(batch, nl),
        in_specs=[
            pl.BlockSpec((tl, CONV_WIDTH), lambda b, l: (b * nl + l, Z_CONV_A // CONV_WIDTH)),
            pl.BlockSpec((tl, CONV_WIDTH), lambda b, l: (b * nl + l, Z_CONV_G // CONV_WIDTH)),
            pl.BlockSpec((1, CONV_HIST, CONV_WIDTH), lambda b, l: (b, 0, 0)),
            pl.BlockSpec((CONV_K, CONV_WIDTH), lambda b, l: (0, 0)),
            vec, vec, vec,
        ],
        out_specs=(pl.BlockSpec((tl, CONV_WIDTH), row),
                   pl.BlockSpec((1, CONV_HIST, CONV_WIDTH), lambda b, l: (b, 0, 0))),
        scratch_shapes=[pltpu.VMEM((CONV_HIST + 2 + tl, CONV_WIDTH), F32),
                        pltpu.VMEM((SUBLANES - 1, CONV_HIST + 2 + tl - SUBLANES, CONV_WIDTH), F32)],
        compiler_params=_params("arbitrary", "arbitrary"),
        name=name,
    )(z, z, hist, w_dw, b_dw, g_cn, b_cn)


def _conv_step_kernel(a_ref, g_ref, hist_ref, wdw_ref, bdw_ref, gcn_ref, bcn_ref, out_ref, nb_ref,
                      *, steps):
    u = [a_ref[:, t, :] * jax.nn.sigmoid(g_ref[:, t, :]) for t in range(steps)]
    full = lambda r: hist_ref[r] if r < CONV_HIST else u[r - CONV_HIST]
    for t in range(steps):
        acc = bdw_ref[...] + wdw_ref[0:1, :] * full(t)
        for j in range(1, CONV_K):
            acc = acc + wdw_ref[j:j + 1, :] * full(t + j)
        y = _layer_norm_rows(acc, gcn_ref[...], bcn_ref[...])
        out_ref[:, t, :] = y * jax.nn.sigmoid(y)
    for r in range(CONV_HIST):
        nb_ref[r] = full(r + steps)


def _conv_step_group(z3, hist_t, w_dw, b_dw, g_cn, b_cn, steps, bb, name):
    batch = z3.shape[0]
    vec = pl.BlockSpec((1, CONV_WIDTH), lambda b: (0, 0))
    hist_spec = pl.BlockSpec((CONV_HIST, bb, CONV_WIDTH), lambda b: (0, b, 0))
    return pl.pallas_call(
        functools.partial(_conv_step_kernel, steps=steps),
        out_shape=(jax.ShapeDtypeStruct((batch, steps, CONV_WIDTH), F32),
                   jax.ShapeDtypeStruct((CONV_HIST, batch, CONV_WIDTH), F32)),
        grid=(batch // bb,),
        in_specs=[
            pl.BlockSpec((bb, SAMPLE_PAD, CONV_WIDTH), lambda b: (b, 0, Z_CONV_A // CONV_WIDTH)),
            pl.BlockSpec((bb, SAMPLE_PAD, CONV_WIDTH), lambda b: (b, 0, Z_CONV_G // CONV_WIDTH)),
            hist_spec,
            pl.BlockSpec((CONV_K, CONV_WIDTH), lambda b: (0, 0)),
            vec, vec, vec,
        ],
        out_specs=(pl.BlockSpec((bb, steps, CONV_WIDTH), lambda b: (b, 0, 0)), hist_spec),
        compiler_params=_params("arbitrary"),
        name=name,
    )(z3, z3, hist_t, w_dw, b_dw, g_cn, b_cn)


def _col_to_row(col, eye):
    n = col.shape[0]
    return jnp.sum(jnp.where(eye, jnp.broadcast_to(col, (n, n)), 0.0), axis=0, keepdims=True)


def _mlstm_kernel(q_ref, k_ref, v_ref, o_ref, gate_ref, c0_ref, n0_ref, m0_ref, gmh_ref,
                  h_ref, c_ref, n_ref, m_ref, *, cl, valid, bb, single_chunk):
    if single_chunk:
        c_in, n_in, m_in = c0_ref, n0_ref, m0_ref
    else:
        c_in, n_in, m_in = c_ref, n_ref, m_ref

        @pl.when(pl.program_id(1) == 0)
        def _():
            c_ref[...] = c0_ref[...]
            n_ref[...] = n0_ref[...]
            m_ref[...] = m0_ref[...]

    rows = lax.broadcasted_iota(jnp.int32, (cl, cl), 0)
    cols = lax.broadcasted_iota(jnp.int32, (cl, cl), 1)
    eye = rows == cols
    tril = rows >= cols
    row_id = lax.broadcasted_iota(jnp.int32, (cl, 1), 0)
    nh, dh = MLSTM_HEADS, MLSTM_HEAD_DIM
    nt_dims = (((1,), (1,)), ((), ()))
    tn_dims = (((0,), (0,)), ((), ()))
    each = lambda f, *lists: [f(*args) for args in zip(*lists)]
    all_pairs = [(bi, h) for bi in range(bb) for h in range(nh)]
    for g0 in range(0, len(all_pairs), MLSTM_STAGE_PAIRS):
        pairs = all_pairs[g0:g0 + MLSTM_STAGE_PAIRS]
        cols_of = [slice(h * dh, (h + 1) * dh) for _, h in pairs]
        gates = {bi: gate_ref[bi] for bi in sorted({bi for bi, _ in pairs})}
        log_sig = {bi: jnp.minimum(g, 0.0) - jnp.log1p(jnp.exp(-jnp.abs(g))) for bi, g in gates.items()}
        ip = each(lambda p: gates[p[0]][:, p[1]:p[1] + 1], pairs)
        lf = each(lambda p: log_sig[p[0]][:, nh + p[1]:nh + p[1] + 1], pairs)
        if valid < cl:
            ip = each(lambda x: jnp.where(row_id < valid, x, -jnp.inf), ip)
            lf = each(lambda x: jnp.where(row_id < valid, x, 0.0), lf)
        lf_row = each(lambda x: _col_to_row(x, eye), lf)
        ip_row = each(lambda x: _col_to_row(x, eye), ip)
        b_col = each(lambda r: jnp.sum(jnp.where(tril, jnp.broadcast_to(r, (cl, cl)), 0.0),
                                       axis=1, keepdims=True), lf_row)
        b_row = each(lambda x: _col_to_row(x, eye), b_col)
        m_prev = each(lambda p: m_in[p[0], p[1]:p[1] + 1, 0:1], pairs)
        log_inter = each(lambda b, m: b + m, b_col, m_prev)
        log_intra = each(lambda bc, br, ir: jnp.where(tril, bc - br + ir, -jnp.inf), b_col, b_row, ip_row)
        m_t = each(lambda le, la: jnp.maximum(le, jnp.max(la, axis=1, keepdims=True)), log_inter, log_intra)
        w_inter = each(lambda le, m: jnp.exp(le - m), log_inter, m_t)
        intra = each(lambda la, m: jnp.exp(la - m), log_intra, m_t)
        m_new = each(lambda m: m[cl - 1:cl, :], m_t)
        b_last = each(lambda b: b[cl - 1:cl, :], b_col)
        decay = each(lambda bl, mp, mn: jnp.exp(bl + mp - mn), b_last, m_prev, m_new)
        w_s = each(lambda bl, b, i, mn: jnp.exp(bl - b + i - mn), b_last, b_col, ip, m_new)
        q = each(lambda p, sl: q_ref[p[0], :, sl], pairs, cols_of)
        k = each(lambda p, sl: k_ref[p[0], :, sl] * (dh ** -0.5), pairs, cols_of)
        vb = each(lambda p, sl: v_ref[p[0], :, sl].astype(BF16), pairs, cols_of)
        qb = each(lambda x: x.astype(BF16), q)
        kb = each(lambda x: x.astype(BF16), k)
        s = each(lambda a, b: lax.dot_general(a, b, nt_dims, preferred_element_type=F32), qb, kb)
        s = each(lambda x, d: x * d, s, intra)
        c_old = each(lambda p: c_in[p[0], p[1]], pairs)
        n_old = each(lambda p: n_in[p[0], p[1]:p[1] + 1, :], pairs)
        inter = each(lambda a, c: jnp.dot(a, c.astype(BF16), preferred_element_type=F32), qb, c_old)
        local = each(lambda x, v: jnp.dot(x.astype(BF16), v, preferred_element_type=F32), s, vb)
        num = each(lambda w, a, b: w * a + b, w_inter, inter, local)
        den = each(lambda w, x, n, ss: w * jnp.sum(x * n, axis=1, keepdims=True)
                   + jnp.sum(ss, axis=1, keepdims=True), w_inter, q, n_old, s)
        hh = each(lambda a, d, m: a / jnp.maximum(jnp.abs(d), jnp.exp(-m)), num, den, m_t)
        kw = each(lambda x, w: x * w, k, w_s)
        outer = each(lambda a, v: lax.dot_general(a.astype(BF16), v, tn_dims, preferred_element_type=F32),
                     kw, vb)
        for i, (bi, h) in enumerate(pairs):
            c_ref[bi, h] = decay[i] * c_old[i] + outer[i]
            n_ref[bi, h:h + 1, :] = decay[i] * n_old[i] + jnp.sum(kw[i], axis=0, keepdims=True)
            m_ref[bi, h:h + 1, :] = jnp.broadcast_to(m_new[i], (1, LANES))
        mu = each(lambda x: jnp.mean(x, axis=-1, keepdims=True), hh)
        hc = each(lambda x, m: x - m, hh, mu)
        var = each(lambda x: jnp.mean(x * x, axis=-1, keepdims=True), hc)
        for i, (bi, h) in enumerate(pairs):
            sl = cols_of[i]
            hn = hc[i] * lax.rsqrt(var[i] + LN_EPS) * gmh_ref[:, sl]
            h_ref[bi, :, sl] = hn * jax.nn.sigmoid(o_ref[bi, :, sl])
    if single_chunk:
        m_ref[:, MLSTM_HEADS:, :] = jnp.zeros((bb, SUBLANES - MLSTM_HEADS, LANES), F32)


def _mlstm_group(z_main, z_tail, c0, n0, m0, g_mh, batch, seq, cl, valid, bb, name):
    nc = seq // cl
    zcol = lambda off: pl.BlockSpec((bb, cl, MLSTM_WIDTH), lambda b, c: (b, c, off // MLSTM_WIDTH))
    state = lambda shape: pl.BlockSpec((bb,) + shape, lambda b, c: (b,) + (0,) * len(shape))
    c_shape = (MLSTM_HEADS, MLSTM_HEAD_DIM, MLSTM_HEAD_DIM)
    n_shape = (MLSTM_HEADS, MLSTM_HEAD_DIM)
    m_shape = (SUBLANES, LANES)
    return pl.pallas_call(
        functools.partial(_mlstm_kernel, cl=cl, valid=valid, bb=bb, single_chunk=nc == 1),
        out_shape=(jax.ShapeDtypeStruct((batch, seq, MLSTM_WIDTH), F32),
                   jax.ShapeDtypeStruct((batch,) + c_shape, F32),
                   jax.ShapeDtypeStruct((batch,) + n_shape, F32),
                   jax.ShapeDtypeStruct((batch,) + m_shape, F32)),
        grid=(batch // bb, nc),
        in_specs=[
            zcol(Z_Q), zcol(Z_K), zcol(Z_V), zcol(Z_O),
            pl.BlockSpec((bb, cl, LANES), lambda b, c: (b, c, ZT_GATE // LANES)),
            state(c_shape), state(n_shape), state(m_shape),
            pl.BlockSpec((1, MLSTM_WIDTH), lambda b, c: (0, 0)),
        ],
        out_specs=(pl.BlockSpec((bb, cl, MLSTM_WIDTH), lambda b, c: (b, c, 0)),
                   state(c_shape), state(n_shape), state(m_shape)),
        compiler_params=_params("arbitrary", "arbitrary"),
        name=name,
    )(z_main, z_main, z_main, z_main, z_tail, c0, n0, m0, g_mh)


def _softmax_rows(s):
    e = jnp.exp(s - jnp.max(s, axis=-1, keepdims=True))
    return e / jnp.sum(e, axis=-1, keepdims=True)


_NT_DIMS = (((1,), (1,)), ((), ()))


def _memattn_head_kernel(q_ref, k_ref, v_ref, o_ref):
    s = lax.dot_general(q_ref[...].astype(BF16), k_ref[...].astype(BF16), _NT_DIMS,
                        preferred_element_type=F32) * (MEM_HEAD_DIM ** -0.5)
    p = _softmax_rows(s)
    o_ref[...] = jnp.dot(p.astype(BF16), v_ref[...].astype(BF16), preferred_element_type=F32)


def _memattn_heads(z_tail, kv, batch, seq, tq, name):
    nq = seq // tq
    dh = MEM_HEAD_DIM
    return pl.pallas_call(
        _memattn_head_kernel,
        out_shape=jax.ShapeDtypeStruct((batch * seq, MEM_WIDTH), F32),
        grid=(batch, MEM_HEADS, nq),
        in_specs=[pl.BlockSpec((tq, dh), lambda b, h, i: (b * nq + i, ZT_QM // dh + h)),
                  pl.BlockSpec((N_MEM, dh), lambda b, h, i: (b, h)),
                  pl.BlockSpec((N_MEM, dh), lambda b, h, i: (b, MEM_HEADS + h))],
        out_specs=pl.BlockSpec((tq, dh), lambda b, h, i: (b * nq + i, h)),
        compiler_params=_params("arbitrary", "arbitrary", "arbitrary"),
        name=name,
    )(z_tail, kv, kv)


def _memattn_packed_kernel(q_ref, k_ref, v_ref, o_ref, *, tq, bb):
    nh, dh = MEM_HEADS, MEM_HEAD_DIM
    shape = (nh * tq, N_MEM * nh)
    row_head = lax.broadcasted_iota(jnp.int32, shape, 0) // tq
    col_head = lax.broadcasted_iota(jnp.int32, shape, 1) % nh
    same_head = row_head == col_head
    scores = []
    for bi in range(bb):
        rs = slice(bi * tq, (bi + 1) * tq)
        q = jnp.concatenate([q_ref[rs, h * dh:(h + 1) * dh] for h in range(nh)], axis=0)
        scores.append(lax.dot_general(q.astype(BF16), k_ref[bi].astype(BF16), _NT_DIMS,
                                      preferred_element_type=F32) * (dh ** -0.5))
    probs = [_softmax_rows(jnp.where(same_head, s, -jnp.inf)).astype(BF16) for s in scores]
    outs = [jnp.dot(p, v_ref[bi].astype(BF16), preferred_element_type=F32) for bi, p in enumerate(probs)]
    for bi, o in enumerate(outs):
        rs = slice(bi * tq, (bi + 1) * tq)
        for h in range(nh):
            o_ref[rs, h * dh:(h + 1) * dh] = o[h * tq:(h + 1) * tq, :]


def _memattn_packed(z_tail, mk, mv, batch, tq, bb, name):
    rows = bb * tq
    kv = pl.BlockSpec((bb, N_MEM * MEM_HEADS, MEM_HEAD_DIM), lambda b: (b, 0, 0))
    return pl.pallas_call(
        functools.partial(_memattn_packed_kernel, tq=tq, bb=bb),
        out_shape=jax.ShapeDtypeStruct((batch * tq, MEM_WIDTH), F32),
        grid=(batch // bb,),
        in_specs=[pl.BlockSpec((rows, MEM_WIDTH), lambda b: (b, ZT_QM // MEM_WIDTH)), kv, kv],
        out_specs=pl.BlockSpec((rows, MEM_WIDTH), lambda b: (b, 0)),
        compiler_params=_params("arbitrary"),
        name=name,
    )(z_tail, mk, mv)


def _route(logits):
    lane = lax.broadcasted_iota(jnp.int32, logits.shape, 1).astype(F32)
    neg = -jnp.inf
    first = lambda mask: jnp.min(jnp.where(mask, lane, float(LANES)), axis=1, keepdims=True)
    is_g = lane < N_GROUPS
    gl = jnp.where(is_g, logits, neg)
    g_max = jnp.max(gl, axis=1, keepdims=True)
    g_sel = first(gl == g_max)
    g_w = 1.0 / jnp.sum(jnp.exp(gl - g_max), axis=1, keepdims=True)
    lo = N_GROUPS + g_sel * EXPERTS_PER_GROUP
    in_grp = (lane >= lo) & (lane < lo + EXPERTS_PER_GROUP)
    el = jnp.where(in_grp, logits, neg)
    v1 = jnp.max(el, axis=1, keepdims=True)
    i1 = first(in_grp & (el == v1))
    rest = in_grp & (lane != i1)
    el2 = jnp.where(rest, logits, neg)
    v2 = jnp.max(el2, axis=1, keepdims=True)
    i2 = first(rest & (el2 == v2))
    t = jnp.exp(v2 - v1)
    w1 = g_w / (1.0 + t)
    w2 = g_w * t / (1.0 + t)
    lane_i = lax.broadcasted_iota(jnp.int32, logits.shape, 1)
    e1 = (i1 - N_GROUPS).astype(jnp.int32)
    e2 = (i2 - N_GROUPS).astype(jnp.int32)
    eid = jnp.where(lane_i == 0, e1, jnp.where(lane_i == 1, e2, 0))
    ew = jnp.where(lane_i == 0, w1, jnp.where(lane_i == 1, w2, 0.0))
    return eid, ew


def _outproj_kernel(cp, hp, mp, xp, cs, hs, ms, xs, wout, g1, b1, wr, br,
                    x1_ref, x1p_ref, eid_ref, ew_ref, *, n_prompt):
    def rows(c, h, m, x, rs):
        groups = jnp.concatenate(
            [c[rs, :].astype(BF16), h[rs, :].astype(BF16), m[rs, :].astype(BF16)], axis=1)
        mix = jnp.dot(groups, wout[...], preferred_element_type=F32)
        x1 = _layer_norm_rows(ALPHA * x[rs, :] + mix, g1[...], b1[...])
        x1_ref[rs, :] = x1
        xh = x1.astype(BF16)
        xh_f32 = xh.astype(F32)
        bits = pltpu.bitcast(xh_f32, jnp.int32)
        x1p_ref[rs, :] = bits[:, :D_MODEL // 2] | lax.shift_right_logical(bits[:, D_MODEL // 2:], 16)
        xl = (x1 - xh_f32).astype(BF16)
        n = x1.shape[0]
        cross = jnp.dot(jnp.concatenate([xh, xl], axis=0), wr[...], preferred_element_type=F32)
        logits = ((cross[:n, :ROUTER_LANES] + cross[:n, ROUTER_LANES:])
                  + (cross[n:, :ROUTER_LANES] + cross[n:, ROUTER_LANES:])) + br[...]
        eid, ew = _route(logits)
        eid_ref[rs, :] = eid
        ew_ref[rs, :] = ew

    def body(c, h, m, x):
        rows(c, h, m, x, slice(None))

    i = pl.program_id(0)
    pl.when(i < n_prompt)(lambda: body(cp, hp, mp, xp))
    pl.when(i >= n_prompt)(lambda: body(cs, hs, ms, xs))


def _outproj_router(prompt, sample, w_out_b, g1, b1, wr, br):
    tm = OUT_TM
    tp = prompt[0].shape[0]
    ts = sample[0].shape[0]
    n_p, n_s = tp // tm, ts // tm
    total = tp + ts
    widths = (CONV_WIDTH, MLSTM_WIDTH, MEM_WIDTH, D_MODEL)
    p_specs = [pl.BlockSpec((tm, w), lambda i: (jnp.minimum(i, n_p - 1), 0)) for w in widths]
    once = pl.Buffered(1)
    s_specs = [pl.BlockSpec((tm, w), lambda i: (jnp.maximum(i - n_p, 0), 0),
                            pipeline_mode=once if n_s == 1 else None) for w in widths]
    full = lambda shape: pl.BlockSpec(shape, lambda i: (0, 0), pipeline_mode=once)
    row = lambda w: pl.BlockSpec((tm, w), lambda i: (i, 0))
    return pl.pallas_call(
        functools.partial(_outproj_kernel, n_prompt=n_p),
        out_shape=(jax.ShapeDtypeStruct((total, D_MODEL), F32),
                   jax.ShapeDtypeStruct((total, D_MODEL // 2), jnp.int32),
                   jax.ShapeDtypeStruct((total, ROUTER_LANES), jnp.int32),
                   jax.ShapeDtypeStruct((total, ROUTER_LANES), F32)),
        grid=(n_p + n_s,),
        in_specs=p_specs + s_specs + [
            full((D_MODEL, D_MODEL)), full((1, D_MODEL)), full((1, D_MODEL)),
            full((D_MODEL, 2 * ROUTER_LANES)), full((1, ROUTER_LANES)),
        ],
        out_specs=(row(D_MODEL), row(D_MODEL // 2), row(ROUTER_LANES), row(ROUTER_LANES)),
        compiler_params=_params("arbitrary"),
        name="outproj_router",
    )(*prompt, *sample, w_out_b, g1, b1, wr, br)


def _moe_kernel(stok_ref, sdst_ref, base_ref, nval_ref, first_ref, count_ref,
                x_hbm, wg_ref, wu_ref, wd_ref, y_hbm, wgb, wub, wdb, xbuf, ybuf, gsem, ssem,
                *, dump_row):
    e = pl.program_id(0)
    last = pl.num_programs(0) - 1
    rc = MOE_ROWS
    ns = MOE_SLOTS

    def gather_start(base, slot):
        for r in range(rc):
            pltpu.make_async_copy(x_hbm.at[pl.ds(stok_ref[base + r], 1)],
                                  xbuf.at[slot, pl.ds(r, 1)], gsem.at[slot]).start()

    def gather_wait(slot):
        pltpu.make_async_copy(x_hbm.at[pl.ds(0, rc)], xbuf.at[slot], gsem.at[slot]).wait()

    def scatter_start(base, n_valid, slot):
        for r in range(rc):
            dst = jnp.where(r < n_valid, sdst_ref[base + r], dump_row + slot * rc + r)
            pltpu.make_async_copy(ybuf.at[slot, pl.ds(r, 1)], y_hbm.at[pl.ds(dst, 1)],
                                  ssem.at[slot]).start()

    def scatter_wait(slot):
        pltpu.make_async_copy(ybuf.at[slot], y_hbm.at[pl.ds(0, rc)], ssem.at[slot]).wait()

    @pl.when(e == 0)
    def _():
        ybuf[...] = jnp.zeros(ybuf.shape, F32)
        for slot in range(ns):
            pltpu.make_async_copy(ybuf.at[slot], y_hbm.at[pl.ds(dump_row + slot * rc, rc)],
                                  ssem.at[slot]).start()
            scatter_wait(slot)
        for ahead in range(ns - 1):
            gather_start(base_ref[1 + ahead], ahead)

    wgb[...] = wg_ref[0].astype(BF16)
    wub[...] = wu_ref[0].astype(BF16)
    wdb[...] = wd_ref[0].astype(BF16)
    first = first_ref[e]

    def chunk_on(slot, g):
        prv = (slot - 1) % ns
        gather_wait(slot)

        @pl.when(g >= ns - 1)
        def _():
            scatter_wait(slot)

        gather_start(base_ref[g + ns], prv)
        scatter_start(base_ref[g], nval_ref[g], prv)
        packed = xbuf[slot]
        x = jnp.concatenate(
            [pltpu.bitcast(packed & jnp.int32(-65536), F32).astype(BF16),
             pltpu.bitcast(packed << 16, F32).astype(BF16)], axis=1)
        hg = jnp.dot(x, wgb[...], preferred_element_type=F32)
        hu = jnp.dot(x, wub[...], preferred_element_type=F32)
        hid = (hg * jax.nn.sigmoid(hg) * hu).astype(BF16)
        ybuf[slot] = jnp.dot(hid, wdb[...], preferred_element_type=F32)

    def chunk(c, carry):
        g = first + c
        cur = lax.rem(g, ns)
        for slot in range(ns):
            pl.when(cur == slot)(functools.partial(chunk_on, slot, g))
        return carry

    lax.fori_loop(0, count_ref[e], chunk, 0)

    @pl.when(e == last)
    def _():
        g_end = first + count_ref[e]
        cur = lax.rem(g_end, ns)
        for slot in range(ns):
            @pl.when(cur == slot)
            def _():
                for ahead in range(ns - 1):
                    gather_wait((slot + ahead) % ns)
                scatter_start(base_ref[g_end], nval_ref[g_end], (slot - 1) % ns)

        for slot in range(ns):
            scatter_wait(slot)


def _moe(x1, tables, w_gate, w_up, w_down):
    total = x1.shape[0]
    dump_row = 2 * total
    wspec = lambda shape: pl.BlockSpec((1,) + shape, lambda e, *_: (e, 0, 0))
    grid_spec = pltpu.PrefetchScalarGridSpec(
        num_scalar_prefetch=len(tables),
        grid=(N_EXPERTS,),
        in_specs=[pl.BlockSpec(memory_space=pl.ANY),
                  wspec((D_MODEL, D_EXPERT)), wspec((D_MODEL, D_EXPERT)), wspec((D_EXPERT, D_MODEL))],
        out_specs=pl.BlockSpec(memory_space=pl.ANY),
        scratch_shapes=[
            pltpu.VMEM((D_MODEL, D_EXPERT), BF16), pltpu.VMEM((D_MODEL, D_EXPERT), BF16),
            pltpu.VMEM((D_EXPERT, D_MODEL), BF16),
            pltpu.VMEM((MOE_SLOTS, MOE_ROWS, D_MODEL // 2), jnp.int32),
            pltpu.VMEM((MOE_SLOTS, MOE_ROWS, D_MODEL), F32),
            pltpu.SemaphoreType.DMA((MOE_SLOTS,)), pltpu.SemaphoreType.DMA((MOE_SLOTS,)),
        ],
    )
    return pl.pallas_call(
        functools.partial(_moe_kernel, dump_row=dump_row),
        out_shape=jax.ShapeDtypeStruct((2 * total + MOE_SLOTS * MOE_ROWS, D_MODEL), F32),
        grid_spec=grid_spec,
        compiler_params=_params("arbitrary"),
        name="moe_experts",
    )(*tables, x1, w_gate, w_up, w_down)


def _final_kernel(x1_ref, y0_ref, y1_ref, ew_ref, g2, b2, o_ref):
    ew = ew_ref[...]
    ffn = ew[:, 0:1] * y0_ref[...] + ew[:, 1:2] * y1_ref[...]
    o_ref[...] = _layer_norm_rows(ALPHA * x1_ref[...] + ffn, g2[...], b2[...])


def _final(x1, y, ew, g2, b2, row0, rows, name):
    tm = FINAL_TM
    total = x1.shape[0]
    off = row0 // tm
    k1 = total // tm
    vec = pl.BlockSpec((1, D_MODEL), lambda i: (0, 0))
    return pl.pallas_call(
        _final_kernel,
        out_shape=jax.ShapeDtypeStruct((rows, D_MODEL), F32),
        grid=(rows // tm,),
        in_specs=[pl.BlockSpec((tm, D_MODEL), lambda i: (i + off, 0)),
                  pl.BlockSpec((tm, D_MODEL), lambda i: (i + off, 0)),
                  pl.BlockSpec((tm, D_MODEL), lambda i: (i + off + k1, 0)),
                  pl.BlockSpec((tm, ROUTER_LANES), lambda i: (i + off, 0)),
                  vec, vec],
        out_specs=pl.BlockSpec((tm, D_MODEL), lambda i: (i, 0)),
        compiler_params=_params("arbitrary"),
        name=name,
    )(x1, y, y, ew, g2, b2)


def _dispatch_tables(eid, total):
    flat_e = eid[:, :2].reshape(-1)
    order = jnp.argsort(flat_e, stable=True).astype(jnp.int32)
    stok = order >> 1
    sdst = (order & 1) * total + stok
    experts = jnp.arange(N_EXPERTS, dtype=jnp.int32)
    cnt = jnp.sum((flat_e[:, None] == experts[None, :]).astype(jnp.int32), axis=0)
    start = jnp.cumsum(cnt) - cnt
    pad = jnp.zeros((MOE_ROWS,), jnp.int32)
    n_chunks = jnp.maximum((cnt + MOE_ROWS - 1) // MOE_ROWS, 1)
    first = jnp.cumsum(n_chunks) - n_chunks
    n_entries = flat_e.shape[0] // MOE_ROWS + N_EXPERTS + MOE_SLOTS + 1
    g = jnp.arange(n_entries, dtype=jnp.int32) - 1
    owner = jnp.sum((g[:, None] >= (first + n_chunks)[None, :]).astype(jnp.int32), axis=1)
    real = (g >= 0) & (owner < N_EXPERTS)
    pick = (owner[:, None] == experts[None, :]).astype(jnp.int32)
    sel = lambda v: jnp.sum(pick * v[None, :], axis=1)
    local = (g - sel(first)) * MOE_ROWS
    base = jnp.where(real, sel(start) + local, 0)
    n_valid = jnp.where(real, sel(cnt) - local, 0)
    return (jnp.concatenate([stok, pad]), jnp.concatenate([sdst, pad]), base, n_valid, first, n_chunks)


def kernel(x_prompt, x_sample, mem_prompt, state_conv, state_mlstm_C, state_mlstm_n, state_mlstm_m,
           cache_mem_k, cache_mem_v, w_in, b_in, w_dw, b_dw, g_cn, b_cn, g_mh, w_mk, w_mv, w_out,
           g_ln1, b_ln1, w_rg, b_rg, w_re, b_re, w_gate, w_up, w_down, g_ln2, b_ln2):
    bp, sp, _ = x_prompt.shape
    bs, ss, _ = x_sample.shape
    tp, ts = bp * sp, bs * ss

    gate_hi = Z_MAIN_COLS + N_GATE_COLS
    tail = lambda wt: jnp.concatenate(
        [wt[gate_hi:], wt[Z_MAIN_COLS:gate_hi],
         jnp.zeros((LANES - N_GATE_COLS,) + wt.shape[1:], wt.dtype)], axis=0)
    w_in_t = jnp.transpose(w_in[0]).astype(BF16)
    w_tail_t = tail(w_in_t)
    b_tail = tail(b_in[0])[None, :]
    w_kv_b = jnp.concatenate([w_mk[0], w_mv[0]], axis=-1).astype(BF16)
    w_out_b = w_out[0].astype(BF16)
    w_r = jnp.concatenate([w_rg[0], w_re[0],
                           jnp.zeros((D_MODEL, ROUTER_LANES - N_GROUPS - N_EXPERTS), F32)], axis=-1)
    w_r_hi = w_r.astype(BF16)
    w_r_lo = (w_r - w_r_hi.astype(F32)).astype(BF16)
    w_r_split = jnp.concatenate([w_r_hi, w_r_lo], axis=1)
    b_r = jnp.concatenate([b_rg[0], b_re[0],
                           jnp.zeros((ROUTER_LANES - N_GROUPS - N_EXPERTS,), F32)])[None, :]
    row = lambda a: a[0][None, :]

    xs_pad = jnp.pad(x_sample, ((0, 0), (0, SAMPLE_PAD - ss), (0, 0))).reshape(bs * SAMPLE_PAD, D_MODEL)
    zm_p, zt_p = _inproj(x_prompt.reshape(tp, D_MODEL), w_in_t, b_in, w_tail_t, b_tail, "inproj_prompt")
    zm_s, zt_s = _inproj(xs_pad, w_in_t, b_in, w_tail_t, b_tail, "inproj_sample")

    kv = _matmul(mem_prompt.reshape(bp * N_MEM, D_MODEL), w_kv_b, KV_TN, "mem_kv")
    mk_p = kv[:, :MEM_WIDTH].reshape(bp, N_MEM, MEM_HEADS, MEM_HEAD_DIM)
    mv_p = kv[:, MEM_WIDTH:].reshape(bp, N_MEM, MEM_HEADS, MEM_HEAD_DIM)

    conv_args = (w_dw[0], row(b_dw), row(g_cn), row(b_cn))
    conv_p, buf_p = _conv_group(zm_p, jnp.zeros((bp, CONV_HIST, CONV_WIDTH), F32), *conv_args,
                                bp, sp, 512, "conv_prompt")
    conv_s, buf_s_t = _conv_step_group(
        zm_s.reshape(bs, SAMPLE_PAD, Z_MAIN_COLS), jnp.transpose(state_conv[0], (1, 0, 2)),
        *conv_args, ss, 32, "conv_sample")
    buf_s = jnp.transpose(buf_s_t, (1, 0, 2))

    m_tile = lambda m: jnp.broadcast_to(
        jnp.pad(m, ((0, 0), (0, SUBLANES - MLSTM_HEADS)))[:, :, None], (m.shape[0], SUBLANES, LANES))
    g_mh_r = row(g_mh)
    seqs = lambda z, b: z.reshape(b, z.shape[0] // b, z.shape[1])
    h_p, c_p, n_p, m_p = _mlstm_group(
        seqs(zm_p, bp), seqs(zt_p, bp),
        jnp.zeros((bp, MLSTM_HEADS, MLSTM_HEAD_DIM, MLSTM_HEAD_DIM), F32),
        jnp.zeros((bp, MLSTM_HEADS, MLSTM_HEAD_DIM), F32), jnp.zeros((bp, SUBLANES, LANES), F32),
        g_mh_r, bp, sp, MLSTM_CHUNK, MLSTM_CHUNK, 2, "mlstm_prompt")
    h_s, c_s, n_s, m_s = _mlstm_group(
        seqs(zm_s, bs), seqs(zt_s, bs), state_mlstm_C[0], state_mlstm_n[0], m_tile(state_mlstm_m[0]),
        g_mh_r, bs, SAMPLE_PAD, SAMPLE_PAD, ss, 8, "mlstm_sample")
    h_p = h_p.reshape(tp, MLSTM_WIDTH)

    mem_p = _memattn_heads(zt_p, kv, bp, sp, sp, "memattn_prompt")
    packed = lambda c: c[0].reshape(bs, N_MEM * MEM_HEADS, MEM_HEAD_DIM)
    mem_s = _memattn_packed(zt_s, packed(cache_mem_k), packed(cache_mem_v), bs, SAMPLE_PAD, 8,
                            "memattn_sample")

    compact = lambda a: a.reshape(bs, SAMPLE_PAD, a.shape[-1])[:, :ss].reshape(ts, a.shape[-1])
    x1, x1_packed, eid, ew = _outproj_router(
        (conv_p, h_p, mem_p, x_prompt.reshape(tp, D_MODEL)),
        (conv_s.reshape(ts, CONV_WIDTH), h_s[:, :ss].reshape(ts, MLSTM_WIDTH), compact(mem_s),
         x_sample.reshape(ts, D_MODEL)),
        w_out_b, row(g_ln1), row(b_ln1), w_r_split, b_r)

    total = tp + ts
    y = _moe(x1_packed, _dispatch_tables(eid, total), w_gate[0], w_up[0], w_down[0])
    g2, b2 = row(g_ln2), row(b_ln2)
    y_p = _final(x1, y, ew, g2, b2, 0, tp, "final_prompt").reshape(bp, sp, D_MODEL)
    y_s = _final(x1, y, ew, g2, b2, tp, ts, "final_sample").reshape(bs, ss, D_MODEL)

    return (y_p, y_s, buf_p[None], buf_s[None], c_p[None], c_s[None], n_p[None], n_s[None],
            m_p[:, :MLSTM_HEADS, 0][None], m_s[:, :MLSTM_HEADS, 0][None], mk_p[None], mv_p[None])
```
